```python
import math
import jax, jax.numpy as jnp
from jax import lax
import numpy as np

D_MODEL = 1024
BATCH = 16
SEQ = 256
DEPTH = 1
DEC_BATCH = 8
DEC_SEQ = 4096
PAST_LEN = 256

GRID_W = 64
HEAD_DIM = 64
N_Q_HEADS = 8
N_KV_HEADS = 2
Q_PER_KV = N_Q_HEADS // N_KV_HEADS
ATTN_WIDTH = N_Q_HEADS * HEAD_DIM
KV_WIDTH = N_KV_HEADS * HEAD_DIM
WINDOW = 128
ATTN_BLOCK = 128
ROPE_BASE = 10000.0
SSM_WIDTH = 512
SSM_GROUP = 16
SSM_GROUPS = SSM_WIDTH // SSM_GROUP
SSM_STATE = 64
DT_MIN = 1e-3
DT_MAX = 1e-1
N_EXPERTS = 64
TOP_K = 6
EXPERT_FF = 256
SHARED_FF = 256
ROUTED_SCALE = 2.5
MOE_BLOCK = 128
N_MOD = 6
IN_WIDTH = ATTN_WIDTH + 2 * KV_WIDTH + SSM_WIDTH + 2 * D_MODEL
DEEPNORM_ALPHA = (2.0 * DEPTH) ** 0.25
DEEPNORM_BETA = (8.0 * DEPTH) ** -0.25
LN_EPS = 1e-5

kernel_name = "hybrid_s5_swa_moe_diffusion_step"


def layer_norm(x, g, b):
    xf = x.astype(jnp.float32)
    mu = jnp.mean(xf, -1, keepdims=True)
    var = jnp.mean(jnp.square(xf - mu), -1, keepdims=True)
    return ((xf - mu) * lax.rsqrt(var + LN_EPS)).astype(x.dtype) * g + b


def modulation(cond, w, b):
    m = jax.nn.silu(cond) @ w + b
    return m.reshape(m.shape[0], N_MOD, D_MODEL)


def axial_rope_tables(n_tokens):
    rows = n_tokens // GRID_W
    t = jnp.arange(rows * GRID_W, dtype=jnp.int32)
    pos = jnp.stack([t // GRID_W, t % GRID_W], axis=-1).astype(jnp.float32)
    n_freq = HEAD_DIM // 4
    inv_freq = ROPE_BASE ** (-jnp.arange(n_freq, dtype=jnp.float32) / n_freq)
    ang = pos[:, :, None] * inv_freq
    return jnp.cos(ang), jnp.sin(ang)


def apply_axial_rope(x, cos, sin):
    B_, L, H, _ = x.shape
    xf = x.astype(jnp.float32).reshape(B_, L, H, 2, 2, HEAD_DIM // 4)
    x1, x2 = xf[..., 0, :], xf[..., 1, :]
    c, s = cos[None, :, None], sin[None, :, None]
    out = jnp.stack([x1 * c - x2 * s, x2 * c + x1 * s], axis=-2)
    return out.reshape(B_, L, H, HEAD_DIM).astype(x.dtype)


def _sink_logits(sink, shape):
    s = sink.astype(jnp.float32).reshape(1, N_KV_HEADS, Q_PER_KV, 1, 1)
    return jnp.broadcast_to(s.reshape((1,) * (len(shape) - 4) + (N_KV_HEADS, Q_PER_KV, 1, 1)), shape[:-1] + (1,))


def attend_context(q, k, v, sink):
    B_, L = q.shape[:2]
    qg = q.reshape(B_, L, N_KV_HEADS, Q_PER_KV, HEAD_DIM)
    s = jnp.einsum('bqhgd,bkhd->bhgqk', qg, k).astype(jnp.float32) * (HEAD_DIM ** -0.5)
    logits = jnp.concatenate([s, _sink_logits(sink, s.shape)], axis=-1)
    p = jax.nn.softmax(logits, axis=-1)[..., :L].astype(v.dtype)
    o = jnp.einsum('bhgqk,bkhd->bqhgd', p, v)
    return o.reshape(B_, L, ATTN_WIDTH)


def attend_latent(q, k, v, k_ctx, v_ctx, sink):
    B_, S = q.shape[:2]
    nb = S // ATTN_BLOCK
    qb = q.reshape(B_, nb, ATTN_BLOCK, N_KV_HEADS, Q_PER_KV, HEAD_DIM)

    def band(t):
        tp = jnp.pad(t, ((0, 0), (ATTN_BLOCK, ATTN_BLOCK), (0, 0), (0, 0)))
        tb = tp.reshape(B_, nb + 2, ATTN_BLOCK, N_KV_HEADS, HEAD_DIM)
        return jnp.concatenate([tb[:, :-2], tb[:, 1:-1], tb[:, 2:]], axis=2)

    kb, vb = band(k), band(v)
    blk = jnp.arange(nb, dtype=jnp.int32)[:, None, None]
    qpos = blk * ATTN_BLOCK + jnp.arange(ATTN_BLOCK, dtype=jnp.int32)[None, :, None]
    kpos = (blk - 1) * ATTN_BLOCK + jnp.arange(3 * ATTN_BLOCK, dtype=jnp.int32)[None, None, :]
    mask = (jnp.abs(qpos - kpos) <= WINDOW) & (kpos >= 0) & (kpos < S)
    scale = HEAD_DIM ** -0.5
    s_loc = jnp.einsum('bnqhgd,bnkhd->bnhgqk', qb, kb).astype(jnp.float32) * scale
    s_loc = jnp.where(mask[None, :, None, None], s_loc, jnp.finfo(jnp.float32).min)
    s_ctx = jnp.einsum('bnqhgd,bchd->bnhgqc', qb, k_ctx).astype(jnp.float32) * scale
    logits = jnp.concatenate([s_loc, s_ctx, _sink_logits(sink, s_ctx.shape)], axis=-1)
    p = jax.nn.softmax(logits, axis=-1).astype(v.dtype)
    n_loc = 3 * ATTN_BLOCK
    n_ctx = k_ctx.shape[1]
    o = (jnp.einsum('bnhgqk,bnkhd->bnqhgd', p[..., :n_loc], vb)
         + jnp.einsum('bnhgqc,bchd->bnqhgd', p[..., n_loc:n_loc + n_ctx], v_ctx))
    return o.reshape(B_, S, ATTN_WIDTH)


def ssm_discretise(a_re, a_im, log_dt, b_re, b_im):
    lam_re = jnp.minimum(a_re, -1e-4)
    lam_im = a_im
    dt = jnp.exp(log_dt)[:, None]
    mag = jnp.exp(lam_re * dt)
    abar_re, abar_im = mag * jnp.cos(lam_im * dt), mag * jnp.sin(lam_im * dt)
    den = jnp.square(lam_re) + jnp.square(lam_im)
    p, qi = abar_re - 1.0, abar_im
    f_re = (p * lam_re + qi * lam_im) / den
    f_im = (qi * lam_re - p * lam_im) / den
    bbar_re = f_re[..., None] * b_re - f_im[..., None] * b_im
    bbar_im = f_re[..., None] * b_im + f_im[..., None] * b_re
    return (abar_re, abar_im), (bbar_re, bbar_im)


def _lin_combine(e1, e2):
    a1r, a1i, b1r, b1i = e1
    a2r, a2i, b2r, b2i = e2
    return (a1r * a2r - a1i * a2i, a1r * a2i + a1i * a2r,
            a2r * b1r - a2i * b1i + b2r, a2r * b1i + a2i * b1r + b2i)


def ssm_scan(u, abar, bbar, h0, reverse):
    abar_re, abar_im = abar
    bu_re = jnp.einsum('blgc,gnc->blgn', u, bbar[0])
    bu_im = jnp.einsum('blgc,gnc->blgn', u, bbar[1])
    if h0 is not None:
        first = -1 if reverse else 0
        h_re, h_im = h0
        bu_re = bu_re.at[:, first].add(abar_re * h_re - abar_im * h_im)
        bu_im = bu_im.at[:, first].add(abar_re * h_im + abar_im * h_re)
    a_re = jnp.broadcast_to(abar_re, bu_re.shape)
    a_im = jnp.broadcast_to(abar_im, bu_re.shape)
    _, _, x_re, x_im = lax.associative_scan(_lin_combine, (a_re, a_im, bu_re, bu_im), axis=1, reverse=reverse)
    return x_re, x_im


def ssm_mix(u, lp, h0):
    B_, L = u.shape[:2]
    uf = u.astype(jnp.float32).reshape(B_, L, SSM_GROUPS, SSM_GROUP)
    y = uf * lp['ssm_d'].astype(jnp.float32).reshape(SSM_GROUPS, SSM_GROUP)
    finals = []
    for d in range(2):
        abar, bbar = ssm_discretise(lp['ssm_a_re'][d].astype(jnp.float32), lp['ssm_a_im'][d].astype(jnp.float32),
                                    lp['ssm_log_dt'][d].astype(jnp.float32),
                                    lp['ssm_b_re'][d].astype(jnp.float32), lp['ssm_b_im'][d].astype(jnp.float32))
        init = None if h0 is None else (h0[:, d, 0].astype(jnp.float32), h0[:, d, 1].astype(jnp.float32))
        x_re, x_im = ssm_scan(uf, abar, bbar, init, reverse=(d == 1))
        y = (y + jnp.einsum('blgn,gcn->blgc', x_re, lp['ssm_c_re'][d].astype(jnp.float32))
             - jnp.einsum('blgn,gcn->blgc', x_im, lp['ssm_c_im'][d].astype(jnp.float32)))
        if h0 is None:
            end = 0 if d == 1 else -1
            finals.append(jnp.stack([x_re[:, end], x_im[:, end]], axis=1))
    y = y.reshape(B_, L, SSM_WIDTH).astype(u.dtype)
    if h0 is None:
        return y, jnp.stack(finals, axis=1).astype(u.dtype)
    return y


def swiglu(x, wg, wu, wd):
    return (jax.nn.silu(x @ wg) * (x @ wu)) @ wd


def grouped_experts(x, idx, wts, w_gate, w_up, w_down):
    T = x.shape[0]
    n_assign = T * TOP_K
    n_blocks = -(-n_assign // MOE_BLOCK) + N_EXPERTS
    n_slots = n_blocks * MOE_BLOCK
    flat_e = idx.reshape(-1)
    flat_tok = jnp.arange(n_assign, dtype=jnp.int32) // TOP_K
    flat_w = wts.reshape(-1)
    order = jnp.argsort(flat_e)
    e_sorted = flat_e[order]
    counts = jnp.bincount(flat_e, length=N_EXPERTS)
    padded = (counts + MOE_BLOCK - 1) // MOE_BLOCK * MOE_BLOCK
    start = jnp.cumsum(counts) - counts
    pad_end = jnp.cumsum(padded)
    pad_start = pad_end - padded
    dest = pad_start[e_sorted] + jnp.arange(n_assign, dtype=jnp.int32) - start[e_sorted]
    slot_tok = jnp.zeros((n_slots,), jnp.int32).at[dest].set(flat_tok[order])
    slot_w = jnp.zeros((n_slots,), x.dtype).at[dest].set(flat_w[order])
    block_e = jnp.minimum(jnp.searchsorted(pad_end, jnp.arange(n_blocks, dtype=jnp.int32) * MOE_BLOCK, side='right'),
                          N_EXPERTS - 1)

    def expert_block(args):
        tok, wb, e = args
        return swiglu(x[tok], w_gate[e], w_up[e], w_down[e]) * wb[:, None]

    yb = lax.map(expert_block, (slot_tok.reshape(n_blocks, MOE_BLOCK), slot_w.reshape(n_blocks, MOE_BLOCK), block_e))
    return jax.ops.segment_sum(yb.reshape(n_slots, D_MODEL), slot_tok, num_segments=T)


def moe_ffn(h, lp):
    lead = h.shape[:-1]
    x = h.reshape(-1, D_MODEL)
    scores = jax.nn.sigmoid((x @ lp['router_w']).astype(jnp.float32))
    _, idx = lax.top_k(scores + lp['router_b'].astype(jnp.float32), TOP_K)
    s_sel = jnp.take_along_axis(scores, idx, axis=-1)
    wts = (s_sel / jnp.sum(s_sel, -1, keepdims=True) * ROUTED_SCALE).astype(x.dtype)
    routed = grouped_experts(x, idx, wts, lp['exp_w_gate'], lp['exp_w_up'], lp['exp_w_down'])
    shared = swiglu(x, lp['sh_w_gate'], lp['sh_w_up'], lp['sh_w_down'])
    return (routed + shared).reshape(lead + (D_MODEL,))


def split_projection(p):
    o1 = ATTN_WIDTH
    o2 = o1 + KV_WIDTH
    o3 = o2 + KV_WIDTH
    o4 = o3 + SSM_WIDTH
    o5 = o4 + D_MODEL
    return p[..., :o1], p[..., o1:o2], p[..., o2:o3], p[..., o3:o4], p[..., o4:o5], p[..., o5:]


def merge_branches(attn_o, ssm_y, ga_logit, gs_logit, lp):
    z = jax.nn.gelu(ssm_y)
    ssm_o = z * jax.nn.sigmoid(z @ lp['w_glu'])
    merged = jax.nn.sigmoid(ga_logit) * (attn_o @ lp['w_attn_br']) + jax.nn.sigmoid(gs_logit) * (ssm_o @ lp['w_ssm_br'])
    return merged @ lp['w_out']


def run_layer(x, mod, lp, cache):
    sh1, sc1, g1, sh2, sc2, g2 = (mod[:, i, None, :] for i in range(N_MOD))
    B_, L = x.shape[:2]
    h = x * (1 + sc1) + sh1
    q, k, v, u, ga, gs = split_projection(h @ lp['w_in'])
    q = q.reshape(B_, L, N_Q_HEADS, HEAD_DIM)
    k = k.reshape(B_, L, N_KV_HEADS, HEAD_DIM)
    v = v.reshape(B_, L, N_KV_HEADS, HEAD_DIM)
    if cache is None:
        attn_o = attend_context(q, k, v, lp['attn_sink'])
        ssm_y, ssm_state = ssm_mix(u, lp, None)
    else:
        k_ctx, v_ctx, h0 = cache
        cos, sin = axial_rope_tables(L)
        attn_o = attend_latent(apply_axial_rope(q, cos, sin), apply_axial_rope(k, cos, sin), v,
                               k_ctx, v_ctx, lp['attn_sink'])
        ssm_y = ssm_mix(u, lp, h0)
    mix = merge_branches(attn_o, ssm_y, ga, gs, lp)
    x = layer_norm(DEEPNORM_ALPHA * x + g1 * mix, lp['ln1_g'], lp['ln1_b'])
    h = x * (1 + sc2) + sh2
    x = layer_norm(DEEPNORM_ALPHA * x + g2 * moe_ffn(h, lp), lp['ln2_g'], lp['ln2_b'])
    if cache is None:
        return x, k, v, ssm_state
    return x


def setup_inputs(seed: int = 0) -> dict:
    key = jax.random.key(seed)
    kit = iter(jax.random.split(key, 48))

    def nrm(shape, scale):
        return scale * jax.random.normal(next(kit), shape, jnp.float32)

    G, N, C = SSM_GROUPS, SSM_STATE, SSM_GROUP
    n_idx = jnp.arange(N, dtype=jnp.float32)
    return {
        'x_prompt': nrm((BATCH, SEQ, D_MODEL), 1.0),
        'x_sample': nrm((DEC_BATCH, DEC_SEQ, D_MODEL), 1.0),
        'cache_k': nrm((DEC_BATCH, DEPTH, PAST_LEN, N_KV_HEADS, HEAD_DIM), 1.0),
        'cache_v': nrm((DEC_BATCH, DEPTH, PAST_LEN, N_KV_HEADS, HEAD_DIM), 1.0),
        'state_ssm': nrm((DEC_BATCH, DEPTH, 2, 2, G, N), 0.1),
        'c': nrm((DEC_BATCH, D_MODEL), 1.0),
        'c_ctx': nrm((D_MODEL,), 1.0),
        'mod_w': nrm((DEPTH, D_MODEL, N_MOD * D_MODEL), 0.3 * D_MODEL ** -0.5),
        'mod_b': nrm((DEPTH, N_MOD * D_MODEL), 0.02),
        'w_in': nrm((DEPTH, D_MODEL, IN_WIDTH), D_MODEL ** -0.5),
        'attn_sink': nrm((DEPTH, N_Q_HEADS), 1.0),
        'w_attn_br': nrm((DEPTH, ATTN_WIDTH, D_MODEL), ATTN_WIDTH ** -0.5),
        'ssm_a_re': -0.5 + nrm((DEPTH, 2, G, N), 0.01),
        'ssm_a_im': math.pi * n_idx + nrm((DEPTH, 2, G, N), 0.01),
        'ssm_log_dt': jax.random.uniform(next(kit), (DEPTH, 2, G), jnp.float32, math.log(DT_MIN), math.log(DT_MAX)),
        'ssm_b_re': nrm((DEPTH, 2, G, N, C), (2.0 * C) ** -0.5),
        'ssm_b_im': nrm((DEPTH, 2, G, N, C), (2.0 * C) ** -0.5),
        'ssm_c_re': nrm((DEPTH, 2, G, C, N), (2.0 * N) ** -0.5),
        'ssm_c_im': nrm((DEPTH, 2, G, C, N), (2.0 * N) ** -0.5),
        'ssm_d': nrm((DEPTH, SSM_WIDTH), 1.0),
        'w_glu': nrm((DEPTH, SSM_WIDTH, SSM_WIDTH), SSM_WIDTH ** -0.5),
        'w_ssm_br': nrm((DEPTH, SSM_WIDTH, D_MODEL), SSM_WIDTH ** -0.5),
        'w_out': nrm((DEPTH, D_MODEL, D_MODEL), DEEPNORM_BETA * D_MODEL ** -0.5),
        'ln1_g': 1.0 + nrm((DEPTH, D_MODEL), 0.02),
        'ln1_b': nrm((DEPTH, D_MODEL), 0.02),
        'ln2_g': 1.0 + nrm((DEPTH, D_MODEL), 0.02),
        'ln2_b': nrm((DEPTH, D_MODEL), 0.02),
        'router_w': nrm((DEPTH, D_MODEL, N_EXPERTS), D_MODEL ** -0.5),
        'router_b': nrm((DEPTH, N_EXPERTS), 0.01),
        'exp_w_gate': nrm((DEPTH, N_EXPERTS, D_MODEL, EXPERT_FF), D_MODEL ** -0.5),
        'exp_w_up': nrm((DEPTH, N_EXPERTS, D_MODEL, EXPERT_FF), D_MODEL ** -0.5),
        'exp_w_down': nrm((DEPTH, N_EXPERTS, EXPERT_FF, D_MODEL), DEEPNORM_BETA * EXPERT_FF ** -0.5),
        'sh_w_gate': nrm((DEPTH, D_MODEL, SHARED_FF), D_MODEL ** -0.5),
        'sh_w_up': nrm((DEPTH, D_MODEL, SHARED_FF), D_MODEL ** -0.5),
        'sh_w_down': nrm((DEPTH, SHARED_FF, D_MODEL), DEEPNORM_BETA * SHARED_FF ** -0.5),
    }


def reference(x_prompt, x_sample, cache_k, cache_v, state_ssm, c, c_ctx, mod_w, mod_b, w_in, attn_sink,
              w_attn_br, ssm_a_re, ssm_a_im, ssm_log_dt, ssm_b_re, ssm_b_im, ssm_c_re, ssm_c_im, ssm_d,
              w_glu, w_ssm_br, w_out, ln1_g, ln1_b, ln2_g, ln2_b, router_w, router_b,
              exp_w_gate, exp_w_up, exp_w_down, sh_w_gate, sh_w_up, sh_w_down):
    xp, xs = x_prompt, x_sample
    ks, vs, ss = [], [], []
    for l in range(DEPTH):
        lp = {
            'w_in': w_in[l], 'attn_sink': attn_sink[l], 'w_attn_br': w_attn_br[l],
            'ssm_a_re': ssm_a_re[l], 'ssm_a_im': ssm_a_im[l], 'ssm_log_dt': ssm_log_dt[l],
            'ssm_b_re': ssm_b_re[l], 'ssm_b_im': ssm_b_im[l], 'ssm_c_re': ssm_c_re[l], 'ssm_c_im': ssm_c_im[l],
            'ssm_d': ssm_d[l], 'w_glu': w_glu[l], 'w_ssm_br': w_ssm_br[l], 'w_out': w_out[l],
            'ln1_g': ln1_g[l], 'ln1_b': ln1_b[l], 'ln2_g': ln2_g[l], 'ln2_b': ln2_b[l],
            'router_w': router_w[l], 'router_b': router_b[l],
            'exp_w_gate': exp_w_gate[l], 'exp_w_up': exp_w_up[l], 'exp_w_down': exp_w_down[l],
            'sh_w_gate': sh_w_gate[l], 'sh_w_up': sh_w_up[l], 'sh_w_down': sh_w_down[l],
        }
        mod_ctx = modulation(c_ctx[None, :], mod_w[l], mod_b[l])
        mod_lat = modulation(c, mod_w[l], mod_b[l])
        xp, k_l, v_l, s_l = run_layer(xp, mod_ctx, lp, None)
        xs = run_layer(xs, mod_lat, lp, (cache_k[:, l], cache_v[:, l], state_ssm[:, l]))
        ks.append(k_l)
        vs.append(v_l)
        ss.append(s_l)
    new_cache_k = jnp.stack(ks, axis=1)
    new_cache_v = jnp.stack(vs, axis=1)
    new_state_ssm = jnp.stack(ss, axis=1)
    return (xp, xs, new_cache_k, new_cache_v, new_state_ssm)
```

```python
import functools
import math

import numpy as np
import jax
import jax.numpy as jnp
from jax import lax
from jax.experimental import pallas as pl
from jax.experimental.pallas import tpu as pltpu

F32 = jnp.float32
BF16 = jnp.bfloat16

GRID_W = 64
HEAD_DIM = 64
N_Q_HEADS = 8
N_KV_HEADS = 2
Q_PER_KV = N_Q_HEADS // N_KV_HEADS
ATTN_WIDTH = N_Q_HEADS * HEAD_DIM
KV_WIDTH = N_KV_HEADS * HEAD_DIM
WINDOW = 128
ROPE_BASE = 10000.0
SSM_WIDTH = 512
SSM_GROUP = 16
SSM_GROUPS = SSM_WIDTH // SSM_GROUP
SSM_STATE = 64
N_EXPERTS = 64
TOP_K = 6
ROUTED_SCALE = 2.5
N_MOD = 6
LN_EPS = 1e-5

SUBLANES = 8
LANES = 128
VMEM_LIMIT = 48 * 1024 * 1024

INPROJ_ROWS = 512
SSM_STEPS = 64
SCAN_STRIP = 512
MERGE_ROWS = 256
DISPATCH_ROWS = 256
GMM_ROWS = 256
COMBINE_ROWS = 128


def _cparams(sem):
    return pltpu.CompilerParams(dimension_semantics=sem, vmem_limit_bytes=VMEM_LIMIT)


def _mod_kernel(c_ref, w_ref, b_ref, o_ref):
    c = c_ref[...]
    s = c * jax.nn.sigmoid(c)
    o_ref[...] = jnp.dot(s, w_ref[...], preferred_element_type=F32,
                         precision=lax.Precision.HIGHEST) + b_ref[...]


def _modulation(cond, w, b):
    n, d = cond.shape
    nout = w.shape[1]
    tn = 512
    return pl.pallas_call(
        _mod_kernel,
        grid=(nout // tn,),
        in_specs=[pl.BlockSpec((n, d), lambda j: (0, 0)),
                  pl.BlockSpec((d, tn), lambda j: (0, j)),
                  pl.BlockSpec((1, tn), lambda j: (0, j))],
        out_specs=pl.BlockSpec((n, tn), lambda j: (0, j)),
        out_shape=jax.ShapeDtypeStruct((n, nout), F32),
        compiler_params=_cparams(("arbitrary",)),
        name="modulation",
    )(cond, w, b.reshape(1, nout))


def _rope_rotate(t, cos, sin):
    lane = lax.broadcasted_iota(jnp.int32, t.shape, 1)
    partner = jnp.where((lane % 32) < 16, pltpu.roll(t, LANES - 16, 1), pltpu.roll(t, 16, 1))
    return t * cos + partner * sin


def _inproj_kernel(x_ref, mod_ref, w_ref, cos_ref, sin_ref, q_ref, k_ref, v_ref, ga_ref, gs_ref, *, rope):
    x = x_ref[...]
    h = (x * (1.0 + mod_ref[1:2, :]) + mod_ref[0:1, :]).astype(BF16)
    p = jnp.dot(h, w_ref[...], preferred_element_type=F32)
    o1 = ATTN_WIDTH
    o2 = o1 + KV_WIDTH
    o3 = o2 + KV_WIDTH
    d = x.shape[1]
    q = p[:, :o1]
    k = p[:, o1:o2]
    if rope:
        cos = cos_ref[...]
        sin = sin_ref[...]
        q = jnp.concatenate([_rope_rotate(q[:, j * LANES:(j + 1) * LANES], cos, sin)
                             for j in range(o1 // LANES)], axis=1)
        k = _rope_rotate(k, cos, sin)
    q_ref[...] = q.astype(BF16)
    k_ref[...] = k
    v_ref[...] = p[:, o2:o3]
    ga_ref[...] = p[:, o3:o3 + d].astype(BF16)
    gs_ref[...] = p[:, o3 + d:].astype(BF16)


def _in_projection(x, mod, w, cos, sin, rope):
    B, L, D = x.shape
    tm = min(INPROJ_ROWS, L)
    per_batch = mod.shape[0] > 1
    nw = w.shape[1]
    row = lambda b, i: (b, i, 0)
    outs = pl.pallas_call(
        functools.partial(_inproj_kernel, rope=rope),
        grid=(B, L // tm),
        in_specs=[pl.BlockSpec((None, tm, D), row),
                  pl.BlockSpec((None, N_MOD, D), (lambda b, i: (b, 0, 0)) if per_batch else (lambda b, i: (0, 0, 0))),
                  pl.BlockSpec((D, nw), lambda b, i: (0, 0)),
                  pl.BlockSpec((tm, LANES), lambda b, i: (i, 0)),
                  pl.BlockSpec((tm, LANES), lambda b, i: (i, 0))],
        out_specs=[pl.BlockSpec((None, tm, ATTN_WIDTH), row),
                   pl.BlockSpec((None, tm, KV_WIDTH), row),
                   pl.BlockSpec((None, tm, KV_WIDTH), row),
                   pl.BlockSpec((None, tm, D), row),
                   pl.BlockSpec((None, tm, D), row)],
        out_shape=[jax.ShapeDtypeStruct((B, L, ATTN_WIDTH), BF16),
                   jax.ShapeDtypeStruct((B, L, KV_WIDTH), F32),
                   jax.ShapeDtypeStruct((B, L, KV_WIDTH), F32),
                   jax.ShapeDtypeStruct((B, L, D), BF16),
                   jax.ShapeDtypeStruct((B, L, D), BF16)],
        compiler_params=_cparams(("parallel", "parallel")),
        name="in_projection",
    )(x, mod, w, cos, sin)
    return outs


def _rope_cos_sin(n_tokens):
    t = jnp.arange(n_tokens, dtype=jnp.int32)
    pos = jnp.stack([t // GRID_W, t % GRID_W], axis=-1).astype(F32)
    n_freq = HEAD_DIM // 4
    inv_freq = ROPE_BASE ** (-jnp.arange(n_freq, dtype=F32) / n_freq)
    ang = pos[:, :, None] * inv_freq
    c, s = jnp.cos(ang), jnp.sin(ang)
    cos = jnp.concatenate([c[:, 0], c[:, 0], c[:, 1], c[:, 1]], axis=-1)
    sin = jnp.concatenate([-s[:, 0], s[:, 0], -s[:, 1], s[:, 1]], axis=-1)
    return jnp.tile(cos, (1, LANES // HEAD_DIM)), jnp.tile(sin, (1, LANES // HEAD_DIM))


def _attend(sink_ref, q, kcat, vcat, mask, o_ref):
    lq = q.shape[0]
    lane = lax.broadcasted_iota(jnp.int32, (1, LANES), 1)
    low = lane < HEAD_DIM
    k_sw = pltpu.roll(kcat, HEAD_DIM, 1)
    v_sw = pltpu.roll(vcat, HEAD_DIM, 1)
    neg = jnp.finfo(F32).min
    for h in range(N_KV_HEADS):
        keep = low if h == 0 else jnp.logical_not(low)
        kd = jnp.where(keep, kcat, k_sw).astype(BF16)
        vd = jnp.where(keep, vcat, v_sw).astype(BF16)
        qs = []
        for j in range(Q_PER_KV):
            head = h * Q_PER_KV + j
            blk = q[:, (head // 2) * LANES:(head // 2 + 1) * LANES]
            sel = low if head % 2 == 0 else jnp.logical_not(low)
            qs.append(jnp.where(sel, blk, jnp.zeros_like(blk)))
        qstack = jnp.concatenate(qs, axis=0)
        s = lax.dot_general(qstack, kd, (((1,), (1,)), ((), ())), preferred_element_type=F32)
        ps, ls = [], []
        for j in range(Q_PER_KV):
            sj = s[j * lq:(j + 1) * lq]
            if mask is not None:
                sj = jnp.where(mask, sj, neg)
            sink = sink_ref[h * Q_PER_KV + j]
            m = jnp.maximum(jnp.max(sj, axis=1, keepdims=True), sink)
            pj = jnp.exp(sj - m)
            ls.append(jnp.sum(pj, axis=1, keepdims=True) + jnp.exp(sink - m))
            ps.append(pj.astype(BF16))
        o = jnp.dot(jnp.concatenate(ps, axis=0), vd, preferred_element_type=F32)
        for jj in range(Q_PER_KV // 2):
            oe = o[(2 * jj) * lq:(2 * jj + 1) * lq] / ls[2 * jj]
            oo = o[(2 * jj + 1) * lq:(2 * jj + 2) * lq] / ls[2 * jj + 1]
            cb = (h * Q_PER_KV) // 2 + jj
            o_ref[:, cb * LANES:(cb + 1) * LANES] = jnp.where(low, oe, oo).astype(o_ref.dtype)


def _attn_ctx_kernel(sink_ref, q_ref, k_ref, v_ref, o_ref):
    _attend(sink_ref, q_ref[...], k_ref[...], v_ref[...], None, o_ref)


def _attn_lat_kernel(sink_ref, q_ref, kp_ref, kc_ref, kn_ref, vp_ref, vc_ref, vn_ref, kx_ref, vx_ref, o_ref):
    n = pl.program_id(1)
    nb = pl.num_programs(1)
    blk = q_ref.shape[0]
    nctx = kx_ref.shape[0]
    kcat = jnp.concatenate([kp_ref[...], kc_ref[...], kn_ref[...], kx_ref[...]], axis=0)
    vcat = jnp.concatenate([vp_ref[...], vc_ref[...], vn_ref[...], vx_ref[...]], axis=0)
    lk = 3 * blk + nctx
    qi = lax.broadcasted_iota(jnp.int32, (blk, lk), 0)
    kj = lax.broadcasted_iota(jnp.int32, (blk, lk), 1)
    rel = kj - blk - qi
    in_band = (rel <= WINDOW) & (rel >= -WINDOW)
    in_seq = ((kj >= blk) | (n > 0)) & ((kj < 2 * blk) | (n < nb - 1))
    mask = (kj >= 3 * blk) | (in_band & in_seq)
    _attend(sink_ref, q_ref[...], kcat, vcat, mask, o_ref)


def _attention_context(q, k, v, sink):
    B, L, _ = q.shape
    row = lambda b: (b, 0, 0)
    return pl.pallas_call(
        _attn_ctx_kernel,
        grid=(B,),
        in_specs=[pl.BlockSpec(memory_space=pltpu.SMEM),
                  pl.BlockSpec((None, L, ATTN_WIDTH), row),
                  pl.BlockSpec((None, L, KV_WIDTH), row),
                  pl.BlockSpec((None, L, KV_WIDTH), row)],
        out_specs=pl.BlockSpec((None, L, ATTN_WIDTH), row),
        out_shape=jax.ShapeDtypeStruct((B, L, ATTN_WIDTH), BF16),
        compiler_params=_cparams(("parallel",)),
        name="attention_context",
    )(sink, q, k, v)


def _attention_latent(q, k, v, k_ctx, v_ctx, sink):
    B, S, _ = q.shape
    blk = WINDOW
    nb = S // blk
    nctx = k_ctx.shape[1]
    cur = lambda b, n: (b, n, 0)
    prv = lambda b, n: (b, jnp.maximum(n - 1, 0), 0)
    nxt = lambda b, n: (b, jnp.minimum(n + 1, nb - 1), 0)
    ctx = lambda b, n: (b, 0, 0)
    kv = lambda im: pl.BlockSpec((None, blk, KV_WIDTH), im)
    return pl.pallas_call(
        _attn_lat_kernel,
        grid=(B, nb),
        in_specs=[pl.BlockSpec(memory_space=pltpu.SMEM),
                  pl.BlockSpec((None, blk, ATTN_WIDTH), cur),
                  kv(prv), kv(cur), kv(nxt), kv(prv), kv(cur), kv(nxt),
                  pl.BlockSpec((None, nctx, KV_WIDTH), ctx),
                  pl.BlockSpec((None, nctx, KV_WIDTH), ctx)],
        out_specs=pl.BlockSpec((None, blk, ATTN_WIDTH), cur),
        out_shape=jax.ShapeDtypeStruct((B, S, ATTN_WIDTH), BF16),
        compiler_params=_cparams(("parallel", "parallel")),
        name="attention_latent",
    )(sink, q, k, k, k, v, v, v, k_ctx, v_ctx)


def _ssm_kernel(x_ref, mod_ref, wu_ref, perm_ref, permt_ref, wb_ref, wc_ref, a_ref, d_ref, h0_ref,
                y_ref, fs_ref, bu_ref, st_ref, *, per_batch):
    i = pl.program_id(2)
    nc = pl.num_programs(2)
    nseq, tm, dm = x_ref.shape
    rows = nseq * tm
    half = a_ref.shape[-1] // 2
    nre = half // 2

    @pl.when(i == 0)
    def _():
        st_ref[...] = h0_ref[...]

    hs = []
    for b in range(nseq):
        mb = b if per_batch else 0
        hs.append((x_ref[b] * (1.0 + mod_ref[mb, 1:2, :]) + mod_ref[mb, 0:1, :]).astype(BF16))
    h = jnp.concatenate(hs, axis=0)
    u = jnp.dot(h, wu_ref[...], preferred_element_type=F32)
    ut = jnp.dot(perm_ref[...], u.astype(BF16), preferred_element_type=F32).astype(BF16)
    for kc in range(2):
        bu_ref[:, kc * half:(kc + 1) * half] = jnp.dot(
            ut[:, kc * (SSM_WIDTH // 2):(kc + 1) * (SSM_WIDTH // 2)], wb_ref[kc],
            preferred_element_type=F32)

    for kc in range(2):
        for s0 in range(0, nre, SCAN_STRIP):
            re = slice(kc * half + s0, kc * half + s0 + SCAN_STRIP)
            im = slice(kc * half + nre + s0, kc * half + nre + s0 + SCAN_STRIP)
            a_re = a_ref[:, re]
            a_im = a_ref[:, im]

            def step(t, carry):
                xr, xi = carry
                r = pl.multiple_of(t * nseq, nseq)
                nr = a_re * xr - a_im * xi + bu_ref[pl.ds(r, nseq), re]
                ni = a_re * xi + a_im * xr + bu_ref[pl.ds(r, nseq), im]
                bu_ref[pl.ds(r, nseq), re] = nr
                bu_ref[pl.ds(r, nseq), im] = ni
                return nr, ni

            xr, xi = lax.fori_loop(0, tm, step, (st_ref[:, re], st_ref[:, im]), unroll=4)
            st_ref[:, re] = xr
            st_ref[:, im] = xi

    yt = jnp.concatenate(
        [jnp.dot(bu_ref[:, kc * half:(kc + 1) * half].astype(BF16), wc_ref[kc], preferred_element_type=F32)
         for kc in range(2)], axis=1)
    y_hi = yt.astype(BF16)
    y_lo = (yt - y_hi.astype(F32)).astype(BF16)
    y = (jnp.dot(permt_ref[...], y_hi, preferred_element_type=F32)
         + jnp.dot(permt_ref[...], y_lo, preferred_element_type=F32))
    y = y + u * d_ref[...]
    for b in range(nseq):
        y_ref[b] = y[b * tm:(b + 1) * tm]

    @pl.when(i == nc - 1)
    def _():
        fs_ref[...] = st_ref[...]


def _ssm_perm(nseq, tm):
    rows = nseq * tm
    p = np.zeros((2, rows, rows), np.float32)
    for d in range(2):
        for b in range(nseq):
            for t in range(tm):
                tt = t if d == 0 else tm - 1 - t
                p[d, tt * nseq + b, b * tm + t] = 1.0
    return p


def _ssm_mix(x, mod, wu, ssm, h0):
    B, L, D = x.shape
    nseq = SUBLANES
    tm = min(SSM_STEPS, L)
    nc = L // tm
    rows = nseq * tm
    per_batch = mod.shape[0] > 1
    ns = 2 * SSM_GROUPS * SSM_STATE
    perm = _ssm_perm(nseq, tm)
    perm_f = jnp.asarray(perm, BF16)
    perm_t = jnp.asarray(np.transpose(perm, (0, 2, 1)), BF16)
    chunk = lambda d, g, i: i + d * (nc - 1 - 2 * i)
    if per_batch:
        mod_spec = pl.BlockSpec((nseq, N_MOD, D), lambda d, g, i: (g, 0, 0))
    else:
        mod_spec = pl.BlockSpec((1, N_MOD, D), lambda d, g, i: (0, 0, 0))
    y, fs = pl.pallas_call(
        functools.partial(_ssm_kernel, per_batch=per_batch),
        grid=(2, B // nseq, nc),
        in_specs=[pl.BlockSpec((nseq, tm, D), lambda d, g, i: (g, chunk(d, g, i), 0)),
                  mod_spec,
                  pl.BlockSpec((D, SSM_WIDTH), lambda d, g, i: (0, 0)),
                  pl.BlockSpec((None, rows, rows), lambda d, g, i: (d, 0, 0)),
                  pl.BlockSpec((None, rows, rows), lambda d, g, i: (d, 0, 0)),
                  pl.BlockSpec((None, 2, SSM_WIDTH // 2, ns // 2), lambda d, g, i: (d, 0, 0, 0)),
                  pl.BlockSpec((None, 2, ns // 2, SSM_WIDTH // 2), lambda d, g, i: (d, 0, 0, 0)),
                  pl.BlockSpec((None, nseq, ns), lambda d, g, i: (d, 0, 0)),
                  pl.BlockSpec((None, 1, SSM_WIDTH), lambda d, g, i: (d, 0, 0)),
                  pl.BlockSpec((None, nseq, ns), lambda d, g, i: (d, g, 0))],
        out_specs=[pl.BlockSpec((None, nseq, tm, SSM_WIDTH), lambda d, g, i: (d, g, chunk(d, g, i), 0)),
                   pl.BlockSpec((None, nseq, ns), lambda d, g, i: (d, g, 0))],
        out_shape=[jax.ShapeDtypeStruct((2, B, L, SSM_WIDTH), F32),
                   jax.ShapeDtypeStruct((2, B, ns), F32)],
        scratch_shapes=[pltpu.VMEM((rows, ns), F32), pltpu.VMEM((nseq, ns), F32)],
        compiler_params=_cparams(("arbitrary", "arbitrary", "arbitrary")),
        name="ssm_scan",
    )(x, mod, wu, perm_f, perm_t, ssm["wb"], ssm["wc"], ssm["a"], ssm["d"], h0)
    return y, fs


def _state_to_lanes(s):
    lead = s.shape[:-3]
    s = s.reshape(lead + (2, 2, SSM_GROUPS // 2, SSM_STATE))
    s = jnp.swapaxes(s, -4, -3)
    return s.reshape(lead + (2 * SSM_GROUPS * SSM_STATE,))


def _lanes_to_state(v):
    lead = v.shape[:-1]
    s = v.reshape(lead + (2, 2, SSM_GROUPS // 2, SSM_STATE))
    s = jnp.swapaxes(s, -4, -3)
    return s.reshape(lead + (2, SSM_GROUPS, SSM_STATE))


def _ssm_params(a_re, a_im, log_dt, b_re, b_im, c_re, c_im, dvec):
    G, N, C = SSM_GROUPS, SSM_STATE, SSM_GROUP
    lam_re = jnp.minimum(a_re, -1e-4)
    lam_im = a_im
    dt = jnp.exp(log_dt)[..., None]
    mag = jnp.exp(lam_re * dt)
    abar_re, abar_im = mag * jnp.cos(lam_im * dt), mag * jnp.sin(lam_im * dt)
    den = jnp.square(lam_re) + jnp.square(lam_im)
    p, qi = abar_re - 1.0, abar_im
    f_re = (p * lam_re + qi * lam_im) / den
    f_im = (qi * lam_re - p * lam_im) / den
    bbar_re = f_re[..., None] * b_re - f_im[..., None] * b_im
    bbar_im = f_re[..., None] * b_im + f_im[..., None] * b_re
    a = _state_to_lanes(jnp.stack([abar_re, abar_im], axis=1))
    a = jnp.broadcast_to(a[:, None, :], (2, SUBLANES, a.shape[-1]))
    eye = jnp.eye(G // 2, dtype=F32)

    def bd_in(bb):
        bb = bb.reshape(2, 2, G // 2, N, C)
        return jnp.einsum('dkgnc,gh->dkgchn', bb, eye).reshape(2, 2, (G // 2) * C, (G // 2) * N)

    def bd_out(cc):
        cc = cc.reshape(2, 2, G // 2, C, N)
        return jnp.einsum('dkgcn,gh->dkgnhc', cc, eye).reshape(2, 2, (G // 2) * N, (G // 2) * C)

    wb = jnp.concatenate([bd_in(bbar_re), bd_in(bbar_im)], axis=-1).astype(BF16)
    wc = jnp.concatenate([bd_out(c_re), -bd_out(c_im)], axis=-2).astype(BF16)
    d = jnp.stack([dvec, jnp.zeros_like(dvec)], axis=0).reshape(2, 1, SSM_WIDTH)
    return {"a": a, "wb": wb, "wc": wc, "d": d}


def _layer_norm(x, g, b):
    mu = jnp.mean(x, axis=-1, keepdims=True)
    xc = x - mu
    var = jnp.mean(xc * xc, axis=-1, keepdims=True)
    return xc * lax.rsqrt(var + LN_EPS) * g + b


def _gelu_tanh(x):
    return 0.5 * x * (1.0 + jnp.tanh(math.sqrt(2.0 / math.pi) * (x + 0.044715 * (x * x * x))))


def _merge_kernel(x_ref, ao_ref, yf_ref, yb_ref, ga_ref, gs_ref, mod_ref, wglu_ref, wa_ref, ws_ref, wo_ref,
                  lng_ref, lnb_ref, rw_ref, x1_ref, h2_ref, sc_ref, *, alpha):
    z = _gelu_tanh(yf_ref[...] + yb_ref[...])
    gate = jax.nn.sigmoid(jnp.dot(z.astype(BF16), wglu_ref[...], preferred_element_type=F32))
    ssm_o = (z * gate).astype(BF16)
    merged = (jax.nn.sigmoid(ga_ref[...].astype(F32)) * jnp.dot(ao_ref[...], wa_ref[...], preferred_element_type=F32)
              + jax.nn.sigmoid(gs_ref[...].astype(F32)) * jnp.dot(ssm_o, ws_ref[...], preferred_element_type=F32))
    mix = jnp.dot(merged.astype(BF16), wo_ref[...], preferred_element_type=F32)
    x1 = _layer_norm(alpha * x_ref[...] + mod_ref[2:3, :] * mix, lng_ref[...], lnb_ref[...])
    h2 = x1 * (1.0 + mod_ref[4:5, :]) + mod_ref[3:4, :]
    x1_ref[...] = x1
    h2_ref[...] = h2
    sc_ref[...] = jax.nn.sigmoid(jnp.dot(h2, rw_ref[...], preferred_element_type=F32,
                                         precision=lax.Precision.HIGHEST))


def _merge(x, attn_o, y, ga, gs, mod, wts, alpha):
    B, L, D = x.shape
    tm = min(MERGE_ROWS, L)
    per_batch = mod.shape[0] > 1
    row = lambda b, i: (b, i, 0)
    full = lambda b, i: (0, 0)
    wspec = lambda a: pl.BlockSpec(a.shape, full)
    return pl.pallas_call(
        functools.partial(_merge_kernel, alpha=alpha),
        grid=(B, L // tm),
        in_specs=[pl.BlockSpec((None, tm, D), row),
                  pl.BlockSpec((None, tm, ATTN_WIDTH), row),
                  pl.BlockSpec((None, None, tm, SSM_WIDTH), lambda b, i: (0, b, i, 0)),
                  pl.BlockSpec((None, None, tm, SSM_WIDTH), lambda b, i: (1, b, i, 0)),
                  pl.BlockSpec((None, tm, D), row),
                  pl.BlockSpec((None, tm, D), row),
                  pl.BlockSpec((None, N_MOD, D), (lambda b, i: (b, 0, 0)) if per_batch else (lambda b, i: (0, 0, 0))),
                  wspec(wts["w_glu"]), wspec(wts["w_attn_br"]), wspec(wts["w_ssm_br"]), wspec(wts["w_out"]),
                  wspec(wts["ln1_g"]), wspec(wts["ln1_b"]), wspec(wts["router_w"])],
        out_specs=[pl.BlockSpec((None, tm, D), row),
                   pl.BlockSpec((None, tm, D), row),
                   pl.BlockSpec((None, tm, N_EXPERTS), row)],
        out_shape=[jax.ShapeDtypeStruct((B, L, D), F32),
                   jax.ShapeDtypeStruct((B, L, D), F32),
                   jax.ShapeDtypeStruct((B, L, N_EXPERTS), F32)],
        compiler_params=_cparams(("parallel", "parallel")),
        name="merge_ln1_router",
    )(x, attn_o, y, y, ga, gs, mod, wts["w_glu"], wts["w_attn_br"], wts["w_ssm_br"], wts["w_out"],
      wts["ln1_g"], wts["ln1_b"], wts["router_w"])


def _dispatch_kernel(dest_ref, h_ref, xs_ref, sem):
    n = h_ref.shape[0] * TOP_K

    def row_copy(a):
        r = a // TOP_K
        return pltpu.make_async_copy(h_ref.at[pl.ds(r, 1)], xs_ref.at[pl.ds(dest_ref[0, a], 1)], sem)

    def start(a, c):
        row_copy(a).start()
        return c

    def wait(a, c):
        row_copy(a).wait()
        return c

    lax.fori_loop(0, n, start, 0)
    lax.fori_loop(0, n, wait, 0)


def _dispatch(h2, dest):
    T, D = h2.shape
    tm = DISPATCH_ROWS
    nt = T // tm
    return pl.pallas_call(
        _dispatch_kernel,
        grid=(nt,),
        in_specs=[pl.BlockSpec((None, 1, tm * TOP_K), lambda i: (i, 0, 0), memory_space=pltpu.SMEM),
                  pl.BlockSpec((tm, D), lambda i: (i, 0))],
        out_specs=pl.BlockSpec(memory_space=pl.ANY),
        out_shape=jax.ShapeDtypeStruct((T * TOP_K, D), h2.dtype),
        scratch_shapes=[pltpu.SemaphoreType.DMA],
        compiler_params=_cparams(("arbitrary",)),
        name="moe_dispatch",
    )(dest.reshape(nt, 1, tm * TOP_K), h2)


def _gmm_kernel(tile_ref, exp_ref, valid_ref, gstart_ref, xs_ref, wg_ref, wu_ref, wd_ref, ys_ref,
                wgb_ref, wub_ref, wdb_ref):
    w = pl.program_id(0)
    e = exp_ref[w]
    t = tile_ref[w]
    prev = jnp.maximum(w - 1, 0)
    new_expert = (w == 0) | (e != exp_ref[prev])
    first_visit = (w == 0) | (t != tile_ref[prev])

    @pl.when(new_expert)
    def _():
        wgb_ref[...] = wg_ref[...].astype(BF16)
        wub_ref[...] = wu_ref[...].astype(BF16)
        wdb_ref[...] = wd_ref[...].astype(BF16)

    @pl.when(valid_ref[w] == 1)
    def _():
        x = xs_ref[...].astype(BF16)
        g = jnp.dot(x, wgb_ref[...], preferred_element_type=F32)
        u = jnp.dot(x, wub_ref[...], preferred_element_type=F32)
        a = (g * jax.nn.sigmoid(g) * u).astype(BF16)
        y = jnp.dot(a, wdb_ref[...], preferred_element_type=F32)
        tg = y.shape[0]
        rows = t * tg + lax.broadcasted_iota(jnp.int32, (tg, 1), 0)
        mine = (rows >= gstart_ref[e]) & (rows < gstart_ref[e + 1])

        @pl.when(first_visit)
        def _():
            ys_ref[...] = jnp.where(mine, y, 0.0)

        @pl.when(jnp.logical_not(first_visit))
        def _():
            ys_ref[...] = jnp.where(mine, y, ys_ref[...])


def _grouped_experts(xs, counts, w_gate, w_up, w_down):
    A, D = xs.shape
    tg = GMM_ROWS
    nt = A // tg
    n_items = nt + N_EXPERTS - 1
    ff = w_gate.shape[-1]
    gend = jnp.cumsum(counts).astype(jnp.int32)
    gstart = jnp.concatenate([jnp.zeros((1,), jnp.int32), gend])
    first_row = jnp.arange(nt, dtype=jnp.int32) * tg
    e_lo = jnp.searchsorted(gend, first_row, side='right').astype(jnp.int32)
    e_hi = jnp.searchsorted(gend, first_row + tg - 1, side='right').astype(jnp.int32)
    per_tile = e_hi - e_lo + 1
    item_end = jnp.cumsum(per_tile).astype(jnp.int32)
    total = item_end[-1]
    wi = jnp.arange(n_items, dtype=jnp.int32)
    tile = jnp.minimum(jnp.searchsorted(item_end, wi, side='right').astype(jnp.int32), nt - 1)
    expert = e_lo[tile] + wi - (item_end[tile] - per_tile[tile])
    valid = (wi < total).astype(jnp.int32)
    expert = jnp.where(valid == 1, expert, e_hi[nt - 1])
    grid_spec = pltpu.PrefetchScalarGridSpec(
        num_scalar_prefetch=4,
        grid=(n_items,),
        in_specs=[pl.BlockSpec((tg, D), lambda w, tl, ex, va, gs: (tl[w], 0)),
                  pl.BlockSpec((None, D, ff), lambda w, tl, ex, va, gs: (ex[w], 0, 0)),
                  pl.BlockSpec((None, D, ff), lambda w, tl, ex, va, gs: (ex[w], 0, 0)),
                  pl.BlockSpec((None, ff, D), lambda w, tl, ex, va, gs: (ex[w], 0, 0))],
        out_specs=pl.BlockSpec((tg, D), lambda w, tl, ex, va, gs: (tl[w], 0)),
        scratch_shapes=[pltpu.VMEM((D, ff), BF16), pltpu.VMEM((D, ff), BF16), pltpu.VMEM((ff, D), BF16)],
    )
    return pl.pallas_call(
        _gmm_kernel,
        grid_spec=grid_spec,
        out_shape=jax.ShapeDtypeStruct((A, D), F32),
        compiler_params=_cparams(("arbitrary",)),
        name="moe_grouped_experts",
    )(tile, expert, valid, gstart, xs, w_gate, w_up, w_down)


def _combine_kernel(dest_ref, wt_ref, x1_ref, h2_ref, mod_ref, sg_ref, su_ref, sd_ref, lng_ref, lnb_ref, ys_ref,
                    o_ref, buf_ref, sem, *, alpha):
    tm = x1_ref.shape[0]
    n = tm * TOP_K

    def row_copy(a):
        r = a // TOP_K
        k = a - r * TOP_K
        return pltpu.make_async_copy(ys_ref.at[pl.ds(dest_ref[0, a], 1)], buf_ref.at[k, pl.ds(r, 1)], sem)

    def start(a, c):
        row_copy(a).start()
        return c

    def wait(a, c):
        row_copy(a).wait()
        return c

    lax.fori_loop(0, n, start, 0)
    h = h2_ref[...].astype(BF16)
    g = jnp.dot(h, sg_ref[...], preferred_element_type=F32)
    u = jnp.dot(h, su_ref[...], preferred_element_type=F32)
    moe = jnp.dot((g * jax.nn.sigmoid(g) * u).astype(BF16), sd_ref[...], preferred_element_type=F32)
    lax.fori_loop(0, n, wait, 0)
    wt = wt_ref[...]
    for k in range(TOP_K):
        moe = moe + wt[:, k:k + 1] * buf_ref[k]
    o_ref[...] = _layer_norm(alpha * x1_ref[...] + mod_ref[5:6, :] * moe, lng_ref[...], lnb_ref[...])


def _combine(ys, dest, wts, x1, h2, mod, shared, alpha):
    B, L, D = x1.shape
    tm = min(COMBINE_ROWS, L)
    nl = L // tm
    per_batch = mod.shape[0] > 1
    row = lambda b, i: (b, i, 0)
    full = lambda b, i: (0, 0)
    wspec = lambda a: pl.BlockSpec(a.shape, full)
    return pl.pallas_call(
        functools.partial(_combine_kernel, alpha=alpha),
        grid=(B, nl),
        in_specs=[pl.BlockSpec((None, 1, tm * TOP_K), lambda b, i: (b * nl + i, 0, 0), memory_space=pltpu.SMEM),
                  pl.BlockSpec((None, tm, TOP_K), row),
                  pl.BlockSpec((None, tm, D), row),
                  pl.BlockSpec((None, tm, D), row),
                  pl.BlockSpec((None, N_MOD, D), (lambda b, i: (b, 0, 0)) if per_batch else (lambda b, i: (0, 0, 0))),
                  wspec(shared["sh_w_gate"]), wspec(shared["sh_w_up"]), wspec(shared["sh_w_down"]),
                  wspec(shared["ln2_g"]), wspec(shared["ln2_b"]),
                  pl.BlockSpec(memory_space=pl.ANY)],
        out_specs=pl.BlockSpec((None, tm, D), row),
        out_shape=jax.ShapeDtypeStruct((B, L, D), F32),
        scratch_shapes=[pltpu.VMEM((TOP_K, tm, D), F32), pltpu.SemaphoreType.DMA],
        compiler_params=_cparams(("arbitrary", "arbitrary")),
        name="moe_combine_ln2",
    )(dest.reshape(B * nl, 1, tm * TOP_K), wts, x1, h2, mod, shared["sh_w_gate"], shared["sh_w_up"],
      shared["sh_w_down"], shared["ln2_g"], shared["ln2_b"], ys)


def _route(scores, router_b):
    T = scores.shape[0]
    _, idx = lax.top_k(scores + router_b, TOP_K)
    s_sel = jnp.take_along_axis(scores, idx, axis=-1)
    wts = s_sel / jnp.sum(s_sel, -1, keepdims=True) * ROUTED_SCALE
    flat_e = idx.reshape(-1)
    onehot = (flat_e[:, None] == jnp.arange(N_EXPERTS, dtype=flat_e.dtype)[None, :]).astype(jnp.int32)
    csum = jnp.cumsum(onehot, axis=0)
    rank = jnp.take_along_axis(csum, flat_e[:, None], axis=1)[:, 0] - 1
    counts = csum[-1]
    start = jnp.cumsum(counts) - counts
    dest = (start[flat_e] + rank).astype(jnp.int32)
    return dest, wts, counts


def _run_group(x, mod, wts, ssm, cache, alpha):
    B, L, D = x.shape
    latent = cache is not None
    if latent:
        k_ctx, v_ctx, h0 = cache
        cos, sin = _rope_cos_sin(L)
    else:
        cos = jnp.zeros((L, LANES), F32)
        sin = cos
        h0 = jnp.zeros((2, B, 2 * SSM_GROUPS * SSM_STATE), F32)
    q, k, v, ga, gs = _in_projection(x, mod, wts["w_qkvg"], cos, sin, rope=latent)
    if latent:
        attn_o = _attention_latent(q, k, v, k_ctx, v_ctx, wts["attn_sink"])
    else:
        attn_o = _attention_context(q, k, v, wts["attn_sink"])
    y, fs = _ssm_mix(x, mod, wts["w_u"], ssm, h0)
    x1, h2, scores = _merge(x, attn_o, y, ga, gs, mod, wts, alpha)
    T = B * L
    dest, rw, counts = _route(scores.reshape(T, N_EXPERTS), wts["router_b"])
    xs = _dispatch(h2.reshape(T, D), dest)
    ys = _grouped_experts(xs, counts, wts["exp_w_gate"], wts["exp_w_up"], wts["exp_w_down"])
    out = _combine(ys, dest, rw.reshape(B, L, TOP_K), x1, h2, mod, wts, alpha)
    return out, k, v, fs


def kernel(x_prompt, x_sample, cache_k, cache_v, state_ssm, c, c_ctx, mod_w, mod_b, w_in, attn_sink, w_attn_br, ssm_a_re, ssm_a_im, ssm_log_dt, ssm_b_re, ssm_b_im, ssm_c_re, ssm_c_im, ssm_d, w_glu, w_ssm_br, w_out, ln1_g, ln1_b, ln2_g, ln2_b, router_w, router_b, exp_w_gate, exp_w_up, exp_w_down, sh_w_gate, sh_w_up, sh_w_down):
    depth = w_in.shape[0]
    assert depth == 1
    alpha = (2.0 * depth) ** 0.25
    D = x_prompt.shape[-1]
    nb_p = x_prompt.shape[0]
    nb_s = x_sample.shape[0]
    l = 0

    ncond = 1 + nb_s
    npad = -ncond % SUBLANES
    cond = jnp.concatenate([c_ctx[None, :], c, jnp.zeros((npad, D), F32)], axis=0)
    mod = _modulation(cond, mod_w[l], mod_b[l]).reshape(ncond + npad, N_MOD, D)
    mod_ctx, mod_lat = mod[0:1], mod[1:ncond]

    o1 = ATTN_WIDTH
    o3 = o1 + 2 * KV_WIDTH
    o4 = o3 + SSM_WIDTH
    wi = w_in[l]
    scale = HEAD_DIM ** -0.5
    wts = {
        "w_qkvg": jnp.concatenate([wi[:, :o1] * scale, wi[:, o1:o3], wi[:, o4:]], axis=1).astype(BF16),
        "w_u": wi[:, o3:o4].astype(BF16),
        "attn_sink": attn_sink[l],
        "w_glu": w_glu[l].astype(BF16), "w_attn_br": w_attn_br[l].astype(BF16),
        "w_ssm_br": w_ssm_br[l].astype(BF16), "w_out": w_out[l].astype(BF16),
        "ln1_g": ln1_g[l].reshape(1, D), "ln1_b": ln1_b[l].reshape(1, D),
        "ln2_g": ln2_g[l].reshape(1, D), "ln2_b": ln2_b[l].reshape(1, D),
        "router_w": router_w[l], "router_b": router_b[l],
        "exp_w_gate": exp_w_gate[l], "exp_w_up": exp_w_up[l], "exp_w_down": exp_w_down[l],
        "sh_w_gate": sh_w_gate[l].astype(BF16), "sh_w_up": sh_w_up[l].astype(BF16),
        "sh_w_down": sh_w_down[l].astype(BF16),
    }
    ssm = _ssm_params(ssm_a_re[l], ssm_a_im[l], ssm_log_dt[l], ssm_b_re[l], ssm_b_im[l],
                      ssm_c_re[l], ssm_c_im[l], ssm_d[l])

    yp, k_p, v_p, fs_p = _run_group(x_prompt, mod_ctx, wts, ssm, None, alpha)

    past = cache_k.shape[2]
    k_ctx = cache_k[:, l].reshape(nb_s, past, KV_WIDTH)
    v_ctx = cache_v[:, l].reshape(nb_s, past, KV_WIDTH)
    h0 = jnp.swapaxes(_state_to_lanes(state_ssm[:, l]), 0, 1)
    ys_, _, _, _ = _run_group(x_sample, mod_lat, wts, ssm, (k_ctx, v_ctx, h0), alpha)

    Lp = x_prompt.shape[1]
    new_k = k_p.reshape(nb_p, 1, Lp, N_KV_HEADS, HEAD_DIM)
    new_v = v_p.reshape(nb_p, 1, Lp, N_KV_HEADS, HEAD_DIM)
    new_s = _lanes_to_state(jnp.swapaxes(fs_p, 0, 1))[:, None]
    return (yp, ys_, new_k, new_v, new_s)
```

```python
import functools
import math

import numpy as np
import jax
import jax.numpy as jnp
from jax import lax
from jax.experimental import pallas as pl
from jax.experimental.pallas import tpu as pltpu

F32 = jnp.float32
BF16 = jnp.bfloat16

GRID_W = 64
HEAD_DIM = 64
N_Q_HEADS = 8
N_KV_HEADS = 2
Q_PER_KV = N_Q_HEADS // N_KV_HEADS
ATTN_WIDTH = N_Q_HEADS * HEAD_DIM
KV_WIDTH = N_KV_HEADS * HEAD_DIM
WINDOW = 128
ROPE_BASE = 10000.0
SSM_WIDTH = 512
SSM_GROUP = 16
SSM_GROUPS = SSM_WIDTH // SSM_GROUP
SSM_STATE = 64
N_EXPERTS = 64
TOP_K = 6
ROUTED_SCALE = 2.5
N_MOD = 6
LN_EPS = 1e-5

SUBLANES = 8
LANES = 128
VMEM_LIMIT = 48 * 1024 * 1024

INPROJ_ROWS = 512
SSM_STEPS = 64
SCAN_STRIP = 512
MERGE_ROWS = 256
DISPATCH_ROWS = 256
GMM_ROWS = 256
COMBINE_ROWS = 128


def _cparams(sem):
    return pltpu.CompilerParams(dimension_semantics=sem, vmem_limit_bytes=VMEM_LIMIT)


def _mod_kernel(c_ref, w_ref, b_ref, o_ref):
    c = c_ref[...]
    s = c * jax.nn.sigmoid(c)
    o_ref[...] = jnp.dot(s, w_ref[...], preferred_element_type=F32,
                         precision=lax.Precision.HIGHEST) + b_ref[...]


def _modulation(cond, w, b):
    n, d = cond.shape
    nout = w.shape[1]
    tn = 512
    return pl.pallas_call(
        _mod_kernel,
        grid=(nout // tn,),
        in_specs=[pl.BlockSpec((n, d), lambda j: (0, 0)),
                  pl.BlockSpec((d, tn), lambda j: (0, j)),
                  pl.BlockSpec((1, tn), lambda j: (0, j))],
        out_specs=pl.BlockSpec((n, tn), lambda j: (0, j)),
        out_shape=jax.ShapeDtypeStruct((n, nout), F32),
        compiler_params=_cparams(("arbitrary",)),
        name="modulation",
    )(cond, w, b.reshape(1, nout))


def _rope_rotate(t, cos, sin):
    lane = lax.broadcasted_iota(jnp.int32, t.shape, 1)
    partner = jnp.where((lane % 32) < 16, pltpu.roll(t, LANES - 16, 1), pltpu.roll(t, 16, 1))
    return t * cos + partner * sin


def _inproj_kernel(x_ref, mod_ref, w_ref, cos_ref, sin_ref, q_ref, k_ref, v_ref, ga_ref, gs_ref, *, rope):
    x = x_ref[...]
    h = (x * (1.0 + mod_ref[1:2, :]) + mod_ref[0:1, :]).astype(BF16)
    p = jnp.dot(h, w_ref[...], preferred_element_type=F32)
    o1 = ATTN_WIDTH
    o2 = o1 + KV_WIDTH
    o3 = o2 + KV_WIDTH
    d = x.shape[1]
    q = p[:, :o1]
    k = p[:, o1:o2]
    if rope:
        cos = cos_ref[...]
        sin = sin_ref[...]
        q = jnp.concatenate([_rope_rotate(q[:, j * LANES:(j + 1) * LANES], cos, sin)
                             for j in range(o1 // LANES)], axis=1)
        k = _rope_rotate(k, cos, sin)
    q_ref[...] = q.astype(BF16)
    k_ref[...] = k
    v_ref[...] = p[:, o2:o3]
    ga_ref[...] = p[:, o3:o3 + d].astype(BF16)
    gs_ref[...] = p[:, o3 + d:].astype(BF16)


def _in_projection(x, mod, w, cos, sin, rope):
    B, L, D = x.shape
    tm = min(INPROJ_ROWS, L)
    per_batch = mod.shape[0] > 1
    nw = w.shape[1]
    row = lambda b, i: (b, i, 0)
    outs = pl.pallas_call(
        functools.partial(_inproj_kernel, rope=rope),
        grid=(B, L // tm),
        in_specs=[pl.BlockSpec((None, tm, D), row),
                  pl.BlockSpec((None, N_MOD, D), (lambda b, i: (b, 0, 0)) if per_batch else (lambda b, i: (0, 0, 0))),
                  pl.BlockSpec((D, nw), lambda b, i: (0, 0)),
                  pl.BlockSpec((tm, LANES), lambda b, i: (i, 0)),
                  pl.BlockSpec((tm, LANES), lambda b, i: (i, 0))],
        out_specs=[pl.BlockSpec((None, tm, ATTN_WIDTH), row),
                   pl.BlockSpec((None, tm, KV_WIDTH), row),
                   pl.BlockSpec((None, tm, KV_WIDTH), row),
                   pl.BlockSpec((None, tm, D), row),
                   pl.BlockSpec((None, tm, D), row)],
        out_shape=[jax.ShapeDtypeStruct((B, L, ATTN_WIDTH), BF16),
                   jax.ShapeDtypeStruct((B, L, KV_WIDTH), F32),
                   jax.ShapeDtypeStruct((B, L, KV_WIDTH), F32),
                   jax.ShapeDtypeStruct((B, L, D), BF16),
                   jax.ShapeDtypeStruct((B, L, D), BF16)],
        compiler_params=_cparams(("parallel", "parallel")),
        name="in_projection",
    )(x, mod, w, cos, sin)
    return outs


def _rope_cos_sin(n_tokens):
    t = jnp.arange(n_tokens, dtype=jnp.int32)
    pos = jnp.stack([t // GRID_W, t % GRID_W], axis=-1).astype(F32)
    n_freq = HEAD_DIM // 4
    inv_freq = ROPE_BASE ** (-jnp.arange(n_freq, dtype=F32) / n_freq)
    ang = pos[:, :, None] * inv_freq
    c, s = jnp.cos(ang), jnp.sin(ang)
    cos = jnp.concatenate([c[:, 0], c[:, 0], c[:, 1], c[:, 1]], axis=-1)
    sin = jnp.concatenate([-s[:, 0], s[:, 0], -s[:, 1], s[:, 1]], axis=-1)
    return jnp.tile(cos, (1, LANES // HEAD_DIM)), jnp.tile(sin, (1, LANES // HEAD_DIM))


def _attend(sink_ref, q, kcat, vcat, mask, o_ref):
    lq = q.shape[0]
    lane = lax.broadcasted_iota(jnp.int32, (1, LANES), 1)
    low = lane < HEAD_DIM
    k_sw = pltpu.roll(kcat, HEAD_DIM, 1)
    v_sw = pltpu.roll(vcat, HEAD_DIM, 1)
    neg = jnp.finfo(F32).min
    for h in range(N_KV_HEADS):
        keep = low if h == 0 else jnp.logical_not(low)
        kd = jnp.where(keep, kcat, k_sw).astype(BF16)
        vd = jnp.where(keep, vcat, v_sw).astype(BF16)
        qs = []
        for j in range(Q_PER_KV):
            head = h * Q_PER_KV + j
            blk = q[:, (head // 2) * LANES:(head // 2 + 1) * LANES]
            sel = low if head % 2 == 0 else jnp.logical_not(low)
            qs.append(jnp.where(sel, blk, jnp.zeros_like(blk)))
        qstack = jnp.concatenate(qs, axis=0)
        s = lax.dot_general(qstack, kd, (((1,), (1,)), ((), ())), preferred_element_type=F32)
        ps, ls = [], []
        for j in range(Q_PER_KV):
            sj = s[j * lq:(j + 1) * lq]
            if mask is not None:
                sj = jnp.where(mask, sj, neg)
            sink = sink_ref[h * Q_PER_KV + j]
            m = jnp.maximum(jnp.max(sj, axis=1, keepdims=True), sink)
            pj = jnp.exp(sj - m)
            ls.append(jnp.sum(pj, axis=1, keepdims=True) + jnp.exp(sink - m))
            ps.append(pj.astype(BF16))
        o = jnp.dot(jnp.concatenate(ps, axis=0), vd, preferred_element_type=F32)
        for jj in range(Q_PER_KV // 2):
            oe = o[(2 * jj) * lq:(2 * jj + 1) * lq] / ls[2 * jj]
            oo = o[(2 * jj + 1) * lq:(2 * jj + 2) * lq] / ls[2 * jj + 1]
            cb = (h * Q_PER_KV) // 2 + jj
            o_ref[:, cb * LANES:(cb + 1) * LANES] = jnp.where(low, oe, oo).astype(o_ref.dtype)


def _attn_ctx_kernel(sink_ref, q_ref, k_ref, v_ref, o_ref):
    _attend(sink_ref, q_ref[...], k_ref[...], v_ref[...], None, o_ref)


def _attn_lat_kernel(sink_ref, q_ref, kp_ref, kc_ref, kn_ref, vp_ref, vc_ref, vn_ref, kx_ref, vx_ref, o_ref):
    n = pl.program_id(1)
    nb = pl.num_programs(1)
    blk = q_ref.shape[0]
    nctx = kx_ref.shape[0]
    kcat = jnp.concatenate([kp_ref[...], kc_ref[...], kn_ref[...], kx_ref[...]], axis=0)
    vcat = jnp.concatenate([vp_ref[...], vc_ref[...], vn_ref[...], vx_ref[...]], axis=0)
    lk = 3 * blk + nctx
    qi = lax.broadcasted_iota(jnp.int32, (blk, lk), 0)
    kj = lax.broadcasted_iota(jnp.int32, (blk, lk), 1)
    rel = kj - blk - qi
    in_band = (rel <= WINDOW) & (rel >= -WINDOW)
    in_seq = ((kj >= blk) | (n > 0)) & ((kj < 2 * blk) | (n < nb - 1))
    mask = (kj >= 3 * blk) | (in_band & in_seq)
    _attend(sink_ref, q_ref[...], kcat, vcat, mask, o_ref)


def _attention_context(q, k, v, sink):
    B, L, _ = q.shape
    row = lambda b: (b, 0, 0)
    return pl.pallas_call(
        _attn_ctx_kernel,
        grid=(B,),
        in_specs=[pl.BlockSpec(memory_space=pltpu.SMEM),
                  pl.BlockSpec((None, L, ATTN_WIDTH), row),
                  pl.BlockSpec((None, L, KV_WIDTH), row),
                  pl.BlockSpec((None, L, KV_WIDTH), row)],
        out_specs=pl.BlockSpec((None, L, ATTN_WIDTH), row),
        out_shape=jax.ShapeDtypeStruct((B, L, ATTN_WIDTH), BF16),
        compiler_params=_cparams(("parallel",)),
        name="attention_context",
    )(sink, q, k, v)


def _attention_latent(q, k, v, k_ctx, v_ctx, sink):
    B, S, _ = q.shape
    blk = WINDOW
    nb = S // blk
    nctx = k_ctx.shape[1]
    cur = lambda b, n: (b, n, 0)
    prv = lambda b, n: (b, jnp.maximum(n - 1, 0), 0)
    nxt = lambda b, n: (b, jnp.minimum(n + 1, nb - 1), 0)
    ctx = lambda b, n: (b, 0, 0)
    kv = lambda im: pl.BlockSpec((None, blk, KV_WIDTH), im)
    return pl.pallas_call(
        _attn_lat_kernel,
        grid=(B, nb),
        in_specs=[pl.BlockSpec(memory_space=pltpu.SMEM),
                  pl.BlockSpec((None, blk, ATTN_WIDTH), cur),
                  kv(prv), kv(cur), kv(nxt), kv(prv), kv(cur), kv(nxt),
                  pl.BlockSpec((None, nctx, KV_WIDTH), ctx),
                  pl.BlockSpec((None, nctx, KV_WIDTH), ctx)],
        out_specs=pl.BlockSpec((None, blk, ATTN_WIDTH), cur),
        out_shape=jax.ShapeDtypeStruct((B, S, ATTN_WIDTH), BF16),
        compiler_params=_cparams(("parallel", "parallel")),
        name="attention_latent",
    )(sink, q, k, k, k, v, v, v, k_ctx, v_ctx)


def _ssm_kernel(x_ref, mod_ref, wu_ref, perm_ref, permt_ref, wb_ref, wc_ref, a_ref, d_ref, h0_ref,
                y_ref, fs_ref, bu_ref, st_ref, *, per_batch):
    i = pl.program_id(2)
    nc = pl.num_programs(2)
    nseq, tm, dm = x_ref.shape
    rows = nseq * tm
    half = a_ref.shape[-1] // 2
    nre = half // 2

    @pl.when(i == 0)
    def _():
        st_ref[...] = h0_ref[...]

    hs = []
    for b in range(nseq):
        mb = b if per_batch else 0
        hs.append((x_ref[b] * (1.0 + mod_ref[mb, 1:2, :]) + mod_ref[mb, 0:1, :]).astype(BF16))
    h = jnp.concatenate(hs, axis=0)
    u = jnp.dot(h, wu_ref[...], preferred_element_type=F32)
    ut = jnp.dot(perm_ref[...], u.astype(BF16), preferred_element_type=F32).astype(BF16)
    for kc in range(2):
        bu_ref[:, kc * half:(kc + 1) * half] = jnp.dot(
            ut[:, kc * (SSM_WIDTH // 2):(kc + 1) * (SSM_WIDTH // 2)], wb_ref[kc],
            preferred_element_type=F32)

    for kc in range(2):
        for s0 in range(0, nre, SCAN_STRIP):
            re = slice(kc * half + s0, kc * half + s0 + SCAN_STRIP)
            im = slice(kc * half + nre + s0, kc * half + nre + s0 + SCAN_STRIP)
            a_re = a_ref[:, re]
            a_im = a_ref[:, im]

            def step(t, carry):
                xr, xi = carry
                r = pl.multiple_of(t * nseq, nseq)
                nr = a_re * xr - a_im * xi + bu_ref[pl.ds(r, nseq), re]
                ni = a_re * xi + a_im * xr + bu_ref[pl.ds(r, nseq), im]
                bu_ref[pl.ds(r, nseq), re] = nr
                bu_ref[pl.ds(r, nseq), im] = ni
                return nr, ni

            xr, xi = lax.fori_loop(0, tm, step, (st_ref[:, re], st_ref[:, im]), unroll=4)
            st_ref[:, re] = xr
            st_ref[:, im] = xi

    yt = jnp.concatenate(
        [jnp.dot(bu_ref[:, kc * half:(kc + 1) * half].astype(BF16), wc_ref[kc], preferred_element_type=F32)
         for kc in range(2)], axis=1)
    y_hi = yt.astype(BF16)
    y_lo = (yt - y_hi.astype(F32)).astype(BF16)
    y = (jnp.dot(permt_ref[...], y_hi, preferred_element_type=F32)
         + jnp.dot(permt_ref[...], y_lo, preferred_element_type=F32))
    y = y + u * d_ref[...]
    for b in range(nseq):
        y_ref[b] = y[b * tm:(b + 1) * tm]

    @pl.when(i == nc - 1)
    def _():
        fs_ref[...] = st_ref[...]


def _ssm_perm(nseq, tm):
    rows = nseq * tm
    p = np.zeros((2, rows, rows), np.float32)
    for d in range(2):
        for b in range(nseq):
            for t in range(tm):
                tt = t if d == 0 else tm - 1 - t
                p[d, tt * nseq + b, b * tm + t] = 1.0
    return p


def _ssm_mix(x, mod, wu, ssm, h0):
    B, L, D = x.shape
    nseq = SUBLANES
    tm = min(SSM_STEPS, L)
    nc = L // tm
    rows = nseq * tm
    per_batch = mod.shape[0] > 1
    ns = 2 * SSM_GROUPS * SSM_STATE
    perm = _ssm_perm(nseq, tm)
    perm_f = jnp.asarray(perm, BF16)
    perm_t = jnp.asarray(np.transpose(perm, (0, 2, 1)), BF16)
    chunk = lambda d, g, i: i + d * (nc - 1 - 2 * i)
    if per_batch:
        mod_spec = pl.BlockSpec((nseq, N_MOD, D), lambda d, g, i: (g, 0, 0))
    else:
        mod_spec = pl.BlockSpec((1, N_MOD, D), lambda d, g, i: (0, 0, 0))
    y, fs = pl.pallas_call(
        functools.partial(_ssm_kernel, per_batch=per_batch),
        grid=(2, B // nseq, nc),
        in_specs=[pl.BlockSpec((nseq, tm, D), lambda d, g, i: (g, chunk(d, g, i), 0)),
                  mod_spec,
                  pl.BlockSpec((D, SSM_WIDTH), lambda d, g, i: (0, 0)),
                  pl.BlockSpec((None, rows, rows), lambda d, g, i: (d, 0, 0)),
                  pl.BlockSpec((None, rows, rows), lambda d, g, i: (d, 0, 0)),
                  pl.BlockSpec((None, 2, SSM_WIDTH // 2, ns // 2), lambda d, g, i: (d, 0, 0, 0)),
                  pl.BlockSpec((None, 2, ns // 2, SSM_WIDTH // 2), lambda d, g, i: (d, 0, 0, 0)),
                  pl.BlockSpec((None, nseq, ns), lambda d, g, i: (d, 0, 0)),
                  pl.BlockSpec((None, 1, SSM_WIDTH), lambda d, g, i: (d, 0, 0)),
                  pl.BlockSpec((None, nseq, ns), lambda d, g, i: (d, g, 0))],
        out_specs=[pl.BlockSpec((None, nseq, tm, SSM_WIDTH), lambda d, g, i: (d, g, chunk(d, g, i), 0)),
                   pl.BlockSpec((None, nseq, ns), lambda d, g, i: (d, g, 0))],
        out_shape=[jax.ShapeDtypeStruct((2, B, L, SSM_WIDTH), F32),
                   jax.ShapeDtypeStruct((2, B, ns), F32)],
        scratch_shapes=[pltpu.VMEM((rows, ns), F32), pltpu.VMEM((nseq, ns), F32)],
        compiler_params=_cparams(("arbitrary", "arbitrary", "arbitrary")),
        name="ssm_scan",
    )(x, mod, wu, perm_f, perm_t, ssm["wb"], ssm["wc"], ssm["a"], ssm["d"], h0)
    return y, fs


def _state_to_lanes(s):
    lead = s.shape[:-3]
    s = s.reshape(lead + (2, 2, SSM_GROUPS // 2, SSM_STATE))
    s = jnp.swapaxes(s, -4, -3)
    return s.reshape(lead + (2 * SSM_GROUPS * SSM_STATE,))


def _lanes_to_state(v):
    lead = v.shape[:-1]
    s = v.reshape(lead + (2, 2, SSM_GROUPS // 2, SSM_STATE))
    s = jnp.swapaxes(s, -4, -3)
    return s.reshape(lead + (2, SSM_GROUPS, SSM_STATE))


def _ssm_params(a_re, a_im, log_dt, b_re, b_im, c_re, c_im, dvec):
    G, N, C = SSM_GROUPS, SSM_STATE, SSM_GROUP
    lam_re = jnp.minimum(a_re, -1e-4)
    lam_im = a_im
    dt = jnp.exp(log_dt)[..., None]
    mag = jnp.exp(lam_re * dt)
    abar_re, abar_im = mag * jnp.cos(lam_im * dt), mag * jnp.sin(lam_im * dt)
    den = jnp.square(lam_re) + jnp.square(lam_im)
    p, qi = abar_re - 1.0, abar_im
    f_re = (p * lam_re + qi * lam_im) / den
    f_im = (qi * lam_re - p * lam_im) / den
    bbar_re = f_re[..., None] * b_re - f_im[..., None] * b_im
    bbar_im = f_re[..., None] * b_im + f_im[..., None] * b_re
    a = _state_to_lanes(jnp.stack([abar_re, abar_im], axis=1))
    a = jnp.broadcast_to(a[:, None, :], (2, SUBLANES, a.shape[-1]))
    eye = jnp.eye(G // 2, dtype=F32)

    def bd_in(bb):
        bb = bb.reshape(2, 2, G // 2, N, C)
        return jnp.einsum('dkgnc,gh->dkgchn', bb, eye).reshape(2, 2, (G // 2) * C, (G // 2) * N)

    def bd_out(cc):
        cc = cc.reshape(2, 2, G // 2, C, N)
        return jnp.einsum('dkgcn,gh->dkgnhc', cc, eye).reshape(2, 2, (G // 2) * N, (G // 2) * C)

    wb = jnp.concatenate([bd_in(bbar_re), bd_in(bbar_im)], axis=-1).astype(BF16)
    wc = jnp.concatenate([bd_out(c_re), -bd_out(c_im)], axis=-2).astype(BF16)
    d = jnp.stack([dvec, jnp.zeros_like(dvec)], axis=0).reshape(2, 1, SSM_WIDTH)
    return {"a": a, "wb": wb, "wc": wc, "d": d}


def _layer_norm(x, g, b):
    mu = jnp.mean(x, axis=-1, keepdims=True)
    xc = x - mu
    var = jnp.mean(xc * xc, axis=-1, keepdims=True)
    return xc * lax.rsqrt(var + LN_EPS) * g + b


def _gelu_tanh(x):
    return 0.5 * x * (1.0 + jnp.tanh(math.sqrt(2.0 / math.pi) * (x + 0.044715 * (x * x * x))))


def _merge_kernel(x_ref, ao_ref, yf_ref, yb_ref, ga_ref, gs_ref, mod_ref, wglu_ref, wa_ref, ws_ref, wo_ref,
                  lng_ref, lnb_ref, rwt_ref, rb_ref, x1_ref, h2_ref, idx_ref, rank_ref, wt_ref, cnt_ref,
                  run_ref, *, alpha):
    @pl.when((pl.program_id(0) == 0) & (pl.program_id(1) == 0))
    def _():
        run_ref[...] = jnp.zeros_like(run_ref)

    z = _gelu_tanh(yf_ref[...] + yb_ref[...])
    gate = jax.nn.sigmoid(jnp.dot(z.astype(BF16), wglu_ref[...], preferred_element_type=F32))
    ssm_o = (z * gate).astype(BF16)
    merged = (jax.nn.sigmoid(ga_ref[...].astype(F32)) * jnp.dot(ao_ref[...], wa_ref[...], preferred_element_type=F32)
              + jax.nn.sigmoid(gs_ref[...].astype(F32)) * jnp.dot(ssm_o, ws_ref[...], preferred_element_type=F32))
    mix = jnp.dot(merged.astype(BF16), wo_ref[...], preferred_element_type=F32)
    x1 = _layer_norm(alpha * x_ref[...] + mod_ref[2:3, :] * mix, lng_ref[...], lnb_ref[...])
    h2 = x1 * (1.0 + mod_ref[4:5, :]) + mod_ref[3:4, :]
    x1_ref[...] = x1
    h2_ref[...] = h2

    logits = lax.dot_general(rwt_ref[...], h2, (((1,), (1,)), ((), ())), preferred_element_type=F32,
                             precision=lax.Precision.HIGHEST)
    score = jax.nn.sigmoid(logits)
    tm = score.shape[1]
    eidx = lax.broadcasted_iota(jnp.int32, score.shape, 0).astype(F32)
    work = score + rb_ref[...]
    picks, sel = [], []
    member = jnp.zeros_like(score)
    for _ in range(TOP_K):
        best = jnp.max(work, axis=0, keepdims=True)
        pick = jnp.min(jnp.where(work == best, eidx, float(N_EXPERTS)), axis=0, keepdims=True)
        hit = eidx == pick
        picks.append(pick)
        sel.append(jnp.sum(jnp.where(hit, score, 0.0), axis=0, keepdims=True))
        member = member + hit.astype(F32)
        work = jnp.where(hit, -jnp.inf, work)
    total = sel[0]
    for s in sel[1:]:
        total = total + s
    before = (lax.broadcasted_iota(jnp.int32, (tm, tm), 0) < lax.broadcasted_iota(jnp.int32, (tm, tm), 1))
    prefix = jnp.dot(member.astype(BF16), before.astype(BF16), preferred_element_type=F32)
    base = prefix + run_ref[...]
    for k in range(TOP_K):
        idx_ref[k:k + 1, :] = picks[k].astype(jnp.int32)
        rank_ref[k:k + 1, :] = jnp.sum(jnp.where(eidx == picks[k], base, 0.0), axis=0,
                                       keepdims=True).astype(jnp.int32)
        wt_ref[k:k + 1, :] = sel[k] / total * ROUTED_SCALE
    run_ref[...] = run_ref[...] + jnp.sum(member, axis=1, keepdims=True)
    cnt_ref[...] = run_ref[...].astype(jnp.int32)


def _merge(x, attn_o, y, ga, gs, mod, wts, alpha):
    B, L, D = x.shape
    tm = min(MERGE_ROWS, L)
    per_batch = mod.shape[0] > 1
    row = lambda b, i: (b, i, 0)
    full = lambda b, i: (0, 0)
    wspec = lambda a: pl.BlockSpec(a.shape, full)
    return pl.pallas_call(
        functools.partial(_merge_kernel, alpha=alpha),
        grid=(B, L // tm),
        in_specs=[pl.BlockSpec((None, tm, D), row),
                  pl.BlockSpec((None, tm, ATTN_WIDTH), row),
                  pl.BlockSpec((None, None, tm, SSM_WIDTH), lambda b, i: (0, b, i, 0)),
                  pl.BlockSpec((None, None, tm, SSM_WIDTH), lambda b, i: (1, b, i, 0)),
                  pl.BlockSpec((None, tm, D), row),
                  pl.BlockSpec((None, tm, D), row),
                  pl.BlockSpec((None, N_MOD, D), (lambda b, i: (b, 0, 0)) if per_batch else (lambda b, i: (0, 0, 0))),
                  wspec(wts["w_glu"]), wspec(wts["w_attn_br"]), wspec(wts["w_ssm_br"]), wspec(wts["w_out"]),
                  wspec(wts["ln1_g"]), wspec(wts["ln1_b"]), wspec(wts["router_wt"]), wspec(wts["router_b"])],
        out_specs=[pl.BlockSpec((None, tm, D), row),
                   pl.BlockSpec((None, tm, D), row),
                   pl.BlockSpec((None, TOP_K, tm), lambda b, i: (b, 0, i)),
                   pl.BlockSpec((None, TOP_K, tm), lambda b, i: (b, 0, i)),
                   pl.BlockSpec((None, TOP_K, tm), lambda b, i: (b, 0, i)),
                   pl.BlockSpec((N_EXPERTS, 1), full)],
        out_shape=[jax.ShapeDtypeStruct((B, L, D), F32),
                   jax.ShapeDtypeStruct((B, L, D), F32),
                   jax.ShapeDtypeStruct((B, TOP_K, L), jnp.int32),
                   jax.ShapeDtypeStruct((B, TOP_K, L), jnp.int32),
                   jax.ShapeDtypeStruct((B, TOP_K, L), F32),
                   jax.ShapeDtypeStruct((N_EXPERTS, 1), jnp.int32)],
        scratch_shapes=[pltpu.VMEM((N_EXPERTS, 1), F32)],
        compiler_params=_cparams(("arbitrary", "arbitrary")),
        name="merge_ln1_router",
    )(x, attn_o, y, y, ga, gs, mod, wts["w_glu"], wts["w_attn_br"], wts["w_ssm_br"], wts["w_out"],
      wts["ln1_g"], wts["ln1_b"], wts["router_wt"], wts["router_b"])


def _dispatch_kernel(dest_ref, h_ref, xs_ref, sem):
    tm = h_ref.shape[0]

    def start(r, c):
        for k in range(TOP_K):
            pltpu.make_async_copy(h_ref.at[pl.ds(r, 1)], xs_ref.at[pl.ds(dest_ref[k, r], 1)], sem).start()
        return c

    lax.fori_loop(0, tm, start, 0, unroll=4)
    for k in range(TOP_K):
        pltpu.make_async_copy(h_ref, xs_ref.at[pl.ds(0, tm)], sem).wait()


def _tile_major(a, tm):
    B, K, L = a.shape
    return a.reshape(B, K, L // tm, tm).transpose(0, 2, 1, 3).reshape(B * (L // tm), K, tm)


def _dispatch(h2, dest):
    B, L, D = h2.shape
    tm = min(DISPATCH_ROWS, L)
    nt = B * L // tm
    return pl.pallas_call(
        _dispatch_kernel,
        grid=(nt,),
        in_specs=[pl.BlockSpec((None, TOP_K, tm), lambda i: (i, 0, 0), memory_space=pltpu.SMEM),
                  pl.BlockSpec((tm, D), lambda i: (i, 0))],
        out_specs=pl.BlockSpec(memory_space=pl.ANY),
        out_shape=jax.ShapeDtypeStruct((B * L * TOP_K, D), h2.dtype),
        scratch_shapes=[pltpu.SemaphoreType.DMA],
        compiler_params=_cparams(("arbitrary",)),
        name="moe_dispatch",
    )(_tile_major(dest, tm), h2.reshape(B * L, D))


def _gmm_kernel(tile_ref, exp_ref, valid_ref, gstart_ref, xs_ref, wg_ref, wu_ref, wd_ref, ys_ref,
                wgb_ref, wub_ref, wdb_ref):
    w = pl.program_id(0)
    e = exp_ref[w]
    t = tile_ref[w]
    prev = jnp.maximum(w - 1, 0)
    new_expert = (w == 0) | (e != exp_ref[prev])
    first_visit = (w == 0) | (t != tile_ref[prev])

    @pl.when(new_expert)
    def _():
        wgb_ref[...] = wg_ref[...].astype(BF16)
        wub_ref[...] = wu_ref[...].astype(BF16)
        wdb_ref[...] = wd_ref[...].astype(BF16)

    @pl.when(valid_ref[w] == 1)
    def _():
        x = xs_ref[...].astype(BF16)
        g = jnp.dot(x, wgb_ref[...], preferred_element_type=F32)
        u = jnp.dot(x, wub_ref[...], preferred_element_type=F32)
        a = (g * jax.nn.sigmoid(g) * u).astype(BF16)
        y = jnp.dot(a, wdb_ref[...], preferred_element_type=F32)
        tg = y.shape[0]
        rows = t * tg + lax.broadcasted_iota(jnp.int32, (tg, 1), 0)
        mine = (rows >= gstart_ref[e]) & (rows < gstart_ref[e + 1])

        @pl.when(first_visit)
        def _():
            ys_ref[...] = jnp.where(mine, y, 0.0)

        @pl.when(jnp.logical_not(first_visit))
        def _():
            ys_ref[...] = jnp.where(mine, y, ys_ref[...])


def _grouped_experts(xs, counts, w_gate, w_up, w_down):
    A, D = xs.shape
    tg = GMM_ROWS
    nt = A // tg
    n_items = nt + N_EXPERTS - 1
    ff = w_gate.shape[-1]
    gend = jnp.cumsum(counts).astype(jnp.int32)
    gstart = jnp.concatenate([jnp.zeros((1,), jnp.int32), gend])
    first_row = jnp.arange(nt, dtype=jnp.int32) * tg
    count_le = lambda ends, v: jnp.sum((ends[None, :] <= v[:, None]).astype(jnp.int32), axis=1)
    e_lo = count_le(gend, first_row)
    e_hi = count_le(gend, first_row + tg - 1)
    per_tile = e_hi - e_lo + 1
    item_end = jnp.cumsum(per_tile).astype(jnp.int32)
    total = item_end[-1]
    wi = jnp.arange(n_items, dtype=jnp.int32)
    tile = jnp.minimum(count_le(item_end, wi), nt - 1)
    in_tile = (tile[:, None] == jnp.arange(nt, dtype=jnp.int32)[None, :]).astype(jnp.int32)
    lookup = lambda table: jnp.sum(in_tile * table[None, :], axis=1)
    expert = lookup(e_lo) + wi - lookup(item_end - per_tile)
    valid = (wi < total).astype(jnp.int32)
    expert = jnp.where(valid == 1, expert, e_hi[nt - 1])
    grid_spec = pltpu.PrefetchScalarGridSpec(
        num_scalar_prefetch=4,
        grid=(n_items,),
        in_specs=[pl.BlockSpec((tg, D), lambda w, tl, ex, va, gs: (tl[w], 0)),
                  pl.BlockSpec((None, D, ff), lambda w, tl, ex, va, gs: (ex[w], 0, 0)),
                  pl.BlockSpec((None, D, ff), lambda w, tl, ex, va, gs: (ex[w], 0, 0)),
                  pl.BlockSpec((None, ff, D), lambda w, tl, ex, va, gs: (ex[w], 0, 0))],
        out_specs=pl.BlockSpec((tg, D), lambda w, tl, ex, va, gs: (tl[w], 0)),
        scratch_shapes=[pltpu.VMEM((D, ff), BF16), pltpu.VMEM((D, ff), BF16), pltpu.VMEM((ff, D), BF16)],
    )
    return pl.pallas_call(
        _gmm_kernel,
        grid_spec=grid_spec,
        out_shape=jax.ShapeDtypeStruct((A, D), F32),
        compiler_params=_cparams(("arbitrary",)),
        name="moe_grouped_experts",
    )(tile, expert, valid, gstart, xs, w_gate, w_up, w_down)


def _combine_kernel(dest_ref, wt_ref, x1_ref, h2_ref, mod_ref, sg_ref, su_ref, sd_ref, lng_ref, lnb_ref, ys_ref,
                    o_ref, buf_ref, sem, *, alpha):
    tm = x1_ref.shape[0]

    def start(r, c):
        for k in range(TOP_K):
            pltpu.make_async_copy(ys_ref.at[pl.ds(dest_ref[k, r], 1)], buf_ref.at[k, pl.ds(r, 1)], sem).start()
        return c

    lax.fori_loop(0, tm, start, 0, unroll=4)
    h = h2_ref[...].astype(BF16)
    g = jnp.dot(h, sg_ref[...], preferred_element_type=F32)
    u = jnp.dot(h, su_ref[...], preferred_element_type=F32)
    moe = jnp.dot((g * jax.nn.sigmoid(g) * u).astype(BF16), sd_ref[...], preferred_element_type=F32)
    for k in range(TOP_K):
        pltpu.make_async_copy(ys_ref.at[pl.ds(0, tm)], buf_ref.at[k], sem).wait()
    wt = wt_ref[...]
    for k in range(TOP_K):
        moe = moe + wt[:, k:k + 1] * buf_ref[k]
    o_ref[...] = _layer_norm(alpha * x1_ref[...] + mod_ref[5:6, :] * moe, lng_ref[...], lnb_ref[...])


def _combine(ys, dest, wts, x1, h2, mod, shared, alpha):
    B, L, D = x1.shape
    tm = min(COMBINE_ROWS, L)
    nl = L // tm
    per_batch = mod.shape[0] > 1
    row = lambda b, i: (b, i, 0)
    full = lambda b, i: (0, 0)
    wspec = lambda a: pl.BlockSpec(a.shape, full)
    return pl.pallas_call(
        functools.partial(_combine_kernel, alpha=alpha),
        grid=(B, nl),
        in_specs=[pl.BlockSpec((None, TOP_K, tm), lambda b, i: (b * nl + i, 0, 0), memory_space=pltpu.SMEM),
                  pl.BlockSpec((None, tm, TOP_K), row),
                  pl.BlockSpec((None, tm, D), row),
                  pl.BlockSpec((None, tm, D), row),
                  pl.BlockSpec((None, N_MOD, D), (lambda b, i: (b, 0, 0)) if per_batch else (lambda b, i: (0, 0, 0))),
                  wspec(shared["sh_w_gate"]), wspec(shared["sh_w_up"]), wspec(shared["sh_w_down"]),
                  wspec(shared["ln2_g"]), wspec(shared["ln2_b"]),
                  pl.BlockSpec(memory_space=pl.ANY)],
        out_specs=pl.BlockSpec((None, tm, D), row),
        out_shape=jax.ShapeDtypeStruct((B, L, D), F32),
        scratch_shapes=[pltpu.VMEM((TOP_K, tm, D), F32), pltpu.SemaphoreType.DMA],
        compiler_params=_cparams(("arbitrary", "arbitrary")),
        name="moe_combine_ln2",
    )(_tile_major(dest, tm), jnp.swapaxes(wts, 1, 2), x1, h2, mod, shared["sh_w_gate"], shared["sh_w_up"],
      shared["sh_w_down"], shared["ln2_g"], shared["ln2_b"], ys)


def _slots(idx, rank, counts):
    start = jnp.cumsum(counts) - counts
    pick = idx[..., None] == jnp.arange(N_EXPERTS, dtype=jnp.int32)
    return jnp.sum(jnp.where(pick, start, 0), axis=-1).astype(jnp.int32) + rank


def _run_group(x, mod, wts, ssm, cache, alpha):
    B, L, D = x.shape
    latent = cache is not None
    if latent:
        k_ctx, v_ctx, h0 = cache
        cos, sin = _rope_cos_sin(L)
    else:
        cos = jnp.zeros((L, LANES), F32)
        sin = cos
        h0 = jnp.zeros((2, B, 2 * SSM_GROUPS * SSM_STATE), F32)
    q, k, v, ga, gs = _in_projection(x, mod, wts["w_qkvg"], cos, sin, rope=latent)
    if latent:
        attn_o = _attention_latent(q, k, v, k_ctx, v_ctx, wts["attn_sink"])
    else:
        attn_o = _attention_context(q, k, v, wts["attn_sink"])
    y, fs = _ssm_mix(x, mod, wts["w_u"], ssm, h0)
    x1, h2, idx, rank, rw, counts = _merge(x, attn_o, y, ga, gs, mod, wts, alpha)
    counts = counts.reshape(N_EXPERTS)
    dest = _slots(idx, rank, counts)
    xs = _dispatch(h2, dest)
    ys = _grouped_experts(xs, counts, wts["exp_w_gate"], wts["exp_w_up"], wts["exp_w_down"])
    out = _combine(ys, dest, rw, x1, h2, mod, wts, alpha)
    return out, k, v, fs


def kernel(x_prompt, x_sample, cache_k, cache_v, state_ssm, c, c_ctx, mod_w, mod_b, w_in, attn_sink, w_attn_br, ssm_a_re, ssm_a_im, ssm_log_dt, ssm_b_re, ssm_b_im, ssm_c_re, ssm_c_im, ssm_d, w_glu, w_ssm_br, w_out, ln1_g, ln1_b, ln2_g, ln2_b, router_w, router_b, exp_w_gate, exp_w_up, exp_w_down, sh_w_gate, sh_w_up, sh_w_down):
    depth = w_in.shape[0]
    assert depth == 1
    alpha = (2.0 * depth) ** 0.25
    D = x_prompt.shape[-1]
    nb_p = x_prompt.shape[0]
    nb_s = x_sample.shape[0]
    l = 0

    ncond = 1 + nb_s
    npad = -ncond % SUBLANES
    cond = jnp.concatenate([c_ctx[None, :], c, jnp.zeros((npad, D), F32)], axis=0)
    mod = _modulation(cond, mod_w[l], mod_b[l]).reshape(ncond + npad, N_MOD, D)
    mod_ctx, mod_lat = mod[0:1], mod[1:ncond]

    o1 = ATTN_WIDTH
    o3 = o1 + 2 * KV_WIDTH
    o4 = o3 + SSM_WIDTH
    wi = w_in[l]
    scale = HEAD_DIM ** -0.5
    wts = {
        "w_qkvg": jnp.concatenate([wi[:, :o1] * scale, wi[:, o1:o3], wi[:, o4:]], axis=1).astype(BF16),
        "w_u": wi[:, o3:o4].astype(BF16),
        "attn_sink": attn_sink[l],
        "w_glu": w_glu[l].astype(BF16), "w_attn_br": w_attn_br[l].astype(BF16),
        "w_ssm_br": w_ssm_br[l].astype(BF16), "w_out": w_out[l].astype(BF16),
        "ln1_g": ln1_g[l].reshape(1, D), "ln1_b": ln1_b[l].reshape(1, D),
        "ln2_g": ln2_g[l].reshape(1, D), "ln2_b": ln2_b[l].reshape(1, D),
        "router_wt": router_w[l].T, "router_b": router_b[l].reshape(N_EXPERTS, 1),
        "exp_w_gate": exp_w_gate[l], "exp_w_up": exp_w_up[l], "exp_w_down": exp_w_down[l],
        "sh_w_gate": sh_w_gate[l].astype(BF16), "sh_w_up": sh_w_up[l].astype(BF16),
        "sh_w_down": sh_w_down[l].astype(BF16),
    }
    ssm = _ssm_params(ssm_a_re[l], ssm_a_im[l], ssm_log_dt[l], ssm_b_re[l], ssm_b_im[l],
                      ssm_c_re[l], ssm_c_im[l], ssm_d[l])

    yp, k_p, v_p, fs_p = _run_group(x_prompt, mod_ctx, wts, ssm, None, alpha)

    past = cache_k.shape[2]
    k_ctx = cache_k[:, l].reshape(nb_s, past, KV_WIDTH)
    v_ctx = cache_v[:, l].reshape(nb_s, past, KV_WIDTH)
    h0 = jnp.swapaxes(_state_to_lanes(state_ssm[:, l]), 0, 1)
    ys_, _, _, _ = _run_group(x_sample, mod_lat, wts, ssm, (k_ctx, v_ctx, h0), alpha)

    Lp = x_prompt.shape[1]
    new_k = k_p.reshape(nb_p, 1, Lp, N_KV_HEADS, HEAD_DIM)
    new_v = v_p.reshape(nb_p, 1, Lp, N_KV_HEADS, HEAD_DIM)
    new_s = _lanes_to_state(jnp.swapaxes(fs_p, 0, 1))[:, None]
    return (yp, ys_, new_k, new_v, new_s)
```

```python
import functools
import math

import numpy as np
import jax
import jax.numpy as jnp
from jax import lax
from jax.experimental import pallas as pl
from jax.experimental.pallas import tpu as pltpu

F32 = jnp.float32
BF16 = jnp.bfloat16

GRID_W = 64
HEAD_DIM = 64
N_Q_HEADS = 8
N_KV_HEADS = 2
Q_PER_KV = N_Q_HEADS // N_KV_HEADS
ATTN_WIDTH = N_Q_HEADS * HEAD_DIM
KV_WIDTH = N_KV_HEADS * HEAD_DIM
WINDOW = 128
ROPE_BASE = 10000.0
SSM_WIDTH = 512
SSM_GROUP = 16
SSM_GROUPS = SSM_WIDTH // SSM_GROUP
SSM_STATE = 64
N_EXPERTS = 64
TOP_K = 6
ROUTED_SCALE = 2.5
N_MOD = 6
LN_EPS = 1e-5

SUBLANES = 8
LANES = 128
VMEM_LIMIT = 48 * 1024 * 1024

INPROJ_ROWS = 512
SSM_STEPS = 64
SCAN_STRIP = 512
MERGE_ROWS = 256
DISPATCH_ROWS = 256
GMM_ROWS = 256
COMBINE_ROWS = 128


def _cparams(sem):
    return pltpu.CompilerParams(dimension_semantics=sem, vmem_limit_bytes=VMEM_LIMIT)


def _mod_kernel(c_ref, w_ref, b_ref, o_ref):
    c = c_ref[...]
    s = c * jax.nn.sigmoid(c)
    o_ref[...] = jnp.dot(s, w_ref[...], preferred_element_type=F32,
                         precision=lax.Precision.HIGHEST) + b_ref[...]


def _modulation(cond, w, b):
    n, d = cond.shape
    nout = w.shape[1]
    tn = 512
    return pl.pallas_call(
        _mod_kernel,
        grid=(nout // tn,),
        in_specs=[pl.BlockSpec((n, d), lambda j: (0, 0)),
                  pl.BlockSpec((d, tn), lambda j: (0, j)),
                  pl.BlockSpec((1, tn), lambda j: (0, j))],
        out_specs=pl.BlockSpec((n, tn), lambda j: (0, j)),
        out_shape=jax.ShapeDtypeStruct((n, nout), F32),
        compiler_params=_cparams(("arbitrary",)),
        name="modulation",
    )(cond, w, b.reshape(1, nout))


def _rope_rotate(t, cos, sin):
    lane = lax.broadcasted_iota(jnp.int32, t.shape, 1)
    partner = jnp.where((lane % 32) < 16, pltpu.roll(t, LANES - 16, 1), pltpu.roll(t, 16, 1))
    return t * cos + partner * sin


def _inproj_kernel(x_ref, mod_ref, w_ref, cos_ref, sin_ref, q_ref, k_ref, v_ref, ga_ref, gs_ref, *, rope):
    x = x_ref[...]
    h = (x * (1.0 + mod_ref[1:2, :]) + mod_ref[0:1, :]).astype(BF16)
    p = jnp.dot(h, w_ref[...], preferred_element_type=F32)
    o1 = ATTN_WIDTH
    o2 = o1 + KV_WIDTH
    o3 = o2 + KV_WIDTH
    d = x.shape[1]
    q = p[:, :o1]
    k = p[:, o1:o2]
    if rope:
        cos = cos_ref[...]
        sin = sin_ref[...]
        q = jnp.concatenate([_rope_rotate(q[:, j * LANES:(j + 1) * LANES], cos, sin)
                             for j in range(o1 // LANES)], axis=1)
        k = _rope_rotate(k, cos, sin)
    q_ref[...] = q.astype(BF16)
    k_ref[...] = k
    v_ref[...] = p[:, o2:o3]
    ga_ref[...] = p[:, o3:o3 + d].astype(BF16)
    gs_ref[...] = p[:, o3 + d:].astype(BF16)


def _in_projection(x, mod, w, cos, sin, rope):
    B, L, D = x.shape
    tm = min(INPROJ_ROWS, L)
    per_batch = mod.shape[0] > 1
    nw = w.shape[1]
    row = lambda b, i: (b, i, 0)
    outs = pl.pallas_call(
        functools.partial(_inproj_kernel, rope=rope),
        grid=(B, L // tm),
        in_specs=[pl.BlockSpec((None, tm, D), row),
                  pl.BlockSpec((None, N_MOD, D), (lambda b, i: (b, 0, 0)) if per_batch else (lambda b, i: (0, 0, 0))),
                  pl.BlockSpec((D, nw), lambda b, i: (0, 0)),
                  pl.BlockSpec((tm, LANES), lambda b, i: (i, 0)),
                  pl.BlockSpec((tm, LANES), lambda b, i: (i, 0))],
        out_specs=[pl.BlockSpec((None, tm, ATTN_WIDTH), row),
                   pl.BlockSpec((None, tm, KV_WIDTH), row),
                   pl.BlockSpec((None, tm, KV_WIDTH), row),
                   pl.BlockSpec((None, tm, D), row),
                   pl.BlockSpec((None, tm, D), row)],
        out_shape=[jax.ShapeDtypeStruct((B, L, ATTN_WIDTH), BF16),
                   jax.ShapeDtypeStruct((B, L, KV_WIDTH), F32),
                   jax.ShapeDtypeStruct((B, L, KV_WIDTH), F32),
                   jax.ShapeDtypeStruct((B, L, D), BF16),
                   jax.ShapeDtypeStruct((B, L, D), BF16)],
        compiler_params=_cparams(("parallel", "parallel")),
        name="in_projection",
    )(x, mod, w, cos, sin)
    return outs


def _rope_cos_sin(n_tokens):
    t = jnp.arange(n_tokens, dtype=jnp.int32)
    pos = jnp.stack([t // GRID_W, t % GRID_W], axis=-1).astype(F32)
    n_freq = HEAD_DIM // 4
    inv_freq = ROPE_BASE ** (-jnp.arange(n_freq, dtype=F32) / n_freq)
    ang = pos[:, :, None] * inv_freq
    c, s = jnp.cos(ang), jnp.sin(ang)
    cos = jnp.concatenate([c[:, 0], c[:, 0], c[:, 1], c[:, 1]], axis=-1)
    sin = jnp.concatenate([-s[:, 0], s[:, 0], -s[:, 1], s[:, 1]], axis=-1)
    return jnp.tile(cos, (1, LANES // HEAD_DIM)), jnp.tile(sin, (1, LANES // HEAD_DIM))


def _attend(sink_ref, q, kcat, vcat, mask, o_ref):
    lq = q.shape[0]
    lane = lax.broadcasted_iota(jnp.int32, (1, LANES), 1)
    low = lane < HEAD_DIM
    k_sw = pltpu.roll(kcat, HEAD_DIM, 1)
    v_sw = pltpu.roll(vcat, HEAD_DIM, 1)
    neg = jnp.finfo(F32).min
    for h in range(N_KV_HEADS):
        keep = low if h == 0 else jnp.logical_not(low)
        kd = jnp.where(keep, kcat, k_sw).astype(BF16)
        vd = jnp.where(keep, vcat, v_sw).astype(BF16)
        qs = []
        for j in range(Q_PER_KV):
            head = h * Q_PER_KV + j
            blk = q[:, (head // 2) * LANES:(head // 2 + 1) * LANES]
            sel = low if head % 2 == 0 else jnp.logical_not(low)
            qs.append(jnp.where(sel, blk, jnp.zeros_like(blk)))
        qstack = jnp.concatenate(qs, axis=0)
        s = lax.dot_general(qstack, kd, (((1,), (1,)), ((), ())), preferred_element_type=F32)
        ps, ls = [], []
        for j in range(Q_PER_KV):
            sj = s[j * lq:(j + 1) * lq]
            if mask is not None:
                sj = jnp.where(mask, sj, neg)
            sink = sink_ref[h * Q_PER_KV + j]
            m = jnp.maximum(jnp.max(sj, axis=1, keepdims=True), sink)
            pj = jnp.exp(sj - m)
            ls.append(jnp.sum(pj, axis=1, keepdims=True) + jnp.exp(sink - m))
            ps.append(pj.astype(BF16))
        o = jnp.dot(jnp.concatenate(ps, axis=0), vd, preferred_element_type=F32)
        for jj in range(Q_PER_KV // 2):
            oe = o[(2 * jj) * lq:(2 * jj + 1) * lq] / ls[2 * jj]
            oo = o[(2 * jj + 1) * lq:(2 * jj + 2) * lq] / ls[2 * jj + 1]
            cb = (h * Q_PER_KV) // 2 + jj
            o_ref[:, cb * LANES:(cb + 1) * LANES] = jnp.where(low, oe, oo).astype(o_ref.dtype)


def _attn_ctx_kernel(sink_ref, q_ref, k_ref, v_ref, o_ref):
    _attend(sink_ref, q_ref[...], k_ref[...], v_ref[...], None, o_ref)


def _attn_lat_kernel(sink_ref, q_ref, kp_ref, kc_ref, kn_ref, vp_ref, vc_ref, vn_ref, kx_ref, vx_ref, o_ref):
    n = pl.program_id(1)
    nb = pl.num_programs(1)
    blk = q_ref.shape[0]
    nctx = kx_ref.shape[0]
    kcat = jnp.concatenate([kp_ref[...], kc_ref[...], kn_ref[...], kx_ref[...]], axis=0)
    vcat = jnp.concatenate([vp_ref[...], vc_ref[...], vn_ref[...], vx_ref[...]], axis=0)
    lk = 3 * blk + nctx
    qi = lax.broadcasted_iota(jnp.int32, (blk, lk), 0)
    kj = lax.broadcasted_iota(jnp.int32, (blk, lk), 1)
    rel = kj - blk - qi
    in_band = (rel <= WINDOW) & (rel >= -WINDOW)
    in_seq = ((kj >= blk) | (n > 0)) & ((kj < 2 * blk) | (n < nb - 1))
    mask = (kj >= 3 * blk) | (in_band & in_seq)
    _attend(sink_ref, q_ref[...], kcat, vcat, mask, o_ref)


def _attention_context(q, k, v, sink):
    B, L, _ = q.shape
    row = lambda b: (b, 0, 0)
    return pl.pallas_call(
        _attn_ctx_kernel,
        grid=(B,),
        in_specs=[pl.BlockSpec(memory_space=pltpu.SMEM),
                  pl.BlockSpec((None, L, ATTN_WIDTH), row),
                  pl.BlockSpec((None, L, KV_WIDTH), row),
                  pl.BlockSpec((None, L, KV_WIDTH), row)],
        out_specs=pl.BlockSpec((None, L, ATTN_WIDTH), row),
        out_shape=jax.ShapeDtypeStruct((B, L, ATTN_WIDTH), BF16),
        compiler_params=_cparams(("parallel",)),
        name="attention_context",
    )(sink, q, k, v)


def _attention_latent(q, k, v, k_ctx, v_ctx, sink):
    B, S, _ = q.shape
    blk = WINDOW
    nb = S // blk
    nctx = k_ctx.shape[1]
    cur = lambda b, n: (b, n, 0)
    prv = lambda b, n: (b, jnp.maximum(n - 1, 0), 0)
    nxt = lambda b, n: (b, jnp.minimum(n + 1, nb - 1), 0)
    ctx = lambda b, n: (b, 0, 0)
    kv = lambda im: pl.BlockSpec((None, blk, KV_WIDTH), im)
    return pl.pallas_call(
        _attn_lat_kernel,
        grid=(B, nb),
        in_specs=[pl.BlockSpec(memory_space=pltpu.SMEM),
                  pl.BlockSpec((None, blk, ATTN_WIDTH), cur),
                  kv(prv), kv(cur), kv(nxt), kv(prv), kv(cur), kv(nxt),
                  pl.BlockSpec((None, nctx, KV_WIDTH), ctx),
                  pl.BlockSpec((None, nctx, KV_WIDTH), ctx)],
        out_specs=pl.BlockSpec((None, blk, ATTN_WIDTH), cur),
        out_shape=jax.ShapeDtypeStruct((B, S, ATTN_WIDTH), BF16),
        compiler_params=_cparams(("parallel", "parallel")),
        name="attention_latent",
    )(sink, q, k, k, k, v, v, v, k_ctx, v_ctx)


def _ssm_kernel(x_ref, mod_ref, wu_ref, perm_ref, permt_ref, wb_ref, wc_ref, a_ref, d_ref, h0_ref,
                y_ref, fs_ref, bu_ref, st_ref, *, per_batch):
    i = pl.program_id(2)
    nc = pl.num_programs(2)
    nseq, tm, dm = x_ref.shape
    rows = nseq * tm
    half = a_ref.shape[-1] // 2
    nre = half // 2

    @pl.when(i == 0)
    def _():
        st_ref[...] = h0_ref[...]

    hs = []
    for b in range(nseq):
        mb = b if per_batch else 0
        hs.append((x_ref[b] * (1.0 + mod_ref[mb, 1:2, :]) + mod_ref[mb, 0:1, :]).astype(BF16))
    h = jnp.concatenate(hs, axis=0)
    u = jnp.dot(h, wu_ref[...], preferred_element_type=F32)
    ut = jnp.dot(perm_ref[...], u.astype(BF16), preferred_element_type=F32).astype(BF16)
    for kc in range(2):
        bu_ref[:, kc * half:(kc + 1) * half] = jnp.dot(
            ut[:, kc * (SSM_WIDTH // 2):(kc + 1) * (SSM_WIDTH // 2)], wb_ref[kc],
            preferred_element_type=F32)

    for kc in range(2):
        for s0 in range(0, nre, SCAN_STRIP):
            re = slice(kc * half + s0, kc * half + s0 + SCAN_STRIP)
            im = slice(kc * half + nre + s0, kc * half + nre + s0 + SCAN_STRIP)
            a_re = a_ref[:, re]
            a_im = a_ref[:, im]

            def step(t, carry):
                xr, xi = carry
                r = pl.multiple_of(t * nseq, nseq)
                nr = a_re * xr - a_im * xi + bu_ref[pl.ds(r, nseq), re]
                ni = a_re * xi + a_im * xr + bu_ref[pl.ds(r, nseq), im]
                bu_ref[pl.ds(r, nseq), re] = nr
                bu_ref[pl.ds(r, nseq), im] = ni
                return nr, ni

            xr, xi = lax.fori_loop(0, tm, step, (st_ref[:, re], st_ref[:, im]), unroll=4)
            st_ref[:, re] = xr
            st_ref[:, im] = xi

    yt = jnp.concatenate(
        [jnp.dot(bu_ref[:, kc * half:(kc + 1) * half].astype(BF16), wc_ref[kc], preferred_element_type=F32)
         for kc in range(2)], axis=1)
    y_hi = yt.astype(BF16)
    y_lo = (yt - y_hi.astype(F32)).astype(BF16)
    y = (jnp.dot(permt_ref[...], y_hi, preferred_element_type=F32)
         + jnp.dot(permt_ref[...], y_lo, preferred_element_type=F32))
    y = y + u * d_ref[...]
    for b in range(nseq):
        y_ref[b] = y[b * tm:(b + 1) * tm]

    @pl.when(i == nc - 1)
    def _():
        fs_ref[...] = st_ref[...]


def _ssm_perm(nseq, tm):
    rows = nseq * tm
    p = np.zeros((2, rows, rows), np.float32)
    for d in range(2):
        for b in range(nseq):
            for t in range(tm):
                tt = t if d == 0 else tm - 1 - t
                p[d, tt * nseq + b, b * tm + t] = 1.0
    return p


def _ssm_mix(x, mod, wu, ssm, h0):
    B, L, D = x.shape
    nseq = SUBLANES
    tm = min(SSM_STEPS, L)
    nc = L // tm
    rows = nseq * tm
    per_batch = mod.shape[0] > 1
    ns = 2 * SSM_GROUPS * SSM_STATE
    perm = _ssm_perm(nseq, tm)
    perm_f = jnp.asarray(perm, BF16)
    perm_t = jnp.asarray(np.transpose(perm, (0, 2, 1)), BF16)
    chunk = lambda d, g, i: i + d * (nc - 1 - 2 * i)
    if per_batch:
        mod_spec = pl.BlockSpec((nseq, N_MOD, D), lambda d, g, i: (g, 0, 0))
    else:
        mod_spec = pl.BlockSpec((1, N_MOD, D), lambda d, g, i: (0, 0, 0))
    y, fs = pl.pallas_call(
        functools.partial(_ssm_kernel, per_batch=per_batch),
        grid=(2, B // nseq, nc),
        in_specs=[pl.BlockSpec((nseq, tm, D), lambda d, g, i: (g, chunk(d, g, i), 0)),
                  mod_spec,
                  pl.BlockSpec((D, SSM_WIDTH), lambda d, g, i: (0, 0)),
                  pl.BlockSpec((None, rows, rows), lambda d, g, i: (d, 0, 0)),
                  pl.BlockSpec((None, rows, rows), lambda d, g, i: (d, 0, 0)),
                  pl.BlockSpec((None, 2, SSM_WIDTH // 2, ns // 2), lambda d, g, i: (d, 0, 0, 0)),
                  pl.BlockSpec((None, 2, ns // 2, SSM_WIDTH // 2), lambda d, g, i: (d, 0, 0, 0)),
                  pl.BlockSpec((None, nseq, ns), lambda d, g, i: (d, 0, 0)),
                  pl.BlockSpec((None, 1, SSM_WIDTH), lambda d, g, i: (d, 0, 0)),
                  pl.BlockSpec((None, nseq, ns), lambda d, g, i: (d, g, 0))],
        out_specs=[pl.BlockSpec((None, nseq, tm, SSM_WIDTH), lambda d, g, i: (d, g, chunk(d, g, i), 0)),
                   pl.BlockSpec((None, nseq, ns), lambda d, g, i: (d, g, 0))],
        out_shape=[jax.ShapeDtypeStruct((2, B, L, SSM_WIDTH), F32),
                   jax.ShapeDtypeStruct((2, B, ns), F32)],
        scratch_shapes=[pltpu.VMEM((rows, ns), F32), pltpu.VMEM((nseq, ns), F32)],
        compiler_params=_cparams(("arbitrary", "arbitrary", "arbitrary")),
        name="ssm_scan",
    )(x, mod, wu, perm_f, perm_t, ssm["wb"], ssm["wc"], ssm["a"], ssm["d"], h0)
    return y, fs


def _state_to_lanes(s):
    lead = s.shape[:-3]
    s = s.reshape(lead + (2, 2, SSM_GROUPS // 2, SSM_STATE))
    s = jnp.swapaxes(s, -4, -3)
    return s.reshape(lead + (2 * SSM_GROUPS * SSM_STATE,))


def _lanes_to_state(v):
    lead = v.shape[:-1]
    s = v.reshape(lead + (2, 2, SSM_GROUPS // 2, SSM_STATE))
    s = jnp.swapaxes(s, -4, -3)
    return s.reshape(lead + (2, SSM_GROUPS, SSM_STATE))


def _ssm_params(a_re, a_im, log_dt, b_re, b_im, c_re, c_im, dvec):
    G, N, C = SSM_GROUPS, SSM_STATE, SSM_GROUP
    lam_re = jnp.minimum(a_re, -1e-4)
    lam_im = a_im
    dt = jnp.exp(log_dt)[..., None]
    mag = jnp.exp(lam_re * dt)
    abar_re, abar_im = mag * jnp.cos(lam_im * dt), mag * jnp.sin(lam_im * dt)
    den = jnp.square(lam_re) + jnp.square(lam_im)
    p, qi = abar_re - 1.0, abar_im
    f_re = (p * lam_re + qi * lam_im) / den
    f_im = (qi * lam_re - p * lam_im) / den
    bbar_re = f_re[..., None] * b_re - f_im[..., None] * b_im
    bbar_im = f_re[..., None] * b_im + f_im[..., None] * b_re
    a = _state_to_lanes(jnp.stack([abar_re, abar_im], axis=1))
    a = jnp.broadcast_to(a[:, None, :], (2, SUBLANES, a.shape[-1]))
    eye = jnp.eye(G // 2, dtype=F32)

    def bd_in(bb):
        bb = bb.reshape(2, 2, G // 2, N, C)
        return jnp.einsum('dkgnc,gh->dkgchn', bb, eye).reshape(2, 2, (G // 2) * C, (G // 2) * N)

    def bd_out(cc):
        cc = cc.reshape(2, 2, G // 2, C, N)
        return jnp.einsum('dkgcn,gh->dkgnhc', cc, eye).reshape(2, 2, (G // 2) * N, (G // 2) * C)

    wb = jnp.concatenate([bd_in(bbar_re), bd_in(bbar_im)], axis=-1).astype(BF16)
    wc = jnp.concatenate([bd_out(c_re), -bd_out(c_im)], axis=-2).astype(BF16)
    d = jnp.stack([dvec, jnp.zeros_like(dvec)], axis=0).reshape(2, 1, SSM_WIDTH)
    return {"a": a, "wb": wb, "wc": wc, "d": d}


def _layer_norm(x, g, b):
    mu = jnp.mean(x, axis=-1, keepdims=True)
    xc = x - mu
    var = jnp.mean(xc * xc, axis=-1, keepdims=True)
    return xc * lax.rsqrt(var + LN_EPS) * g + b


def _pack_rows(x):
    w = x.shape[1] // 2
    return pltpu.pack_elementwise([x[:, :w], x[:, w:]], packed_dtype=BF16)


def _unpack_rows(p):
    return (pltpu.unpack_elementwise(p, index=0, packed_dtype=BF16, unpacked_dtype=F32),
            pltpu.unpack_elementwise(p, index=1, packed_dtype=BF16, unpacked_dtype=F32))


def _gelu_tanh(x):
    return 0.5 * x * (1.0 + jnp.tanh(math.sqrt(2.0 / math.pi) * (x + 0.044715 * (x * x * x))))


def _merge_kernel(x_ref, ao_ref, yf_ref, yb_ref, ga_ref, gs_ref, mod_ref, wglu_ref, wa_ref, ws_ref, wo_ref,
                  lng_ref, lnb_ref, rwt_ref, rb_ref, x1_ref, h2_ref, idx_ref, rank_ref, wt_ref, cnt_ref,
                  run_ref, *, alpha):
    @pl.when((pl.program_id(0) == 0) & (pl.program_id(1) == 0))
    def _():
        run_ref[...] = jnp.zeros_like(run_ref)

    z = _gelu_tanh(yf_ref[...] + yb_ref[...])
    gate = jax.nn.sigmoid(jnp.dot(z.astype(BF16), wglu_ref[...], preferred_element_type=F32))
    ssm_o = (z * gate).astype(BF16)
    merged = (jax.nn.sigmoid(ga_ref[...].astype(F32)) * jnp.dot(ao_ref[...], wa_ref[...], preferred_element_type=F32)
              + jax.nn.sigmoid(gs_ref[...].astype(F32)) * jnp.dot(ssm_o, ws_ref[...], preferred_element_type=F32))
    mix = jnp.dot(merged.astype(BF16), wo_ref[...], preferred_element_type=F32)
    x1 = _layer_norm(alpha * x_ref[...] + mod_ref[2:3, :] * mix, lng_ref[...], lnb_ref[...])
    h2 = x1 * (1.0 + mod_ref[4:5, :]) + mod_ref[3:4, :]
    x1_ref[...] = x1
    h2_ref[...] = _pack_rows(h2)

    logits = lax.dot_general(rwt_ref[...], h2, (((1,), (1,)), ((), ())), preferred_element_type=F32,
                             precision=lax.Precision.HIGHEST)
    score = jax.nn.sigmoid(logits)
    tm = score.shape[1]
    eidx = lax.broadcasted_iota(jnp.int32, score.shape, 0).astype(F32)
    work = score + rb_ref[...]
    picks, sel = [], []
    member = jnp.zeros_like(score)
    for _ in range(TOP_K):
        best = jnp.max(work, axis=0, keepdims=True)
        pick = jnp.min(jnp.where(work == best, eidx, float(N_EXPERTS)), axis=0, keepdims=True)
        hit = eidx == pick
        picks.append(pick)
        sel.append(jnp.sum(jnp.where(hit, score, 0.0), axis=0, keepdims=True))
        member = member + hit.astype(F32)
        work = jnp.where(hit, -jnp.inf, work)
    total = sel[0]
    for s in sel[1:]:
        total = total + s
    before = (lax.broadcasted_iota(jnp.int32, (tm, tm), 0) < lax.broadcasted_iota(jnp.int32, (tm, tm), 1))
    prefix = jnp.dot(member.astype(BF16), before.astype(BF16), preferred_element_type=F32)
    base = prefix + run_ref[...]
    for k in range(TOP_K):
        idx_ref[k:k + 1, :] = picks[k].astype(jnp.int32)
        rank_ref[k:k + 1, :] = jnp.sum(jnp.where(eidx == picks[k], base, 0.0), axis=0,
                                       keepdims=True).astype(jnp.int32)
        wt_ref[k:k + 1, :] = sel[k] / total * ROUTED_SCALE
    run_ref[...] = run_ref[...] + jnp.sum(member, axis=1, keepdims=True)
    cnt_ref[...] = run_ref[...].astype(jnp.int32)


def _merge(x, attn_o, y, ga, gs, mod, wts, alpha):
    B, L, D = x.shape
    tm = min(MERGE_ROWS, L)
    per_batch = mod.shape[0] > 1
    row = lambda b, i: (b, i, 0)
    full = lambda b, i: (0, 0)
    wspec = lambda a: pl.BlockSpec(a.shape, full)
    return pl.pallas_call(
        functools.partial(_merge_kernel, alpha=alpha),
        grid=(B, L // tm),
        in_specs=[pl.BlockSpec((None, tm, D), row),
                  pl.BlockSpec((None, tm, ATTN_WIDTH), row),
                  pl.BlockSpec((None, None, tm, SSM_WIDTH), lambda b, i: (0, b, i, 0)),
                  pl.BlockSpec((None, None, tm, SSM_WIDTH), lambda b, i: (1, b, i, 0)),
                  pl.BlockSpec((None, tm, D), row),
                  pl.BlockSpec((None, tm, D), row),
                  pl.BlockSpec((None, N_MOD, D), (lambda b, i: (b, 0, 0)) if per_batch else (lambda b, i: (0, 0, 0))),
                  wspec(wts["w_glu"]), wspec(wts["w_attn_br"]), wspec(wts["w_ssm_br"]), wspec(wts["w_out"]),
                  wspec(wts["ln1_g"]), wspec(wts["ln1_b"]), wspec(wts["router_wt"]), wspec(wts["router_b"])],
        out_specs=[pl.BlockSpec((None, tm, D), row),
                   pl.BlockSpec((None, tm, D // 2), row),
                   pl.BlockSpec((None, TOP_K, tm), lambda b, i: (b, 0, i)),
                   pl.BlockSpec((None, TOP_K, tm), lambda b, i: (b, 0, i)),
                   pl.BlockSpec((None, TOP_K, tm), lambda b, i: (b, 0, i)),
                   pl.BlockSpec((N_EXPERTS, 1), full)],
        out_shape=[jax.ShapeDtypeStruct((B, L, D), F32),
                   jax.ShapeDtypeStruct((B, L, D // 2), jnp.int32),
                   jax.ShapeDtypeStruct((B, TOP_K, L), jnp.int32),
                   jax.ShapeDtypeStruct((B, TOP_K, L), jnp.int32),
                   jax.ShapeDtypeStruct((B, TOP_K, L), F32),
                   jax.ShapeDtypeStruct((N_EXPERTS, 1), jnp.int32)],
        scratch_shapes=[pltpu.VMEM((N_EXPERTS, 1), F32)],
        compiler_params=_cparams(("arbitrary", "arbitrary")),
        name="merge_ln1_router",
    )(x, attn_o, y, y, ga, gs, mod, wts["w_glu"], wts["w_attn_br"], wts["w_ssm_br"], wts["w_out"],
      wts["ln1_g"], wts["ln1_b"], wts["router_wt"], wts["router_b"])


def _dispatch_kernel(dest_ref, h_ref, xs_ref, sem):
    tm = h_ref.shape[0]

    def start(r, c):
        for k in range(TOP_K):
            pltpu.make_async_copy(h_ref.at[pl.ds(r, 1)], xs_ref.at[pl.ds(dest_ref[0, r * TOP_K + k], 1)], sem).start()
        return c

    lax.fori_loop(0, tm, start, 0, unroll=4)
    for k in range(TOP_K):
        pltpu.make_async_copy(h_ref, xs_ref.at[pl.ds(0, tm)], sem).wait()


def _token_major(a, tm):
    B, K, L = a.shape
    return jnp.swapaxes(a, 1, 2).reshape(B * (L // tm), 1, tm * K)


def _dispatch(h2, dest):
    B, L, W = h2.shape
    tm = min(DISPATCH_ROWS, L)
    nt = B * L // tm
    return pl.pallas_call(
        _dispatch_kernel,
        grid=(nt,),
        in_specs=[pl.BlockSpec((None, 1, tm * TOP_K), lambda i: (i, 0, 0), memory_space=pltpu.SMEM),
                  pl.BlockSpec((tm, W), lambda i: (i, 0))],
        out_specs=pl.BlockSpec(memory_space=pl.ANY),
        out_shape=jax.ShapeDtypeStruct((B * L * TOP_K, W), h2.dtype),
        scratch_shapes=[pltpu.SemaphoreType.DMA],
        compiler_params=_cparams(("arbitrary",)),
        name="moe_dispatch",
    )(_token_major(dest, tm), h2.reshape(B * L, W))


def _gmm_kernel(tile_ref, exp_ref, valid_ref, gstart_ref, xs_ref, wg_ref, wu_ref, wd_ref, ys_ref,
                wgb_ref, wub_ref, wdb_ref):
    w = pl.program_id(0)
    e = exp_ref[w]
    t = tile_ref[w]
    prev = jnp.maximum(w - 1, 0)
    new_expert = (w == 0) | (e != exp_ref[prev])
    first_visit = (w == 0) | (t != tile_ref[prev])

    @pl.when(new_expert)
    def _():
        wgb_ref[...] = wg_ref[...].astype(BF16)
        wub_ref[...] = wu_ref[...].astype(BF16)
        wdb_ref[...] = wd_ref[...].astype(BF16)

    @pl.when(valid_ref[w] == 1)
    def _():
        x = jnp.concatenate(_unpack_rows(xs_ref[...]), axis=1).astype(BF16)
        g = jnp.dot(x, wgb_ref[...], preferred_element_type=F32)
        u = jnp.dot(x, wub_ref[...], preferred_element_type=F32)
        a = (g * jax.nn.sigmoid(g) * u).astype(BF16)
        y = _pack_rows(jnp.dot(a, wdb_ref[...], preferred_element_type=F32))
        tg = y.shape[0]
        rows = t * tg + lax.broadcasted_iota(jnp.int32, (tg, 1), 0)
        mine = (rows >= gstart_ref[e]) & (rows < gstart_ref[e + 1])

        @pl.when(first_visit)
        def _():
            ys_ref[...] = jnp.where(mine, y, 0)

        @pl.when(jnp.logical_not(first_visit))
        def _():
            ys_ref[...] = jnp.where(mine, y, ys_ref[...])


def _grouped_experts(xs, counts, w_gate, w_up, w_down):
    A, W = xs.shape
    D = w_gate.shape[-2]
    tg = GMM_ROWS
    nt = A // tg
    n_items = nt + N_EXPERTS - 1
    ff = w_gate.shape[-1]
    gend = jnp.cumsum(counts).astype(jnp.int32)
    gstart = jnp.concatenate([jnp.zeros((1,), jnp.int32), gend])
    first_row = jnp.arange(nt, dtype=jnp.int32) * tg
    count_le = lambda ends, v: jnp.sum((ends[None, :] <= v[:, None]).astype(jnp.int32), axis=1)
    e_lo = count_le(gend, first_row)
    e_hi = count_le(gend, first_row + tg - 1)
    per_tile = e_hi - e_lo + 1
    item_end = jnp.cumsum(per_tile).astype(jnp.int32)
    total = item_end[-1]
    wi = jnp.arange(n_items, dtype=jnp.int32)
    tile = jnp.minimum(count_le(item_end, wi), nt - 1)
    in_tile = (tile[:, None] == jnp.arange(nt, dtype=jnp.int32)[None, :]).astype(jnp.int32)
    lookup = lambda table: jnp.sum(in_tile * table[None, :], axis=1)
    expert = lookup(e_lo) + wi - lookup(item_end - per_tile)
    valid = (wi < total).astype(jnp.int32)
    expert = jnp.where(valid == 1, expert, e_hi[nt - 1])
    grid_spec = pltpu.PrefetchScalarGridSpec(
        num_scalar_prefetch=4,
        grid=(n_items,),
        in_specs=[pl.BlockSpec((tg, W), lambda w, tl, ex, va, gs: (tl[w], 0)),
                  pl.BlockSpec((None, D, ff), lambda w, tl, ex, va, gs: (ex[w], 0, 0)),
                  pl.BlockSpec((None, D, ff), lambda w, tl, ex, va, gs: (ex[w], 0, 0)),
                  pl.BlockSpec((None, ff, D), lambda w, tl, ex, va, gs: (ex[w], 0, 0))],
        out_specs=pl.BlockSpec((tg, W), lambda w, tl, ex, va, gs: (tl[w], 0)),
        scratch_shapes=[pltpu.VMEM((D, ff), BF16), pltpu.VMEM((D, ff), BF16), pltpu.VMEM((ff, D), BF16)],
    )
    return pl.pallas_call(
        _gmm_kernel,
        grid_spec=grid_spec,
        out_shape=jax.ShapeDtypeStruct((A, W), xs.dtype),
        compiler_params=_cparams(("arbitrary",)),
        name="moe_grouped_experts",
    )(tile, expert, valid, gstart, xs, w_gate, w_up, w_down)


def _combine_kernel(dcur_ref, dnxt_ref, wt_ref, x1_ref, h2_ref, mod_ref, sg_ref, su_ref, sd_ref, lng_ref, lnb_ref,
                    ys_ref, o_ref, buf_ref, sem, *, alpha):
    i = pl.program_id(0)
    n = pl.num_programs(0)
    tm = x1_ref.shape[0]
    slot = i % 2

    def gather(dest_ref, s):
        def start(r, c):
            for k in range(TOP_K):
                pltpu.make_async_copy(ys_ref.at[pl.ds(dest_ref[0, r * TOP_K + k], 1)],
                                      buf_ref.at[s, k, pl.ds(r, 1)], sem.at[s]).start()
            return c

        lax.fori_loop(0, tm, start, 0, unroll=4)

    @pl.when(i == 0)
    def _():
        gather(dcur_ref, slot)

    @pl.when(i + 1 < n)
    def _():
        gather(dnxt_ref, 1 - slot)

    h = jnp.concatenate(_unpack_rows(h2_ref[...]), axis=1).astype(BF16)
    g = jnp.dot(h, sg_ref[...], preferred_element_type=F32)
    u = jnp.dot(h, su_ref[...], preferred_element_type=F32)
    moe = jnp.dot((g * jax.nn.sigmoid(g) * u).astype(BF16), sd_ref[...], preferred_element_type=F32)
    for k in range(TOP_K):
        pltpu.make_async_copy(ys_ref.at[pl.ds(0, tm)], buf_ref.at[slot, k], sem.at[slot]).wait()
    wt = wt_ref[...]
    lo = jnp.zeros((tm, buf_ref.shape[-1]), F32)
    hi = lo
    for k in range(TOP_K):
        rl, rh = _unpack_rows(buf_ref[slot, k])
        lo = lo + wt[:, k:k + 1] * rl
        hi = hi + wt[:, k:k + 1] * rh
    moe = moe + jnp.concatenate([lo, hi], axis=1)
    o_ref[...] = _layer_norm(alpha * x1_ref[...] + mod_ref[0, 5:6, :] * moe, lng_ref[...], lnb_ref[...])


def _combine(ys, dest, wts, x1, h2, mod, shared, alpha):
    B, L, D = x1.shape
    W = ys.shape[-1]
    tm = min(COMBINE_ROWS, L)
    nl = L // tm
    nt = B * nl
    per_batch = mod.shape[0] > 1
    row = lambda i: (i, 0)
    full = lambda i: (0, 0)
    wspec = lambda a: pl.BlockSpec(a.shape, full)
    slots = _token_major(dest, tm)
    return pl.pallas_call(
        functools.partial(_combine_kernel, alpha=alpha),
        grid=(nt,),
        in_specs=[pl.BlockSpec((None, 1, tm * TOP_K), lambda i: (i, 0, 0), memory_space=pltpu.SMEM),
                  pl.BlockSpec((None, 1, tm * TOP_K), lambda i: (jnp.minimum(i + 1, nt - 1), 0, 0),
                               memory_space=pltpu.SMEM),
                  pl.BlockSpec((tm, TOP_K), row),
                  pl.BlockSpec((tm, D), row),
                  pl.BlockSpec((tm, W), row),
                  pl.BlockSpec((1, N_MOD, D), (lambda i: (i // nl, 0, 0)) if per_batch else (lambda i: (0, 0, 0))),
                  wspec(shared["sh_w_gate"]), wspec(shared["sh_w_up"]), wspec(shared["sh_w_down"]),
                  wspec(shared["ln2_g"]), wspec(shared["ln2_b"]),
                  pl.BlockSpec(memory_space=pl.ANY)],
        out_specs=pl.BlockSpec((tm, D), row),
        out_shape=jax.ShapeDtypeStruct((B * L, D), F32),
        scratch_shapes=[pltpu.VMEM((2, TOP_K, tm, W), ys.dtype), pltpu.SemaphoreType.DMA((2,))],
        compiler_params=_cparams(("arbitrary",)),
        name="moe_combine_ln2",
    )(slots, slots, jnp.swapaxes(wts, 1, 2).reshape(B * L, TOP_K), x1.reshape(B * L, D), h2.reshape(B * L, W),
      mod, shared["sh_w_gate"], shared["sh_w_up"], shared["sh_w_down"], shared["ln2_g"], shared["ln2_b"],
      ys).reshape(B, L, D)


def _slots(idx, rank, counts):
    start = jnp.cumsum(counts) - counts
    pick = idx[..., None] == jnp.arange(N_EXPERTS, dtype=jnp.int32)
    return jnp.sum(jnp.where(pick, start, 0), axis=-1).astype(jnp.int32) + rank


def _run_group(x, mod, wts, ssm, cache, alpha):
    B, L, D = x.shape
    latent = cache is not None
    if latent:
        k_ctx, v_ctx, h0 = cache
        cos, sin = _rope_cos_sin(L)
    else:
        cos = jnp.zeros((L, LANES), F32)
        sin = cos
        h0 = jnp.zeros((2, B, 2 * SSM_GROUPS * SSM_STATE), F32)
    q, k, v, ga, gs = _in_projection(x, mod, wts["w_qkvg"], cos, sin, rope=latent)
    if latent:
        attn_o = _attention_latent(q, k, v, k_ctx, v_ctx, wts["attn_sink"])
    else:
        attn_o = _attention_context(q, k, v, wts["attn_sink"])
    y, fs = _ssm_mix(x, mod, wts["w_u"], ssm, h0)
    x1, h2, idx, rank, rw, counts = _merge(x, attn_o, y, ga, gs, mod, wts, alpha)
    counts = counts.reshape(N_EXPERTS)
    dest = _slots(idx, rank, counts)
    xs = _dispatch(h2, dest)
    ys = _grouped_experts(xs, counts, wts["exp_w_gate"], wts["exp_w_up"], wts["exp_w_down"])
    out = _combine(ys, dest, rw, x1, h2, mod, wts, alpha)
    return out, k, v, fs


def kernel(x_prompt, x_sample, cache_k, cache_v, state_ssm, c, c_ctx, mod_w, mod_b, w_in, attn_sink, w_attn_br, ssm_a_re, ssm_a_im, ssm_log_dt, ssm_b_re, ssm_b_im, ssm_c_re, ssm_c_im, ssm_d, w_glu, w_ssm_br, w_out, ln1_g, ln1_b, ln2_g, ln2_b, router_w, router_b, exp_w_gate, exp_w_up, exp_w_down, sh_w_gate, sh_w_up, sh_w_down):
    depth = w_in.shape[0]
    assert depth == 1
    alpha = (2.0 * depth) ** 0.25
    D = x_prompt.shape[-1]
    nb_p = x_prompt.shape[0]
    nb_s = x_sample.shape[0]
    l = 0

    ncond = 1 + nb_s
    npad = -ncond % SUBLANES
    cond = jnp.concatenate([c_ctx[None, :], c, jnp.zeros((npad, D), F32)], axis=0)
    mod = _modulation(cond, mod_w[l], mod_b[l]).reshape(ncond + npad, N_MOD, D)
    mod_ctx, mod_lat = mod[0:1], mod[1:ncond]

    o1 = ATTN_WIDTH
    o3 = o1 + 2 * KV_WIDTH
    o4 = o3 + SSM_WIDTH
    wi = w_in[l]
    scale = HEAD_DIM ** -0.5
    wts = {
        "w_qkvg": jnp.concatenate([wi[:, :o1] * scale, wi[:, o1:o3], wi[:, o4:]], axis=1).astype(BF16),
        "w_u": wi[:, o3:o4].astype(BF16),
        "attn_sink": attn_sink[l],
        "w_glu": w_glu[l].astype(BF16), "w_attn_br": w_attn_br[l].astype(BF16),
        "w_ssm_br": w_ssm_br[l].astype(BF16), "w_out": w_out[l].astype(BF16),
        "ln1_g": ln1_g[l].reshape(1, D), "ln1_b": ln1_b[l].reshape(1, D),
        "ln2_g": ln2_g[l].reshape(1, D), "ln2_b": ln2_b[l].reshape(1, D),
        "router_wt": router_w[l].T, "router_b": router_b[l].reshape(N_EXPERTS, 1),
        "exp_w_gate": exp_w_gate[l], "exp_w_up": exp_w_up[l], "exp_w_down": exp_w_down[l],
        "sh_w_gate": sh_w_gate[l].astype(BF16), "sh_w_up": sh_w_up[l].astype(BF16),
        "sh_w_down": sh_w_down[l].astype(BF16),
    }
    ssm = _ssm_params(ssm_a_re[l], ssm_a_im[l], ssm_log_dt[l], ssm_b_re[l], ssm_b_im[l],
                      ssm_c_re[l], ssm_c_im[l], ssm_d[l])

    yp, k_p, v_p, fs_p = _run_group(x_prompt, mod_ctx, wts, ssm, None, alpha)

    past = cache_k.shape[2]
    k_ctx = cache_k[:, l].reshape(nb_s, past, KV_WIDTH)
    v_ctx = cache_v[:, l].reshape(nb_s, past, KV_WIDTH)
    h0 = jnp.swapaxes(_state_to_lanes(state_ssm[:, l]), 0, 1)
    ys_, _, _, _ = _run_group(x_sample, mod_lat, wts, ssm, (k_ctx, v_ctx, h0), alpha)

    Lp = x_prompt.shape[1]
    new_k = k_p.reshape(nb_p, 1, Lp, N_KV_HEADS, HEAD_DIM)
    new_v = v_p.reshape(nb_p, 1, Lp, N_KV_HEADS, HEAD_DIM)
    new_s = _lanes_to_state(jnp.swapaxes(fs_p, 0, 1))[:, None]
    return (yp, ys_, new_k, new_v, new_s)
```

```python
import functools
import math

import numpy as np
import jax
import jax.numpy as jnp
from jax import lax
from jax.experimental import pallas as pl
from jax.experimental.pallas import tpu as pltpu

F32 = jnp.float32
BF16 = jnp.bfloat16

GRID_W = 64
HEAD_DIM = 64
N_Q_HEADS = 8
N_KV_HEADS = 2
Q_PER_KV = N_Q_HEADS // N_KV_HEADS
ATTN_WIDTH = N_Q_HEADS * HEAD_DIM
KV_WIDTH = N_KV_HEADS * HEAD_DIM
WINDOW = 128
ROPE_BASE = 10000.0
SSM_WIDTH = 512
SSM_GROUP = 16
SSM_GROUPS = SSM_WIDTH // SSM_GROUP
SSM_STATE = 64
N_EXPERTS = 64
TOP_K = 6
ROUTED_SCALE = 2.5
N_MOD = 6
LN_EPS = 1e-5

SUBLANES = 8
LANES = 128
VMEM_LIMIT = 48 * 1024 * 1024

INPROJ_ROWS = 512
SSM_STEPS = 64
SSM_STRIPS = 4
MERGE_ROWS = 512
MERGE_SPLIT = 2
DISPATCH_ROWS = 256
GMM_ROWS = 512
GMM_SPLIT = 1
COMBINE_ROWS = 128


def _cparams(sem):
    return pltpu.CompilerParams(dimension_semantics=sem, vmem_limit_bytes=VMEM_LIMIT)


def _mod_kernel(c_ref, w_ref, b_ref, o_ref):
    c = c_ref[...]
    s = c * jax.nn.sigmoid(c)
    o_ref[...] = jnp.dot(s, w_ref[...], preferred_element_type=F32,
                         precision=lax.Precision.HIGHEST) + b_ref[...]


def _modulation(cond, w, b):
    n, d = cond.shape
    nout = w.shape[1]
    tn = 512
    return pl.pallas_call(
        _mod_kernel,
        grid=(nout // tn,),
        in_specs=[pl.BlockSpec((n, d), lambda j: (0, 0)),
                  pl.BlockSpec((d, tn), lambda j: (0, j)),
                  pl.BlockSpec((1, tn), lambda j: (0, j))],
        out_specs=pl.BlockSpec((n, tn), lambda j: (0, j)),
        out_shape=jax.ShapeDtypeStruct((n, nout), F32),
        compiler_params=_cparams(("arbitrary",)),
        name="modulation",
    )(cond, w, b.reshape(1, nout))


def _rope_rotate(t, cos, sin):
    lane = lax.broadcasted_iota(jnp.int32, t.shape, 1)
    partner = jnp.where((lane % 32) < 16, pltpu.roll(t, LANES - 16, 1), pltpu.roll(t, 16, 1))
    return t * cos + partner * sin


def _inproj_kernel(x_ref, mod_ref, w_ref, cos_ref, sin_ref, q_ref, k_ref, v_ref, u_ref, ga_ref, gs_ref, *, rope):
    x = x_ref[...]
    h = (x * (1.0 + mod_ref[1:2, :]) + mod_ref[0:1, :]).astype(BF16)
    p = jnp.dot(h, w_ref[...], preferred_element_type=F32)
    o1 = ATTN_WIDTH
    o2 = o1 + KV_WIDTH
    o3 = o2 + KV_WIDTH
    d = x.shape[1]
    q = p[:, :o1]
    k = p[:, o1:o2]
    if rope:
        cos = cos_ref[...]
        sin = sin_ref[...]
        q = jnp.concatenate([_rope_rotate(q[:, j * LANES:(j + 1) * LANES], cos, sin)
                             for j in range(o1 // LANES)], axis=1)
        k = _rope_rotate(k, cos, sin)
    o4 = o3 + SSM_WIDTH
    q_ref[...] = q.astype(BF16)
    k_ref[...] = k
    v_ref[...] = p[:, o2:o3]
    u_ref[...] = p[:, o3:o4]
    ga_ref[...] = p[:, o4:o4 + d].astype(BF16)
    gs_ref[...] = p[:, o4 + d:].astype(BF16)


def _in_projection(x, mod, w, cos, sin, rope):
    B, L, D = x.shape
    tm = min(INPROJ_ROWS, L)
    per_batch = mod.shape[0] > 1
    nw = w.shape[1]
    row = lambda b, i: (b, i, 0)
    outs = pl.pallas_call(
        functools.partial(_inproj_kernel, rope=rope),
        grid=(B, L // tm),
        in_specs=[pl.BlockSpec((None, tm, D), row),
                  pl.BlockSpec((None, N_MOD, D), (lambda b, i: (b, 0, 0)) if per_batch else (lambda b, i: (0, 0, 0))),
                  pl.BlockSpec((D, nw), lambda b, i: (0, 0)),
                  pl.BlockSpec((tm, LANES), lambda b, i: (i, 0)),
                  pl.BlockSpec((tm, LANES), lambda b, i: (i, 0))],
        out_specs=[pl.BlockSpec((None, tm, ATTN_WIDTH), row),
                   pl.BlockSpec((None, tm, KV_WIDTH), row),
                   pl.BlockSpec((None, tm, KV_WIDTH), row),
                   pl.BlockSpec((None, tm, SSM_WIDTH), row),
                   pl.BlockSpec((None, tm, D), row),
                   pl.BlockSpec((None, tm, D), row)],
        out_shape=[jax.ShapeDtypeStruct((B, L, ATTN_WIDTH), BF16),
                   jax.ShapeDtypeStruct((B, L, KV_WIDTH), F32),
                   jax.ShapeDtypeStruct((B, L, KV_WIDTH), F32),
                   jax.ShapeDtypeStruct((B, L, SSM_WIDTH), F32),
                   jax.ShapeDtypeStruct((B, L, D), BF16),
                   jax.ShapeDtypeStruct((B, L, D), BF16)],
        compiler_params=_cparams(("parallel", "parallel")),
        name="in_projection",
    )(x, mod, w, cos, sin)
    return outs


def _rope_cos_sin(n_tokens):
    t = jnp.arange(n_tokens, dtype=jnp.int32)
    pos = jnp.stack([t // GRID_W, t % GRID_W], axis=-1).astype(F32)
    n_freq = HEAD_DIM // 4
    inv_freq = ROPE_BASE ** (-jnp.arange(n_freq, dtype=F32) / n_freq)
    ang = pos[:, :, None] * inv_freq
    c, s = jnp.cos(ang), jnp.sin(ang)
    cos = jnp.concatenate([c[:, 0], c[:, 0], c[:, 1], c[:, 1]], axis=-1)
    sin = jnp.concatenate([-s[:, 0], s[:, 0], -s[:, 1], s[:, 1]], axis=-1)
    return jnp.tile(cos, (1, LANES // HEAD_DIM)), jnp.tile(sin, (1, LANES // HEAD_DIM))


def _attend(sink_ref, q, kcat, vcat, mask, o_ref):
    lq = q.shape[0]
    lane = lax.broadcasted_iota(jnp.int32, (1, LANES), 1)
    low = lane < HEAD_DIM
    k_sw = pltpu.roll(kcat, HEAD_DIM, 1)
    v_sw = pltpu.roll(vcat, HEAD_DIM, 1)
    neg = jnp.finfo(F32).min
    for h in range(N_KV_HEADS):
        keep = low if h == 0 else jnp.logical_not(low)
        kd = jnp.where(keep, kcat, k_sw).astype(BF16)
        vd = jnp.where(keep, vcat, v_sw).astype(BF16)
        qs = []
        for j in range(Q_PER_KV):
            head = h * Q_PER_KV + j
            blk = q[:, (head // 2) * LANES:(head // 2 + 1) * LANES]
            sel = low if head % 2 == 0 else jnp.logical_not(low)
            qs.append(jnp.where(sel, blk, jnp.zeros_like(blk)))
        qstack = jnp.concatenate(qs, axis=0)
        s = lax.dot_general(qstack, kd, (((1,), (1,)), ((), ())), preferred_element_type=F32)
        ps, ls = [], []
        for j in range(Q_PER_KV):
            sj = s[j * lq:(j + 1) * lq]
            if mask is not None:
                sj = jnp.where(mask, sj, neg)
            sink = sink_ref[h * Q_PER_KV + j]
            m = jnp.maximum(jnp.max(sj, axis=1, keepdims=True), sink)
            pj = jnp.exp(sj - m)
            ls.append(jnp.sum(pj, axis=1, keepdims=True) + jnp.exp(sink - m))
            ps.append(pj.astype(BF16))
        o = jnp.dot(jnp.concatenate(ps, axis=0), vd, preferred_element_type=F32)
        for jj in range(Q_PER_KV // 2):
            oe = o[(2 * jj) * lq:(2 * jj + 1) * lq] / ls[2 * jj]
            oo = o[(2 * jj + 1) * lq:(2 * jj + 2) * lq] / ls[2 * jj + 1]
            cb = (h * Q_PER_KV) // 2 + jj
            o_ref[:, cb * LANES:(cb + 1) * LANES] = jnp.where(low, oe, oo).astype(o_ref.dtype)


def _attn_ctx_kernel(sink_ref, q_ref, k_ref, v_ref, o_ref):
    _attend(sink_ref, q_ref[...], k_ref[...], v_ref[...], None, o_ref)


def _attn_lat_kernel(sink_ref, q_ref, kp_ref, kc_ref, kn_ref, vp_ref, vc_ref, vn_ref, kx_ref, vx_ref, o_ref):
    n = pl.program_id(1)
    nb = pl.num_programs(1)
    blk = q_ref.shape[0]
    nctx = kx_ref.shape[0]
    kcat = jnp.concatenate([kp_ref[...], kc_ref[...], kn_ref[...], kx_ref[...]], axis=0)
    vcat = jnp.concatenate([vp_ref[...], vc_ref[...], vn_ref[...], vx_ref[...]], axis=0)
    lk = 3 * blk + nctx
    qi = lax.broadcasted_iota(jnp.int32, (blk, lk), 0)
    kj = lax.broadcasted_iota(jnp.int32, (blk, lk), 1)
    rel = kj - blk - qi
    in_band = (rel <= WINDOW) & (rel >= -WINDOW)
    in_seq = ((kj >= blk) | (n > 0)) & ((kj < 2 * blk) | (n < nb - 1))
    mask = (kj >= 3 * blk) | (in_band & in_seq)
    _attend(sink_ref, q_ref[...], kcat, vcat, mask, o_ref)


def _attention_context(q, k, v, sink):
    B, L, _ = q.shape
    row = lambda b: (b, 0, 0)
    return pl.pallas_call(
        _attn_ctx_kernel,
        grid=(B,),
        in_specs=[pl.BlockSpec(memory_space=pltpu.SMEM),
                  pl.BlockSpec((None, L, ATTN_WIDTH), row),
                  pl.BlockSpec((None, L, KV_WIDTH), row),
                  pl.BlockSpec((None, L, KV_WIDTH), row)],
        out_specs=pl.BlockSpec((None, L, ATTN_WIDTH), row),
        out_shape=jax.ShapeDtypeStruct((B, L, ATTN_WIDTH), BF16),
        compiler_params=_cparams(("parallel",)),
        name="attention_context",
    )(sink, q, k, v)


def _attention_latent(q, k, v, k_ctx, v_ctx, sink):
    B, S, _ = q.shape
    blk = WINDOW
    nb = S // blk
    nctx = k_ctx.shape[1]
    cur = lambda b, n: (b, n, 0)
    prv = lambda b, n: (b, jnp.maximum(n - 1, 0), 0)
    nxt = lambda b, n: (b, jnp.minimum(n + 1, nb - 1), 0)
    ctx = lambda b, n: (b, 0, 0)
    kv = lambda im: pl.BlockSpec((None, blk, KV_WIDTH), im)
    return pl.pallas_call(
        _attn_lat_kernel,
        grid=(B, nb),
        in_specs=[pl.BlockSpec(memory_space=pltpu.SMEM),
                  pl.BlockSpec((None, blk, ATTN_WIDTH), cur),
                  kv(prv), kv(cur), kv(nxt), kv(prv), kv(cur), kv(nxt),
                  pl.BlockSpec((None, nctx, KV_WIDTH), ctx),
                  pl.BlockSpec((None, nctx, KV_WIDTH), ctx)],
        out_specs=pl.BlockSpec((None, blk, ATTN_WIDTH), cur),
        out_shape=jax.ShapeDtypeStruct((B, S, ATTN_WIDTH), BF16),
        compiler_params=_cparams(("parallel", "parallel")),
        name="attention_latent",
    )(sink, q, k, k, k, v, v, v, k_ctx, v_ctx)


def _ssm_kernel(u_ref, wb_ref, wc_ref, a_ref, d_ref, h0_ref, y_ref, fs_ref, ut_ref, yt_ref, st_ref, *bu_refs):
    rev = pl.program_id(0)
    i = pl.program_id(2)
    nc = pl.num_programs(2)
    nseq, tm, _ = u_ref.shape
    sw = a_ref.shape[-1] // SSM_STRIPS
    nre = sw // 2

    @pl.when(i == 0)
    def _():
        st_ref[...] = h0_ref[...]

    for b in range(nseq):
        ub = u_ref[b]
        for s in range(SSM_STRIPS):
            ut_ref[s, pl.ds(b, tm, stride=nseq), :] = ub[:, s * LANES:(s + 1) * LANES]
    for s in range(SSM_STRIPS):
        bu_refs[s][...] = jnp.dot(ut_ref[s].astype(BF16), wb_ref[s], preferred_element_type=F32)

    for s in range(SSM_STRIPS):
        bu_ref = bu_refs[s]
        a_re = a_ref[:, s * sw:s * sw + nre]
        a_im = a_ref[:, s * sw + nre:(s + 1) * sw]

        def step(t, carry):
            xr, xi = carry
            r = pl.multiple_of((t + rev * (tm - 1 - 2 * t)) * nseq, nseq)
            nr = a_re * xr - a_im * xi + bu_ref[pl.ds(r, nseq), 0:nre]
            ni = a_re * xi + a_im * xr + bu_ref[pl.ds(r, nseq), nre:sw]
            bu_ref[pl.ds(r, nseq), 0:nre] = nr
            bu_ref[pl.ds(r, nseq), nre:sw] = ni
            return nr, ni

        xr, xi = lax.fori_loop(0, tm, step, (st_ref[:, s * sw:s * sw + nre], st_ref[:, s * sw + nre:(s + 1) * sw]),
                               unroll=True)
        st_ref[:, s * sw:s * sw + nre] = xr
        st_ref[:, s * sw + nre:(s + 1) * sw] = xi
        yt_ref[s] = jnp.dot(bu_ref[...].astype(BF16), wc_ref[s], preferred_element_type=F32)

    for b in range(nseq):
        yb = jnp.concatenate([yt_ref[s, pl.ds(b, tm, stride=nseq), :] for s in range(SSM_STRIPS)], axis=1)
        y_ref[b] = yb + u_ref[b] * d_ref[...]

    @pl.when(i == nc - 1)
    def _():
        fs_ref[...] = st_ref[...]


def _ssm_mix(u, ssm, h0):
    B, L, _ = u.shape
    nseq = SUBLANES
    tm = min(SSM_STEPS, L)
    nc = L // tm
    rows = nseq * tm
    ns = 2 * SSM_GROUPS * SSM_STATE
    chunk = lambda d, g, i: i + d * (nc - 1 - 2 * i)
    y, fs = pl.pallas_call(
        _ssm_kernel,
        grid=(2, B // nseq, nc),
        in_specs=[pl.BlockSpec((nseq, tm, SSM_WIDTH), lambda d, g, i: (g, chunk(d, g, i), 0)),
                  pl.BlockSpec((None, SSM_STRIPS, LANES, ns // SSM_STRIPS), lambda d, g, i: (d, 0, 0, 0)),
                  pl.BlockSpec((None, SSM_STRIPS, ns // SSM_STRIPS, LANES), lambda d, g, i: (d, 0, 0, 0)),
                  pl.BlockSpec((None, nseq, ns), lambda d, g, i: (d, 0, 0)),
                  pl.BlockSpec((None, 1, SSM_WIDTH), lambda d, g, i: (d, 0, 0)),
                  pl.BlockSpec((None, nseq, ns), lambda d, g, i: (d, g, 0))],
        out_specs=[pl.BlockSpec((None, nseq, tm, SSM_WIDTH), lambda d, g, i: (d, g, chunk(d, g, i), 0)),
                   pl.BlockSpec((None, nseq, ns), lambda d, g, i: (d, g, 0))],
        out_shape=[jax.ShapeDtypeStruct((2, B, L, SSM_WIDTH), F32),
                   jax.ShapeDtypeStruct((2, B, ns), F32)],
        scratch_shapes=[pltpu.VMEM((SSM_STRIPS, rows, LANES), F32), pltpu.VMEM((SSM_STRIPS, rows, LANES), F32),
                        pltpu.VMEM((nseq, ns), F32)]
        + [pltpu.VMEM((rows, ns // SSM_STRIPS), F32) for _ in range(SSM_STRIPS)],
        compiler_params=_cparams(("arbitrary", "arbitrary", "arbitrary")),
        name="ssm_scan",
    )(u, ssm["wb"], ssm["wc"], ssm["a"], ssm["d"], h0)
    return y, fs


def _state_to_lanes(s):
    lead = s.shape[:-3]
    s = s.reshape(lead + (2, SSM_STRIPS, SSM_GROUPS // SSM_STRIPS, SSM_STATE))
    s = jnp.swapaxes(s, -4, -3)
    return s.reshape(lead + (2 * SSM_GROUPS * SSM_STATE,))


def _lanes_to_state(v):
    lead = v.shape[:-1]
    s = v.reshape(lead + (SSM_STRIPS, 2, SSM_GROUPS // SSM_STRIPS, SSM_STATE))
    s = jnp.swapaxes(s, -4, -3)
    return s.reshape(lead + (2, SSM_GROUPS, SSM_STATE))


def _ssm_params(a_re, a_im, log_dt, b_re, b_im, c_re, c_im, dvec):
    G, N, C = SSM_GROUPS, SSM_STATE, SSM_GROUP
    lam_re = jnp.minimum(a_re, -1e-4)
    lam_im = a_im
    dt = jnp.exp(log_dt)[..., None]
    mag = jnp.exp(lam_re * dt)
    abar_re, abar_im = mag * jnp.cos(lam_im * dt), mag * jnp.sin(lam_im * dt)
    den = jnp.square(lam_re) + jnp.square(lam_im)
    p, qi = abar_re - 1.0, abar_im
    f_re = (p * lam_re + qi * lam_im) / den
    f_im = (qi * lam_re - p * lam_im) / den
    bbar_re = f_re[..., None] * b_re - f_im[..., None] * b_im
    bbar_im = f_re[..., None] * b_im + f_im[..., None] * b_re
    a = _state_to_lanes(jnp.stack([abar_re, abar_im], axis=1))
    a = jnp.broadcast_to(a[:, None, :], (2, SUBLANES, a.shape[-1]))
    S = SSM_STRIPS
    gs = G // S
    eye = jnp.eye(gs, dtype=F32)

    def bd_in(bb):
        bb = bb.reshape(2, S, gs, N, C)
        return jnp.einsum('dkgnc,gh->dkgchn', bb, eye).reshape(2, S, gs * C, gs * N)

    def bd_out(cc):
        cc = cc.reshape(2, S, gs, C, N)
        return jnp.einsum('dkgcn,gh->dkgnhc', cc, eye).reshape(2, S, gs * N, gs * C)

    wb = jnp.concatenate([bd_in(bbar_re), bd_in(bbar_im)], axis=-1).astype(BF16)
    wc = jnp.concatenate([bd_out(c_re), -bd_out(c_im)], axis=-2).astype(BF16)
    d = jnp.stack([dvec, jnp.zeros_like(dvec)], axis=0).reshape(2, 1, SSM_WIDTH)
    return {"a": a, "wb": wb, "wc": wc, "d": d}


def _layer_norm(x, g, b):
    mu = jnp.mean(x, axis=-1, keepdims=True)
    xc = x - mu
    var = jnp.mean(xc * xc, axis=-1, keepdims=True)
    return xc * lax.rsqrt(var + LN_EPS) * g + b


def _pack_rows(x):
    w = x.shape[1] // 2
    return pltpu.pack_elementwise([x[:, :w], x[:, w:]], packed_dtype=BF16)


def _unpack_rows(p):
    return (pltpu.unpack_elementwise(p, index=0, packed_dtype=BF16, unpacked_dtype=F32),
            pltpu.unpack_elementwise(p, index=1, packed_dtype=BF16, unpacked_dtype=F32))


def _gelu_tanh(x):
    return 0.5 * x * (1.0 + jnp.tanh(math.sqrt(2.0 / math.pi) * (x + 0.044715 * (x * x * x))))


def _merge_kernel(x_ref, ao_ref, yf_ref, yb_ref, ga_ref, gs_ref, mod_ref, wglu_ref, wa_ref, ws_ref, wo_ref,
                  lng_ref, lnb_ref, rwt_ref, rb_ref, x1_ref, h2_ref, idx_ref, rank_ref, wt_ref, cnt_ref,
                  run_ref, *, alpha):
    @pl.when((pl.program_id(0) == 0) & (pl.program_id(1) == 0))
    def _():
        run_ref[...] = jnp.zeros_like(run_ref)

    def mix_rows(rs):
        z = _gelu_tanh(yf_ref[rs, :] + yb_ref[rs, :])
        gate = jax.nn.sigmoid(jnp.dot(z.astype(BF16), wglu_ref[...], preferred_element_type=F32))
        ssm_o = (z * gate).astype(BF16)
        merged = (jax.nn.sigmoid(ga_ref[rs, :].astype(F32))
                  * jnp.dot(ao_ref[rs, :], wa_ref[...], preferred_element_type=F32)
                  + jax.nn.sigmoid(gs_ref[rs, :].astype(F32))
                  * jnp.dot(ssm_o, ws_ref[...], preferred_element_type=F32))
        mix = jnp.dot(merged.astype(BF16), wo_ref[...], preferred_element_type=F32)
        x1 = _layer_norm(alpha * x_ref[rs, :] + mod_ref[2:3, :] * mix, lng_ref[...], lnb_ref[...])
        h2 = x1 * (1.0 + mod_ref[4:5, :]) + mod_ref[3:4, :]
        x1_ref[rs, :] = x1
        h2_ref[rs, :] = _pack_rows(h2)
        return h2

    nrow = x_ref.shape[0]
    sub = nrow // MERGE_SPLIT
    h2 = jnp.concatenate([mix_rows(slice(j * sub, (j + 1) * sub)) for j in range(MERGE_SPLIT)], axis=0)

    logits = lax.dot_general(rwt_ref[...], h2, (((1,), (1,)), ((), ())), preferred_element_type=F32,
                             precision=lax.Precision.HIGHEST)
    score = jax.nn.sigmoid(logits)
    tm = score.shape[1]
    eidx = lax.broadcasted_iota(jnp.int32, score.shape, 0).astype(F32)
    work = score + rb_ref[...]
    picks, sel = [], []
    member = jnp.zeros_like(score)
    for _ in range(TOP_K):
        best = jnp.max(work, axis=0, keepdims=True)
        pick = jnp.min(jnp.where(work == best, eidx, float(N_EXPERTS)), axis=0, keepdims=True)
        hit = eidx == pick
        picks.append(pick)
        sel.append(jnp.sum(jnp.where(hit, score, 0.0), axis=0, keepdims=True))
        member = member + hit.astype(F32)
        work = jnp.where(hit, -jnp.inf, work)
    total = sel[0]
    for s in sel[1:]:
        total = total + s
    before = (lax.broadcasted_iota(jnp.int32, (tm, tm), 0) < lax.broadcasted_iota(jnp.int32, (tm, tm), 1))
    prefix = jnp.dot(member.astype(BF16), before.astype(BF16), preferred_element_type=F32)
    base = prefix + run_ref[...]
    for k in range(TOP_K):
        idx_ref[k:k + 1, :] = picks[k].astype(jnp.int32)
        rank_ref[k:k + 1, :] = jnp.sum(jnp.where(eidx == picks[k], base, 0.0), axis=0,
                                       keepdims=True).astype(jnp.int32)
        wt_ref[k:k + 1, :] = sel[k] / total * ROUTED_SCALE
    run_ref[...] = run_ref[...] + jnp.sum(member, axis=1, keepdims=True)
    cnt_ref[...] = run_ref[...].astype(jnp.int32)


def _merge(x, attn_o, y, ga, gs, mod, wts, alpha):
    B, L, D = x.shape
    tm = min(MERGE_ROWS, L)
    per_batch = mod.shape[0] > 1
    row = lambda b, i: (b, i, 0)
    full = lambda b, i: (0, 0)
    wspec = lambda a: pl.BlockSpec(a.shape, full)
    return pl.pallas_call(
        functools.partial(_merge_kernel, alpha=alpha),
        grid=(B, L // tm),
        in_specs=[pl.BlockSpec((None, tm, D), row),
                  pl.BlockSpec((None, tm, ATTN_WIDTH), row),
                  pl.BlockSpec((None, None, tm, SSM_WIDTH), lambda b, i: (0, b, i, 0)),
                  pl.BlockSpec((None, None, tm, SSM_WIDTH), lambda b, i: (1, b, i, 0)),
                  pl.BlockSpec((None, tm, D), row),
                  pl.BlockSpec((None, tm, D), row),
                  pl.BlockSpec((None, N_MOD, D), (lambda b, i: (b, 0, 0)) if per_batch else (lambda b, i: (0, 0, 0))),
                  wspec(wts["w_glu"]), wspec(wts["w_attn_br"]), wspec(wts["w_ssm_br"]), wspec(wts["w_out"]),
                  wspec(wts["ln1_g"]), wspec(wts["ln1_b"]), wspec(wts["router_wt"]), wspec(wts["router_b"])],
        out_specs=[pl.BlockSpec((None, tm, D), row),
                   pl.BlockSpec((None, tm, D // 2), row),
                   pl.BlockSpec((None, TOP_K, tm), lambda b, i: (b, 0, i)),
                   pl.BlockSpec((None, TOP_K, tm), lambda b, i: (b, 0, i)),
                   pl.BlockSpec((None, TOP_K, tm), lambda b, i: (b, 0, i)),
                   pl.BlockSpec((N_EXPERTS, 1), full)],
        out_shape=[jax.ShapeDtypeStruct((B, L, D), F32),
                   jax.ShapeDtypeStruct((B, L, D // 2), jnp.int32),
                   jax.ShapeDtypeStruct((B, TOP_K, L), jnp.int32),
                   jax.ShapeDtypeStruct((B, TOP_K, L), jnp.int32),
                   jax.ShapeDtypeStruct((B, TOP_K, L), F32),
                   jax.ShapeDtypeStruct((N_EXPERTS, 1), jnp.int32)],
        scratch_shapes=[pltpu.VMEM((N_EXPERTS, 1), F32)],
        compiler_params=_cparams(("arbitrary", "arbitrary")),
        name="merge_ln1_router",
    )(x, attn_o, y, y, ga, gs, mod, wts["w_glu"], wts["w_attn_br"], wts["w_ssm_br"], wts["w_out"],
      wts["ln1_g"], wts["ln1_b"], wts["router_wt"], wts["router_b"])


def _dispatch_kernel(dest_ref, h_ref, xs_ref, sem):
    tm = h_ref.shape[0]

    def start(r, c):
        for k in range(TOP_K):
            pltpu.make_async_copy(h_ref.at[pl.ds(r, 1)], xs_ref.at[pl.ds(dest_ref[0, r * TOP_K + k], 1)], sem).start()
        return c

    lax.fori_loop(0, tm, start, 0, unroll=4)
    for k in range(TOP_K):
        pltpu.make_async_copy(h_ref, xs_ref.at[pl.ds(0, tm)], sem).wait()


def _token_major(a, tm):
    B, K, L = a.shape
    return jnp.swapaxes(a, 1, 2).reshape(B * (L // tm), 1, tm * K)


def _dispatch(h2, dest):
    B, L, W = h2.shape
    tm = min(DISPATCH_ROWS, L)
    nt = B * L // tm
    return pl.pallas_call(
        _dispatch_kernel,
        grid=(nt,),
        in_specs=[pl.BlockSpec((None, 1, tm * TOP_K), lambda i: (i, 0, 0), memory_space=pltpu.SMEM),
                  pl.BlockSpec((tm, W), lambda i: (i, 0))],
        out_specs=pl.BlockSpec(memory_space=pl.ANY),
        out_shape=jax.ShapeDtypeStruct((B * L * TOP_K, W), h2.dtype),
        scratch_shapes=[pltpu.SemaphoreType.DMA],
        compiler_params=_cparams(("arbitrary",)),
        name="moe_dispatch",
    )(_token_major(dest, tm), h2.reshape(B * L, W))


def _gmm_kernel(tile_ref, exp_ref, valid_ref, gstart_ref, xs_ref, wg_ref, wu_ref, wd_ref, ys_ref,
                wgb_ref, wub_ref, wdb_ref):
    w = pl.program_id(0)
    e = exp_ref[w]
    t = tile_ref[w]
    prev = jnp.maximum(w - 1, 0)
    new_expert = (w == 0) | (e != exp_ref[prev])
    first_visit = (w == 0) | (t != tile_ref[prev])

    @pl.when(new_expert)
    def _():
        wgb_ref[...] = wg_ref[...].astype(BF16)
        wub_ref[...] = wu_ref[...].astype(BF16)
        wdb_ref[...] = wd_ref[...].astype(BF16)

    @pl.when(valid_ref[w] == 1)
    def _():
        tg = xs_ref.shape[0]
        sub = tg // GMM_SPLIT

        def expert_rows(rs):
            x = jnp.concatenate(_unpack_rows(xs_ref[rs, :]), axis=1).astype(BF16)
            g = jnp.dot(x, wgb_ref[...], preferred_element_type=F32)
            u = jnp.dot(x, wub_ref[...], preferred_element_type=F32)
            a = (g * jax.nn.sigmoid(g) * u).astype(BF16)
            return _pack_rows(jnp.dot(a, wdb_ref[...], preferred_element_type=F32))

        y = jnp.concatenate([expert_rows(slice(j * sub, (j + 1) * sub)) for j in range(GMM_SPLIT)], axis=0)
        rows = t * tg + lax.broadcasted_iota(jnp.int32, (tg, 1), 0)
        mine = (rows >= gstart_ref[e]) & (rows < gstart_ref[e + 1])

        @pl.when(first_visit)
        def _():
            ys_ref[...] = jnp.where(mine, y, 0)

        @pl.when(jnp.logical_not(first_visit))
        def _():
            ys_ref[...] = jnp.where(mine, y, ys_ref[...])


def _grouped_experts(xs, counts, w_gate, w_up, w_down):
    A, W = xs.shape
    D = w_gate.shape[-2]
    tg = GMM_ROWS
    nt = A // tg
    n_items = nt + N_EXPERTS - 1
    ff = w_gate.shape[-1]
    gend = jnp.cumsum(counts).astype(jnp.int32)
    gstart = jnp.concatenate([jnp.zeros((1,), jnp.int32), gend])
    first_row = jnp.arange(nt, dtype=jnp.int32) * tg
    count_le = lambda ends, v: jnp.sum((ends[None, :] <= v[:, None]).astype(jnp.int32), axis=1)
    e_lo = count_le(gend, first_row)
    e_hi = count_le(gend, first_row + tg - 1)
    per_tile = e_hi - e_lo + 1
    item_end = jnp.cumsum(per_tile).astype(jnp.int32)
    total = item_end[-1]
    wi = jnp.arange(n_items, dtype=jnp.int32)
    tile = jnp.minimum(count_le(item_end, wi), nt - 1)
    in_tile = (tile[:, None] == jnp.arange(nt, dtype=jnp.int32)[None, :]).astype(jnp.int32)
    lookup = lambda table: jnp.sum(in_tile * table[None, :], axis=1)
    expert = lookup(e_lo) + wi - lookup(item_end - per_tile)
    valid = (wi < total).astype(jnp.int32)
    expert = jnp.where(valid == 1, expert, e_hi[nt - 1])
    grid_spec = pltpu.PrefetchScalarGridSpec(
        num_scalar_prefetch=4,
        grid=(n_items,),
        in_specs=[pl.BlockSpec((tg, W), lambda w, tl, ex, va, gs: (tl[w], 0)),
                  pl.BlockSpec((None, D, ff), lambda w, tl, ex, va, gs: (ex[w], 0, 0)),
                  pl.BlockSpec((None, D, ff), lambda w, tl, ex, va, gs: (ex[w], 0, 0)),
                  pl.BlockSpec((None, ff, D), lambda w, tl, ex, va, gs: (ex[w], 0, 0))],
        out_specs=pl.BlockSpec((tg, W), lambda w, tl, ex, va, gs: (tl[w], 0)),
        scratch_shapes=[pltpu.VMEM((D, ff), BF16), pltpu.VMEM((D, ff), BF16), pltpu.VMEM((ff, D), BF16)],
    )
    return pl.pallas_call(
        _gmm_kernel,
        grid_spec=grid_spec,
        out_shape=jax.ShapeDtypeStruct((A, W), xs.dtype),
        compiler_params=_cparams(("arbitrary",)),
        name="moe_grouped_experts",
    )(tile, expert, valid, gstart, xs, w_gate, w_up, w_down)


def _combine_kernel(dcur_ref, dnxt_ref, wt_ref, x1_ref, h2_ref, mod_ref, sg_ref, su_ref, sd_ref, lng_ref, lnb_ref,
                    ys_ref, o_ref, buf_ref, sem, *, alpha):
    i = pl.program_id(0)
    n = pl.num_programs(0)
    tm = x1_ref.shape[0]
    slot = i % 2

    def row_gather(dest_ref, s, r):
        for k in range(TOP_K):
            pltpu.make_async_copy(ys_ref.at[pl.ds(dest_ref[0, r * TOP_K + k], 1)],
                                  buf_ref.at[s, k, pl.ds(r, 1)], sem.at[s]).start()

    @pl.when(i == 0)
    def _():
        def start(r, c):
            row_gather(dcur_ref, slot, r)
            return c

        lax.fori_loop(0, tm, start, 0, unroll=4)

    for r in range(tm):
        row_gather(dnxt_ref, 1 - slot, r)

    h = jnp.concatenate(_unpack_rows(h2_ref[...]), axis=1).astype(BF16)
    g = jnp.dot(h, sg_ref[...], preferred_element_type=F32)
    u = jnp.dot(h, su_ref[...], preferred_element_type=F32)
    moe = jnp.dot((g * jax.nn.sigmoid(g) * u).astype(BF16), sd_ref[...], preferred_element_type=F32)
    for k in range(TOP_K):
        pltpu.make_async_copy(ys_ref.at[pl.ds(0, tm)], buf_ref.at[slot, k], sem.at[slot]).wait()
    wt = wt_ref[...]
    lo = jnp.zeros((tm, buf_ref.shape[-1]), F32)
    hi = lo
    for k in range(TOP_K):
        rl, rh = _unpack_rows(buf_ref[slot, k])
        lo = lo + wt[:, k:k + 1] * rl
        hi = hi + wt[:, k:k + 1] * rh
    moe = moe + jnp.concatenate([lo, hi], axis=1)
    o_ref[...] = _layer_norm(alpha * x1_ref[...] + mod_ref[0, 5:6, :] * moe, lng_ref[...], lnb_ref[...])

    @pl.when(i == n - 1)
    def _():
        for k in range(TOP_K):
            pltpu.make_async_copy(ys_ref.at[pl.ds(0, tm)], buf_ref.at[1 - slot, k], sem.at[1 - slot]).wait()


def _combine(ys, dest, wts, x1, h2, mod, shared, alpha):
    B, L, D = x1.shape
    W = ys.shape[-1]
    tm = min(COMBINE_ROWS, L)
    nl = L // tm
    nt = B * nl
    per_batch = mod.shape[0] > 1
    row = lambda i: (i, 0)
    full = lambda i: (0, 0)
    wspec = lambda a: pl.BlockSpec(a.shape, full)
    slots = _token_major(dest, tm)
    return pl.pallas_call(
        functools.partial(_combine_kernel, alpha=alpha),
        grid=(nt,),
        in_specs=[pl.BlockSpec((None, 1, tm * TOP_K), lambda i: (i, 0, 0), memory_space=pltpu.SMEM),
                  pl.BlockSpec((None, 1, tm * TOP_K), lambda i: (jnp.minimum(i + 1, nt - 1), 0, 0),
                               memory_space=pltpu.SMEM),
                  pl.BlockSpec((tm, TOP_K), row),
                  pl.BlockSpec((tm, D), row),
                  pl.BlockSpec((tm, W), row),
                  pl.BlockSpec((1, N_MOD, D), (lambda i: (i // nl, 0, 0)) if per_batch else (lambda i: (0, 0, 0))),
                  wspec(shared["sh_w_gate"]), wspec(shared["sh_w_up"]), wspec(shared["sh_w_down"]),
                  wspec(shared["ln2_g"]), wspec(shared["ln2_b"]),
                  pl.BlockSpec(memory_space=pl.ANY)],
        out_specs=pl.BlockSpec((tm, D), row),
        out_shape=jax.ShapeDtypeStruct((B * L, D), F32),
        scratch_shapes=[pltpu.VMEM((2, TOP_K, tm, W), ys.dtype), pltpu.SemaphoreType.DMA((2,))],
        compiler_params=_cparams(("arbitrary",)),
        name="moe_combine_ln2",
    )(slots, slots, jnp.swapaxes(wts, 1, 2).reshape(B * L, TOP_K), x1.reshape(B * L, D), h2.reshape(B * L, W),
      mod, shared["sh_w_gate"], shared["sh_w_up"], shared["sh_w_down"], shared["ln2_g"], shared["ln2_b"],
      ys).reshape(B, L, D)


def _slots(idx, rank, counts):
    start = jnp.cumsum(counts) - counts
    pick = idx[..., None] == jnp.arange(N_EXPERTS, dtype=jnp.int32)
    return jnp.sum(jnp.where(pick, start, 0), axis=-1).astype(jnp.int32) + rank


def _run_group(x, mod, wts, ssm, cache, alpha):
    B, L, D = x.shape
    latent = cache is not None
    if latent:
        k_ctx, v_ctx, h0 = cache
        cos, sin = _rope_cos_sin(L)
    else:
        cos = jnp.zeros((L, LANES), F32)
        sin = cos
        h0 = jnp.zeros((2, B, 2 * SSM_GROUPS * SSM_STATE), F32)
    q, k, v, u, ga, gs = _in_projection(x, mod, wts["w_in"], cos, sin, rope=latent)
    if latent:
        attn_o = _attention_latent(q, k, v, k_ctx, v_ctx, wts["attn_sink"])
    else:
        attn_o = _attention_context(q, k, v, wts["attn_sink"])
    y, fs = _ssm_mix(u, ssm, h0)
    x1, h2, idx, rank, rw, counts = _merge(x, attn_o, y, ga, gs, mod, wts, alpha)
    counts = counts.reshape(N_EXPERTS)
    dest = _slots(idx, rank, counts)
    xs = _dispatch(h2, dest)
    ys = _grouped_experts(xs, counts, wts["exp_w_gate"], wts["exp_w_up"], wts["exp_w_down"])
    out = _combine(ys, dest, rw, x1, h2, mod, wts, alpha)
    return out, k, v, fs


def kernel(x_prompt, x_sample, cache_k, cache_v, state_ssm, c, c_ctx, mod_w, mod_b, w_in, attn_sink, w_attn_br, ssm_a_re, ssm_a_im, ssm_log_dt, ssm_b_re, ssm_b_im, ssm_c_re, ssm_c_im, ssm_d, w_glu, w_ssm_br, w_out, ln1_g, ln1_b, ln2_g, ln2_b, router_w, router_b, exp_w_gate, exp_w_up, exp_w_down, sh_w_gate, sh_w_up, sh_w_down):
    depth = w_in.shape[0]
    assert depth == 1
    alpha = (2.0 * depth) ** 0.25
    D = x_prompt.shape[-1]
    nb_p = x_prompt.shape[0]
    nb_s = x_sample.shape[0]
    l = 0

    ncond = 1 + nb_s
    npad = -ncond % SUBLANES
    cond = jnp.concatenate([c_ctx[None, :], c, jnp.zeros((npad, D), F32)], axis=0)
    mod = _modulation(cond, mod_w[l], mod_b[l]).reshape(ncond + npad, N_MOD, D)
    mod_ctx, mod_lat = mod[0:1], mod[1:ncond]

    o1 = ATTN_WIDTH
    o3 = o1 + 2 * KV_WIDTH
    o4 = o3 + SSM_WIDTH
    wi = w_in[l]
    scale = HEAD_DIM ** -0.5
    wts = {
        "w_in": jnp.concatenate([wi[:, :o1] * scale, wi[:, o1:]], axis=1).astype(BF16),
        "attn_sink": attn_sink[l],
        "w_glu": w_glu[l].astype(BF16), "w_attn_br": w_attn_br[l].astype(BF16),
        "w_ssm_br": w_ssm_br[l].astype(BF16), "w_out": w_out[l].astype(BF16),
        "ln1_g": ln1_g[l].reshape(1, D), "ln1_b": ln1_b[l].reshape(1, D),
        "ln2_g": ln2_g[l].reshape(1, D), "ln2_b": ln2_b[l].reshape(1, D),
        "router_wt": router_w[l].T, "router_b": router_b[l].reshape(N_EXPERTS, 1),
        "exp_w_gate": exp_w_gate[l], "exp_w_up": exp_w_up[l], "exp_w_down": exp_w_down[l],
        "sh_w_gate": sh_w_gate[l].astype(BF16), "sh_w_up": sh_w_up[l].astype(BF16),
        "sh_w_down": sh_w_down[l].astype(BF16),
    }
    ssm = _ssm_params(ssm_a_re[l], ssm_a_im[l], ssm_log_dt[l], ssm_b_re[l], ssm_b_im[l],
                      ssm_c_re[l], ssm_c_im[l], ssm_d[l])

    yp, k_p, v_p, fs_p = _run_group(x_prompt, mod_ctx, wts, ssm, None, alpha)

    past = cache_k.shape[2]
    k_ctx = cache_k[:, l].reshape(nb_s, past, KV_WIDTH)
    v_ctx = cache_v[:, l].reshape(nb_s, past, KV_WIDTH)
    h0 = jnp.swapaxes(_state_to_lanes(state_ssm[:, l]), 0, 1)
    ys_, _, _, _ = _run_group(x_sample, mod_lat, wts, ssm, (k_ctx, v_ctx, h0), alpha)

    Lp = x_prompt.shape[1]
    new_k = k_p.reshape(nb_p, 1, Lp, N_KV_HEADS, HEAD_DIM)
    new_v = v_p.reshape(nb_p, 1, Lp, N_KV_HEADS, HEAD_DIM)
    new_s = _lanes_to_state(jnp.swapaxes(fs_p, 0, 1))[:, None]
    return (yp, ys_, new_k, new_v, new_s)
```

```python
import functools
import math

import numpy as np
import jax
import jax.numpy as jnp
from jax import lax
from jax.experimental import pallas as pl
from jax.experimental.pallas import tpu as pltpu

F32 = jnp.float32
BF16 = jnp.bfloat16

GRID_W = 64
HEAD_DIM = 64
N_Q_HEADS = 8
N_KV_HEADS = 2
Q_PER_KV = N_Q_HEADS // N_KV_HEADS
ATTN_WIDTH = N_Q_HEADS * HEAD_DIM
KV_WIDTH = N_KV_HEADS * HEAD_DIM
WINDOW = 128
ROPE_BASE = 10000.0
SSM_WIDTH = 512
SSM_GROUP = 16
SSM_GROUPS = SSM_WIDTH // SSM_GROUP
SSM_STATE = 64
N_EXPERTS = 64
TOP_K = 6
ROUTED_SCALE = 2.5
N_MOD = 6
LN_EPS = 1e-5

SUBLANES = 8
LANES = 128
VMEM_LIMIT = 48 * 1024 * 1024

INPROJ_ROWS = 512
SSM_STEPS = 64
SSM_STRIPS = 4
MERGE_ROWS = 512
MERGE_SPLIT = 2
DISPATCH_ROWS = 256
GMM_ROWS = 512
GMM_SPLIT = 1
COMBINE_ROWS = 128


def _cparams(sem):
    return pltpu.CompilerParams(dimension_semantics=sem, vmem_limit_bytes=VMEM_LIMIT)


def _mod_kernel(c_ref, w_ref, b_ref, o_ref):
    c = c_ref[...]
    s = c * jax.nn.sigmoid(c)
    o_ref[...] = jnp.dot(s, w_ref[...], preferred_element_type=F32,
                         precision=lax.Precision.HIGHEST) + b_ref[...]


def _modulation(cond, w, b):
    n, d = cond.shape
    nout = w.shape[1]
    tn = 512
    return pl.pallas_call(
        _mod_kernel,
        grid=(nout // tn,),
        in_specs=[pl.BlockSpec((n, d), lambda j: (0, 0)),
                  pl.BlockSpec((d, tn), lambda j: (0, j)),
                  pl.BlockSpec((1, tn), lambda j: (0, j))],
        out_specs=pl.BlockSpec((n, tn), lambda j: (0, j)),
        out_shape=jax.ShapeDtypeStruct((n, nout), F32),
        compiler_params=_cparams(("arbitrary",)),
        name="modulation",
    )(cond, w, b.reshape(1, nout))


def _rope_rotate(t, cos, sin):
    lane = lax.broadcasted_iota(jnp.int32, t.shape, 1)
    partner = jnp.where((lane % 32) < 16, pltpu.roll(t, LANES - 16, 1), pltpu.roll(t, 16, 1))
    return t * cos + partner * sin


def _inproj_kernel(x_ref, mod_ref, w_ref, cos_ref, sin_ref, q_ref, k_ref, v_ref, u_ref, ga_ref, gs_ref, *, rope):
    x = x_ref[...]
    h = (x * (1.0 + mod_ref[1:2, :]) + mod_ref[0:1, :]).astype(BF16)
    p = jnp.dot(h, w_ref[...], preferred_element_type=F32)
    o1 = ATTN_WIDTH
    o2 = o1 + KV_WIDTH
    o3 = o2 + KV_WIDTH
    d = x.shape[1]
    q = p[:, :o1]
    k = p[:, o1:o2]
    if rope:
        cos = cos_ref[...]
        sin = sin_ref[...]
        q = jnp.concatenate([_rope_rotate(q[:, j * LANES:(j + 1) * LANES], cos, sin)
                             for j in range(o1 // LANES)], axis=1)
        k = _rope_rotate(k, cos, sin)
    o4 = o3 + SSM_WIDTH
    q_ref[...] = q.astype(BF16)
    k_ref[...] = k
    v_ref[...] = p[:, o2:o3]
    u_ref[...] = p[:, o3:o4]
    ga_ref[...] = p[:, o4:o4 + d].astype(BF16)
    gs_ref[...] = p[:, o4 + d:].astype(BF16)


def _in_projection(x, mod, w, cos, sin, rope):
    B, L, D = x.shape
    tm = min(INPROJ_ROWS, L)
    per_batch = mod.shape[0] > 1
    nw = w.shape[1]
    row = lambda b, i: (b, i, 0)
    outs = pl.pallas_call(
        functools.partial(_inproj_kernel, rope=rope),
        grid=(B, L // tm),
        in_specs=[pl.BlockSpec((None, tm, D), row),
                  pl.BlockSpec((None, N_MOD, D), (lambda b, i: (b, 0, 0)) if per_batch else (lambda b, i: (0, 0, 0))),
                  pl.BlockSpec((D, nw), lambda b, i: (0, 0)),
                  pl.BlockSpec((tm, LANES), lambda b, i: (i, 0)),
                  pl.BlockSpec((tm, LANES), lambda b, i: (i, 0))],
        out_specs=[pl.BlockSpec((None, tm, ATTN_WIDTH), row),
                   pl.BlockSpec((None, tm, KV_WIDTH), row),
                   pl.BlockSpec((None, tm, KV_WIDTH), row),
                   pl.BlockSpec((None, tm, SSM_WIDTH), row),
                   pl.BlockSpec((None, tm, D), row),
                   pl.BlockSpec((None, tm, D), row)],
        out_shape=[jax.ShapeDtypeStruct((B, L, ATTN_WIDTH), BF16),
                   jax.ShapeDtypeStruct((B, L, KV_WIDTH), F32),
                   jax.ShapeDtypeStruct((B, L, KV_WIDTH), F32),
                   jax.ShapeDtypeStruct((B, L, SSM_WIDTH), F32),
                   jax.ShapeDtypeStruct((B, L, D), BF16),
                   jax.ShapeDtypeStruct((B, L, D), BF16)],
        compiler_params=_cparams(("parallel", "parallel")),
        name="in_projection",
    )(x, mod, w, cos, sin)
    return outs


def _rope_cos_sin(n_tokens):
    t = jnp.arange(n_tokens, dtype=jnp.int32)
    pos = jnp.stack([t // GRID_W, t % GRID_W], axis=-1).astype(F32)
    n_freq = HEAD_DIM // 4
    inv_freq = ROPE_BASE ** (-jnp.arange(n_freq, dtype=F32) / n_freq)
    ang = pos[:, :, None] * inv_freq
    c, s = jnp.cos(ang), jnp.sin(ang)
    cos = jnp.concatenate([c[:, 0], c[:, 0], c[:, 1], c[:, 1]], axis=-1)
    sin = jnp.concatenate([-s[:, 0], s[:, 0], -s[:, 1], s[:, 1]], axis=-1)
    return jnp.tile(cos, (1, LANES // HEAD_DIM)), jnp.tile(sin, (1, LANES // HEAD_DIM))


def _attend(sink_ref, q, kcat, vcat, mask, o_ref):
    lq = q.shape[0]
    lane = lax.broadcasted_iota(jnp.int32, (1, LANES), 1)
    low = lane < HEAD_DIM
    k_sw = pltpu.roll(kcat, HEAD_DIM, 1)
    v_sw = pltpu.roll(vcat, HEAD_DIM, 1)
    neg = jnp.finfo(F32).min
    for h in range(N_KV_HEADS):
        keep = low if h == 0 else jnp.logical_not(low)
        kd = jnp.where(keep, kcat, k_sw).astype(BF16)
        vd = jnp.where(keep, vcat, v_sw).astype(BF16)
        qs = []
        for j in range(Q_PER_KV):
            head = h * Q_PER_KV + j
            blk = q[:, (head // 2) * LANES:(head // 2 + 1) * LANES]
            sel = low if head % 2 == 0 else jnp.logical_not(low)
            qs.append(jnp.where(sel, blk, jnp.zeros_like(blk)))
        qstack = jnp.concatenate(qs, axis=0)
        s = lax.dot_general(qstack, kd, (((1,), (1,)), ((), ())), preferred_element_type=F32)
        ps, ls = [], []
        for j in range(Q_PER_KV):
            sj = s[j * lq:(j + 1) * lq]
            if mask is not None:
                sj = jnp.where(mask, sj, neg)
            sink = sink_ref[h * Q_PER_KV + j]
            m = jnp.maximum(jnp.max(sj, axis=1, keepdims=True), sink)
            pj = jnp.exp(sj - m)
            ls.append(jnp.sum(pj, axis=1, keepdims=True) + jnp.exp(sink - m))
            ps.append(pj.astype(BF16))
        o = jnp.dot(jnp.concatenate(ps, axis=0), vd, preferred_element_type=F32)
        for jj in range(Q_PER_KV // 2):
            oe = o[(2 * jj) * lq:(2 * jj + 1) * lq] / ls[2 * jj]
            oo = o[(2 * jj + 1) * lq:(2 * jj + 2) * lq] / ls[2 * jj + 1]
            cb = (h * Q_PER_KV) // 2 + jj
            o_ref[:, cb * LANES:(cb + 1) * LANES] = jnp.where(low, oe, oo).astype(o_ref.dtype)


def _attn_ctx_kernel(sink_ref, q_ref, k_ref, v_ref, o_ref):
    _attend(sink_ref, q_ref[...], k_ref[...], v_ref[...], None, o_ref)


def _attn_lat_kernel(sink_ref, q_ref, kp_ref, kc_ref, kn_ref, vp_ref, vc_ref, vn_ref, kx_ref, vx_ref, o_ref):
    n = pl.program_id(1)
    nb = pl.num_programs(1)
    blk = q_ref.shape[0]
    nctx = kx_ref.shape[0]
    kcat = jnp.concatenate([kp_ref[...], kc_ref[...], kn_ref[...], kx_ref[...]], axis=0)
    vcat = jnp.concatenate([vp_ref[...], vc_ref[...], vn_ref[...], vx_ref[...]], axis=0)
    lk = 3 * blk + nctx
    qi = lax.broadcasted_iota(jnp.int32, (blk, lk), 0)
    kj = lax.broadcasted_iota(jnp.int32, (blk, lk), 1)
    rel = kj - blk - qi
    in_band = (rel <= WINDOW) & (rel >= -WINDOW)
    in_seq = ((kj >= blk) | (n > 0)) & ((kj < 2 * blk) | (n < nb - 1))
    mask = (kj >= 3 * blk) | (in_band & in_seq)
    _attend(sink_ref, q_ref[...], kcat, vcat, mask, o_ref)


def _attention_context(q, k, v, sink):
    B, L, _ = q.shape
    row = lambda b: (b, 0, 0)
    return pl.pallas_call(
        _attn_ctx_kernel,
        grid=(B,),
        in_specs=[pl.BlockSpec(memory_space=pltpu.SMEM),
                  pl.BlockSpec((None, L, ATTN_WIDTH), row),
                  pl.BlockSpec((None, L, KV_WIDTH), row),
                  pl.BlockSpec((None, L, KV_WIDTH), row)],
        out_specs=pl.BlockSpec((None, L, ATTN_WIDTH), row),
        out_shape=jax.ShapeDtypeStruct((B, L, ATTN_WIDTH), BF16),
        compiler_params=_cparams(("parallel",)),
        name="attention_context",
    )(sink, q, k, v)


def _attention_latent(q, k, v, k_ctx, v_ctx, sink):
    B, S, _ = q.shape
    blk = WINDOW
    nb = S // blk
    nctx = k_ctx.shape[1]
    cur = lambda b, n: (b, n, 0)
    prv = lambda b, n: (b, jnp.maximum(n - 1, 0), 0)
    nxt = lambda b, n: (b, jnp.minimum(n + 1, nb - 1), 0)
    ctx = lambda b, n: (b, 0, 0)
    kv = lambda im: pl.BlockSpec((None, blk, KV_WIDTH), im)
    return pl.pallas_call(
        _attn_lat_kernel,
        grid=(B, nb),
        in_specs=[pl.BlockSpec(memory_space=pltpu.SMEM),
                  pl.BlockSpec((None, blk, ATTN_WIDTH), cur),
                  kv(prv), kv(cur), kv(nxt), kv(prv), kv(cur), kv(nxt),
                  pl.BlockSpec((None, nctx, KV_WIDTH), ctx),
                  pl.BlockSpec((None, nctx, KV_WIDTH), ctx)],
        out_specs=pl.BlockSpec((None, blk, ATTN_WIDTH), cur),
        out_shape=jax.ShapeDtypeStruct((B, S, ATTN_WIDTH), BF16),
        compiler_params=_cparams(("parallel", "parallel")),
        name="attention_latent",
    )(sink, q, k, k, k, v, v, v, k_ctx, v_ctx)


def _ssm_kernel(u_ref, wb_ref, wc_ref, a_ref, d_ref, h0_ref, y_ref, fs_ref, ut_ref, yt_ref, st_ref, *bu_refs):
    rev = pl.program_id(0)
    i = pl.program_id(2)
    nc = pl.num_programs(2)
    nseq, tm, _ = u_ref.shape
    sw = a_ref.shape[-1] // SSM_STRIPS
    nre = sw // 2

    @pl.when(i == 0)
    def _():
        st_ref[...] = h0_ref[...]

    for b in range(nseq):
        ub = u_ref[b]
        for s in range(SSM_STRIPS):
            ut_ref[s, pl.ds(b, tm, stride=nseq), :] = ub[:, s * LANES:(s + 1) * LANES]
    for s in range(SSM_STRIPS):
        bu_refs[s][...] = jnp.dot(ut_ref[s].astype(BF16), wb_ref[s], preferred_element_type=F32)

    for s in range(SSM_STRIPS):
        bu_ref = bu_refs[s]
        a_re = a_ref[:, s * sw:s * sw + nre]
        a_im = a_ref[:, s * sw + nre:(s + 1) * sw]

        def step(t, carry):
            xr, xi = carry
            r = pl.multiple_of((t + rev * (tm - 1 - 2 * t)) * nseq, nseq)
            nr = a_re * xr - a_im * xi + bu_ref[pl.ds(r, nseq), 0:nre]
            ni = a_re * xi + a_im * xr + bu_ref[pl.ds(r, nseq), nre:sw]
            bu_ref[pl.ds(r, nseq), 0:nre] = nr
            bu_ref[pl.ds(r, nseq), nre:sw] = ni
            return nr, ni

        xr, xi = lax.fori_loop(0, tm, step, (st_ref[:, s * sw:s * sw + nre], st_ref[:, s * sw + nre:(s + 1) * sw]),
                               unroll=True)
        st_ref[:, s * sw:s * sw + nre] = xr
        st_ref[:, s * sw + nre:(s + 1) * sw] = xi
        yt_ref[s] = jnp.dot(bu_ref[...].astype(BF16), wc_ref[s], preferred_element_type=F32)

    for b in range(nseq):
        yb = jnp.concatenate([yt_ref[s, pl.ds(b, tm, stride=nseq), :] for s in range(SSM_STRIPS)], axis=1)
        y_ref[b] = yb + u_ref[b] * d_ref[...]

    @pl.when(i == nc - 1)
    def _():
        fs_ref[...] = st_ref[...]


def _ssm_mix(u, ssm, h0):
    B, L, _ = u.shape
    nseq = SUBLANES
    tm = min(SSM_STEPS, L)
    nc = L // tm
    rows = nseq * tm
    ns = 2 * SSM_GROUPS * SSM_STATE
    chunk = lambda d, g, i: i + d * (nc - 1 - 2 * i)
    y, fs = pl.pallas_call(
        _ssm_kernel,
        grid=(2, B // nseq, nc),
        in_specs=[pl.BlockSpec((nseq, tm, SSM_WIDTH), lambda d, g, i: (g, chunk(d, g, i), 0)),
                  pl.BlockSpec((None, SSM_STRIPS, LANES, ns // SSM_STRIPS), lambda d, g, i: (d, 0, 0, 0)),
                  pl.BlockSpec((None, SSM_STRIPS, ns // SSM_STRIPS, LANES), lambda d, g, i: (d, 0, 0, 0)),
                  pl.BlockSpec((None, nseq, ns), lambda d, g, i: (d, 0, 0)),
                  pl.BlockSpec((None, 1, SSM_WIDTH), lambda d, g, i: (d, 0, 0)),
                  pl.BlockSpec((None, nseq, ns), lambda d, g, i: (d, g, 0))],
        out_specs=[pl.BlockSpec((None, nseq, tm, SSM_WIDTH), lambda d, g, i: (d, g, chunk(d, g, i), 0)),
                   pl.BlockSpec((None, nseq, ns), lambda d, g, i: (d, g, 0))],
        out_shape=[jax.ShapeDtypeStruct((2, B, L, SSM_WIDTH), F32),
                   jax.ShapeDtypeStruct((2, B, ns), F32)],
        scratch_shapes=[pltpu.VMEM((SSM_STRIPS, rows, LANES), F32), pltpu.VMEM((SSM_STRIPS, rows, LANES), F32),
                        pltpu.VMEM((nseq, ns), F32)]
        + [pltpu.VMEM((rows, ns // SSM_STRIPS), F32) for _ in range(SSM_STRIPS)],
        compiler_params=_cparams(("arbitrary", "arbitrary", "arbitrary")),
        name="ssm_scan",
    )(u, ssm["wb"], ssm["wc"], ssm["a"], ssm["d"], h0)
    return y, fs


def _state_to_lanes(s):
    lead = s.shape[:-3]
    s = s.reshape(lead + (2, SSM_STRIPS, SSM_GROUPS // SSM_STRIPS, SSM_STATE))
    s = jnp.swapaxes(s, -4, -3)
    return s.reshape(lead + (2 * SSM_GROUPS * SSM_STATE,))


def _lanes_to_state(v):
    lead = v.shape[:-1]
    s = v.reshape(lead + (SSM_STRIPS, 2, SSM_GROUPS // SSM_STRIPS, SSM_STATE))
    s = jnp.swapaxes(s, -4, -3)
    return s.reshape(lead + (2, SSM_GROUPS, SSM_STATE))


def _ssm_params(a_re, a_im, log_dt, b_re, b_im, c_re, c_im, dvec):
    G, N, C = SSM_GROUPS, SSM_STATE, SSM_GROUP
    lam_re = jnp.minimum(a_re, -1e-4)
    lam_im = a_im
    dt = jnp.exp(log_dt)[..., None]
    mag = jnp.exp(lam_re * dt)
    abar_re, abar_im = mag * jnp.cos(lam_im * dt), mag * jnp.sin(lam_im * dt)
    den = jnp.square(lam_re) + jnp.square(lam_im)
    p, qi = abar_re - 1.0, abar_im
    f_re = (p * lam_re + qi * lam_im) / den
    f_im = (qi * lam_re - p * lam_im) / den
    bbar_re = f_re[..., None] * b_re - f_im[..., None] * b_im
    bbar_im = f_re[..., None] * b_im + f_im[..., None] * b_re
    a = _state_to_lanes(jnp.stack([abar_re, abar_im], axis=1))
    a = jnp.broadcast_to(a[:, None, :], (2, SUBLANES, a.shape[-1]))
    S = SSM_STRIPS
    gs = G // S
    eye = jnp.eye(gs, dtype=F32)

    def bd_in(bb):
        bb = bb.reshape(2, S, gs, N, C)
        return jnp.einsum('dkgnc,gh->dkgchn', bb, eye).reshape(2, S, gs * C, gs * N)

    def bd_out(cc):
        cc = cc.reshape(2, S, gs, C, N)
        return jnp.einsum('dkgcn,gh->dkgnhc', cc, eye).reshape(2, S, gs * N, gs * C)

    wb = jnp.concatenate([bd_in(bbar_re), bd_in(bbar_im)], axis=-1).astype(BF16)
    wc = jnp.concatenate([bd_out(c_re), -bd_out(c_im)], axis=-2).astype(BF16)
    d = jnp.stack([dvec, jnp.zeros_like(dvec)], axis=0).reshape(2, 1, SSM_WIDTH)
    return {"a": a, "wb": wb, "wc": wc, "d": d}


def _layer_norm(x, g, b):
    mu = jnp.mean(x, axis=-1, keepdims=True)
    xc = x - mu
    var = jnp.mean(xc * xc, axis=-1, keepdims=True)
    return xc * lax.rsqrt(var + LN_EPS) * g + b


def _pack_rows(x):
    w = x.shape[1] // 2
    return pltpu.pack_elementwise([x[:, :w], x[:, w:]], packed_dtype=BF16)


def _unpack_rows(p):
    return (pltpu.unpack_elementwise(p, index=0, packed_dtype=BF16, unpacked_dtype=F32),
            pltpu.unpack_elementwise(p, index=1, packed_dtype=BF16, unpacked_dtype=F32))


def _gelu_tanh(x):
    return 0.5 * x * (1.0 + jnp.tanh(math.sqrt(2.0 / math.pi) * (x + 0.044715 * (x * x * x))))


def _merge_kernel(x_ref, ao_ref, yf_ref, yb_ref, ga_ref, gs_ref, mod_ref, wglu_ref, wa_ref, ws_ref, wo_ref,
                  lng_ref, lnb_ref, rwt_ref, rb_ref, run0_ref, x1_ref, h2_ref, idx_ref, rank_ref, wt_ref, cnt_ref,
                  run_ref, *, alpha):
    @pl.when((pl.program_id(0) == 0) & (pl.program_id(1) == 0))
    def _():
        run_ref[...] = run0_ref[...].astype(F32)

    def mix_rows(rs):
        z = _gelu_tanh(yf_ref[rs, :] + yb_ref[rs, :])
        gate = jax.nn.sigmoid(jnp.dot(z.astype(BF16), wglu_ref[...], preferred_element_type=F32))
        ssm_o = (z * gate).astype(BF16)
        merged = (jax.nn.sigmoid(ga_ref[rs, :].astype(F32))
                  * jnp.dot(ao_ref[rs, :], wa_ref[...], preferred_element_type=F32)
                  + jax.nn.sigmoid(gs_ref[rs, :].astype(F32))
                  * jnp.dot(ssm_o, ws_ref[...], preferred_element_type=F32))
        mix = jnp.dot(merged.astype(BF16), wo_ref[...], preferred_element_type=F32)
        x1 = _layer_norm(alpha * x_ref[rs, :] + mod_ref[2:3, :] * mix, lng_ref[...], lnb_ref[...])
        h2 = x1 * (1.0 + mod_ref[4:5, :]) + mod_ref[3:4, :]
        x1_ref[rs, :] = x1
        h2_ref[rs, :] = _pack_rows(h2)
        return h2

    nrow = x_ref.shape[0]
    sub = nrow // MERGE_SPLIT
    h2 = jnp.concatenate([mix_rows(slice(j * sub, (j + 1) * sub)) for j in range(MERGE_SPLIT)], axis=0)

    logits = lax.dot_general(rwt_ref[...], h2, (((1,), (1,)), ((), ())), preferred_element_type=F32,
                             precision=lax.Precision.HIGHEST)
    score = jax.nn.sigmoid(logits)
    tm = score.shape[1]
    eidx = lax.broadcasted_iota(jnp.int32, score.shape, 0).astype(F32)
    work = score + rb_ref[...]
    picks, sel = [], []
    member = jnp.zeros_like(score)
    for _ in range(TOP_K):
        best = jnp.max(work, axis=0, keepdims=True)
        pick = jnp.min(jnp.where(work == best, eidx, float(N_EXPERTS)), axis=0, keepdims=True)
        hit = eidx == pick
        picks.append(pick)
        sel.append(jnp.sum(jnp.where(hit, score, 0.0), axis=0, keepdims=True))
        member = member + hit.astype(F32)
        work = jnp.where(hit, -jnp.inf, work)
    total = sel[0]
    for s in sel[1:]:
        total = total + s
    before = (lax.broadcasted_iota(jnp.int32, (tm, tm), 0) < lax.broadcasted_iota(jnp.int32, (tm, tm), 1))
    prefix = jnp.dot(member.astype(BF16), before.astype(BF16), preferred_element_type=F32)
    base = prefix + run_ref[...]
    for k in range(TOP_K):
        idx_ref[k:k + 1, :] = picks[k].astype(jnp.int32)
        rank_ref[k:k + 1, :] = jnp.sum(jnp.where(eidx == picks[k], base, 0.0), axis=0,
                                       keepdims=True).astype(jnp.int32)
        wt_ref[k:k + 1, :] = sel[k] / total * ROUTED_SCALE
    run_ref[...] = run_ref[...] + jnp.sum(member, axis=1, keepdims=True)
    cnt_ref[...] = run_ref[...].astype(jnp.int32)


def _merge(x, attn_o, y, ga, gs, mod, wts, run0, alpha):
    B, L, D = x.shape
    tm = min(MERGE_ROWS, L)
    per_batch = mod.shape[0] > 1
    row = lambda b, i: (b, i, 0)
    full = lambda b, i: (0, 0)
    wspec = lambda a: pl.BlockSpec(a.shape, full)
    return pl.pallas_call(
        functools.partial(_merge_kernel, alpha=alpha),
        grid=(B, L // tm),
        in_specs=[pl.BlockSpec((None, tm, D), row),
                  pl.BlockSpec((None, tm, ATTN_WIDTH), row),
                  pl.BlockSpec((None, None, tm, SSM_WIDTH), lambda b, i: (0, b, i, 0)),
                  pl.BlockSpec((None, None, tm, SSM_WIDTH), lambda b, i: (1, b, i, 0)),
                  pl.BlockSpec((None, tm, D), row),
                  pl.BlockSpec((None, tm, D), row),
                  pl.BlockSpec((None, N_MOD, D), (lambda b, i: (b, 0, 0)) if per_batch else (lambda b, i: (0, 0, 0))),
                  wspec(wts["w_glu"]), wspec(wts["w_attn_br"]), wspec(wts["w_ssm_br"]), wspec(wts["w_out"]),
                  wspec(wts["ln1_g"]), wspec(wts["ln1_b"]), wspec(wts["router_wt"]), wspec(wts["router_b"]),
                  wspec(run0)],
        out_specs=[pl.BlockSpec((None, tm, D), row),
                   pl.BlockSpec((None, tm, D // 2), row),
                   pl.BlockSpec((None, TOP_K, tm), lambda b, i: (b, 0, i)),
                   pl.BlockSpec((None, TOP_K, tm), lambda b, i: (b, 0, i)),
                   pl.BlockSpec((None, TOP_K, tm), lambda b, i: (b, 0, i)),
                   pl.BlockSpec((N_EXPERTS, 1), full)],
        out_shape=[jax.ShapeDtypeStruct((B, L, D), F32),
                   jax.ShapeDtypeStruct((B, L, D // 2), jnp.int32),
                   jax.ShapeDtypeStruct((B, TOP_K, L), jnp.int32),
                   jax.ShapeDtypeStruct((B, TOP_K, L), jnp.int32),
                   jax.ShapeDtypeStruct((B, TOP_K, L), F32),
                   jax.ShapeDtypeStruct((N_EXPERTS, 1), jnp.int32)],
        scratch_shapes=[pltpu.VMEM((N_EXPERTS, 1), F32)],
        compiler_params=_cparams(("arbitrary", "arbitrary")),
        name="merge_ln1_router",
    )(x, attn_o, y, y, ga, gs, mod, wts["w_glu"], wts["w_attn_br"], wts["w_ssm_br"], wts["w_out"],
      wts["ln1_g"], wts["ln1_b"], wts["router_wt"], wts["router_b"], run0)


def _dispatch_kernel(dest_ref, *refs, tiles):
    xs_ref, sem = refs[-2:]
    i = pl.program_id(0)
    first = 0
    for h_ref, nt in zip(refs[:-2], tiles):
        tm = h_ref.shape[0]

        @pl.when((i >= first) & (i < first + nt))
        def _(h_ref=h_ref, tm=tm):
            def start(r, c):
                for k in range(TOP_K):
                    pltpu.make_async_copy(h_ref.at[pl.ds(r, 1)], xs_ref.at[pl.ds(dest_ref[0, r * TOP_K + k], 1)],
                                          sem).start(priority=k % 2)
                return c

            lax.fori_loop(0, tm, start, 0, unroll=4)
            for k in range(TOP_K):
                pltpu.make_async_copy(h_ref, xs_ref.at[pl.ds(0, tm)], sem).wait()

        first += nt


def _token_major(a, tm):
    B, K, L = a.shape
    return jnp.swapaxes(a, 1, 2).reshape(B * (L // tm), 1, tm * K)


def _dispatch(h2s, dests):
    tm = DISPATCH_ROWS
    W = h2s[0].shape[-1]
    tiles = [h.shape[0] * h.shape[1] // tm for h in h2s]
    n_slots = sum(d.size for d in dests)
    in_specs = [pl.BlockSpec((None, 1, tm * TOP_K), lambda i: (i, 0, 0), memory_space=pltpu.SMEM)]
    first = 0
    for nt in tiles:
        in_specs.append(pl.BlockSpec((tm, W), lambda i, first=first, nt=nt: (jnp.clip(i - first, 0, nt - 1), 0)))
        first += nt
    return pl.pallas_call(
        functools.partial(_dispatch_kernel, tiles=tiles),
        grid=(sum(tiles),),
        in_specs=in_specs,
        out_specs=pl.BlockSpec(memory_space=pl.ANY),
        out_shape=jax.ShapeDtypeStruct((n_slots, W), h2s[0].dtype),
        scratch_shapes=[pltpu.SemaphoreType.DMA],
        compiler_params=_cparams(("arbitrary",)),
        name="moe_dispatch",
    )(jnp.concatenate([_token_major(d, tm) for d in dests], axis=0), *[h.reshape(-1, W) for h in h2s])


def _gmm_kernel(tile_ref, exp_ref, valid_ref, gstart_ref, xs_ref, wg_ref, wu_ref, wd_ref, ys_ref,
                wgb_ref, wub_ref, wdb_ref):
    w = pl.program_id(0)
    e = exp_ref[w]
    t = tile_ref[w]
    prev = jnp.maximum(w - 1, 0)
    new_expert = (w == 0) | (e != exp_ref[prev])
    first_visit = (w == 0) | (t != tile_ref[prev])

    @pl.when(new_expert)
    def _():
        wgb_ref[...] = wg_ref[...].astype(BF16)
        wub_ref[...] = wu_ref[...].astype(BF16)
        wdb_ref[...] = wd_ref[...].astype(BF16)

    @pl.when(valid_ref[w] == 1)
    def _():
        tg = xs_ref.shape[0]
        sub = tg // GMM_SPLIT

        def expert_rows(rs):
            x = jnp.concatenate(_unpack_rows(xs_ref[rs, :]), axis=1).astype(BF16)
            g = jnp.dot(x, wgb_ref[...], preferred_element_type=F32)
            u = jnp.dot(x, wub_ref[...], preferred_element_type=F32)
            a = (g * jax.nn.sigmoid(g) * u).astype(BF16)
            return _pack_rows(jnp.dot(a, wdb_ref[...], preferred_element_type=F32))

        y = jnp.concatenate([expert_rows(slice(j * sub, (j + 1) * sub)) for j in range(GMM_SPLIT)], axis=0)
        rows = t * tg + lax.broadcasted_iota(jnp.int32, (tg, 1), 0)
        mine = (rows >= gstart_ref[e]) & (rows < gstart_ref[e + 1])

        @pl.when(first_visit)
        def _():
            ys_ref[...] = jnp.where(mine, y, 0)

        @pl.when(jnp.logical_not(first_visit))
        def _():
            ys_ref[...] = jnp.where(mine, y, ys_ref[...])


def _grouped_experts(xs, counts, w_gate, w_up, w_down):
    A, W = xs.shape
    D = w_gate.shape[-2]
    tg = GMM_ROWS
    nt = A // tg
    n_items = nt + N_EXPERTS - 1
    ff = w_gate.shape[-1]
    gend = jnp.cumsum(counts).astype(jnp.int32)
    gstart = jnp.concatenate([jnp.zeros((1,), jnp.int32), gend])
    first_row = jnp.arange(nt, dtype=jnp.int32) * tg
    count_le = lambda ends, v: jnp.sum((ends[None, :] <= v[:, None]).astype(jnp.int32), axis=1)
    e_lo = count_le(gend, first_row)
    e_hi = count_le(gend, first_row + tg - 1)
    per_tile = e_hi - e_lo + 1
    item_end = jnp.cumsum(per_tile).astype(jnp.int32)
    total = item_end[-1]
    wi = jnp.arange(n_items, dtype=jnp.int32)
    tile = jnp.minimum(count_le(item_end, wi), nt - 1)
    in_tile = (tile[:, None] == jnp.arange(nt, dtype=jnp.int32)[None, :]).astype(jnp.int32)
    lookup = lambda table: jnp.sum(in_tile * table[None, :], axis=1)
    expert = lookup(e_lo) + wi - lookup(item_end - per_tile)
    valid = (wi < total).astype(jnp.int32)
    expert = jnp.where(valid == 1, expert, e_hi[nt - 1])
    grid_spec = pltpu.PrefetchScalarGridSpec(
        num_scalar_prefetch=4,
        grid=(n_items,),
        in_specs=[pl.BlockSpec((tg, W), lambda w, tl, ex, va, gs: (tl[w], 0)),
                  pl.BlockSpec((None, D, ff), lambda w, tl, ex, va, gs: (ex[w], 0, 0)),
                  pl.BlockSpec((None, D, ff), lambda w, tl, ex, va, gs: (ex[w], 0, 0)),
                  pl.BlockSpec((None, ff, D), lambda w, tl, ex, va, gs: (ex[w], 0, 0))],
        out_specs=pl.BlockSpec((tg, W), lambda w, tl, ex, va, gs: (tl[w], 0)),
        scratch_shapes=[pltpu.VMEM((D, ff), BF16), pltpu.VMEM((D, ff), BF16), pltpu.VMEM((ff, D), BF16)],
    )
    return pl.pallas_call(
        _gmm_kernel,
        grid_spec=grid_spec,
        out_shape=jax.ShapeDtypeStruct((A, W), xs.dtype),
        compiler_params=_cparams(("arbitrary",)),
        name="moe_grouped_experts",
    )(tile, expert, valid, gstart, xs, w_gate, w_up, w_down)


def _combine_kernel(dcur_ref, dnxt_ref, wt_ref, x1_ref, h2_ref, mod_ref, sg_ref, su_ref, sd_ref, lng_ref, lnb_ref,
                    ys_ref, o_ref, buf_ref, sem, *, alpha):
    i = pl.program_id(0)
    n = pl.num_programs(0)
    tm = x1_ref.shape[0] // 2

    def row_gather(dest_ref, first, s, r):
        for k in range(TOP_K):
            pltpu.make_async_copy(ys_ref.at[pl.ds(dest_ref[0, first + r * TOP_K + k], 1)],
                                  buf_ref.at[s, k, pl.ds(r, 1)], sem.at[s]).start()

    def finish_tile(s, rows):
        h = jnp.concatenate(_unpack_rows(h2_ref[rows, :]), axis=1).astype(BF16)
        g = jnp.dot(h, sg_ref[...], preferred_element_type=F32)
        u = jnp.dot(h, su_ref[...], preferred_element_type=F32)
        moe = jnp.dot((g * jax.nn.sigmoid(g) * u).astype(BF16), sd_ref[...], preferred_element_type=F32)
        for k in range(TOP_K):
            pltpu.make_async_copy(ys_ref.at[pl.ds(0, tm)], buf_ref.at[s, k], sem.at[s]).wait()
        wt = wt_ref[rows, :]
        lo = jnp.zeros((tm, buf_ref.shape[-1]), F32)
        hi = lo
        for k in range(TOP_K):
            rl, rh = _unpack_rows(buf_ref[s, k])
            lo = lo + wt[:, k:k + 1] * rl
            hi = hi + wt[:, k:k + 1] * rh
        moe = moe + jnp.concatenate([lo, hi], axis=1)
        o_ref[rows, :] = _layer_norm(alpha * x1_ref[rows, :] + mod_ref[0, 5:6, :] * moe, lng_ref[...], lnb_ref[...])

    @pl.when(i == 0)
    def _():
        def start(r, c):
            row_gather(dcur_ref, 0, 0, r)
            return c

        lax.fori_loop(0, tm, start, 0, unroll=4)

    for r in range(tm):
        row_gather(dcur_ref, tm * TOP_K, 1, r)
    finish_tile(0, slice(0, tm))
    for r in range(tm):
        row_gather(dnxt_ref, 0, 0, r)
    finish_tile(1, slice(tm, 2 * tm))

    @pl.when(i == n - 1)
    def _():
        for k in range(TOP_K):
            pltpu.make_async_copy(ys_ref.at[pl.ds(0, tm)], buf_ref.at[0, k], sem.at[0]).wait()


def _combine(ys, dest, wts, x1, h2, mod, shared, alpha):
    B, L, D = x1.shape
    W = ys.shape[-1]
    tm = COMBINE_ROWS
    ts = 2 * tm
    nl = L // ts
    nt = B * nl
    per_batch = mod.shape[0] > 1
    row = lambda i: (i, 0)
    full = lambda i: (0, 0)
    wspec = lambda a: pl.BlockSpec(a.shape, full)
    slots = _token_major(dest, ts)
    return pl.pallas_call(
        functools.partial(_combine_kernel, alpha=alpha),
        grid=(nt,),
        in_specs=[pl.BlockSpec((None, 1, ts * TOP_K), lambda i: (i, 0, 0), memory_space=pltpu.SMEM),
                  pl.BlockSpec((None, 1, ts * TOP_K), lambda i: (jnp.minimum(i + 1, nt - 1), 0, 0),
                               memory_space=pltpu.SMEM),
                  pl.BlockSpec((ts, TOP_K), row),
                  pl.BlockSpec((ts, D), row),
                  pl.BlockSpec((ts, W), row),
                  pl.BlockSpec((1, N_MOD, D), (lambda i: (i // nl, 0, 0)) if per_batch else (lambda i: (0, 0, 0))),
                  wspec(shared["sh_w_gate"]), wspec(shared["sh_w_up"]), wspec(shared["sh_w_down"]),
                  wspec(shared["ln2_g"]), wspec(shared["ln2_b"]),
                  pl.BlockSpec(memory_space=pl.ANY)],
        out_specs=pl.BlockSpec((ts, D), row),
        out_shape=jax.ShapeDtypeStruct((B * L, D), F32),
        scratch_shapes=[pltpu.VMEM((2, TOP_K, tm, W), ys.dtype), pltpu.SemaphoreType.DMA((2,))],
        compiler_params=_cparams(("arbitrary",)),
        name="moe_combine_ln2",
    )(slots, slots, jnp.swapaxes(wts, 1, 2).reshape(B * L, TOP_K), x1.reshape(B * L, D), h2.reshape(B * L, W),
      mod, shared["sh_w_gate"], shared["sh_w_up"], shared["sh_w_down"], shared["ln2_g"], shared["ln2_b"],
      ys).reshape(B, L, D)


def _slots(idx, rank, counts):
    start = jnp.cumsum(counts) - counts
    pick = idx[..., None] == jnp.arange(N_EXPERTS, dtype=jnp.int32)
    return jnp.sum(jnp.where(pick, start, 0), axis=-1).astype(jnp.int32) + rank


def _mixer_and_router(x, mod, wts, ssm, cache, run0, alpha):
    B, L, D = x.shape
    latent = cache is not None
    if latent:
        k_ctx, v_ctx, h0 = cache
        cos, sin = _rope_cos_sin(L)
    else:
        cos = jnp.zeros((L, LANES), F32)
        sin = cos
        h0 = jnp.zeros((2, B, 2 * SSM_GROUPS * SSM_STATE), F32)
    q, k, v, u, ga, gs = _in_projection(x, mod, wts["w_in"], cos, sin, rope=latent)
    if latent:
        attn_o = _attention_latent(q, k, v, k_ctx, v_ctx, wts["attn_sink"])
    else:
        attn_o = _attention_context(q, k, v, wts["attn_sink"])
    y, fs = _ssm_mix(u, ssm, h0)
    x1, h2, idx, rank, rw, counts = _merge(x, attn_o, y, ga, gs, mod, wts, run0, alpha)
    return {"x1": x1, "h2": h2, "idx": idx, "rank": rank, "rw": rw, "mod": mod}, counts, k, v, fs


def _moe(groups, counts, wts, alpha):
    counts = counts.reshape(N_EXPERTS)
    for g in groups:
        g["dest"] = _slots(g["idx"], g["rank"], counts)
    xs = _dispatch([g["h2"] for g in groups], [g["dest"] for g in groups])
    ys = _grouped_experts(xs, counts, wts["exp_w_gate"], wts["exp_w_up"], wts["exp_w_down"])
    return [_combine(ys, g["dest"], g["rw"], g["x1"], g["h2"], g["mod"], wts, alpha) for g in groups]


def kernel(x_prompt, x_sample, cache_k, cache_v, state_ssm, c, c_ctx, mod_w, mod_b, w_in, attn_sink, w_attn_br, ssm_a_re, ssm_a_im, ssm_log_dt, ssm_b_re, ssm_b_im, ssm_c_re, ssm_c_im, ssm_d, w_glu, w_ssm_br, w_out, ln1_g, ln1_b, ln2_g, ln2_b, router_w, router_b, exp_w_gate, exp_w_up, exp_w_down, sh_w_gate, sh_w_up, sh_w_down):
    depth = w_in.shape[0]
    assert depth == 1
    alpha = (2.0 * depth) ** 0.25
    D = x_prompt.shape[-1]
    nb_p = x_prompt.shape[0]
    nb_s = x_sample.shape[0]
    l = 0

    ncond = 1 + nb_s
    npad = -ncond % SUBLANES
    cond = jnp.concatenate([c_ctx[None, :], c, jnp.zeros((npad, D), F32)], axis=0)
    mod = _modulation(cond, mod_w[l], mod_b[l]).reshape(ncond + npad, N_MOD, D)
    mod_ctx, mod_lat = mod[0:1], mod[1:ncond]

    o1 = ATTN_WIDTH
    o3 = o1 + 2 * KV_WIDTH
    o4 = o3 + SSM_WIDTH
    wi = w_in[l]
    scale = HEAD_DIM ** -0.5
    wts = {
        "w_in": jnp.concatenate([wi[:, :o1] * scale, wi[:, o1:]], axis=1).astype(BF16),
        "attn_sink": attn_sink[l],
        "w_glu": w_glu[l].astype(BF16), "w_attn_br": w_attn_br[l].astype(BF16),
        "w_ssm_br": w_ssm_br[l].astype(BF16), "w_out": w_out[l].astype(BF16),
        "ln1_g": ln1_g[l].reshape(1, D), "ln1_b": ln1_b[l].reshape(1, D),
        "ln2_g": ln2_g[l].reshape(1, D), "ln2_b": ln2_b[l].reshape(1, D),
        "router_wt": router_w[l].T, "router_b": router_b[l].reshape(N_EXPERTS, 1),
        "exp_w_gate": exp_w_gate[l], "exp_w_up": exp_w_up[l], "exp_w_down": exp_w_down[l],
        "sh_w_gate": sh_w_gate[l].astype(BF16), "sh_w_up": sh_w_up[l].astype(BF16),
        "sh_w_down": sh_w_down[l].astype(BF16),
    }
    ssm = _ssm_params(ssm_a_re[l], ssm_a_im[l], ssm_log_dt[l], ssm_b_re[l], ssm_b_im[l],
                      ssm_c_re[l], ssm_c_im[l], ssm_d[l])

    no_tokens_yet = jnp.zeros((N_EXPERTS, 1), jnp.int32)
    grp_p, counts, k_p, v_p, fs_p = _mixer_and_router(x_prompt, mod_ctx, wts, ssm, None, no_tokens_yet, alpha)

    past = cache_k.shape[2]
    k_ctx = cache_k[:, l].reshape(nb_s, past, KV_WIDTH)
    v_ctx = cache_v[:, l].reshape(nb_s, past, KV_WIDTH)
    h0 = jnp.swapaxes(_state_to_lanes(state_ssm[:, l]), 0, 1)
    grp_s, counts, _, _, _ = _mixer_and_router(x_sample, mod_lat, wts, ssm, (k_ctx, v_ctx, h0), counts, alpha)
    yp, ys_ = _moe([grp_p, grp_s], counts, wts, alpha)

    Lp = x_prompt.shape[1]
    new_k = k_p.reshape(nb_p, 1, Lp, N_KV_HEADS, HEAD_DIM)
    new_v = v_p.reshape(nb_p, 1, Lp, N_KV_HEADS, HEAD_DIM)
    new_s = _lanes_to_state(jnp.swapaxes(fs_p, 0, 1))[:, None]
    return (yp, ys_, new_k, new_v, new_s)
```

```python
import functools
import math

import numpy as np
import jax
import jax.numpy as jnp
from jax import lax
from jax.experimental import pallas as pl
from jax.experimental.pallas import tpu as pltpu

F32 = jnp.float32
BF16 = jnp.bfloat16

GRID_W = 64
HEAD_DIM = 64
N_Q_HEADS = 8
N_KV_HEADS = 2
Q_PER_KV = N_Q_HEADS // N_KV_HEADS
ATTN_WIDTH = N_Q_HEADS * HEAD_DIM
KV_WIDTH = N_KV_HEADS * HEAD_DIM
WINDOW = 128
ROPE_BASE = 10000.0
SSM_WIDTH = 512
SSM_GROUP = 16
SSM_GROUPS = SSM_WIDTH // SSM_GROUP
SSM_STATE = 64
N_EXPERTS = 64
TOP_K = 6
ROUTED_SCALE = 2.5
N_MOD = 6
LN_EPS = 1e-5
LOG2_E = math.log2(math.e)

SUBLANES = 8
LANES = 128
VMEM_LIMIT = 48 * 1024 * 1024

INPROJ_ROWS = 512
ATTN_QBLOCKS = 4
SSM_STEPS = 64
SSM_STRIPS = 4
MERGE_ROWS = 512
MERGE_SPLIT = 2
DISPATCH_ROWS = 256
GMM_ROWS = 512
GMM_SPLIT = 1
COMBINE_ROWS = 128


def _cparams(sem):
    return pltpu.CompilerParams(dimension_semantics=sem, vmem_limit_bytes=VMEM_LIMIT)


def _mod_kernel(c_ref, w_ref, b_ref, o_ref):
    c = c_ref[...]
    s = c * jax.nn.sigmoid(c)
    o_ref[...] = jnp.dot(s, w_ref[...], preferred_element_type=F32,
                         precision=lax.Precision.HIGHEST) + b_ref[...]


def _modulation(cond, w, b):
    n, d = cond.shape
    nout = w.shape[1]
    tn = 512
    return pl.pallas_call(
        _mod_kernel,
        grid=(nout // tn,),
        in_specs=[pl.BlockSpec((n, d), lambda j: (0, 0)),
                  pl.BlockSpec((d, tn), lambda j: (0, j)),
                  pl.BlockSpec((1, tn), lambda j: (0, j))],
        out_specs=pl.BlockSpec((n, tn), lambda j: (0, j)),
        out_shape=jax.ShapeDtypeStruct((n, nout), F32),
        compiler_params=_cparams(("arbitrary",)),
        name="modulation",
    )(cond, w, b.reshape(1, nout))


def _rope_rotate(t, cos, sin):
    lane = lax.broadcasted_iota(jnp.int32, t.shape, 1)
    partner = jnp.where((lane % 32) < 16, pltpu.roll(t, LANES - 16, 1), pltpu.roll(t, 16, 1))
    return t * cos + partner * sin


def _inproj_kernel(x_ref, mod_ref, w_ref, cos_ref, sin_ref, q_ref, k_ref, v_ref, u_ref, ga_ref, gs_ref, *, rope):
    x = x_ref[...]
    h = (x * (1.0 + mod_ref[1:2, :]) + mod_ref[0:1, :]).astype(BF16)
    p = jnp.dot(h, w_ref[...], preferred_element_type=F32)
    o1 = ATTN_WIDTH
    o2 = o1 + KV_WIDTH
    o3 = o2 + KV_WIDTH
    d = x.shape[1]
    q = p[:, :o1]
    k = p[:, o1:o2]
    if rope:
        cos = cos_ref[...]
        sin = sin_ref[...]
        q = jnp.concatenate([_rope_rotate(q[:, j * LANES:(j + 1) * LANES], cos, sin)
                             for j in range(o1 // LANES)], axis=1)
        k = _rope_rotate(k, cos, sin)
    o4 = o3 + SSM_WIDTH
    q_ref[...] = q.astype(BF16)
    k_ref[...] = k
    v_ref[...] = p[:, o2:o3]
    u_ref[...] = p[:, o3:o4]
    ga_ref[...] = p[:, o4:o4 + d].astype(BF16)
    gs_ref[...] = p[:, o4 + d:].astype(BF16)


def _in_projection(x, mod, w, cos, sin, rope):
    B, L, D = x.shape
    tm = min(INPROJ_ROWS, L)
    per_batch = mod.shape[0] > 1
    nw = w.shape[1]
    row = lambda b, i: (b, i, 0)
    outs = pl.pallas_call(
        functools.partial(_inproj_kernel, rope=rope),
        grid=(B, L // tm),
        in_specs=[pl.BlockSpec((None, tm, D), row),
                  pl.BlockSpec((None, N_MOD, D), (lambda b, i: (b, 0, 0)) if per_batch else (lambda b, i: (0, 0, 0))),
                  pl.BlockSpec((D, nw), lambda b, i: (0, 0)),
                  pl.BlockSpec((tm, LANES), lambda b, i: (i, 0)),
                  pl.BlockSpec((tm, LANES), lambda b, i: (i, 0))],
        out_specs=[pl.BlockSpec((None, tm, ATTN_WIDTH), row),
                   pl.BlockSpec((None, tm, KV_WIDTH), row),
                   pl.BlockSpec((None, tm, KV_WIDTH), row),
                   pl.BlockSpec((None, tm, SSM_WIDTH), row),
                   pl.BlockSpec((None, tm, D), row),
                   pl.BlockSpec((None, tm, D), row)],
        out_shape=[jax.ShapeDtypeStruct((B, L, ATTN_WIDTH), BF16),
                   jax.ShapeDtypeStruct((B, L, KV_WIDTH), F32),
                   jax.ShapeDtypeStruct((B, L, KV_WIDTH), F32),
                   jax.ShapeDtypeStruct((B, L, SSM_WIDTH), F32),
                   jax.ShapeDtypeStruct((B, L, D), BF16),
                   jax.ShapeDtypeStruct((B, L, D), BF16)],
        compiler_params=_cparams(("parallel", "parallel")),
        name="in_projection",
    )(x, mod, w, cos, sin)
    return outs


def _rope_cos_sin(n_tokens):
    t = jnp.arange(n_tokens, dtype=jnp.int32)
    pos = jnp.stack([t // GRID_W, t % GRID_W], axis=-1).astype(F32)
    n_freq = HEAD_DIM // 4
    inv_freq = ROPE_BASE ** (-jnp.arange(n_freq, dtype=F32) / n_freq)
    ang = pos[:, :, None] * inv_freq
    c, s = jnp.cos(ang), jnp.sin(ang)
    cos = jnp.concatenate([c[:, 0], c[:, 0], c[:, 1], c[:, 1]], axis=-1)
    sin = jnp.concatenate([-s[:, 0], s[:, 0], -s[:, 1], s[:, 1]], axis=-1)
    return jnp.tile(cos, (1, LANES // HEAD_DIM)), jnp.tile(sin, (1, LANES // HEAD_DIM))


def _attend(sink_ref, q, kcat, vcat, mask, o_ref, row0=0):
    lq = q.shape[0]
    lane = lax.broadcasted_iota(jnp.int32, (1, LANES), 1)
    low = lane < HEAD_DIM
    k_sw = pltpu.roll(kcat, HEAD_DIM, 1)
    v_sw = pltpu.roll(vcat, HEAD_DIM, 1)
    neg = jnp.finfo(F32).min
    scores, vds = [], []
    for h in range(N_KV_HEADS):
        keep = low if h == 0 else jnp.logical_not(low)
        kd = jnp.where(keep, kcat, k_sw).astype(BF16)
        vds.append(jnp.where(keep, vcat, v_sw).astype(BF16))
        qs = []
        for j in range(Q_PER_KV):
            head = h * Q_PER_KV + j
            blk = q[:, (head // 2) * LANES:(head // 2 + 1) * LANES]
            sel = low if head % 2 == 0 else jnp.logical_not(low)
            qs.append(jnp.where(sel, blk, jnp.zeros_like(blk)))
        qstack = jnp.concatenate(qs, axis=0)
        scores.append(lax.dot_general(qstack, kd, (((1,), (1,)), ((), ())), preferred_element_type=F32))
    probs, denoms = [], []
    for h in range(N_KV_HEADS):
        s = scores[h]
        ps, ls = [], []
        for j in range(Q_PER_KV):
            sj = s[j * lq:(j + 1) * lq]
            if mask is not None:
                lm = mask.shape[1]
                sj = jnp.concatenate([jnp.where(mask, sj[:, :lm], neg), sj[:, lm:]], axis=1)
            sink = sink_ref[h * Q_PER_KV + j] * LOG2_E
            m = jnp.maximum(jnp.max(sj, axis=1, keepdims=True), sink)
            pj = jnp.exp2(sj - m)
            ls.append(jnp.sum(pj, axis=1, keepdims=True) + jnp.exp2(sink - m))
            ps.append(pj.astype(BF16))
        probs.append(jnp.concatenate(ps, axis=0))
        denoms.append(ls)
    for h in range(N_KV_HEADS):
        ls = denoms[h]
        o = jnp.dot(probs[h], vds[h], preferred_element_type=F32)
        for jj in range(Q_PER_KV // 2):
            oe = o[(2 * jj) * lq:(2 * jj + 1) * lq] / ls[2 * jj]
            oo = o[(2 * jj + 1) * lq:(2 * jj + 2) * lq] / ls[2 * jj + 1]
            cb = (h * Q_PER_KV) // 2 + jj
            o_ref[row0:row0 + lq, cb * LANES:(cb + 1) * LANES] = jnp.where(low, oe, oo).astype(o_ref.dtype)


def _attn_ctx_kernel(sink_ref, q_ref, k_ref, v_ref, o_ref):
    _attend(sink_ref, q_ref[...], k_ref[...], v_ref[...], None, o_ref)


def _attn_lat_kernel(sink_ref, q_ref, kp_ref, kc_ref, kn_ref, vp_ref, vc_ref, vn_ref, kx_ref, vx_ref, o_ref):
    n = pl.program_id(1)
    nb = pl.num_programs(1)
    blk = kp_ref.shape[0]
    nq = q_ref.shape[0] // blk
    klocal = jnp.concatenate([kp_ref[...], kc_ref[...], kn_ref[...]], axis=0)
    vlocal = jnp.concatenate([vp_ref[...], vc_ref[...], vn_ref[...]], axis=0)
    qi = lax.broadcasted_iota(jnp.int32, (blk, 3 * blk), 0)
    kj = lax.broadcasted_iota(jnp.int32, (blk, 3 * blk), 1)
    rel = kj - blk - qi
    in_band = (rel <= WINDOW) & (rel >= -WINDOW)
    for j in range(nq):
        mask = in_band
        if j == 0:
            mask = mask & ((kj >= blk) | (n > 0))
        if j == nq - 1:
            mask = mask & ((kj < 2 * blk) | (n < nb - 1))
        kcat = jnp.concatenate([klocal[j * blk:(j + 3) * blk], kx_ref[...]], axis=0)
        vcat = jnp.concatenate([vlocal[j * blk:(j + 3) * blk], vx_ref[...]], axis=0)
        _attend(sink_ref, q_ref[j * blk:(j + 1) * blk, :], kcat, vcat, mask, o_ref, j * blk)


def _attention_context(q, k, v, sink):
    B, L, _ = q.shape
    row = lambda b: (b, 0, 0)
    return pl.pallas_call(
        _attn_ctx_kernel,
        grid=(B,),
        in_specs=[pl.BlockSpec(memory_space=pltpu.SMEM),
                  pl.BlockSpec((None, L, ATTN_WIDTH), row),
                  pl.BlockSpec((None, L, KV_WIDTH), row),
                  pl.BlockSpec((None, L, KV_WIDTH), row)],
        out_specs=pl.BlockSpec((None, L, ATTN_WIDTH), row),
        out_shape=jax.ShapeDtypeStruct((B, L, ATTN_WIDTH), BF16),
        compiler_params=_cparams(("parallel",)),
        name="attention_context",
    )(sink, q, k, v)


def _attention_latent(q, k, v, k_ctx, v_ctx, sink):
    B, S, _ = q.shape
    blk = WINDOW
    nq = ATTN_QBLOCKS
    nb = S // blk
    nctx = k_ctx.shape[1]
    cur = lambda b, n: (b, n, 0)
    prv = lambda b, n: (b, jnp.maximum(n * nq - 1, 0), 0)
    nxt = lambda b, n: (b, jnp.minimum(n * nq + nq, nb - 1), 0)
    ctx = lambda b, n: (b, 0, 0)
    edge = lambda im: pl.BlockSpec((None, blk, KV_WIDTH), im)
    mid = pl.BlockSpec((None, nq * blk, KV_WIDTH), cur)
    return pl.pallas_call(
        _attn_lat_kernel,
        grid=(B, nb // nq),
        in_specs=[pl.BlockSpec(memory_space=pltpu.SMEM),
                  pl.BlockSpec((None, nq * blk, ATTN_WIDTH), cur),
                  edge(prv), mid, edge(nxt), edge(prv), mid, edge(nxt),
                  pl.BlockSpec((None, nctx, KV_WIDTH), ctx),
                  pl.BlockSpec((None, nctx, KV_WIDTH), ctx)],
        out_specs=pl.BlockSpec((None, nq * blk, ATTN_WIDTH), cur),
        out_shape=jax.ShapeDtypeStruct((B, S, ATTN_WIDTH), BF16),
        compiler_params=_cparams(("parallel", "parallel")),
        name="attention_latent",
    )(sink, q, k, k, k, v, v, v, k_ctx, v_ctx)


def _ssm_kernel(u_ref, wb_ref, wc_ref, a_ref, d_ref, h0_ref, y_ref, fs_ref, ut_ref, yt_ref, st_ref, *bu_refs):
    rev = pl.program_id(0)
    i = pl.program_id(2)
    nc = pl.num_programs(2)
    nseq, tm, _ = u_ref.shape
    sw = a_ref.shape[-1] // SSM_STRIPS
    nre = sw // 2

    @pl.when(i == 0)
    def _():
        st_ref[...] = h0_ref[...]

    for b in range(nseq):
        ub = u_ref[b]
        for s in range(SSM_STRIPS):
            ut_ref[s, pl.ds(b, tm, stride=nseq), :] = ub[:, s * LANES:(s + 1) * LANES]
    for s in range(SSM_STRIPS):
        bu_refs[s][...] = jnp.dot(ut_ref[s].astype(BF16), wb_ref[s], preferred_element_type=F32)

    for s in range(SSM_STRIPS):
        bu_ref = bu_refs[s]
        a_re = a_ref[:, s * sw:s * sw + nre]
        a_im = a_ref[:, s * sw + nre:(s + 1) * sw]

        def step(t, carry):
            xr, xi = carry
            r = pl.multiple_of((t + rev * (tm - 1 - 2 * t)) * nseq, nseq)
            nr = a_re * xr - a_im * xi + bu_ref[pl.ds(r, nseq), 0:nre]
            ni = a_re * xi + a_im * xr + bu_ref[pl.ds(r, nseq), nre:sw]
            bu_ref[pl.ds(r, nseq), 0:nre] = nr
            bu_ref[pl.ds(r, nseq), nre:sw] = ni
            return nr, ni

        xr, xi = lax.fori_loop(0, tm, step, (st_ref[:, s * sw:s * sw + nre], st_ref[:, s * sw + nre:(s + 1) * sw]),
                               unroll=True)
        st_ref[:, s * sw:s * sw + nre] = xr
        st_ref[:, s * sw + nre:(s + 1) * sw] = xi
        yt_ref[s] = jnp.dot(bu_ref[...].astype(BF16), wc_ref[s], preferred_element_type=F32)

    for b in range(nseq):
        yb = jnp.concatenate([yt_ref[s, pl.ds(b, tm, stride=nseq), :] for s in range(SSM_STRIPS)], axis=1)
        y_ref[b] = yb + u_ref[b] * d_ref[...]

    @pl.when(i == nc - 1)
    def _():
        fs_ref[...] = st_ref[...]


def _ssm_mix(u, ssm, h0):
    B, L, _ = u.shape
    nseq = SUBLANES
    tm = min(SSM_STEPS, L)
    nc = L // tm
    rows = nseq * tm
    ns = 2 * SSM_GROUPS * SSM_STATE
    chunk = lambda d, g, i: i + d * (nc - 1 - 2 * i)
    y, fs = pl.pallas_call(
        _ssm_kernel,
        grid=(2, B // nseq, nc),
        in_specs=[pl.BlockSpec((nseq, tm, SSM_WIDTH), lambda d, g, i: (g, chunk(d, g, i), 0)),
                  pl.BlockSpec((None, SSM_STRIPS, LANES, ns // SSM_STRIPS), lambda d, g, i: (d, 0, 0, 0)),
                  pl.BlockSpec((None, SSM_STRIPS, ns // SSM_STRIPS, LANES), lambda d, g, i: (d, 0, 0, 0)),
                  pl.BlockSpec((None, nseq, ns), lambda d, g, i: (d, 0, 0)),
                  pl.BlockSpec((None, 1, SSM_WIDTH), lambda d, g, i: (d, 0, 0)),
                  pl.BlockSpec((None, nseq, ns), lambda d, g, i: (d, g, 0))],
        out_specs=[pl.BlockSpec((None, nseq, tm, SSM_WIDTH), lambda d, g, i: (d, g, chunk(d, g, i), 0)),
                   pl.BlockSpec((None, nseq, ns), lambda d, g, i: (d, g, 0))],
        out_shape=[jax.ShapeDtypeStruct((2, B, L, SSM_WIDTH), F32),
                   jax.ShapeDtypeStruct((2, B, ns), F32)],
        scratch_shapes=[pltpu.VMEM((SSM_STRIPS, rows, LANES), F32), pltpu.VMEM((SSM_STRIPS, rows, LANES), F32),
                        pltpu.VMEM((nseq, ns), F32)]
        + [pltpu.VMEM((rows, ns // SSM_STRIPS), F32) for _ in range(SSM_STRIPS)],
        compiler_params=_cparams(("arbitrary", "arbitrary", "arbitrary")),
        name="ssm_scan",
    )(u, ssm["wb"], ssm["wc"], ssm["a"], ssm["d"], h0)
    return y, fs


def _state_to_lanes(s):
    lead = s.shape[:-3]
    s = s.reshape(lead + (2, SSM_STRIPS, SSM_GROUPS // SSM_STRIPS, SSM_STATE))
    s = jnp.swapaxes(s, -4, -3)
    return s.reshape(lead + (2 * SSM_GROUPS * SSM_STATE,))


def _lanes_to_state(v):
    lead = v.shape[:-1]
    s = v.reshape(lead + (SSM_STRIPS, 2, SSM_GROUPS // SSM_STRIPS, SSM_STATE))
    s = jnp.swapaxes(s, -4, -3)
    return s.reshape(lead + (2, SSM_GROUPS, SSM_STATE))


def _ssm_params(a_re, a_im, log_dt, b_re, b_im, c_re, c_im, dvec):
    G, N, C = SSM_GROUPS, SSM_STATE, SSM_GROUP
    lam_re = jnp.minimum(a_re, -1e-4)
    lam_im = a_im
    dt = jnp.exp(log_dt)[..., None]
    mag = jnp.exp(lam_re * dt)
    abar_re, abar_im = mag * jnp.cos(lam_im * dt), mag * jnp.sin(lam_im * dt)
    den = jnp.square(lam_re) + jnp.square(lam_im)
    p, qi = abar_re - 1.0, abar_im
    f_re = (p * lam_re + qi * lam_im) / den
    f_im = (qi * lam_re - p * lam_im) / den
    bbar_re = f_re[..., None] * b_re - f_im[..., None] * b_im
    bbar_im = f_re[..., None] * b_im + f_im[..., None] * b_re
    a = _state_to_lanes(jnp.stack([abar_re, abar_im], axis=1))
    a = jnp.broadcast_to(a[:, None, :], (2, SUBLANES, a.shape[-1]))
    S = SSM_STRIPS
    gs = G // S
    eye = jnp.eye(gs, dtype=F32)

    def bd_in(bb):
        bb = bb.reshape(2, S, gs, N, C)
        return jnp.einsum('dkgnc,gh->dkgchn', bb, eye).reshape(2, S, gs * C, gs * N)

    def bd_out(cc):
        cc = cc.reshape(2, S, gs, C, N)
        return jnp.einsum('dkgcn,gh->dkgnhc', cc, eye).reshape(2, S, gs * N, gs * C)

    wb = jnp.concatenate([bd_in(bbar_re), bd_in(bbar_im)], axis=-1).astype(BF16)
    wc = jnp.concatenate([bd_out(c_re), -bd_out(c_im)], axis=-2).astype(BF16)
    d = jnp.stack([dvec, jnp.zeros_like(dvec)], axis=0).reshape(2, 1, SSM_WIDTH)
    return {"a": a, "wb": wb, "wc": wc, "d": d}


def _layer_norm(x, g, b):
    mu = jnp.mean(x, axis=-1, keepdims=True)
    xc = x - mu
    var = jnp.mean(xc * xc, axis=-1, keepdims=True)
    return xc * lax.rsqrt(var + LN_EPS) * g + b


def _pack_rows(x):
    w = x.shape[1] // 2
    return pltpu.pack_elementwise([x[:, :w], x[:, w:]], packed_dtype=BF16)


def _unpack_rows(p):
    return (pltpu.unpack_elementwise(p, index=0, packed_dtype=BF16, unpacked_dtype=F32),
            pltpu.unpack_elementwise(p, index=1, packed_dtype=BF16, unpacked_dtype=F32))


def _gelu_tanh(x):
    return 0.5 * x * (1.0 + jnp.tanh(math.sqrt(2.0 / math.pi) * (x + 0.044715 * (x * x * x))))


def _merge_kernel(x_ref, ao_ref, yf_ref, yb_ref, ga_ref, gs_ref, mod_ref, wglu_ref, wa_ref, ws_ref, wo_ref,
                  lng_ref, lnb_ref, rwt_ref, rb_ref, run0_ref, x1_ref, h2_ref, idx_ref, rank_ref, wt_ref, cnt_ref,
                  run_ref, *, alpha):
    @pl.when((pl.program_id(0) == 0) & (pl.program_id(1) == 0))
    def _():
        run_ref[...] = run0_ref[...].astype(F32)

    nrow = x_ref.shape[0]
    sub = nrow // MERGE_SPLIT
    groups = [slice(j * sub, (j + 1) * sub) for j in range(MERGE_SPLIT)]
    dot = functools.partial(jnp.dot, preferred_element_type=F32)
    z = [_gelu_tanh(yf_ref[rs, :] + yb_ref[rs, :]) for rs in groups]
    attn_br = [dot(ao_ref[rs, :], wa_ref[...]) for rs in groups]
    gate = [jax.nn.sigmoid(dot(zg.astype(BF16), wglu_ref[...])) for zg in z]
    ssm_br = [dot((zg * gg).astype(BF16), ws_ref[...]) for zg, gg in zip(z, gate)]
    merged = [jax.nn.sigmoid(ga_ref[rs, :].astype(F32)) * ab + jax.nn.sigmoid(gs_ref[rs, :].astype(F32)) * sb
              for rs, ab, sb in zip(groups, attn_br, ssm_br)]
    mix = [dot(mg.astype(BF16), wo_ref[...]) for mg in merged]
    h2s = []
    for rs, mg in zip(groups, mix):
        x1 = _layer_norm(alpha * x_ref[rs, :] + mod_ref[2:3, :] * mg, lng_ref[...], lnb_ref[...])
        h2g = x1 * (1.0 + mod_ref[4:5, :]) + mod_ref[3:4, :]
        x1_ref[rs, :] = x1
        h2_ref[rs, :] = _pack_rows(h2g)
        h2s.append(h2g)
    h2 = jnp.concatenate(h2s, axis=0)

    logits = lax.dot_general(rwt_ref[...], h2, (((1,), (1,)), ((), ())), preferred_element_type=F32,
                             precision=lax.Precision.HIGHEST)
    score = jax.nn.sigmoid(logits)
    tm = score.shape[1]
    eidx = lax.broadcasted_iota(jnp.int32, score.shape, 0).astype(F32)
    work = score + rb_ref[...]
    picks, sel = [], []
    member = jnp.zeros_like(score)
    for _ in range(TOP_K):
        best = jnp.max(work, axis=0, keepdims=True)
        pick = jnp.min(jnp.where(work == best, eidx, float(N_EXPERTS)), axis=0, keepdims=True)
        hit = eidx == pick
        picks.append(pick)
        sel.append(jnp.sum(jnp.where(hit, score, 0.0), axis=0, keepdims=True))
        member = member + hit.astype(F32)
        work = jnp.where(hit, -jnp.inf, work)
    total = sel[0]
    for s in sel[1:]:
        total = total + s
    before = (lax.broadcasted_iota(jnp.int32, (tm, tm), 0) < lax.broadcasted_iota(jnp.int32, (tm, tm), 1))
    prefix = jnp.dot(member.astype(BF16), before.astype(BF16), preferred_element_type=F32)
    base = prefix + run_ref[...]
    for k in range(TOP_K):
        idx_ref[k:k + 1, :] = picks[k].astype(jnp.int32)
        rank_ref[k:k + 1, :] = jnp.sum(jnp.where(eidx == picks[k], base, 0.0), axis=0,
                                       keepdims=True).astype(jnp.int32)
        wt_ref[k:k + 1, :] = sel[k] / total * ROUTED_SCALE
    run_ref[...] = run_ref[...] + jnp.sum(member, axis=1, keepdims=True)
    cnt_ref[...] = run_ref[...].astype(jnp.int32)


def _merge(x, attn_o, y, ga, gs, mod, wts, run0, alpha):
    B, L, D = x.shape
    tm = min(MERGE_ROWS, L)
    per_batch = mod.shape[0] > 1
    row = lambda b, i: (b, i, 0)
    full = lambda b, i: (0, 0)
    wspec = lambda a: pl.BlockSpec(a.shape, full)
    return pl.pallas_call(
        functools.partial(_merge_kernel, alpha=alpha),
        grid=(B, L // tm),
        in_specs=[pl.BlockSpec((None, tm, D), row),
                  pl.BlockSpec((None, tm, ATTN_WIDTH), row),
                  pl.BlockSpec((None, None, tm, SSM_WIDTH), lambda b, i: (0, b, i, 0)),
                  pl.BlockSpec((None, None, tm, SSM_WIDTH), lambda b, i: (1, b, i, 0)),
                  pl.BlockSpec((None, tm, D), row),
                  pl.BlockSpec((None, tm, D), row),
                  pl.BlockSpec((None, N_MOD, D), (lambda b, i: (b, 0, 0)) if per_batch else (lambda b, i: (0, 0, 0))),
                  wspec(wts["w_glu"]), wspec(wts["w_attn_br"]), wspec(wts["w_ssm_br"]), wspec(wts["w_out"]),
                  wspec(wts["ln1_g"]), wspec(wts["ln1_b"]), wspec(wts["router_wt"]), wspec(wts["router_b"]),
                  wspec(run0)],
        out_specs=[pl.BlockSpec((None, tm, D), row),
                   pl.BlockSpec((None, tm, D // 2), row),
                   pl.BlockSpec((None, TOP_K, tm), lambda b, i: (b, 0, i)),
                   pl.BlockSpec((None, TOP_K, tm), lambda b, i: (b, 0, i)),
                   pl.BlockSpec((None, TOP_K, tm), lambda b, i: (b, 0, i)),
                   pl.BlockSpec((N_EXPERTS, 1), full)],
        out_shape=[jax.ShapeDtypeStruct((B, L, D), F32),
                   jax.ShapeDtypeStruct((B, L, D // 2), jnp.int32),
                   jax.ShapeDtypeStruct((B, TOP_K, L), jnp.int32),
                   jax.ShapeDtypeStruct((B, TOP_K, L), jnp.int32),
                   jax.ShapeDtypeStruct((B, TOP_K, L), F32),
                   jax.ShapeDtypeStruct((N_EXPERTS, 1), jnp.int32)],
        scratch_shapes=[pltpu.VMEM((N_EXPERTS, 1), F32)],
        compiler_params=_cparams(("arbitrary", "arbitrary")),
        name="merge_ln1_router",
    )(x, attn_o, y, y, ga, gs, mod, wts["w_glu"], wts["w_attn_br"], wts["w_ssm_br"], wts["w_out"],
      wts["ln1_g"], wts["ln1_b"], wts["router_wt"], wts["router_b"], run0)


def _dispatch_kernel(dest_ref, *refs, tiles):
    xs_ref, sem = refs[-2:]
    i = pl.program_id(0)
    first = 0
    for h_ref, nt in zip(refs[:-2], tiles):
        tm = h_ref.shape[0]

        @pl.when((i >= first) & (i < first + nt))
        def _(h_ref=h_ref, tm=tm):
            def start(r, c):
                for k in range(TOP_K):
                    pltpu.make_async_copy(h_ref.at[pl.ds(r, 1)], xs_ref.at[pl.ds(dest_ref[0, r * TOP_K + k], 1)],
                                          sem).start(priority=k % 2)
                return c

            lax.fori_loop(0, tm, start, 0, unroll=4)
            for k in range(TOP_K):
                pltpu.make_async_copy(h_ref, xs_ref.at[pl.ds(0, tm)], sem).wait()

        first += nt


def _token_major(a, tm):
    B, K, L = a.shape
    return jnp.swapaxes(a, 1, 2).reshape(B * (L // tm), 1, tm * K)


def _dispatch(h2s, dests):
    tm = DISPATCH_ROWS
    W = h2s[0].shape[-1]
    tiles = [h.shape[0] * h.shape[1] // tm for h in h2s]
    n_slots = sum(d.size for d in dests)
    in_specs = [pl.BlockSpec((None, 1, tm * TOP_K), lambda i: (i, 0, 0), memory_space=pltpu.SMEM)]
    first = 0
    for nt in tiles:
        in_specs.append(pl.BlockSpec((tm, W), lambda i, first=first, nt=nt: (jnp.clip(i - first, 0, nt - 1), 0)))
        first += nt
    return pl.pallas_call(
        functools.partial(_dispatch_kernel, tiles=tiles),
        grid=(sum(tiles),),
        in_specs=in_specs,
        out_specs=pl.BlockSpec(memory_space=pl.ANY),
        out_shape=jax.ShapeDtypeStruct((n_slots, W), h2s[0].dtype),
        scratch_shapes=[pltpu.SemaphoreType.DMA],
        compiler_params=_cparams(("arbitrary",)),
        name="moe_dispatch",
    )(jnp.concatenate([_token_major(d, tm) for d in dests], axis=0), *[h.reshape(-1, W) for h in h2s])


def _gmm_kernel(tile_ref, exp_ref, valid_ref, gstart_ref, xs_ref, wg_ref, wu_ref, wd_ref, ys_ref,
                wgb_ref, wub_ref, wdb_ref):
    w = pl.program_id(0)
    e = exp_ref[w]
    t = tile_ref[w]
    prev = jnp.maximum(w - 1, 0)
    new_expert = (w == 0) | (e != exp_ref[prev])
    first_visit = (w == 0) | (t != tile_ref[prev])

    @pl.when(new_expert)
    def _():
        wgb_ref[...] = wg_ref[...].astype(BF16)
        wub_ref[...] = wu_ref[...].astype(BF16)
        wdb_ref[...] = wd_ref[...].astype(BF16)

    @pl.when(valid_ref[w] == 1)
    def _():
        tg = xs_ref.shape[0]
        sub = tg // GMM_SPLIT

        groups = [slice(j * sub, (j + 1) * sub) for j in range(GMM_SPLIT)]
        dot = functools.partial(jnp.dot, preferred_element_type=F32)
        x = [jnp.concatenate(_unpack_rows(xs_ref[rs, :]), axis=1).astype(BF16) for rs in groups]
        g = [dot(xg, wgb_ref[...]) for xg in x]
        u = [dot(xg, wub_ref[...]) for xg in x]
        a = [(gg * jax.nn.sigmoid(gg) * ug).astype(BF16) for gg, ug in zip(g, u)]
        y = jnp.concatenate([_pack_rows(dot(ag, wdb_ref[...])) for ag in a], axis=0)
        rows = t * tg + lax.broadcasted_iota(jnp.int32, (tg, 1), 0)
        mine = (rows >= gstart_ref[e]) & (rows < gstart_ref[e + 1])

        @pl.when(first_visit)
        def _():
            ys_ref[...] = jnp.where(mine, y, 0)

        @pl.when(jnp.logical_not(first_visit))
        def _():
            ys_ref[...] = jnp.where(mine, y, ys_ref[...])


def _grouped_experts(xs, counts, w_gate, w_up, w_down):
    A, W = xs.shape
    D = w_gate.shape[-2]
    tg = GMM_ROWS
    nt = A // tg
    n_items = nt + N_EXPERTS - 1
    ff = w_gate.shape[-1]
    gend = jnp.cumsum(counts).astype(jnp.int32)
    gstart = jnp.concatenate([jnp.zeros((1,), jnp.int32), gend])
    first_row = jnp.arange(nt, dtype=jnp.int32) * tg
    count_le = lambda ends, v: jnp.sum((ends[None, :] <= v[:, None]).astype(jnp.int32), axis=1)
    e_lo = count_le(gend, first_row)
    e_hi = count_le(gend, first_row + tg - 1)
    per_tile = e_hi - e_lo + 1
    item_end = jnp.cumsum(per_tile).astype(jnp.int32)
    total = item_end[-1]
    wi = jnp.arange(n_items, dtype=jnp.int32)
    tile = jnp.minimum(count_le(item_end, wi), nt - 1)
    in_tile = (tile[:, None] == jnp.arange(nt, dtype=jnp.int32)[None, :]).astype(jnp.int32)
    lookup = lambda table: jnp.sum(in_tile * table[None, :], axis=1)
    expert = lookup(e_lo) + wi - lookup(item_end - per_tile)
    valid = (wi < total).astype(jnp.int32)
    expert = jnp.where(valid == 1, expert, e_hi[nt - 1])
    grid_spec = pltpu.PrefetchScalarGridSpec(
        num_scalar_prefetch=4,
        grid=(n_items,),
        in_specs=[pl.BlockSpec((tg, W), lambda w, tl, ex, va, gs: (tl[w], 0)),
                  pl.BlockSpec((None, D, ff), lambda w, tl, ex, va, gs: (ex[w], 0, 0)),
                  pl.BlockSpec((None, D, ff), lambda w, tl, ex, va, gs: (ex[w], 0, 0)),
                  pl.BlockSpec((None, ff, D), lambda w, tl, ex, va, gs: (ex[w], 0, 0))],
        out_specs=pl.BlockSpec((tg, W), lambda w, tl, ex, va, gs: (tl[w], 0)),
        scratch_shapes=[pltpu.VMEM((D, ff), BF16), pltpu.VMEM((D, ff), BF16), pltpu.VMEM((ff, D), BF16)],
    )
    return pl.pallas_call(
        _gmm_kernel,
        grid_spec=grid_spec,
        out_shape=jax.ShapeDtypeStruct((A, W), xs.dtype),
        compiler_params=_cparams(("arbitrary",)),
        name="moe_grouped_experts",
    )(tile, expert, valid, gstart, xs, w_gate, w_up, w_down)


def _combine_kernel(dcur_ref, dnxt_ref, wt_ref, x1_ref, h2_ref, mod_ref, sg_ref, su_ref, sd_ref, lng_ref, lnb_ref,
                    ys_ref, o_ref, buf_ref, sem, *, alpha):
    i = pl.program_id(0)
    n = pl.num_programs(0)
    tm = x1_ref.shape[0] // 2

    def row_gather(dest_ref, first, s, r):
        for k in range(TOP_K):
            pltpu.make_async_copy(ys_ref.at[pl.ds(dest_ref[0, first + r * TOP_K + k], 1)],
                                  buf_ref.at[s, k, pl.ds(r, 1)], sem.at[s]).start()

    def finish_tile(s, rows):
        h = jnp.concatenate(_unpack_rows(h2_ref[rows, :]), axis=1).astype(BF16)
        g = jnp.dot(h, sg_ref[...], preferred_element_type=F32)
        u = jnp.dot(h, su_ref[...], preferred_element_type=F32)
        moe = jnp.dot((g * jax.nn.sigmoid(g) * u).astype(BF16), sd_ref[...], preferred_element_type=F32)
        for k in range(TOP_K):
            pltpu.make_async_copy(ys_ref.at[pl.ds(0, tm)], buf_ref.at[s, k], sem.at[s]).wait()
        wt = wt_ref[rows, :]
        lo = jnp.zeros((tm, buf_ref.shape[-1]), F32)
        hi = lo
        for k in range(TOP_K):
            rl, rh = _unpack_rows(buf_ref[s, k])
            lo = lo + wt[:, k:k + 1] * rl
            hi = hi + wt[:, k:k + 1] * rh
        moe = moe + jnp.concatenate([lo, hi], axis=1)
        o_ref[rows, :] = _layer_norm(alpha * x1_ref[rows, :] + mod_ref[0, 5:6, :] * moe, lng_ref[...], lnb_ref[...])

    @pl.when(i == 0)
    def _():
        def start(r, c):
            row_gather(dcur_ref, 0, 0, r)
            return c

        lax.fori_loop(0, tm, start, 0, unroll=4)

    for r in range(tm):
        row_gather(dcur_ref, tm * TOP_K, 1, r)
    finish_tile(0, slice(0, tm))
    for r in range(tm):
        row_gather(dnxt_ref, 0, 0, r)
    finish_tile(1, slice(tm, 2 * tm))

    @pl.when(i == n - 1)
    def _():
        for k in range(TOP_K):
            pltpu.make_async_copy(ys_ref.at[pl.ds(0, tm)], buf_ref.at[0, k], sem.at[0]).wait()


def _combine(ys, dest, wts, x1, h2, mod, shared, alpha):
    B, L, D = x1.shape
    W = ys.shape[-1]
    tm = COMBINE_ROWS
    ts = 2 * tm
    nl = L // ts
    nt = B * nl
    per_batch = mod.shape[0] > 1
    row = lambda i: (i, 0)
    full = lambda i: (0, 0)
    wspec = lambda a: pl.BlockSpec(a.shape, full)
    slots = _token_major(dest, ts)
    return pl.pallas_call(
        functools.partial(_combine_kernel, alpha=alpha),
        grid=(nt,),
        in_specs=[pl.BlockSpec((None, 1, ts * TOP_K), lambda i: (i, 0, 0), memory_space=pltpu.SMEM),
                  pl.BlockSpec((None, 1, ts * TOP_K), lambda i: (jnp.minimum(i + 1, nt - 1), 0, 0),
                               memory_space=pltpu.SMEM),
                  pl.BlockSpec((ts, TOP_K), row),
                  pl.BlockSpec((ts, D), row),
                  pl.BlockSpec((ts, W), row),
                  pl.BlockSpec((1, N_MOD, D), (lambda i: (i // nl, 0, 0)) if per_batch else (lambda i: (0, 0, 0))),
                  wspec(shared["sh_w_gate"]), wspec(shared["sh_w_up"]), wspec(shared["sh_w_down"]),
                  wspec(shared["ln2_g"]), wspec(shared["ln2_b"]),
                  pl.BlockSpec(memory_space=pl.ANY)],
        out_specs=pl.BlockSpec((ts, D), row),
        out_shape=jax.ShapeDtypeStruct((B * L, D), F32),
        scratch_shapes=[pltpu.VMEM((2, TOP_K, tm, W), ys.dtype), pltpu.SemaphoreType.DMA((2,))],
        compiler_params=_cparams(("arbitrary",)),
        name="moe_combine_ln2",
    )(slots, slots, jnp.swapaxes(wts, 1, 2).reshape(B * L, TOP_K), x1.reshape(B * L, D), h2.reshape(B * L, W),
      mod, shared["sh_w_gate"], shared["sh_w_up"], shared["sh_w_down"], shared["ln2_g"], shared["ln2_b"],
      ys).reshape(B, L, D)


def _slots(idx, rank, counts):
    start = jnp.cumsum(counts) - counts
    pick = idx[..., None] == jnp.arange(N_EXPERTS, dtype=jnp.int32)
    return jnp.sum(jnp.where(pick, start, 0), axis=-1).astype(jnp.int32) + rank


def _mixer_and_router(x, mod, wts, ssm, cache, run0, alpha):
    B, L, D = x.shape
    latent = cache is not None
    if latent:
        k_ctx, v_ctx, h0 = cache
        cos, sin = _rope_cos_sin(L)
    else:
        cos = jnp.zeros((L, LANES), F32)
        sin = cos
        h0 = jnp.zeros((2, B, 2 * SSM_GROUPS * SSM_STATE), F32)
    q, k, v, u, ga, gs = _in_projection(x, mod, wts["w_in"], cos, sin, rope=latent)
    if latent:
        attn_o = _attention_latent(q, k, v, k_ctx, v_ctx, wts["attn_sink"])
    else:
        attn_o = _attention_context(q, k, v, wts["attn_sink"])
    y, fs = _ssm_mix(u, ssm, h0)
    x1, h2, idx, rank, rw, counts = _merge(x, attn_o, y, ga, gs, mod, wts, run0, alpha)
    return {"x1": x1, "h2": h2, "idx": idx, "rank": rank, "rw": rw, "mod": mod}, counts, k, v, fs


def _moe(groups, counts, wts, alpha):
    counts = counts.reshape(N_EXPERTS)
    for g in groups:
        g["dest"] = _slots(g["idx"], g["rank"], counts)
    xs = _dispatch([g["h2"] for g in groups], [g["dest"] for g in groups])
    ys = _grouped_experts(xs, counts, wts["exp_w_gate"], wts["exp_w_up"], wts["exp_w_down"])
    return [_combine(ys, g["dest"], g["rw"], g["x1"], g["h2"], g["mod"], wts, alpha) for g in groups]


def kernel(x_prompt, x_sample, cache_k, cache_v, state_ssm, c, c_ctx, mod_w, mod_b, w_in, attn_sink, w_attn_br, ssm_a_re, ssm_a_im, ssm_log_dt, ssm_b_re, ssm_b_im, ssm_c_re, ssm_c_im, ssm_d, w_glu, w_ssm_br, w_out, ln1_g, ln1_b, ln2_g, ln2_b, router_w, router_b, exp_w_gate, exp_w_up, exp_w_down, sh_w_gate, sh_w_up, sh_w_down):
    depth = w_in.shape[0]
    assert depth == 1
    alpha = (2.0 * depth) ** 0.25
    D = x_prompt.shape[-1]
    nb_p = x_prompt.shape[0]
    nb_s = x_sample.shape[0]
    l = 0

    ncond = 1 + nb_s
    npad = -ncond % SUBLANES
    cond = jnp.concatenate([c_ctx[None, :], c, jnp.zeros((npad, D), F32)], axis=0)
    mod = _modulation(cond, mod_w[l], mod_b[l]).reshape(ncond + npad, N_MOD, D)
    mod_ctx, mod_lat = mod[0:1], mod[1:ncond]

    o1 = ATTN_WIDTH
    o3 = o1 + 2 * KV_WIDTH
    o4 = o3 + SSM_WIDTH
    wi = w_in[l]
    scale = HEAD_DIM ** -0.5 * LOG2_E
    wts = {
        "w_in": jnp.concatenate([wi[:, :o1] * scale, wi[:, o1:]], axis=1).astype(BF16),
        "attn_sink": attn_sink[l],
        "w_glu": w_glu[l].astype(BF16), "w_attn_br": w_attn_br[l].astype(BF16),
        "w_ssm_br": w_ssm_br[l].astype(BF16), "w_out": w_out[l].astype(BF16),
        "ln1_g": ln1_g[l].reshape(1, D), "ln1_b": ln1_b[l].reshape(1, D),
        "ln2_g": ln2_g[l].reshape(1, D), "ln2_b": ln2_b[l].reshape(1, D),
        "router_wt": router_w[l].T, "router_b": router_b[l].reshape(N_EXPERTS, 1),
        "exp_w_gate": exp_w_gate[l], "exp_w_up": exp_w_up[l], "exp_w_down": exp_w_down[l],
        "sh_w_gate": sh_w_gate[l].astype(BF16), "sh_w_up": sh_w_up[l].astype(BF16),
        "sh_w_down": sh_w_down[l].astype(BF16),
    }
    ssm = _ssm_params(ssm_a_re[l], ssm_a_im[l], ssm_log_dt[l], ssm_b_re[l], ssm_b_im[l],
                      ssm_c_re[l], ssm_c_im[l], ssm_d[l])

    no_tokens_yet = jnp.zeros((N_EXPERTS, 1), jnp.int32)
    grp_p, counts, k_p, v_p, fs_p = _mixer_and_router(x_prompt, mod_ctx, wts, ssm, None, no_tokens_yet, alpha)

    past = cache_k.shape[2]
    k_ctx = cache_k[:, l].reshape(nb_s, past, KV_WIDTH)
    v_ctx = cache_v[:, l].reshape(nb_s, past, KV_WIDTH)
    h0 = jnp.swapaxes(_state_to_lanes(state_ssm[:, l]), 0, 1)
    grp_s, counts, _, _, _ = _mixer_and_router(x_sample, mod_lat, wts, ssm, (k_ctx, v_ctx, h0), counts, alpha)
    yp, ys_ = _moe([grp_p, grp_s], counts, wts, alpha)

    Lp = x_prompt.shape[1]
    new_k = k_p.reshape(nb_p, 1, Lp, N_KV_HEADS, HEAD_DIM)
    new_v = v_p.reshape(nb_p, 1, Lp, N_KV_HEADS, HEAD_DIM)
    new_s = _lanes_to_state(jnp.swapaxes(fs_p, 0, 1))[:, None]
    return (yp, ys_, new_k, new_v, new_s)
```

```python
import functools
import math

import numpy as np
import jax
import jax.numpy as jnp
from jax import lax
from jax.experimental import pallas as pl
from jax.experimental.pallas import tpu as pltpu
from jax.experimental.pallas import tpu_sc as plsc

F32 = jnp.float32
BF16 = jnp.bfloat16

GRID_W = 64
HEAD_DIM = 64
N_Q_HEADS = 8
N_KV_HEADS = 2
Q_PER_KV = N_Q_HEADS // N_KV_HEADS
ATTN_WIDTH = N_Q_HEADS * HEAD_DIM
KV_WIDTH = N_KV_HEADS * HEAD_DIM
WINDOW = 128
ROPE_BASE = 10000.0
SSM_WIDTH = 512
SSM_GROUP = 16
SSM_GROUPS = SSM_WIDTH // SSM_GROUP
SSM_STATE = 64
N_EXPERTS = 64
TOP_K = 6
ROUTED_SCALE = 2.5
N_MOD = 6
LN_EPS = 1e-5
LOG2_E = math.log2(math.e)

SUBLANES = 8
LANES = 128
VMEM_LIMIT = 48 * 1024 * 1024

INPROJ_ROWS = 512
ATTN_QBLOCKS = 4
SSM_STEPS = 64
SSM_STRIPS = 4
MERGE_ROWS = 512
MERGE_SPLIT = 2
DISPATCH_ROWS = 256
GMM_ROWS = 512
GMM_SPLIT = 1
COMBINE_ROWS = 256
SC_WINDOW = 128


def _cparams(sem):
    return pltpu.CompilerParams(dimension_semantics=sem, vmem_limit_bytes=VMEM_LIMIT)


def _mod_kernel(c_ref, w_ref, b_ref, o_ref):
    c = c_ref[...]
    s = c * jax.nn.sigmoid(c)
    o_ref[...] = jnp.dot(s, w_ref[...], preferred_element_type=F32,
                         precision=lax.Precision.HIGHEST) + b_ref[...]


def _modulation(cond, w, b):
    n, d = cond.shape
    nout = w.shape[1]
    tn = 512
    return pl.pallas_call(
        _mod_kernel,
        grid=(nout // tn,),
        in_specs=[pl.BlockSpec((n, d), lambda j: (0, 0)),
                  pl.BlockSpec((d, tn), lambda j: (0, j)),
                  pl.BlockSpec((1, tn), lambda j: (0, j))],
        out_specs=pl.BlockSpec((n, tn), lambda j: (0, j)),
        out_shape=jax.ShapeDtypeStruct((n, nout), F32),
        compiler_params=_cparams(("arbitrary",)),
        name="modulation",
    )(cond, w, b.reshape(1, nout))


def _rope_rotate(t, cos, sin):
    lane = lax.broadcasted_iota(jnp.int32, t.shape, 1)
    partner = jnp.where((lane % 32) < 16, pltpu.roll(t, LANES - 16, 1), pltpu.roll(t, 16, 1))
    return t * cos + partner * sin


def _inproj_kernel(x_ref, mod_ref, w_ref, cos_ref, sin_ref, q_ref, k_ref, v_ref, u_ref, ga_ref, gs_ref, *, rope):
    x = x_ref[...]
    h = (x * (1.0 + mod_ref[1:2, :]) + mod_ref[0:1, :]).astype(BF16)
    p = jnp.dot(h, w_ref[...], preferred_element_type=F32)
    o1 = ATTN_WIDTH
    o2 = o1 + KV_WIDTH
    o3 = o2 + KV_WIDTH
    d = x.shape[1]
    q = p[:, :o1]
    k = p[:, o1:o2]
    if rope:
        cos = cos_ref[...]
        sin = sin_ref[...]
        q = jnp.concatenate([_rope_rotate(q[:, j * LANES:(j + 1) * LANES], cos, sin)
                             for j in range(o1 // LANES)], axis=1)
        k = _rope_rotate(k, cos, sin)
    o4 = o3 + SSM_WIDTH
    q_ref[...] = q.astype(BF16)
    k_ref[...] = k
    v_ref[...] = p[:, o2:o3]
    u_ref[...] = p[:, o3:o4]
    ga_ref[...] = p[:, o4:o4 + d].astype(BF16)
    gs_ref[...] = p[:, o4 + d:].astype(BF16)


def _in_projection(x, mod, w, cos, sin, rope):
    B, L, D = x.shape
    tm = min(INPROJ_ROWS, L)
    per_batch = mod.shape[0] > 1
    nw = w.shape[1]
    row = lambda b, i: (b, i, 0)
    outs = pl.pallas_call(
        functools.partial(_inproj_kernel, rope=rope),
        grid=(B, L // tm),
        in_specs=[pl.BlockSpec((None, tm, D), row),
                  pl.BlockSpec((None, N_MOD, D), (lambda b, i: (b, 0, 0)) if per_batch else (lambda b, i: (0, 0, 0))),
                  pl.BlockSpec((D, nw), lambda b, i: (0, 0)),
                  pl.BlockSpec((tm, LANES), lambda b, i: (i, 0)),
                  pl.BlockSpec((tm, LANES), lambda b, i: (i, 0))],
        out_specs=[pl.BlockSpec((None, tm, ATTN_WIDTH), row),
                   pl.BlockSpec((None, tm, KV_WIDTH), row),
                   pl.BlockSpec((None, tm, KV_WIDTH), row),
                   pl.BlockSpec((None, tm, SSM_WIDTH), row),
                   pl.BlockSpec((None, tm, D), row),
                   pl.BlockSpec((None, tm, D), row)],
        out_shape=[jax.ShapeDtypeStruct((B, L, ATTN_WIDTH), BF16),
                   jax.ShapeDtypeStruct((B, L, KV_WIDTH), F32),
                   jax.ShapeDtypeStruct((B, L, KV_WIDTH), F32),
                   jax.ShapeDtypeStruct((B, L, SSM_WIDTH), F32),
                   jax.ShapeDtypeStruct((B, L, D), BF16),
                   jax.ShapeDtypeStruct((B, L, D), BF16)],
        compiler_params=_cparams(("parallel", "parallel")),
        name="in_projection",
    )(x, mod, w, cos, sin)
    return outs


def _rope_cos_sin(n_tokens):
    t = jnp.arange(n_tokens, dtype=jnp.int32)
    pos = jnp.stack([t // GRID_W, t % GRID_W], axis=-1).astype(F32)
    n_freq = HEAD_DIM // 4
    inv_freq = ROPE_BASE ** (-jnp.arange(n_freq, dtype=F32) / n_freq)
    ang = pos[:, :, None] * inv_freq
    c, s = jnp.cos(ang), jnp.sin(ang)
    cos = jnp.concatenate([c[:, 0], c[:, 0], c[:, 1], c[:, 1]], axis=-1)
    sin = jnp.concatenate([-s[:, 0], s[:, 0], -s[:, 1], s[:, 1]], axis=-1)
    return jnp.tile(cos, (1, LANES // HEAD_DIM)), jnp.tile(sin, (1, LANES // HEAD_DIM))


def _attend(sink_ref, q, kcat, vcat, mask, o_ref, row0=0):
    lq = q.shape[0]
    lane = lax.broadcasted_iota(jnp.int32, (1, LANES), 1)
    low = lane < HEAD_DIM
    k_sw = pltpu.roll(kcat, HEAD_DIM, 1)
    v_sw = pltpu.roll(vcat, HEAD_DIM, 1)
    neg = jnp.finfo(F32).min
    scores, vds = [], []
    for h in range(N_KV_HEADS):
        keep = low if h == 0 else jnp.logical_not(low)
        kd = jnp.where(keep, kcat, k_sw).astype(BF16)
        vds.append(jnp.where(keep, vcat, v_sw).astype(BF16))
        qs = []
        for j in range(Q_PER_KV):
            head = h * Q_PER_KV + j
            blk = q[:, (head // 2) * LANES:(head // 2 + 1) * LANES]
            sel = low if head % 2 == 0 else jnp.logical_not(low)
            qs.append(jnp.where(sel, blk, jnp.zeros_like(blk)))
        qstack = jnp.concatenate(qs, axis=0)
        scores.append(lax.dot_general(qstack, kd, (((1,), (1,)), ((), ())), preferred_element_type=F32))
    probs, denoms = [], []
    for h in range(N_KV_HEADS):
        s = scores[h]
        ps, ls = [], []
        for j in range(Q_PER_KV):
            sj = s[j * lq:(j + 1) * lq]
            if mask is not None:
                lm = mask.shape[1]
                sj = jnp.concatenate([jnp.where(mask, sj[:, :lm], neg), sj[:, lm:]], axis=1)
            sink = sink_ref[h * Q_PER_KV + j] * LOG2_E
            m = jnp.maximum(jnp.max(sj, axis=1, keepdims=True), sink)
            pj = jnp.exp2(sj - m)
            ls.append(jnp.sum(pj, axis=1, keepdims=True) + jnp.exp2(sink - m))
            ps.append(pj.astype(BF16))
        probs.append(jnp.concatenate(ps, axis=0))
        denoms.append(ls)
    for h in range(N_KV_HEADS):
        ls = denoms[h]
        o = jnp.dot(probs[h], vds[h], preferred_element_type=F32)
        for jj in range(Q_PER_KV // 2):
            oe = o[(2 * jj) * lq:(2 * jj + 1) * lq] / ls[2 * jj]
            oo = o[(2 * jj + 1) * lq:(2 * jj + 2) * lq] / ls[2 * jj + 1]
            cb = (h * Q_PER_KV) // 2 + jj
            o_ref[row0:row0 + lq, cb * LANES:(cb + 1) * LANES] = jnp.where(low, oe, oo).astype(o_ref.dtype)


def _attn_ctx_kernel(sink_ref, q_ref, k_ref, v_ref, o_ref):
    _attend(sink_ref, q_ref[...], k_ref[...], v_ref[...], None, o_ref)


def _attn_lat_kernel(sink_ref, q_ref, kp_ref, kc_ref, kn_ref, vp_ref, vc_ref, vn_ref, kx_ref, vx_ref, o_ref):
    n = pl.program_id(1)
    nb = pl.num_programs(1)
    blk = kp_ref.shape[0]
    nq = q_ref.shape[0] // blk
    klocal = jnp.concatenate([kp_ref[...], kc_ref[...], kn_ref[...]], axis=0)
    vlocal = jnp.concatenate([vp_ref[...], vc_ref[...], vn_ref[...]], axis=0)
    qi = lax.broadcasted_iota(jnp.int32, (blk, 3 * blk), 0)
    kj = lax.broadcasted_iota(jnp.int32, (blk, 3 * blk), 1)
    rel = kj - blk - qi
    in_band = (rel <= WINDOW) & (rel >= -WINDOW)
    for j in range(nq):
        mask = in_band
        if j == 0:
            mask = mask & ((kj >= blk) | (n > 0))
        if j == nq - 1:
            mask = mask & ((kj < 2 * blk) | (n < nb - 1))
        kcat = jnp.concatenate([klocal[j * blk:(j + 3) * blk], kx_ref[...]], axis=0)
        vcat = jnp.concatenate([vlocal[j * blk:(j + 3) * blk], vx_ref[...]], axis=0)
        _attend(sink_ref, q_ref[j * blk:(j + 1) * blk, :], kcat, vcat, mask, o_ref, j * blk)


def _attention_context(q, k, v, sink):
    B, L, _ = q.shape
    row = lambda b: (b, 0, 0)
    return pl.pallas_call(
        _attn_ctx_kernel,
        grid=(B,),
        in_specs=[pl.BlockSpec(memory_space=pltpu.SMEM),
                  pl.BlockSpec((None, L, ATTN_WIDTH), row),
                  pl.BlockSpec((None, L, KV_WIDTH), row),
                  pl.BlockSpec((None, L, KV_WIDTH), row)],
        out_specs=pl.BlockSpec((None, L, ATTN_WIDTH), row),
        out_shape=jax.ShapeDtypeStruct((B, L, ATTN_WIDTH), BF16),
        compiler_params=_cparams(("parallel",)),
        name="attention_context",
    )(sink, q, k, v)


def _attention_latent(q, k, v, k_ctx, v_ctx, sink):
    B, S, _ = q.shape
    blk = WINDOW
    nq = ATTN_QBLOCKS
    nb = S // blk
    nctx = k_ctx.shape[1]
    cur = lambda b, n: (b, n, 0)
    prv = lambda b, n: (b, jnp.maximum(n * nq - 1, 0), 0)
    nxt = lambda b, n: (b, jnp.minimum(n * nq + nq, nb - 1), 0)
    ctx = lambda b, n: (b, 0, 0)
    edge = lambda im: pl.BlockSpec((None, blk, KV_WIDTH), im)
    mid = pl.BlockSpec((None, nq * blk, KV_WIDTH), cur)
    return pl.pallas_call(
        _attn_lat_kernel,
        grid=(B, nb // nq),
        in_specs=[pl.BlockSpec(memory_space=pltpu.SMEM),
                  pl.BlockSpec((None, nq * blk, ATTN_WIDTH), cur),
                  edge(prv), mid, edge(nxt), edge(prv), mid, edge(nxt),
                  pl.BlockSpec((None, nctx, KV_WIDTH), ctx),
                  pl.BlockSpec((None, nctx, KV_WIDTH), ctx)],
        out_specs=pl.BlockSpec((None, nq * blk, ATTN_WIDTH), cur),
        out_shape=jax.ShapeDtypeStruct((B, S, ATTN_WIDTH), BF16),
        compiler_params=_cparams(("parallel", "parallel")),
        name="attention_latent",
    )(sink, q, k, k, k, v, v, v, k_ctx, v_ctx)


def _ssm_kernel(u_ref, wb_ref, wc_ref, a_ref, d_ref, h0_ref, y_ref, fs_ref, ut_ref, yt_ref, st_ref, *bu_refs):
    rev = pl.program_id(0)
    i = pl.program_id(2)
    nc = pl.num_programs(2)
    nseq, tm, _ = u_ref.shape
    sw = a_ref.shape[-1] // SSM_STRIPS
    nre = sw // 2

    @pl.when(i == 0)
    def _():
        st_ref[...] = h0_ref[...]

    for b in range(nseq):
        ub = u_ref[b]
        for s in range(SSM_STRIPS):
            ut_ref[s, pl.ds(b, tm, stride=nseq), :] = ub[:, s * LANES:(s + 1) * LANES]
    for s in range(SSM_STRIPS):
        bu_refs[s][...] = jnp.dot(ut_ref[s].astype(BF16), wb_ref[s], preferred_element_type=F32)

    for s in range(SSM_STRIPS):
        bu_ref = bu_refs[s]
        a_re = a_ref[:, s * sw:s * sw + nre]
        a_im = a_ref[:, s * sw + nre:(s + 1) * sw]

        def step(t, carry):
            xr, xi = carry
            r = pl.multiple_of((t + rev * (tm - 1 - 2 * t)) * nseq, nseq)
            nr = a_re * xr - a_im * xi + bu_ref[pl.ds(r, nseq), 0:nre]
            ni = a_re * xi + a_im * xr + bu_ref[pl.ds(r, nseq), nre:sw]
            bu_ref[pl.ds(r, nseq), 0:nre] = nr
            bu_ref[pl.ds(r, nseq), nre:sw] = ni
            return nr, ni

        xr, xi = lax.fori_loop(0, tm, step, (st_ref[:, s * sw:s * sw + nre], st_ref[:, s * sw + nre:(s + 1) * sw]),
                               unroll=True)
        st_ref[:, s * sw:s * sw + nre] = xr
        st_ref[:, s * sw + nre:(s + 1) * sw] = xi
        yt_ref[s] = jnp.dot(bu_ref[...].astype(BF16), wc_ref[s], preferred_element_type=F32)

    for b in range(nseq):
        yb = jnp.concatenate([yt_ref[s, pl.ds(b, tm, stride=nseq), :] for s in range(SSM_STRIPS)], axis=1)
        y_ref[b] = yb + u_ref[b] * d_ref[...]

    @pl.when(i == nc - 1)
    def _():
        fs_ref[...] = st_ref[...]


def _ssm_mix(u, ssm, h0):
    B, L, _ = u.shape
    nseq = SUBLANES
    tm = min(SSM_STEPS, L)
    nc = L // tm
    rows = nseq * tm
    ns = 2 * SSM_GROUPS * SSM_STATE
    chunk = lambda d, g, i: i + d * (nc - 1 - 2 * i)
    y, fs = pl.pallas_call(
        _ssm_kernel,
        grid=(2, B // nseq, nc),
        in_specs=[pl.BlockSpec((nseq, tm, SSM_WIDTH), lambda d, g, i: (g, chunk(d, g, i), 0)),
                  pl.BlockSpec((None, SSM_STRIPS, LANES, ns // SSM_STRIPS), lambda d, g, i: (d, 0, 0, 0)),
                  pl.BlockSpec((None, SSM_STRIPS, ns // SSM_STRIPS, LANES), lambda d, g, i: (d, 0, 0, 0)),
                  pl.BlockSpec((None, nseq, ns), lambda d, g, i: (d, 0, 0)),
                  pl.BlockSpec((None, 1, SSM_WIDTH), lambda d, g, i: (d, 0, 0)),
                  pl.BlockSpec((None, nseq, ns), lambda d, g, i: (d, g, 0))],
        out_specs=[pl.BlockSpec((None, nseq, tm, SSM_WIDTH), lambda d, g, i: (d, g, chunk(d, g, i), 0)),
                   pl.BlockSpec((None, nseq, ns), lambda d, g, i: (d, g, 0))],
        out_shape=[jax.ShapeDtypeStruct((2, B, L, SSM_WIDTH), F32),
                   jax.ShapeDtypeStruct((2, B, ns), F32)],
        scratch_shapes=[pltpu.VMEM((SSM_STRIPS, rows, LANES), F32), pltpu.VMEM((SSM_STRIPS, rows, LANES), F32),
                        pltpu.VMEM((nseq, ns), F32)]
        + [pltpu.VMEM((rows, ns // SSM_STRIPS), F32) for _ in range(SSM_STRIPS)],
        compiler_params=_cparams(("arbitrary", "arbitrary", "arbitrary")),
        name="ssm_scan",
    )(u, ssm["wb"], ssm["wc"], ssm["a"], ssm["d"], h0)
    return y, fs


def _state_to_lanes(s):
    lead = s.shape[:-3]
    s = s.reshape(lead + (2, SSM_STRIPS, SSM_GROUPS // SSM_STRIPS, SSM_STATE))
    s = jnp.swapaxes(s, -4, -3)
    return s.reshape(lead + (2 * SSM_GROUPS * SSM_STATE,))


def _lanes_to_state(v):
    lead = v.shape[:-1]
    s = v.reshape(lead + (SSM_STRIPS, 2, SSM_GROUPS // SSM_STRIPS, SSM_STATE))
    s = jnp.swapaxes(s, -4, -3)
    return s.reshape(lead + (2, SSM_GROUPS, SSM_STATE))


def _ssm_params(a_re, a_im, log_dt, b_re, b_im, c_re, c_im, dvec):
    G, N, C = SSM_GROUPS, SSM_STATE, SSM_GROUP
    lam_re = jnp.minimum(a_re, -1e-4)
    lam_im = a_im
    dt = jnp.exp(log_dt)[..., None]
    mag = jnp.exp(lam_re * dt)
    abar_re, abar_im = mag * jnp.cos(lam_im * dt), mag * jnp.sin(lam_im * dt)
    den = jnp.square(lam_re) + jnp.square(lam_im)
    p, qi = abar_re - 1.0, abar_im
    f_re = (p * lam_re + qi * lam_im) / den
    f_im = (qi * lam_re - p * lam_im) / den
    bbar_re = f_re[..., None] * b_re - f_im[..., None] * b_im
    bbar_im = f_re[..., None] * b_im + f_im[..., None] * b_re
    a = _state_to_lanes(jnp.stack([abar_re, abar_im], axis=1))
    a = jnp.broadcast_to(a[:, None, :], (2, SUBLANES, a.shape[-1]))
    S = SSM_STRIPS
    gs = G // S
    eye = jnp.eye(gs, dtype=F32)

    def bd_in(bb):
        bb = bb.reshape(2, S, gs, N, C)
        return jnp.einsum('dkgnc,gh->dkgchn', bb, eye).reshape(2, S, gs * C, gs * N)

    def bd_out(cc):
        cc = cc.reshape(2, S, gs, C, N)
        return jnp.einsum('dkgcn,gh->dkgnhc', cc, eye).reshape(2, S, gs * N, gs * C)

    wb = jnp.concatenate([bd_in(bbar_re), bd_in(bbar_im)], axis=-1).astype(BF16)
    wc = jnp.concatenate([bd_out(c_re), -bd_out(c_im)], axis=-2).astype(BF16)
    d = jnp.stack([dvec, jnp.zeros_like(dvec)], axis=0).reshape(2, 1, SSM_WIDTH)
    return {"a": a, "wb": wb, "wc": wc, "d": d}


def _layer_norm(x, g, b):
    mu = jnp.mean(x, axis=-1, keepdims=True)
    xc = x - mu
    var = jnp.mean(xc * xc, axis=-1, keepdims=True)
    return xc * lax.rsqrt(var + LN_EPS) * g + b


def _pack_rows(x):
    w = x.shape[1] // 2
    return pltpu.pack_elementwise([x[:, :w], x[:, w:]], packed_dtype=BF16)


def _unpack_rows(p):
    return (pltpu.unpack_elementwise(p, index=0, packed_dtype=BF16, unpacked_dtype=F32),
            pltpu.unpack_elementwise(p, index=1, packed_dtype=BF16, unpacked_dtype=F32))


def _gelu_tanh(x):
    return 0.5 * x * (1.0 + jnp.tanh(math.sqrt(2.0 / math.pi) * (x + 0.044715 * (x * x * x))))


def _merge_kernel(x_ref, ao_ref, yf_ref, yb_ref, ga_ref, gs_ref, mod_ref, wglu_ref, wa_ref, ws_ref, wo_ref,
                  lng_ref, lnb_ref, rwt_ref, rb_ref, run0_ref, x1_ref, h2_ref, idx_ref, rank_ref, wt_ref, cnt_ref,
                  run_ref, *, alpha):
    @pl.when((pl.program_id(0) == 0) & (pl.program_id(1) == 0))
    def _():
        run_ref[...] = run0_ref[...].astype(F32)

    nrow = x_ref.shape[0]
    sub = nrow // MERGE_SPLIT
    groups = [slice(j * sub, (j + 1) * sub) for j in range(MERGE_SPLIT)]
    dot = functools.partial(jnp.dot, preferred_element_type=F32)
    z = [_gelu_tanh(yf_ref[rs, :] + yb_ref[rs, :]) for rs in groups]
    attn_br = [dot(ao_ref[rs, :], wa_ref[...]) for rs in groups]
    gate = [jax.nn.sigmoid(dot(zg.astype(BF16), wglu_ref[...])) for zg in z]
    ssm_br = [dot((zg * gg).astype(BF16), ws_ref[...]) for zg, gg in zip(z, gate)]
    merged = [jax.nn.sigmoid(ga_ref[rs, :].astype(F32)) * ab + jax.nn.sigmoid(gs_ref[rs, :].astype(F32)) * sb
              for rs, ab, sb in zip(groups, attn_br, ssm_br)]
    mix = [dot(mg.astype(BF16), wo_ref[...]) for mg in merged]
    h2s = []
    for rs, mg in zip(groups, mix):
        x1 = _layer_norm(alpha * x_ref[rs, :] + mod_ref[2:3, :] * mg, lng_ref[...], lnb_ref[...])
        h2g = x1 * (1.0 + mod_ref[4:5, :]) + mod_ref[3:4, :]
        x1_ref[rs, :] = x1
        h2_ref[rs, :] = _pack_rows(h2g)
        h2s.append(h2g)
    h2 = jnp.concatenate(h2s, axis=0)

    logits = lax.dot_general(rwt_ref[...], h2, (((1,), (1,)), ((), ())), preferred_element_type=F32,
                             precision=lax.Precision.HIGHEST)
    score = jax.nn.sigmoid(logits)
    tm = score.shape[1]
    eidx = lax.broadcasted_iota(jnp.int32, score.shape, 0).astype(F32)
    work = score + rb_ref[...]
    picks, sel = [], []
    member = jnp.zeros_like(score)
    for _ in range(TOP_K):
        best = jnp.max(work, axis=0, keepdims=True)
        pick = jnp.min(jnp.where(work == best, eidx, float(N_EXPERTS)), axis=0, keepdims=True)
        hit = eidx == pick
        picks.append(pick)
        sel.append(jnp.sum(jnp.where(hit, score, 0.0), axis=0, keepdims=True))
        member = member + hit.astype(F32)
        work = jnp.where(hit, -jnp.inf, work)
    total = sel[0]
    for s in sel[1:]:
        total = total + s
    before = (lax.broadcasted_iota(jnp.int32, (tm, tm), 0) < lax.broadcasted_iota(jnp.int32, (tm, tm), 1))
    prefix = jnp.dot(member.astype(BF16), before.astype(BF16), preferred_element_type=F32)
    base = prefix + run_ref[...]
    for k in range(TOP_K):
        idx_ref[k:k + 1, :] = picks[k].astype(jnp.int32)
        rank_ref[k:k + 1, :] = jnp.sum(jnp.where(eidx == picks[k], base, 0.0), axis=0,
                                       keepdims=True).astype(jnp.int32)
        wt_ref[k:k + 1, :] = sel[k] / total * ROUTED_SCALE
    run_ref[...] = run_ref[...] + jnp.sum(member, axis=1, keepdims=True)
    cnt_ref[...] = run_ref[...].astype(jnp.int32)


def _merge(x, attn_o, y, ga, gs, mod, wts, run0, alpha):
    B, L, D = x.shape
    tm = min(MERGE_ROWS, L)
    per_batch = mod.shape[0] > 1
    row = lambda b, i: (b, i, 0)
    full = lambda b, i: (0, 0)
    wspec = lambda a: pl.BlockSpec(a.shape, full)
    return pl.pallas_call(
        functools.partial(_merge_kernel, alpha=alpha),
        grid=(B, L // tm),
        in_specs=[pl.BlockSpec((None, tm, D), row),
                  pl.BlockSpec((None, tm, ATTN_WIDTH), row),
                  pl.BlockSpec((None, None, tm, SSM_WIDTH), lambda b, i: (0, b, i, 0)),
                  pl.BlockSpec((None, None, tm, SSM_WIDTH), lambda b, i: (1, b, i, 0)),
                  pl.BlockSpec((None, tm, D), row),
                  pl.BlockSpec((None, tm, D), row),
                  pl.BlockSpec((None, N_MOD, D), (lambda b, i: (b, 0, 0)) if per_batch else (lambda b, i: (0, 0, 0))),
                  wspec(wts["w_glu"]), wspec(wts["w_attn_br"]), wspec(wts["w_ssm_br"]), wspec(wts["w_out"]),
                  wspec(wts["ln1_g"]), wspec(wts["ln1_b"]), wspec(wts["router_wt"]), wspec(wts["router_b"]),
                  wspec(run0)],
        out_specs=[pl.BlockSpec((None, tm, D), row),
                   pl.BlockSpec((None, tm, D // 2), row),
                   pl.BlockSpec((None, TOP_K, tm), lambda b, i: (b, 0, i)),
                   pl.BlockSpec((None, TOP_K, tm), lambda b, i: (b, 0, i)),
                   pl.BlockSpec((None, TOP_K, tm), lambda b, i: (b, 0, i)),
                   pl.BlockSpec((N_EXPERTS, 1), full)],
        out_shape=[jax.ShapeDtypeStruct((B, L, D), F32),
                   jax.ShapeDtypeStruct((B, L, D // 2), jnp.int32),
                   jax.ShapeDtypeStruct((B, TOP_K, L), jnp.int32),
                   jax.ShapeDtypeStruct((B, TOP_K, L), jnp.int32),
                   jax.ShapeDtypeStruct((B, TOP_K, L), F32),
                   jax.ShapeDtypeStruct((N_EXPERTS, 1), jnp.int32)],
        scratch_shapes=[pltpu.VMEM((N_EXPERTS, 1), F32)],
        compiler_params=_cparams(("arbitrary", "arbitrary")),
        name="merge_ln1_router",
    )(x, attn_o, y, y, ga, gs, mod, wts["w_glu"], wts["w_attn_br"], wts["w_ssm_br"], wts["w_out"],
      wts["ln1_g"], wts["ln1_b"], wts["router_wt"], wts["router_b"], run0)


def _dispatch_kernel(dest_ref, *refs, tiles):
    xs_ref, sem = refs[-2:]
    i = pl.program_id(0)
    first = 0
    for h_ref, nt in zip(refs[:-2], tiles):
        tm = h_ref.shape[0]

        @pl.when((i >= first) & (i < first + nt))
        def _(h_ref=h_ref, tm=tm):
            def start(r, c):
                for k in range(TOP_K):
                    pltpu.make_async_copy(h_ref.at[pl.ds(r, 1)], xs_ref.at[pl.ds(dest_ref[0, r * TOP_K + k], 1)],
                                          sem).start(priority=k % 2)
                return c

            lax.fori_loop(0, tm, start, 0, unroll=4)
            for k in range(TOP_K):
                pltpu.make_async_copy(h_ref, xs_ref.at[pl.ds(0, tm)], sem).wait()

        first += nt


def _token_major(a, tm):
    B, K, L = a.shape
    return jnp.swapaxes(a, 1, 2).reshape(B * (L // tm), 1, tm * K)


def _dispatch(h2s, dests):
    tm = DISPATCH_ROWS
    W = h2s[0].shape[-1]
    tiles = [h.shape[0] * h.shape[1] // tm for h in h2s]
    n_slots = sum(d.size for d in dests)
    in_specs = [pl.BlockSpec((None, 1, tm * TOP_K), lambda i: (i, 0, 0), memory_space=pltpu.SMEM)]
    first = 0
    for nt in tiles:
        in_specs.append(pl.BlockSpec((tm, W), lambda i, first=first, nt=nt: (jnp.clip(i - first, 0, nt - 1), 0)))
        first += nt
    return pl.pallas_call(
        functools.partial(_dispatch_kernel, tiles=tiles),
        grid=(sum(tiles),),
        in_specs=in_specs,
        out_specs=pl.BlockSpec(memory_space=pl.ANY),
        out_shape=jax.ShapeDtypeStruct((n_slots, W), h2s[0].dtype),
        scratch_shapes=[pltpu.SemaphoreType.DMA],
        compiler_params=_cparams(("arbitrary",)),
        name="moe_dispatch",
    )(jnp.concatenate([_token_major(d, tm) for d in dests], axis=0), *[h.reshape(-1, W) for h in h2s])


def _gmm_kernel(tile_ref, exp_ref, valid_ref, gstart_ref, xs_ref, wg_ref, wu_ref, wd_ref, ys_ref,
                wgb_ref, wub_ref, wdb_ref):
    w = pl.program_id(0)
    e = exp_ref[w]
    t = tile_ref[w]
    prev = jnp.maximum(w - 1, 0)
    new_expert = (w == 0) | (e != exp_ref[prev])
    first_visit = (w == 0) | (t != tile_ref[prev])

    @pl.when(new_expert)
    def _():
        wgb_ref[...] = wg_ref[...].astype(BF16)
        wub_ref[...] = wu_ref[...].astype(BF16)
        wdb_ref[...] = wd_ref[...].astype(BF16)

    @pl.when(valid_ref[w] == 1)
    def _():
        tg = xs_ref.shape[0]
        sub = tg // GMM_SPLIT

        groups = [slice(j * sub, (j + 1) * sub) for j in range(GMM_SPLIT)]
        dot = functools.partial(jnp.dot, preferred_element_type=F32)
        x = [jnp.concatenate(_unpack_rows(xs_ref[rs, :]), axis=1).astype(BF16) for rs in groups]
        g = [dot(xg, wgb_ref[...]) for xg in x]
        u = [dot(xg, wub_ref[...]) for xg in x]
        a = [(gg * jax.nn.sigmoid(gg) * ug).astype(BF16) for gg, ug in zip(g, u)]
        y = jnp.concatenate([_pack_rows(dot(ag, wdb_ref[...])) for ag in a], axis=0)
        rows = t * tg + lax.broadcasted_iota(jnp.int32, (tg, 1), 0)
        mine = (rows >= gstart_ref[e]) & (rows < gstart_ref[e + 1])

        @pl.when(first_visit)
        def _():
            ys_ref[...] = jnp.where(mine, y, 0)

        @pl.when(jnp.logical_not(first_visit))
        def _():
            ys_ref[...] = jnp.where(mine, y, ys_ref[...])


def _grouped_experts(xs, counts, w_gate, w_up, w_down):
    A, W = xs.shape
    D = w_gate.shape[-2]
    tg = GMM_ROWS
    nt = A // tg
    n_items = nt + N_EXPERTS - 1
    ff = w_gate.shape[-1]
    gend = jnp.cumsum(counts).astype(jnp.int32)
    gstart = jnp.concatenate([jnp.zeros((1,), jnp.int32), gend])
    first_row = jnp.arange(nt, dtype=jnp.int32) * tg
    count_le = lambda ends, v: jnp.sum((ends[None, :] <= v[:, None]).astype(jnp.int32), axis=1)
    e_lo = count_le(gend, first_row)
    e_hi = count_le(gend, first_row + tg - 1)
    per_tile = e_hi - e_lo + 1
    item_end = jnp.cumsum(per_tile).astype(jnp.int32)
    total = item_end[-1]
    wi = jnp.arange(n_items, dtype=jnp.int32)
    tile = jnp.minimum(count_le(item_end, wi), nt - 1)
    in_tile = (tile[:, None] == jnp.arange(nt, dtype=jnp.int32)[None, :]).astype(jnp.int32)
    lookup = lambda table: jnp.sum(in_tile * table[None, :], axis=1)
    expert = lookup(e_lo) + wi - lookup(item_end - per_tile)
    valid = (wi < total).astype(jnp.int32)
    expert = jnp.where(valid == 1, expert, e_hi[nt - 1])
    grid_spec = pltpu.PrefetchScalarGridSpec(
        num_scalar_prefetch=4,
        grid=(n_items,),
        in_specs=[pl.BlockSpec((tg, W), lambda w, tl, ex, va, gs: (tl[w], 0)),
                  pl.BlockSpec((None, D, ff), lambda w, tl, ex, va, gs: (ex[w], 0, 0)),
                  pl.BlockSpec((None, D, ff), lambda w, tl, ex, va, gs: (ex[w], 0, 0)),
                  pl.BlockSpec((None, ff, D), lambda w, tl, ex, va, gs: (ex[w], 0, 0))],
        out_specs=pl.BlockSpec((tg, W), lambda w, tl, ex, va, gs: (tl[w], 0)),
        scratch_shapes=[pltpu.VMEM((D, ff), BF16), pltpu.VMEM((D, ff), BF16), pltpu.VMEM((ff, D), BF16)],
    )
    return pl.pallas_call(
        _gmm_kernel,
        grid_spec=grid_spec,
        out_shape=jax.ShapeDtypeStruct((A, W), xs.dtype),
        compiler_params=_cparams(("arbitrary",)),
        name="moe_grouped_experts",
    )(tile, expert, valid, gstart, xs, w_gate, w_up, w_down)


def _gather_rows_sc(rows, index):
    n = index.shape[0]
    w = rows.shape[1]
    mesh = plsc.VectorSubcoreMesh(core_axis_name="core", subcore_axis_name="subcore")
    per_subcore = n // SC_WINDOW // (mesh.num_cores * mesh.num_subcores)
    assert per_subcore * SC_WINDOW * mesh.num_cores * mesh.num_subcores == n

    @pl.kernel(out_type=jax.ShapeDtypeStruct((n, w), rows.dtype), mesh=mesh,
               scratch_types=[pltpu.VMEM((1, SC_WINDOW), jnp.int32), pltpu.VMEM((SC_WINDOW, w), rows.dtype)])
    def gather(rows_hbm, index_hbm, out_hbm, index_v, rows_v):
        worker = lax.axis_index("core") * mesh.num_subcores + lax.axis_index("subcore")

        @pl.loop(0, per_subcore)
        def _(j):
            first = (worker * per_subcore + j) * SC_WINDOW
            pltpu.sync_copy(index_hbm.at[:, pl.ds(first, SC_WINDOW)], index_v)
            pltpu.sync_copy(rows_hbm.at[index_v.at[0]], rows_v)
            pltpu.sync_copy(rows_v, out_hbm.at[pl.ds(first, SC_WINDOW)])

    return gather(rows, index.reshape(1, n))


def _combine_rows_kernel(yg_ref, wt_ref, x1_ref, h2_ref, mod_ref, sg_ref, su_ref, sd_ref, lng_ref, lnb_ref, o_ref,
                         *, alpha):
    h = jnp.concatenate(_unpack_rows(h2_ref[...]), axis=1).astype(BF16)
    g = jnp.dot(h, sg_ref[...], preferred_element_type=F32)
    u = jnp.dot(h, su_ref[...], preferred_element_type=F32)
    moe = jnp.dot((g * jax.nn.sigmoid(g) * u).astype(BF16), sd_ref[...], preferred_element_type=F32)
    wt = wt_ref[...]
    w = h2_ref.shape[1]
    lo = jnp.zeros(h2_ref.shape, F32)
    hi = lo
    for k in range(TOP_K):
        rl, rh = _unpack_rows(yg_ref[:, k * w:(k + 1) * w])
        lo = lo + wt[:, k:k + 1] * rl
        hi = hi + wt[:, k:k + 1] * rh
    moe = moe + jnp.concatenate([lo, hi], axis=1)
    o_ref[...] = _layer_norm(alpha * x1_ref[...] + mod_ref[5:6, :] * moe, lng_ref[...], lnb_ref[...])


def _combine_rows(ys, dest, wts, x1, h2, mod, shared, alpha):
    B, L, D = x1.shape
    W = ys.shape[-1]
    tm = min(COMBINE_ROWS, L)
    per_batch = mod.shape[0] > 1
    slots = jnp.swapaxes(dest, 1, 2).reshape(B * L * TOP_K)
    yg = _gather_rows_sc(ys, slots).reshape(B, L, TOP_K * W)
    row = lambda b, i: (b, i, 0)
    full = lambda b, i: (0, 0)
    wspec = lambda a: pl.BlockSpec(a.shape, full)
    return pl.pallas_call(
        functools.partial(_combine_rows_kernel, alpha=alpha),
        grid=(B, L // tm),
        in_specs=[pl.BlockSpec((None, tm, TOP_K * W), row),
                  pl.BlockSpec((None, tm, TOP_K), row),
                  pl.BlockSpec((None, tm, D), row),
                  pl.BlockSpec((None, tm, W), row),
                  pl.BlockSpec((None, N_MOD, D), (lambda b, i: (b, 0, 0)) if per_batch else (lambda b, i: (0, 0, 0))),
                  wspec(shared["sh_w_gate"]), wspec(shared["sh_w_up"]), wspec(shared["sh_w_down"]),
                  wspec(shared["ln2_g"]), wspec(shared["ln2_b"])],
        out_specs=pl.BlockSpec((None, tm, D), row),
        out_shape=jax.ShapeDtypeStruct((B, L, D), F32),
        compiler_params=_cparams(("parallel", "parallel")),
        name="moe_combine_ln2",
    )(yg, jnp.swapaxes(wts, 1, 2), x1, h2, mod, shared["sh_w_gate"], shared["sh_w_up"], shared["sh_w_down"],
      shared["ln2_g"], shared["ln2_b"])


def _slots(idx, rank, counts):
    start = jnp.cumsum(counts) - counts
    pick = idx[..., None] == jnp.arange(N_EXPERTS, dtype=jnp.int32)
    return jnp.sum(jnp.where(pick, start, 0), axis=-1).astype(jnp.int32) + rank


def _mixer_and_router(x, mod, wts, ssm, cache, run0, alpha):
    B, L, D = x.shape
    latent = cache is not None
    if latent:
        k_ctx, v_ctx, h0 = cache
        cos, sin = _rope_cos_sin(L)
    else:
        cos = jnp.zeros((L, LANES), F32)
        sin = cos
        h0 = jnp.zeros((2, B, 2 * SSM_GROUPS * SSM_STATE), F32)
    q, k, v, u, ga, gs = _in_projection(x, mod, wts["w_in"], cos, sin, rope=latent)
    if latent:
        attn_o = _attention_latent(q, k, v, k_ctx, v_ctx, wts["attn_sink"])
    else:
        attn_o = _attention_context(q, k, v, wts["attn_sink"])
    y, fs = _ssm_mix(u, ssm, h0)
    x1, h2, idx, rank, rw, counts = _merge(x, attn_o, y, ga, gs, mod, wts, run0, alpha)
    return {"x1": x1, "h2": h2, "idx": idx, "rank": rank, "rw": rw, "mod": mod}, counts, k, v, fs


def _moe(groups, counts, wts, alpha):
    counts = counts.reshape(N_EXPERTS)
    for g in groups:
        g["dest"] = _slots(g["idx"], g["rank"], counts)
    xs = _dispatch([g["h2"] for g in groups], [g["dest"] for g in groups])
    ys = _grouped_experts(xs, counts, wts["exp_w_gate"], wts["exp_w_up"], wts["exp_w_down"])
    return [_combine_rows(ys, g["dest"], g["rw"], g["x1"], g["h2"], g["mod"], wts, alpha) for g in groups]


def kernel(x_prompt, x_sample, cache_k, cache_v, state_ssm, c, c_ctx, mod_w, mod_b, w_in, attn_sink, w_attn_br, ssm_a_re, ssm_a_im, ssm_log_dt, ssm_b_re, ssm_b_im, ssm_c_re, ssm_c_im, ssm_d, w_glu, w_ssm_br, w_out, ln1_g, ln1_b, ln2_g, ln2_b, router_w, router_b, exp_w_gate, exp_w_up, exp_w_down, sh_w_gate, sh_w_up, sh_w_down):
    depth = w_in.shape[0]
    assert depth == 1
    alpha = (2.0 * depth) ** 0.25
    D = x_prompt.shape[-1]
    nb_p = x_prompt.shape[0]
    nb_s = x_sample.shape[0]
    l = 0

    ncond = 1 + nb_s
    npad = -ncond % SUBLANES
    cond = jnp.concatenate([c_ctx[None, :], c, jnp.zeros((npad, D), F32)], axis=0)
    mod = _modulation(cond, mod_w[l], mod_b[l]).reshape(ncond + npad, N_MOD, D)
    mod_ctx, mod_lat = mod[0:1], mod[1:ncond]

    o1 = ATTN_WIDTH
    o3 = o1 + 2 * KV_WIDTH
    o4 = o3 + SSM_WIDTH
    wi = w_in[l]
    scale = HEAD_DIM ** -0.5 * LOG2_E
    wts = {
        "w_in": jnp.concatenate([wi[:, :o1] * scale, wi[:, o1:]], axis=1).astype(BF16),
        "attn_sink": attn_sink[l],
        "w_glu": w_glu[l].astype(BF16), "w_attn_br": w_attn_br[l].astype(BF16),
        "w_ssm_br": w_ssm_br[l].astype(BF16), "w_out": w_out[l].astype(BF16),
        "ln1_g": ln1_g[l].reshape(1, D), "ln1_b": ln1_b[l].reshape(1, D),
        "ln2_g": ln2_g[l].reshape(1, D), "ln2_b": ln2_b[l].reshape(1, D),
        "router_wt": router_w[l].T, "router_b": router_b[l].reshape(N_EXPERTS, 1),
        "exp_w_gate": exp_w_gate[l], "exp_w_up": exp_w_up[l], "exp_w_down": exp_w_down[l],
        "sh_w_gate": sh_w_gate[l].astype(BF16), "sh_w_up": sh_w_up[l].astype(BF16),
        "sh_w_down": sh_w_down[l].astype(BF16),
    }
    ssm = _ssm_params(ssm_a_re[l], ssm_a_im[l], ssm_log_dt[l], ssm_b_re[l], ssm_b_im[l],
                      ssm_c_re[l], ssm_c_im[l], ssm_d[l])

    no_tokens_yet = jnp.zeros((N_EXPERTS, 1), jnp.int32)
    grp_p, counts, k_p, v_p, fs_p = _mixer_and_router(x_prompt, mod_ctx, wts, ssm, None, no_tokens_yet, alpha)

    past = cache_k.shape[2]
    k_ctx = cache_k[:, l].reshape(nb_s, past, KV_WIDTH)
    v_ctx = cache_v[:, l].reshape(nb_s, past, KV_WIDTH)
    h0 = jnp.swapaxes(_state_to_lanes(state_ssm[:, l]), 0, 1)
    grp_s, counts, _, _, _ = _mixer_and_router(x_sample, mod_lat, wts, ssm, (k_ctx, v_ctx, h0), counts, alpha)
    yp, ys_ = _moe([grp_p, grp_s], counts, wts, alpha)

    Lp = x_prompt.shape[1]
    new_k = k_p.reshape(nb_p, 1, Lp, N_KV_HEADS, HEAD_DIM)
    new_v = v_p.reshape(nb_p, 1, Lp, N_KV_HEADS, HEAD_DIM)
    new_s = _lanes_to_state(jnp.swapaxes(fs_p, 0, 1))[:, None]
    return (yp, ys_, new_k, new_v, new_s)
```

```python
import functools
import math

import numpy as np
import jax
import jax.numpy as jnp
from jax import lax
from jax.experimental import pallas as pl
from jax.experimental.pallas import tpu as pltpu
from jax.experimental.pallas import tpu_sc as plsc

F32 = jnp.float32
BF16 = jnp.bfloat16

GRID_W = 64
HEAD_DIM = 64
N_Q_HEADS = 8
N_KV_HEADS = 2
Q_PER_KV = N_Q_HEADS // N_KV_HEADS
ATTN_WIDTH = N_Q_HEADS * HEAD_DIM
KV_WIDTH = N_KV_HEADS * HEAD_DIM
WINDOW = 128
ROPE_BASE = 10000.0
SSM_WIDTH = 512
SSM_GROUP = 16
SSM_GROUPS = SSM_WIDTH // SSM_GROUP
SSM_STATE = 64
N_EXPERTS = 64
TOP_K = 6
ROUTED_SCALE = 2.5
N_MOD = 6
LN_EPS = 1e-5
LOG2_E = math.log2(math.e)

SUBLANES = 8
LANES = 128
VMEM_LIMIT = 48 * 1024 * 1024

INPROJ_ROWS = 512
ATTN_QBLOCKS = 4
SSM_STEPS = 64
SSM_STRIPS = 4
MERGE_ROWS = 512
MERGE_SPLIT = 2
GMM_ROWS = 512
GMM_SPLIT = 1
COMBINE_ROWS = 256
SC_WINDOW = 128


def _cparams(sem):
    return pltpu.CompilerParams(dimension_semantics=sem, vmem_limit_bytes=VMEM_LIMIT)


def _mod_kernel(c_ref, w_ref, b_ref, o_ref):
    c = c_ref[...]
    s = c * jax.nn.sigmoid(c)
    o_ref[...] = jnp.dot(s, w_ref[...], preferred_element_type=F32,
                         precision=lax.Precision.HIGHEST) + b_ref[...]


def _modulation(cond, w, b):
    n, d = cond.shape
    nout = w.shape[1]
    tn = 512
    return pl.pallas_call(
        _mod_kernel,
        grid=(nout // tn,),
        in_specs=[pl.BlockSpec((n, d), lambda j: (0, 0)),
                  pl.BlockSpec((d, tn), lambda j: (0, j)),
                  pl.BlockSpec((1, tn), lambda j: (0, j))],
        out_specs=pl.BlockSpec((n, tn), lambda j: (0, j)),
        out_shape=jax.ShapeDtypeStruct((n, nout), F32),
        compiler_params=_cparams(("arbitrary",)),
        name="modulation",
    )(cond, w, b.reshape(1, nout))


def _rope_rotate(t, cos, sin):
    lane = lax.broadcasted_iota(jnp.int32, t.shape, 1)
    partner = jnp.where((lane % 32) < 16, pltpu.roll(t, LANES - 16, 1), pltpu.roll(t, 16, 1))
    return t * cos + partner * sin


def _inproj_kernel(x_ref, mod_ref, w_ref, cos_ref, sin_ref, q_ref, k_ref, v_ref, u_ref, ga_ref, gs_ref, *, rope):
    x = x_ref[...]
    h = (x * (1.0 + mod_ref[1:2, :]) + mod_ref[0:1, :]).astype(BF16)
    p = jnp.dot(h, w_ref[...], preferred_element_type=F32)
    o1 = ATTN_WIDTH
    o2 = o1 + KV_WIDTH
    o3 = o2 + KV_WIDTH
    d = x.shape[1]
    q = p[:, :o1]
    k = p[:, o1:o2]
    if rope:
        cos = cos_ref[...]
        sin = sin_ref[...]
        q = jnp.concatenate([_rope_rotate(q[:, j * LANES:(j + 1) * LANES], cos, sin)
                             for j in range(o1 // LANES)], axis=1)
        k = _rope_rotate(k, cos, sin)
    o4 = o3 + SSM_WIDTH
    q_ref[...] = q.astype(BF16)
    k_ref[...] = k
    v_ref[...] = p[:, o2:o3]
    u_ref[...] = p[:, o3:o4]
    ga_ref[...] = p[:, o4:o4 + d].astype(BF16)
    gs_ref[...] = p[:, o4 + d:].astype(BF16)


def _in_projection(x, mod, w, cos, sin, rope):
    B, L, D = x.shape
    tm = min(INPROJ_ROWS, L)
    per_batch = mod.shape[0] > 1
    nw = w.shape[1]
    row = lambda b, i: (b, i, 0)
    outs = pl.pallas_call(
        functools.partial(_inproj_kernel, rope=rope),
        grid=(B, L // tm),
        in_specs=[pl.BlockSpec((None, tm, D), row),
                  pl.BlockSpec((None, N_MOD, D), (lambda b, i: (b, 0, 0)) if per_batch else (lambda b, i: (0, 0, 0))),
                  pl.BlockSpec((D, nw), lambda b, i: (0, 0)),
                  pl.BlockSpec((tm, LANES), lambda b, i: (i, 0)),
                  pl.BlockSpec((tm, LANES), lambda b, i: (i, 0))],
        out_specs=[pl.BlockSpec((None, tm, ATTN_WIDTH), row),
                   pl.BlockSpec((None, tm, KV_WIDTH), row),
                   pl.BlockSpec((None, tm, KV_WIDTH), row),
                   pl.BlockSpec((None, tm, SSM_WIDTH), row),
                   pl.BlockSpec((None, tm, D), row),
                   pl.BlockSpec((None, tm, D), row)],
        out_shape=[jax.ShapeDtypeStruct((B, L, ATTN_WIDTH), BF16),
                   jax.ShapeDtypeStruct((B, L, KV_WIDTH), F32),
                   jax.ShapeDtypeStruct((B, L, KV_WIDTH), F32),
                   jax.ShapeDtypeStruct((B, L, SSM_WIDTH), F32),
                   jax.ShapeDtypeStruct((B, L, D), BF16),
                   jax.ShapeDtypeStruct((B, L, D), BF16)],
        compiler_params=_cparams(("parallel", "parallel")),
        name="in_projection",
    )(x, mod, w, cos, sin)
    return outs


def _rope_cos_sin(n_tokens):
    t = jnp.arange(n_tokens, dtype=jnp.int32)
    pos = jnp.stack([t // GRID_W, t % GRID_W], axis=-1).astype(F32)
    n_freq = HEAD_DIM // 4
    inv_freq = ROPE_BASE ** (-jnp.arange(n_freq, dtype=F32) / n_freq)
    ang = pos[:, :, None] * inv_freq
    c, s = jnp.cos(ang), jnp.sin(ang)
    cos = jnp.concatenate([c[:, 0], c[:, 0], c[:, 1], c[:, 1]], axis=-1)
    sin = jnp.concatenate([-s[:, 0], s[:, 0], -s[:, 1], s[:, 1]], axis=-1)
    return jnp.tile(cos, (1, LANES // HEAD_DIM)), jnp.tile(sin, (1, LANES // HEAD_DIM))


def _attend(sink_ref, q, kcat, vcat, mask, o_ref, row0=0):
    lq = q.shape[0]
    lane = lax.broadcasted_iota(jnp.int32, (1, LANES), 1)
    low = lane < HEAD_DIM
    k_sw = pltpu.roll(kcat, HEAD_DIM, 1)
    v_sw = pltpu.roll(vcat, HEAD_DIM, 1)
    neg = jnp.finfo(F32).min
    scores, vds = [], []
    for h in range(N_KV_HEADS):
        keep = low if h == 0 else jnp.logical_not(low)
        kd = jnp.where(keep, kcat, k_sw).astype(BF16)
        vds.append(jnp.where(keep, vcat, v_sw).astype(BF16))
        qs = []
        for j in range(Q_PER_KV):
            head = h * Q_PER_KV + j
            blk = q[:, (head // 2) * LANES:(head // 2 + 1) * LANES]
            sel = low if head % 2 == 0 else jnp.logical_not(low)
            qs.append(jnp.where(sel, blk, jnp.zeros_like(blk)))
        qstack = jnp.concatenate(qs, axis=0)
        scores.append(lax.dot_general(qstack, kd, (((1,), (1,)), ((), ())), preferred_element_type=F32))
    probs, denoms = [], []
    for h in range(N_KV_HEADS):
        s = scores[h]
        ps, ls = [], []
        for j in range(Q_PER_KV):
            sj = s[j * lq:(j + 1) * lq]
            if mask is not None:
                lm = mask.shape[1]
                sj = jnp.concatenate([jnp.where(mask, sj[:, :lm], neg), sj[:, lm:]], axis=1)
            sink = sink_ref[h * Q_PER_KV + j] * LOG2_E
            m = jnp.maximum(jnp.max(sj, axis=1, keepdims=True), sink)
            pj = jnp.exp2(sj - m)
            ls.append(jnp.sum(pj, axis=1, keepdims=True) + jnp.exp2(sink - m))
            ps.append(pj.astype(BF16))
        probs.append(jnp.concatenate(ps, axis=0))
        denoms.append(ls)
    for h in range(N_KV_HEADS):
        ls = denoms[h]
        o = jnp.dot(probs[h], vds[h], preferred_element_type=F32)
        for jj in range(Q_PER_KV // 2):
            oe = o[(2 * jj) * lq:(2 * jj + 1) * lq] / ls[2 * jj]
            oo = o[(2 * jj + 1) * lq:(2 * jj + 2) * lq] / ls[2 * jj + 1]
            cb = (h * Q_PER_KV) // 2 + jj
            o_ref[row0:row0 + lq, cb * LANES:(cb + 1) * LANES] = jnp.where(low, oe, oo).astype(o_ref.dtype)


def _attn_ctx_kernel(sink_ref, q_ref, k_ref, v_ref, o_ref):
    _attend(sink_ref, q_ref[...], k_ref[...], v_ref[...], None, o_ref)


def _attn_lat_kernel(sink_ref, q_ref, kp_ref, kc_ref, kn_ref, vp_ref, vc_ref, vn_ref, kx_ref, vx_ref, o_ref):
    n = pl.program_id(1)
    nb = pl.num_programs(1)
    blk = kp_ref.shape[0]
    nq = q_ref.shape[0] // blk
    klocal = jnp.concatenate([kp_ref[...], kc_ref[...], kn_ref[...]], axis=0)
    vlocal = jnp.concatenate([vp_ref[...], vc_ref[...], vn_ref[...]], axis=0)
    qi = lax.broadcasted_iota(jnp.int32, (blk, 3 * blk), 0)
    kj = lax.broadcasted_iota(jnp.int32, (blk, 3 * blk), 1)
    rel = kj - blk - qi
    in_band = (rel <= WINDOW) & (rel >= -WINDOW)
    for j in range(nq):
        mask = in_band
        if j == 0:
            mask = mask & ((kj >= blk) | (n > 0))
        if j == nq - 1:
            mask = mask & ((kj < 2 * blk) | (n < nb - 1))
        kcat = jnp.concatenate([klocal[j * blk:(j + 3) * blk], kx_ref[...]], axis=0)
        vcat = jnp.concatenate([vlocal[j * blk:(j + 3) * blk], vx_ref[...]], axis=0)
        _attend(sink_ref, q_ref[j * blk:(j + 1) * blk, :], kcat, vcat, mask, o_ref, j * blk)


def _attention_context(q, k, v, sink):
    B, L, _ = q.shape
    row = lambda b: (b, 0, 0)
    return pl.pallas_call(
        _attn_ctx_kernel,
        grid=(B,),
        in_specs=[pl.BlockSpec(memory_space=pltpu.SMEM),
                  pl.BlockSpec((None, L, ATTN_WIDTH), row),
                  pl.BlockSpec((None, L, KV_WIDTH), row),
                  pl.BlockSpec((None, L, KV_WIDTH), row)],
        out_specs=pl.BlockSpec((None, L, ATTN_WIDTH), row),
        out_shape=jax.ShapeDtypeStruct((B, L, ATTN_WIDTH), BF16),
        compiler_params=_cparams(("parallel",)),
        name="attention_context",
    )(sink, q, k, v)


def _attention_latent(q, k, v, k_ctx, v_ctx, sink):
    B, S, _ = q.shape
    blk = WINDOW
    nq = ATTN_QBLOCKS
    nb = S // blk
    nctx = k_ctx.shape[1]
    cur = lambda b, n: (b, n, 0)
    prv = lambda b, n: (b, jnp.maximum(n * nq - 1, 0), 0)
    nxt = lambda b, n: (b, jnp.minimum(n * nq + nq, nb - 1), 0)
    ctx = lambda b, n: (b, 0, 0)
    edge = lambda im: pl.BlockSpec((None, blk, KV_WIDTH), im)
    mid = pl.BlockSpec((None, nq * blk, KV_WIDTH), cur)
    return pl.pallas_call(
        _attn_lat_kernel,
        grid=(B, nb // nq),
        in_specs=[pl.BlockSpec(memory_space=pltpu.SMEM),
                  pl.BlockSpec((None, nq * blk, ATTN_WIDTH), cur),
                  edge(prv), mid, edge(nxt), edge(prv), mid, edge(nxt),
                  pl.BlockSpec((None, nctx, KV_WIDTH), ctx),
                  pl.BlockSpec((None, nctx, KV_WIDTH), ctx)],
        out_specs=pl.BlockSpec((None, nq * blk, ATTN_WIDTH), cur),
        out_shape=jax.ShapeDtypeStruct((B, S, ATTN_WIDTH), BF16),
        compiler_params=_cparams(("parallel", "parallel")),
        name="attention_latent",
    )(sink, q, k, k, k, v, v, v, k_ctx, v_ctx)


def _ssm_kernel(u_ref, wb_ref, wc_ref, a_ref, d_ref, h0_ref, y_ref, fs_ref, ut_ref, yt_ref, st_ref, *bu_refs):
    rev = pl.program_id(0)
    i = pl.program_id(2)
    nc = pl.num_programs(2)
    nseq, tm, _ = u_ref.shape
    sw = a_ref.shape[-1] // SSM_STRIPS
    nre = sw // 2

    @pl.when(i == 0)
    def _():
        st_ref[...] = h0_ref[...]

    for b in range(nseq):
        ub = u_ref[b]
        for s in range(SSM_STRIPS):
            ut_ref[s, pl.ds(b, tm, stride=nseq), :] = ub[:, s * LANES:(s + 1) * LANES]
    for s in range(SSM_STRIPS):
        bu_refs[s][...] = jnp.dot(ut_ref[s].astype(BF16), wb_ref[s], preferred_element_type=F32)

    for s in range(SSM_STRIPS):
        bu_ref = bu_refs[s]
        a_re = a_ref[:, s * sw:s * sw + nre]
        a_im = a_ref[:, s * sw + nre:(s + 1) * sw]

        def step(t, carry):
            xr, xi = carry
            r = pl.multiple_of((t + rev * (tm - 1 - 2 * t)) * nseq, nseq)
            nr = a_re * xr - a_im * xi + bu_ref[pl.ds(r, nseq), 0:nre]
            ni = a_re * xi + a_im * xr + bu_ref[pl.ds(r, nseq), nre:sw]
            bu_ref[pl.ds(r, nseq), 0:nre] = nr
            bu_ref[pl.ds(r, nseq), nre:sw] = ni
            return nr, ni

        xr, xi = lax.fori_loop(0, tm, step, (st_ref[:, s * sw:s * sw + nre], st_ref[:, s * sw + nre:(s + 1) * sw]),
                               unroll=True)
        st_ref[:, s * sw:s * sw + nre] = xr
        st_ref[:, s * sw + nre:(s + 1) * sw] = xi
        yt_ref[s] = jnp.dot(bu_ref[...].astype(BF16), wc_ref[s], preferred_element_type=F32)

    for b in range(nseq):
        yb = jnp.concatenate([yt_ref[s, pl.ds(b, tm, stride=nseq), :] for s in range(SSM_STRIPS)], axis=1)
        y_ref[b] = yb + u_ref[b] * d_ref[...]

    @pl.when(i == nc - 1)
    def _():
        fs_ref[...] = st_ref[...]


def _ssm_mix(u, ssm, h0):
    B, L, _ = u.shape
    nseq = SUBLANES
    tm = min(SSM_STEPS, L)
    nc = L // tm
    rows = nseq * tm
    ns = 2 * SSM_GROUPS * SSM_STATE
    chunk = lambda d, g, i: i + d * (nc - 1 - 2 * i)
    y, fs = pl.pallas_call(
        _ssm_kernel,
        grid=(2, B // nseq, nc),
        in_specs=[pl.BlockSpec((nseq, tm, SSM_WIDTH), lambda d, g, i: (g, chunk(d, g, i), 0)),
                  pl.BlockSpec((None, SSM_STRIPS, LANES, ns // SSM_STRIPS), lambda d, g, i: (d, 0, 0, 0)),
                  pl.BlockSpec((None, SSM_STRIPS, ns // SSM_STRIPS, LANES), lambda d, g, i: (d, 0, 0, 0)),
                  pl.BlockSpec((None, nseq, ns), lambda d, g, i: (d, 0, 0)),
                  pl.BlockSpec((None, 1, SSM_WIDTH), lambda d, g, i: (d, 0, 0)),
                  pl.BlockSpec((None, nseq, ns), lambda d, g, i: (d, g, 0))],
        out_specs=[pl.BlockSpec((None, nseq, tm, SSM_WIDTH), lambda d, g, i: (d, g, chunk(d, g, i), 0)),
                   pl.BlockSpec((None, nseq, ns), lambda d, g, i: (d, g, 0))],
        out_shape=[jax.ShapeDtypeStruct((2, B, L, SSM_WIDTH), F32),
                   jax.ShapeDtypeStruct((2, B, ns), F32)],
        scratch_shapes=[pltpu.VMEM((SSM_STRIPS, rows, LANES), F32), pltpu.VMEM((SSM_STRIPS, rows, LANES), F32),
                        pltpu.VMEM((nseq, ns), F32)]
        + [pltpu.VMEM((rows, ns // SSM_STRIPS), F32) for _ in range(SSM_STRIPS)],
        compiler_params=_cparams(("arbitrary", "arbitrary", "arbitrary")),
        name="ssm_scan",
    )(u, ssm["wb"], ssm["wc"], ssm["a"], ssm["d"], h0)
    return y, fs


def _state_to_lanes(s):
    lead = s.shape[:-3]
    s = s.reshape(lead + (2, SSM_STRIPS, SSM_GROUPS // SSM_STRIPS, SSM_STATE))
    s = jnp.swapaxes(s, -4, -3)
    return s.reshape(lead + (2 * SSM_GROUPS * SSM_STATE,))


def _lanes_to_state(v):
    lead = v.shape[:-1]
    s = v.reshape(lead + (SSM_STRIPS, 2, SSM_GROUPS // SSM_STRIPS, SSM_STATE))
    s = jnp.swapaxes(s, -4, -3)
    return s.reshape(lead + (2, SSM_GROUPS, SSM_STATE))


def _ssm_params(a_re, a_im, log_dt, b_re, b_im, c_re, c_im, dvec):
    G, N, C = SSM_GROUPS, SSM_STATE, SSM_GROUP
    lam_re = jnp.minimum(a_re, -1e-4)
    lam_im = a_im
    dt = jnp.exp(log_dt)[..., None]
    mag = jnp.exp(lam_re * dt)
    abar_re, abar_im = mag * jnp.cos(lam_im * dt), mag * jnp.sin(lam_im * dt)
    den = jnp.square(lam_re) + jnp.square(lam_im)
    p, qi = abar_re - 1.0, abar_im
    f_re = (p * lam_re + qi * lam_im) / den
    f_im = (qi * lam_re - p * lam_im) / den
    bbar_re = f_re[..., None] * b_re - f_im[..., None] * b_im
    bbar_im = f_re[..., None] * b_im + f_im[..., None] * b_re
    a = _state_to_lanes(jnp.stack([abar_re, abar_im], axis=1))
    a = jnp.broadcast_to(a[:, None, :], (2, SUBLANES, a.shape[-1]))
    S = SSM_STRIPS
    gs = G // S
    eye = jnp.eye(gs, dtype=F32)

    def bd_in(bb):
        bb = bb.reshape(2, S, gs, N, C)
        return jnp.einsum('dkgnc,gh->dkgchn', bb, eye).reshape(2, S, gs * C, gs * N)

    def bd_out(cc):
        cc = cc.reshape(2, S, gs, C, N)
        return jnp.einsum('dkgcn,gh->dkgnhc', cc, eye).reshape(2, S, gs * N, gs * C)

    wb = jnp.concatenate([bd_in(bbar_re), bd_in(bbar_im)], axis=-1).astype(BF16)
    wc = jnp.concatenate([bd_out(c_re), -bd_out(c_im)], axis=-2).astype(BF16)
    d = jnp.stack([dvec, jnp.zeros_like(dvec)], axis=0).reshape(2, 1, SSM_WIDTH)
    return {"a": a, "wb": wb, "wc": wc, "d": d}


def _layer_norm(x, g, b):
    mu = jnp.mean(x, axis=-1, keepdims=True)
    xc = x - mu
    var = jnp.mean(xc * xc, axis=-1, keepdims=True)
    return xc * lax.rsqrt(var + LN_EPS) * g + b


def _pack_rows(x):
    w = x.shape[1] // 2
    return pltpu.pack_elementwise([x[:, :w], x[:, w:]], packed_dtype=BF16)


def _unpack_rows(p):
    return (pltpu.unpack_elementwise(p, index=0, packed_dtype=BF16, unpacked_dtype=F32),
            pltpu.unpack_elementwise(p, index=1, packed_dtype=BF16, unpacked_dtype=F32))


def _gelu_tanh(x):
    return 0.5 * x * (1.0 + jnp.tanh(math.sqrt(2.0 / math.pi) * (x + 0.044715 * (x * x * x))))


def _merge_kernel(x_ref, ao_ref, yf_ref, yb_ref, ga_ref, gs_ref, mod_ref, wglu_ref, wa_ref, ws_ref, wo_ref,
                  lng_ref, lnb_ref, rwt_ref, rb_ref, run0_ref, x1_ref, h2_ref, idx_ref, rank_ref, wt_ref, cnt_ref,
                  run_ref, *, alpha):
    @pl.when((pl.program_id(0) == 0) & (pl.program_id(1) == 0))
    def _():
        run_ref[...] = run0_ref[...].astype(F32)

    nrow = x_ref.shape[0]
    sub = nrow // MERGE_SPLIT
    groups = [slice(j * sub, (j + 1) * sub) for j in range(MERGE_SPLIT)]
    dot = functools.partial(jnp.dot, preferred_element_type=F32)
    z = [_gelu_tanh(yf_ref[rs, :] + yb_ref[rs, :]) for rs in groups]
    attn_br = [dot(ao_ref[rs, :], wa_ref[...]) for rs in groups]
    gate = [jax.nn.sigmoid(dot(zg.astype(BF16), wglu_ref[...])) for zg in z]
    ssm_br = [dot((zg * gg).astype(BF16), ws_ref[...]) for zg, gg in zip(z, gate)]
    merged = [jax.nn.sigmoid(ga_ref[rs, :].astype(F32)) * ab + jax.nn.sigmoid(gs_ref[rs, :].astype(F32)) * sb
              for rs, ab, sb in zip(groups, attn_br, ssm_br)]
    mix = [dot(mg.astype(BF16), wo_ref[...]) for mg in merged]
    h2s = []
    for rs, mg in zip(groups, mix):
        x1 = _layer_norm(alpha * x_ref[rs, :] + mod_ref[2:3, :] * mg, lng_ref[...], lnb_ref[...])
        h2g = x1 * (1.0 + mod_ref[4:5, :]) + mod_ref[3:4, :]
        x1_ref[rs, :] = x1
        h2_ref[rs, :] = _pack_rows(h2g)
        h2s.append(h2g)
    h2 = jnp.concatenate(h2s, axis=0)

    logits = lax.dot_general(rwt_ref[...], h2, (((1,), (1,)), ((), ())), preferred_element_type=F32,
                             precision=lax.Precision.HIGHEST)
    score = jax.nn.sigmoid(logits)
    tm = score.shape[1]
    eidx = lax.broadcasted_iota(jnp.int32, score.shape, 0).astype(F32)
    work = score + rb_ref[...]
    picks, sel = [], []
    member = jnp.zeros_like(score)
    for _ in range(TOP_K):
        best = jnp.max(work, axis=0, keepdims=True)
        pick = jnp.min(jnp.where(work == best, eidx, float(N_EXPERTS)), axis=0, keepdims=True)
        hit = eidx == pick
        picks.append(pick)
        sel.append(jnp.sum(jnp.where(hit, score, 0.0), axis=0, keepdims=True))
        member = member + hit.astype(F32)
        work = jnp.where(hit, -jnp.inf, work)
    total = sel[0]
    for s in sel[1:]:
        total = total + s
    before = (lax.broadcasted_iota(jnp.int32, (tm, tm), 0) < lax.broadcasted_iota(jnp.int32, (tm, tm), 1))
    prefix = jnp.dot(member.astype(BF16), before.astype(BF16), preferred_element_type=F32)
    base = prefix + run_ref[...]
    for k in range(TOP_K):
        idx_ref[k:k + 1, :] = picks[k].astype(jnp.int32)
        rank_ref[k:k + 1, :] = jnp.sum(jnp.where(eidx == picks[k], base, 0.0), axis=0,
                                       keepdims=True).astype(jnp.int32)
        wt_ref[k:k + 1, :] = sel[k] / total * ROUTED_SCALE
    run_ref[...] = run_ref[...] + jnp.sum(member, axis=1, keepdims=True)
    cnt_ref[...] = run_ref[...].astype(jnp.int32)


def _merge(x, attn_o, y, ga, gs, mod, wts, run0, alpha):
    B, L, D = x.shape
    tm = min(MERGE_ROWS, L)
    per_batch = mod.shape[0] > 1
    row = lambda b, i: (b, i, 0)
    full = lambda b, i: (0, 0)
    wspec = lambda a: pl.BlockSpec(a.shape, full)
    return pl.pallas_call(
        functools.partial(_merge_kernel, alpha=alpha),
        grid=(B, L // tm),
        in_specs=[pl.BlockSpec((None, tm, D), row),
                  pl.BlockSpec((None, tm, ATTN_WIDTH), row),
                  pl.BlockSpec((None, None, tm, SSM_WIDTH), lambda b, i: (0, b, i, 0)),
                  pl.BlockSpec((None, None, tm, SSM_WIDTH), lambda b, i: (1, b, i, 0)),
                  pl.BlockSpec((None, tm, D), row),
                  pl.BlockSpec((None, tm, D), row),
                  pl.BlockSpec((None, N_MOD, D), (lambda b, i: (b, 0, 0)) if per_batch else (lambda b, i: (0, 0, 0))),
                  wspec(wts["w_glu"]), wspec(wts["w_attn_br"]), wspec(wts["w_ssm_br"]), wspec(wts["w_out"]),
                  wspec(wts["ln1_g"]), wspec(wts["ln1_b"]), wspec(wts["router_wt"]), wspec(wts["router_b"]),
                  wspec(run0)],
        out_specs=[pl.BlockSpec((None, tm, D), row),
                   pl.BlockSpec((None, tm, D // 2), row),
                   pl.BlockSpec((None, TOP_K, tm), lambda b, i: (b, 0, i)),
                   pl.BlockSpec((None, TOP_K, tm), lambda b, i: (b, 0, i)),
                   pl.BlockSpec((None, TOP_K, tm), lambda b, i: (b, 0, i)),
                   pl.BlockSpec((N_EXPERTS, 1), full)],
        out_shape=[jax.ShapeDtypeStruct((B, L, D), F32),
                   jax.ShapeDtypeStruct((B, L, D // 2), jnp.int32),
                   jax.ShapeDtypeStruct((B, TOP_K, L), jnp.int32),
                   jax.ShapeDtypeStruct((B, TOP_K, L), jnp.int32),
                   jax.ShapeDtypeStruct((B, TOP_K, L), F32),
                   jax.ShapeDtypeStruct((N_EXPERTS, 1), jnp.int32)],
        scratch_shapes=[pltpu.VMEM((N_EXPERTS, 1), F32)],
        compiler_params=_cparams(("arbitrary", "arbitrary")),
        name="merge_ln1_router",
    )(x, attn_o, y, y, ga, gs, mod, wts["w_glu"], wts["w_attn_br"], wts["w_ssm_br"], wts["w_out"],
      wts["ln1_g"], wts["ln1_b"], wts["router_wt"], wts["router_b"], run0)


def _dispatch(h2s, dests):
    W = h2s[0].shape[-1]
    n_slots = sum(d.size for d in dests)
    mesh = plsc.VectorSubcoreMesh(core_axis_name="core", subcore_axis_name="subcore")
    workers = mesh.num_cores * mesh.num_subcores
    tokens = [h.shape[0] * h.shape[1] for h in h2s]
    per_worker = [t // SC_WINDOW // workers for t in tokens]
    assert all(p * SC_WINDOW * workers == t for p, t in zip(per_worker, tokens))

    @pl.kernel(out_type=jax.ShapeDtypeStruct((n_slots, W), h2s[0].dtype), mesh=mesh,
               scratch_types=[pltpu.VMEM((1, SC_WINDOW), jnp.int32), pltpu.VMEM((SC_WINDOW, W), h2s[0].dtype)])
    def scatter(*refs):
        out_hbm, index_v, rows_v = refs[-3:]
        worker = lax.axis_index("core") * mesh.num_subcores + lax.axis_index("subcore")
        for g, (ntok, per) in enumerate(zip(tokens, per_worker)):
            rows_hbm, slots_hbm = refs[2 * g], refs[2 * g + 1]

            @pl.loop(0, per)
            def _(j, rows_hbm=rows_hbm, slots_hbm=slots_hbm, ntok=ntok, per=per):
                first = (worker * per + j) * SC_WINDOW
                pltpu.sync_copy(rows_hbm.at[pl.ds(first, SC_WINDOW)], rows_v)
                for k in range(TOP_K):
                    pltpu.sync_copy(slots_hbm.at[:, pl.ds(k * ntok + first, SC_WINDOW)], index_v)
                    pltpu.sync_copy(rows_v, out_hbm.at[index_v.at[0]])

    args = []
    for h, d in zip(h2s, dests):
        args += [h.reshape(-1, W), jnp.swapaxes(d, 0, 1).reshape(1, d.size)]
    return scatter(*args)


def _gmm_kernel(tile_ref, exp_ref, valid_ref, gstart_ref, xs_ref, wg_ref, wu_ref, wd_ref, ys_ref,
                wgb_ref, wub_ref, wdb_ref):
    w = pl.program_id(0)
    e = exp_ref[w]
    t = tile_ref[w]
    prev = jnp.maximum(w - 1, 0)
    new_expert = (w == 0) | (e != exp_ref[prev])
    first_visit = (w == 0) | (t != tile_ref[prev])

    @pl.when(new_expert)
    def _():
        wgb_ref[...] = wg_ref[...].astype(BF16)
        wub_ref[...] = wu_ref[...].astype(BF16)
        wdb_ref[...] = wd_ref[...].astype(BF16)

    @pl.when(valid_ref[w] == 1)
    def _():
        tg = xs_ref.shape[0]
        sub = tg // GMM_SPLIT

        groups = [slice(j * sub, (j + 1) * sub) for j in range(GMM_SPLIT)]
        dot = functools.partial(jnp.dot, preferred_element_type=F32)
        x = [jnp.concatenate(_unpack_rows(xs_ref[rs, :]), axis=1).astype(BF16) for rs in groups]
        g = [dot(xg, wgb_ref[...]) for xg in x]
        u = [dot(xg, wub_ref[...]) for xg in x]
        a = [(gg * jax.nn.sigmoid(gg) * ug).astype(BF16) for gg, ug in zip(g, u)]
        y = jnp.concatenate([_pack_rows(dot(ag, wdb_ref[...])) for ag in a], axis=0)
        rows = t * tg + lax.broadcasted_iota(jnp.int32, (tg, 1), 0)
        mine = (rows >= gstart_ref[e]) & (rows < gstart_ref[e + 1])

        @pl.when(first_visit)
        def _():
            ys_ref[...] = jnp.where(mine, y, 0)

        @pl.when(jnp.logical_not(first_visit))
        def _():
            ys_ref[...] = jnp.where(mine, y, ys_ref[...])


def _grouped_experts(xs, counts, w_gate, w_up, w_down):
    A, W = xs.shape
    D = w_gate.shape[-2]
    tg = GMM_ROWS
    nt = A // tg
    n_items = nt + N_EXPERTS - 1
    ff = w_gate.shape[-1]
    gend = jnp.cumsum(counts).astype(jnp.int32)
    gstart = jnp.concatenate([jnp.zeros((1,), jnp.int32), gend])
    first_row = jnp.arange(nt, dtype=jnp.int32) * tg
    count_le = lambda ends, v: jnp.sum((ends[None, :] <= v[:, None]).astype(jnp.int32), axis=1)
    e_lo = count_le(gend, first_row)
    e_hi = count_le(gend, first_row + tg - 1)
    per_tile = e_hi - e_lo + 1
    item_end = jnp.cumsum(per_tile).astype(jnp.int32)
    total = item_end[-1]
    wi = jnp.arange(n_items, dtype=jnp.int32)
    tile = jnp.minimum(count_le(item_end, wi), nt - 1)
    in_tile = (tile[:, None] == jnp.arange(nt, dtype=jnp.int32)[None, :]).astype(jnp.int32)
    lookup = lambda table: jnp.sum(in_tile * table[None, :], axis=1)
    expert = lookup(e_lo) + wi - lookup(item_end - per_tile)
    valid = (wi < total).astype(jnp.int32)
    expert = jnp.where(valid == 1, expert, e_hi[nt - 1])
    grid_spec = pltpu.PrefetchScalarGridSpec(
        num_scalar_prefetch=4,
        grid=(n_items,),
        in_specs=[pl.BlockSpec((tg, W), lambda w, tl, ex, va, gs: (tl[w], 0)),
                  pl.BlockSpec((None, D, ff), lambda w, tl, ex, va, gs: (ex[w], 0, 0)),
                  pl.BlockSpec((None, D, ff), lambda w, tl, ex, va, gs: (ex[w], 0, 0)),
                  pl.BlockSpec((None, ff, D), lambda w, tl, ex, va, gs: (ex[w], 0, 0))],
        out_specs=pl.BlockSpec((tg, W), lambda w, tl, ex, va, gs: (tl[w], 0)),
        scratch_shapes=[pltpu.VMEM((D, ff), BF16), pltpu.VMEM((D, ff), BF16), pltpu.VMEM((ff, D), BF16)],
    )
    return pl.pallas_call(
        _gmm_kernel,
        grid_spec=grid_spec,
        out_shape=jax.ShapeDtypeStruct((A, W), xs.dtype),
        compiler_params=_cparams(("arbitrary",)),
        name="moe_grouped_experts",
    )(tile, expert, valid, gstart, xs, w_gate, w_up, w_down)


def _gather_rows_sc(rows, index):
    n = index.shape[0]
    w = rows.shape[1]
    mesh = plsc.VectorSubcoreMesh(core_axis_name="core", subcore_axis_name="subcore")
    per_subcore = n // SC_WINDOW // (mesh.num_cores * mesh.num_subcores)
    assert per_subcore * SC_WINDOW * mesh.num_cores * mesh.num_subcores == n

    @pl.kernel(out_type=jax.ShapeDtypeStruct((n, w), rows.dtype), mesh=mesh,
               scratch_types=[pltpu.VMEM((1, SC_WINDOW), jnp.int32), pltpu.VMEM((SC_WINDOW, w), rows.dtype)])
    def gather(rows_hbm, index_hbm, out_hbm, index_v, rows_v):
        worker = lax.axis_index("core") * mesh.num_subcores + lax.axis_index("subcore")

        @pl.loop(0, per_subcore)
        def _(j):
            first = (worker * per_subcore + j) * SC_WINDOW
            pltpu.sync_copy(index_hbm.at[:, pl.ds(first, SC_WINDOW)], index_v)
            pltpu.sync_copy(rows_hbm.at[index_v.at[0]], rows_v)
            pltpu.sync_copy(rows_v, out_hbm.at[pl.ds(first, SC_WINDOW)])

    return gather(rows, index.reshape(1, n))


def _combine_rows_kernel(yg_ref, wt_ref, x1_ref, h2_ref, mod_ref, sg_ref, su_ref, sd_ref, lng_ref, lnb_ref, o_ref,
                         *, alpha):
    h = jnp.concatenate(_unpack_rows(h2_ref[...]), axis=1).astype(BF16)
    g = jnp.dot(h, sg_ref[...], preferred_element_type=F32)
    u = jnp.dot(h, su_ref[...], preferred_element_type=F32)
    moe = jnp.dot((g * jax.nn.sigmoid(g) * u).astype(BF16), sd_ref[...], preferred_element_type=F32)
    wt = wt_ref[...]
    lo = jnp.zeros(h2_ref.shape, F32)
    hi = lo
    for k in range(TOP_K):
        rl, rh = _unpack_rows(yg_ref[k])
        lo = lo + wt[:, k:k + 1] * rl
        hi = hi + wt[:, k:k + 1] * rh
    moe = moe + jnp.concatenate([lo, hi], axis=1)
    o_ref[...] = _layer_norm(alpha * x1_ref[...] + mod_ref[5:6, :] * moe, lng_ref[...], lnb_ref[...])


def _combine_rows(ys, dest, wts, x1, h2, mod, shared, alpha):
    B, L, D = x1.shape
    W = ys.shape[-1]
    tm = min(COMBINE_ROWS, L)
    per_batch = mod.shape[0] > 1
    slots = jnp.swapaxes(dest, 0, 1).reshape(TOP_K * B * L)
    yg = _gather_rows_sc(ys, slots).reshape(TOP_K, B, L, W)
    row = lambda b, i: (b, i, 0)
    full = lambda b, i: (0, 0)
    wspec = lambda a: pl.BlockSpec(a.shape, full)
    return pl.pallas_call(
        functools.partial(_combine_rows_kernel, alpha=alpha),
        grid=(B, L // tm),
        in_specs=[pl.BlockSpec((TOP_K, None, tm, W), lambda b, i: (0, b, i, 0)),
                  pl.BlockSpec((None, tm, TOP_K), row),
                  pl.BlockSpec((None, tm, D), row),
                  pl.BlockSpec((None, tm, W), row),
                  pl.BlockSpec((None, N_MOD, D), (lambda b, i: (b, 0, 0)) if per_batch else (lambda b, i: (0, 0, 0))),
                  wspec(shared["sh_w_gate"]), wspec(shared["sh_w_up"]), wspec(shared["sh_w_down"]),
                  wspec(shared["ln2_g"]), wspec(shared["ln2_b"])],
        out_specs=pl.BlockSpec((None, tm, D), row),
        out_shape=jax.ShapeDtypeStruct((B, L, D), F32),
        compiler_params=_cparams(("parallel", "parallel")),
        name="moe_combine_ln2",
    )(yg, jnp.swapaxes(wts, 1, 2), x1, h2, mod, shared["sh_w_gate"], shared["sh_w_up"], shared["sh_w_down"],
      shared["ln2_g"], shared["ln2_b"])


def _slots(idx, rank, counts):
    start = jnp.cumsum(counts) - counts
    pick = idx[..., None] == jnp.arange(N_EXPERTS, dtype=jnp.int32)
    return jnp.sum(jnp.where(pick, start, 0), axis=-1).astype(jnp.int32) + rank


def _mixer_and_router(x, mod, wts, ssm, cache, run0, alpha):
    B, L, D = x.shape
    latent = cache is not None
    if latent:
        k_ctx, v_ctx, h0 = cache
        cos, sin = _rope_cos_sin(L)
    else:
        cos = jnp.zeros((L, LANES), F32)
        sin = cos
        h0 = jnp.zeros((2, B, 2 * SSM_GROUPS * SSM_STATE), F32)
    q, k, v, u, ga, gs = _in_projection(x, mod, wts["w_in"], cos, sin, rope=latent)
    if latent:
        attn_o = _attention_latent(q, k, v, k_ctx, v_ctx, wts["attn_sink"])
    else:
        attn_o = _attention_context(q, k, v, wts["attn_sink"])
    y, fs = _ssm_mix(u, ssm, h0)
    x1, h2, idx, rank, rw, counts = _merge(x, attn_o, y, ga, gs, mod, wts, run0, alpha)
    return {"x1": x1, "h2": h2, "idx": idx, "rank": rank, "rw": rw, "mod": mod}, counts, k, v, fs


def _moe(groups, counts, wts, alpha):
    counts = counts.reshape(N_EXPERTS)
    for g in groups:
        g["dest"] = _slots(g["idx"], g["rank"], counts)
    xs = _dispatch([g["h2"] for g in groups], [g["dest"] for g in groups])
    ys = _grouped_experts(xs, counts, wts["exp_w_gate"], wts["exp_w_up"], wts["exp_w_down"])
    return [_combine_rows(ys, g["dest"], g["rw"], g["x1"], g["h2"], g["mod"], wts, alpha) for g in groups]


def kernel(x_prompt, x_sample, cache_k, cache_v, state_ssm, c, c_ctx, mod_w, mod_b, w_in, attn_sink, w_attn_br, ssm_a_re, ssm_a_im, ssm_log_dt, ssm_b_re, ssm_b_im, ssm_c_re, ssm_c_im, ssm_d, w_glu, w_ssm_br, w_out, ln1_g, ln1_b, ln2_g, ln2_b, router_w, router_b, exp_w_gate, exp_w_up, exp_w_down, sh_w_gate, sh_w_up, sh_w_down):
    depth = w_in.shape[0]
    assert depth == 1
    alpha = (2.0 * depth) ** 0.25
    D = x_prompt.shape[-1]
    nb_p = x_prompt.shape[0]
    nb_s = x_sample.shape[0]
    l = 0

    ncond = 1 + nb_s
    npad = -ncond % SUBLANES
    cond = jnp.concatenate([c_ctx[None, :], c, jnp.zeros((npad, D), F32)], axis=0)
    mod = _modulation(cond, mod_w[l], mod_b[l]).reshape(ncond + npad, N_MOD, D)
    mod_ctx, mod_lat = mod[0:1], mod[1:ncond]

    o1 = ATTN_WIDTH
    o3 = o1 + 2 * KV_WIDTH
    o4 = o3 + SSM_WIDTH
    wi = w_in[l]
    scale = HEAD_DIM ** -0.5 * LOG2_E
    wts = {
        "w_in": jnp.concatenate([wi[:, :o1] * scale, wi[:, o1:]], axis=1).astype(BF16),
        "attn_sink": attn_sink[l],
        "w_glu": w_glu[l].astype(BF16), "w_attn_br": w_attn_br[l].astype(BF16),
        "w_ssm_br": w_ssm_br[l].astype(BF16), "w_out": w_out[l].astype(BF16),
        "ln1_g": ln1_g[l].reshape(1, D), "ln1_b": ln1_b[l].reshape(1, D),
        "ln2_g": ln2_g[l].reshape(1, D), "ln2_b": ln2_b[l].reshape(1, D),
        "router_wt": router_w[l].T, "router_b": router_b[l].reshape(N_EXPERTS, 1),
        "exp_w_gate": exp_w_gate[l], "exp_w_up": exp_w_up[l], "exp_w_down": exp_w_down[l],
        "sh_w_gate": sh_w_gate[l].astype(BF16), "sh_w_up": sh_w_up[l].astype(BF16),
        "sh_w_down": sh_w_down[l].astype(BF16),
    }
    ssm = _ssm_params(ssm_a_re[l], ssm_a_im[l], ssm_log_dt[l], ssm_b_re[l], ssm_b_im[l],
                      ssm_c_re[l], ssm_c_im[l], ssm_d[l])

    no_tokens_yet = jnp.zeros((N_EXPERTS, 1), jnp.int32)
    grp_p, counts, k_p, v_p, fs_p = _mixer_and_router(x_prompt, mod_ctx, wts, ssm, None, no_tokens_yet, alpha)

    past = cache_k.shape[2]
    k_ctx = cache_k[:, l].reshape(nb_s, past, KV_WIDTH)
    v_ctx = cache_v[:, l].reshape(nb_s, past, KV_WIDTH)
    h0 = jnp.swapaxes(_state_to_lanes(state_ssm[:, l]), 0, 1)
    grp_s, counts, _, _, _ = _mixer_and_router(x_sample, mod_lat, wts, ssm, (k_ctx, v_ctx, h0), counts, alpha)
    yp, ys_ = _moe([grp_p, grp_s], counts, wts, alpha)

    Lp = x_prompt.shape[1]
    new_k = k_p.reshape(nb_p, 1, Lp, N_KV_HEADS, HEAD_DIM)
    new_v = v_p.reshape(nb_p, 1, Lp, N_KV_HEADS, HEAD_DIM)
    new_s = _lanes_to_state(jnp.swapaxes(fs_p, 0, 1))[:, None]
    return (yp, ys_, new_k, new_v, new_s)
```

```python
import functools
import math

import numpy as np
import jax
import jax.numpy as jnp
from jax import lax
from jax.experimental import pallas as pl
from jax.experimental.pallas import tpu as pltpu
from jax.experimental.pallas import tpu_sc as plsc

F32 = jnp.float32
BF16 = jnp.bfloat16

GRID_W = 64
HEAD_DIM = 64
N_Q_HEADS = 8
N_KV_HEADS = 2
Q_PER_KV = N_Q_HEADS // N_KV_HEADS
ATTN_WIDTH = N_Q_HEADS * HEAD_DIM
KV_WIDTH = N_KV_HEADS * HEAD_DIM
WINDOW = 128
ROPE_BASE = 10000.0
SSM_WIDTH = 512
SSM_GROUP = 16
SSM_GROUPS = SSM_WIDTH // SSM_GROUP
SSM_STATE = 64
N_EXPERTS = 64
TOP_K = 6
ROUTED_SCALE = 2.5
N_MOD = 6
LN_EPS = 1e-5
LOG2_E = math.log2(math.e)

SUBLANES = 8
LANES = 128
VMEM_LIMIT = 48 * 1024 * 1024

INPROJ_ROWS = 512
ATTN_QBLOCKS = 4
SSM_STEPS = 64
SSM_STRIPS = 4
MERGE_ROWS = 512
MERGE_SPLIT = 2
GMM_ROWS = 2048
GMM_SPLIT = 4
COMBINE_ROWS = 256
COMBINE_PIECE_TOKENS = 8192
SC_WINDOW = 128


def _cparams(sem):
    return pltpu.CompilerParams(dimension_semantics=sem, vmem_limit_bytes=VMEM_LIMIT)


def _mod_kernel(c_ref, w_ref, b_ref, o_ref):
    c = c_ref[...]
    s = c * jax.nn.sigmoid(c)
    o_ref[...] = jnp.dot(s, w_ref[...], preferred_element_type=F32,
                         precision=lax.Precision.HIGHEST) + b_ref[...]


def _modulation(cond, w, b):
    n, d = cond.shape
    nout = w.shape[1]
    tn = 512
    return pl.pallas_call(
        _mod_kernel,
        grid=(nout // tn,),
        in_specs=[pl.BlockSpec((n, d), lambda j: (0, 0)),
                  pl.BlockSpec((d, tn), lambda j: (0, j)),
                  pl.BlockSpec((1, tn), lambda j: (0, j))],
        out_specs=pl.BlockSpec((n, tn), lambda j: (0, j)),
        out_shape=jax.ShapeDtypeStruct((n, nout), F32),
        compiler_params=_cparams(("arbitrary",)),
        name="modulation",
    )(cond, w, b.reshape(1, nout))


def _rope_rotate(t, cos, sin):
    lane = lax.broadcasted_iota(jnp.int32, t.shape, 1)
    partner = jnp.where((lane % 32) < 16, pltpu.roll(t, LANES - 16, 1), pltpu.roll(t, 16, 1))
    return t * cos + partner * sin


def _inproj_kernel(x_ref, mod_ref, w_ref, cos_ref, sin_ref, q_ref, k_ref, v_ref, u_ref, ga_ref, gs_ref, *, rope):
    x = x_ref[...]
    h = (x * (1.0 + mod_ref[1:2, :]) + mod_ref[0:1, :]).astype(BF16)
    p = jnp.dot(h, w_ref[...], preferred_element_type=F32)
    o1 = ATTN_WIDTH
    o2 = o1 + KV_WIDTH
    o3 = o2 + KV_WIDTH
    d = x.shape[1]
    q = p[:, :o1]
    k = p[:, o1:o2]
    if rope:
        cos = cos_ref[...]
        sin = sin_ref[...]
        q = jnp.concatenate([_rope_rotate(q[:, j * LANES:(j + 1) * LANES], cos, sin)
                             for j in range(o1 // LANES)], axis=1)
        k = _rope_rotate(k, cos, sin)
    o4 = o3 + SSM_WIDTH
    q_ref[...] = q.astype(BF16)
    k_ref[...] = k
    v_ref[...] = p[:, o2:o3]
    u_ref[...] = p[:, o3:o4]
    ga_ref[...] = p[:, o4:o4 + d].astype(BF16)
    gs_ref[...] = p[:, o4 + d:].astype(BF16)


def _in_projection(x, mod, w, cos, sin, rope):
    B, L, D = x.shape
    tm = min(INPROJ_ROWS, L)
    per_batch = mod.shape[0] > 1
    nw = w.shape[1]
    row = lambda b, i: (b, i, 0)
    outs = pl.pallas_call(
        functools.partial(_inproj_kernel, rope=rope),
        grid=(B, L // tm),
        in_specs=[pl.BlockSpec((None, tm, D), row),
                  pl.BlockSpec((None, N_MOD, D), (lambda b, i: (b, 0, 0)) if per_batch else (lambda b, i: (0, 0, 0))),
                  pl.BlockSpec((D, nw), lambda b, i: (0, 0)),
                  pl.BlockSpec((tm, LANES), lambda b, i: (i, 0)),
                  pl.BlockSpec((tm, LANES), lambda b, i: (i, 0))],
        out_specs=[pl.BlockSpec((None, tm, ATTN_WIDTH), row),
                   pl.BlockSpec((None, tm, KV_WIDTH), row),
                   pl.BlockSpec((None, tm, KV_WIDTH), row),
                   pl.BlockSpec((None, tm, SSM_WIDTH), row),
                   pl.BlockSpec((None, tm, D), row),
                   pl.BlockSpec((None, tm, D), row)],
        out_shape=[jax.ShapeDtypeStruct((B, L, ATTN_WIDTH), BF16),
                   jax.ShapeDtypeStruct((B, L, KV_WIDTH), F32),
                   jax.ShapeDtypeStruct((B, L, KV_WIDTH), F32),
                   jax.ShapeDtypeStruct((B, L, SSM_WIDTH), F32),
                   jax.ShapeDtypeStruct((B, L, D), BF16),
                   jax.ShapeDtypeStruct((B, L, D), BF16)],
        compiler_params=_cparams(("parallel", "parallel")),
        name="in_projection",
    )(x, mod, w, cos, sin)
    return outs


def _rope_cos_sin(n_tokens):
    t = jnp.arange(n_tokens, dtype=jnp.int32)
    pos = jnp.stack([t // GRID_W, t % GRID_W], axis=-1).astype(F32)
    n_freq = HEAD_DIM // 4
    inv_freq = ROPE_BASE ** (-jnp.arange(n_freq, dtype=F32) / n_freq)
    ang = pos[:, :, None] * inv_freq
    c, s = jnp.cos(ang), jnp.sin(ang)
    cos = jnp.concatenate([c[:, 0], c[:, 0], c[:, 1], c[:, 1]], axis=-1)
    sin = jnp.concatenate([-s[:, 0], s[:, 0], -s[:, 1], s[:, 1]], axis=-1)
    return jnp.tile(cos, (1, LANES // HEAD_DIM)), jnp.tile(sin, (1, LANES // HEAD_DIM))


def _attend(sink_ref, q, kcat, vcat, mask, o_ref, row0=0):
    lq = q.shape[0]
    lane = lax.broadcasted_iota(jnp.int32, (1, LANES), 1)
    low = lane < HEAD_DIM
    k_sw = pltpu.roll(kcat, HEAD_DIM, 1)
    v_sw = pltpu.roll(vcat, HEAD_DIM, 1)
    neg = jnp.finfo(F32).min
    scores, vds = [], []
    for h in range(N_KV_HEADS):
        keep = low if h == 0 else jnp.logical_not(low)
        kd = jnp.where(keep, kcat, k_sw).astype(BF16)
        vds.append(jnp.where(keep, vcat, v_sw).astype(BF16))
        qs = []
        for j in range(Q_PER_KV):
            head = h * Q_PER_KV + j
            blk = q[:, (head // 2) * LANES:(head // 2 + 1) * LANES]
            sel = low if head % 2 == 0 else jnp.logical_not(low)
            qs.append(jnp.where(sel, blk, jnp.zeros_like(blk)))
        qstack = jnp.concatenate(qs, axis=0)
        scores.append(lax.dot_general(qstack, kd, (((1,), (1,)), ((), ())), preferred_element_type=F32))
    probs, denoms = [], []
    for h in range(N_KV_HEADS):
        s = scores[h]
        ps, ls = [], []
        for j in range(Q_PER_KV):
            sj = s[j * lq:(j + 1) * lq]
            if mask is not None:
                lm = mask.shape[1]
                sj = jnp.concatenate([jnp.where(mask, sj[:, :lm], neg), sj[:, lm:]], axis=1)
            sink = sink_ref[h * Q_PER_KV + j] * LOG2_E
            m = jnp.maximum(jnp.max(sj, axis=1, keepdims=True), sink)
            pj = jnp.exp2(sj - m)
            ls.append(jnp.sum(pj, axis=1, keepdims=True) + jnp.exp2(sink - m))
            ps.append(pj.astype(BF16))
        probs.append(jnp.concatenate(ps, axis=0))
        denoms.append(ls)
    for h in range(N_KV_HEADS):
        ls = denoms[h]
        o = jnp.dot(probs[h], vds[h], preferred_element_type=F32)
        for jj in range(Q_PER_KV // 2):
            oe = o[(2 * jj) * lq:(2 * jj + 1) * lq] / ls[2 * jj]
            oo = o[(2 * jj + 1) * lq:(2 * jj + 2) * lq] / ls[2 * jj + 1]
            cb = (h * Q_PER_KV) // 2 + jj
            o_ref[row0:row0 + lq, cb * LANES:(cb + 1) * LANES] = jnp.where(low, oe, oo).astype(o_ref.dtype)


def _attn_ctx_kernel(sink_ref, q_ref, k_ref, v_ref, o_ref):
    _attend(sink_ref, q_ref[...], k_ref[...], v_ref[...], None, o_ref)


def _attn_lat_kernel(sink_ref, q_ref, kp_ref, kc_ref, kn_ref, vp_ref, vc_ref, vn_ref, kx_ref, vx_ref, o_ref):
    n = pl.program_id(1)
    nb = pl.num_programs(1)
    blk = kp_ref.shape[0]
    nq = q_ref.shape[0] // blk
    klocal = jnp.concatenate([kp_ref[...], kc_ref[...], kn_ref[...]], axis=0)
    vlocal = jnp.concatenate([vp_ref[...], vc_ref[...], vn_ref[...]], axis=0)
    qi = lax.broadcasted_iota(jnp.int32, (blk, 3 * blk), 0)
    kj = lax.broadcasted_iota(jnp.int32, (blk, 3 * blk), 1)
    rel = kj - blk - qi
    in_band = (rel <= WINDOW) & (rel >= -WINDOW)
    for j in range(nq):
        mask = in_band
        if j == 0:
            mask = mask & ((kj >= blk) | (n > 0))
        if j == nq - 1:
            mask = mask & ((kj < 2 * blk) | (n < nb - 1))
        kcat = jnp.concatenate([klocal[j * blk:(j + 3) * blk], kx_ref[...]], axis=0)
        vcat = jnp.concatenate([vlocal[j * blk:(j + 3) * blk], vx_ref[...]], axis=0)
        _attend(sink_ref, q_ref[j * blk:(j + 1) * blk, :], kcat, vcat, mask, o_ref, j * blk)


def _attention_context(q, k, v, sink):
    B, L, _ = q.shape
    row = lambda b: (b, 0, 0)
    return pl.pallas_call(
        _attn_ctx_kernel,
        grid=(B,),
        in_specs=[pl.BlockSpec(memory_space=pltpu.SMEM),
                  pl.BlockSpec((None, L, ATTN_WIDTH), row),
                  pl.BlockSpec((None, L, KV_WIDTH), row),
                  pl.BlockSpec((None, L, KV_WIDTH), row)],
        out_specs=pl.BlockSpec((None, L, ATTN_WIDTH), row),
        out_shape=jax.ShapeDtypeStruct((B, L, ATTN_WIDTH), BF16),
        compiler_params=_cparams(("parallel",)),
        name="attention_context",
    )(sink, q, k, v)


def _attention_latent(q, k, v, k_ctx, v_ctx, sink):
    B, S, _ = q.shape
    blk = WINDOW
    nq = ATTN_QBLOCKS
    nb = S // blk
    nctx = k_ctx.shape[1]
    cur = lambda b, n: (b, n, 0)
    prv = lambda b, n: (b, jnp.maximum(n * nq - 1, 0), 0)
    nxt = lambda b, n: (b, jnp.minimum(n * nq + nq, nb - 1), 0)
    ctx = lambda b, n: (b, 0, 0)
    edge = lambda im: pl.BlockSpec((None, blk, KV_WIDTH), im)
    mid = pl.BlockSpec((None, nq * blk, KV_WIDTH), cur)
    return pl.pallas_call(
        _attn_lat_kernel,
        grid=(B, nb // nq),
        in_specs=[pl.BlockSpec(memory_space=pltpu.SMEM),
                  pl.BlockSpec((None, nq * blk, ATTN_WIDTH), cur),
                  edge(prv), mid, edge(nxt), edge(prv), mid, edge(nxt),
                  pl.BlockSpec((None, nctx, KV_WIDTH), ctx),
                  pl.BlockSpec((None, nctx, KV_WIDTH), ctx)],
        out_specs=pl.BlockSpec((None, nq * blk, ATTN_WIDTH), cur),
        out_shape=jax.ShapeDtypeStruct((B, S, ATTN_WIDTH), BF16),
        compiler_params=_cparams(("parallel", "parallel")),
        name="attention_latent",
    )(sink, q, k, k, k, v, v, v, k_ctx, v_ctx)


def _ssm_kernel(u_ref, wb_ref, wc_ref, a_ref, d_ref, h0_ref, y_ref, fs_ref, ut_ref, yt_ref, st_ref, *bu_refs):
    rev = pl.program_id(0)
    i = pl.program_id(2)
    nc = pl.num_programs(2)
    nseq, tm, _ = u_ref.shape
    sw = a_ref.shape[-1] // SSM_STRIPS
    nre = sw // 2

    @pl.when(i == 0)
    def _():
        st_ref[...] = h0_ref[...]

    for b in range(nseq):
        ub = u_ref[b]
        for s in range(SSM_STRIPS):
            ut_ref[s, pl.ds(b, tm, stride=nseq), :] = ub[:, s * LANES:(s + 1) * LANES]
    for s in range(SSM_STRIPS):
        bu_refs[s][...] = jnp.dot(ut_ref[s].astype(BF16), wb_ref[s], preferred_element_type=F32)

    for s in range(SSM_STRIPS):
        bu_ref = bu_refs[s]
        a_re = a_ref[:, s * sw:s * sw + nre]
        a_im = a_ref[:, s * sw + nre:(s + 1) * sw]

        def step(t, carry):
            xr, xi = carry
            r = pl.multiple_of((t + rev * (tm - 1 - 2 * t)) * nseq, nseq)
            nr = a_re * xr - a_im * xi + bu_ref[pl.ds(r, nseq), 0:nre]
            ni = a_re * xi + a_im * xr + bu_ref[pl.ds(r, nseq), nre:sw]
            bu_ref[pl.ds(r, nseq), 0:nre] = nr
            bu_ref[pl.ds(r, nseq), nre:sw] = ni
            return nr, ni

        xr, xi = lax.fori_loop(0, tm, step, (st_ref[:, s * sw:s * sw + nre], st_ref[:, s * sw + nre:(s + 1) * sw]),
                               unroll=True)
        st_ref[:, s * sw:s * sw + nre] = xr
        st_ref[:, s * sw + nre:(s + 1) * sw] = xi
        yt_ref[s] = jnp.dot(bu_ref[...].astype(BF16), wc_ref[s], preferred_element_type=F32)

    for b in range(nseq):
        yb = jnp.concatenate([yt_ref[s, pl.ds(b, tm, stride=nseq), :] for s in range(SSM_STRIPS)], axis=1)
        y_ref[b] = yb + u_ref[b] * d_ref[...]

    @pl.when(i == nc - 1)
    def _():
        fs_ref[...] = st_ref[...]


def _ssm_mix(u, ssm, h0):
    B, L, _ = u.shape
    nseq = SUBLANES
    tm = min(SSM_STEPS, L)
    nc = L // tm
    rows = nseq * tm
    ns = 2 * SSM_GROUPS * SSM_STATE
    chunk = lambda d, g, i: i + d * (nc - 1 - 2 * i)
    y, fs = pl.pallas_call(
        _ssm_kernel,
        grid=(2, B // nseq, nc),
        in_specs=[pl.BlockSpec((nseq, tm, SSM_WIDTH), lambda d, g, i: (g, chunk(d, g, i), 0)),
                  pl.BlockSpec((None, SSM_STRIPS, LANES, ns // SSM_STRIPS), lambda d, g, i: (d, 0, 0, 0)),
                  pl.BlockSpec((None, SSM_STRIPS, ns // SSM_STRIPS, LANES), lambda d, g, i: (d, 0, 0, 0)),
                  pl.BlockSpec((None, nseq, ns), lambda d, g, i: (d, 0, 0)),
                  pl.BlockSpec((None, 1, SSM_WIDTH), lambda d, g, i: (d, 0, 0)),
                  pl.BlockSpec((None, nseq, ns), lambda d, g, i: (d, g, 0))],
        out_specs=[pl.BlockSpec((None, nseq, tm, SSM_WIDTH), lambda d, g, i: (d, g, chunk(d, g, i), 0)),
                   pl.BlockSpec((None, nseq, ns), lambda d, g, i: (d, g, 0))],
        out_shape=[jax.ShapeDtypeStruct((2, B, L, SSM_WIDTH), F32),
                   jax.ShapeDtypeStruct((2, B, ns), F32)],
        scratch_shapes=[pltpu.VMEM((SSM_STRIPS, rows, LANES), F32), pltpu.VMEM((SSM_STRIPS, rows, LANES), F32),
                        pltpu.VMEM((nseq, ns), F32)]
        + [pltpu.VMEM((rows, ns // SSM_STRIPS), F32) for _ in range(SSM_STRIPS)],
        compiler_params=_cparams(("arbitrary", "arbitrary", "arbitrary")),
        name="ssm_scan",
    )(u, ssm["wb"], ssm["wc"], ssm["a"], ssm["d"], h0)
    return y, fs


def _state_to_lanes(s):
    lead = s.shape[:-3]
    s = s.reshape(lead + (2, SSM_STRIPS, SSM_GROUPS // SSM_STRIPS, SSM_STATE))
    s = jnp.swapaxes(s, -4, -3)
    return s.reshape(lead + (2 * SSM_GROUPS * SSM_STATE,))


def _lanes_to_state(v):
    lead = v.shape[:-1]
    s = v.reshape(lead + (SSM_STRIPS, 2, SSM_GROUPS // SSM_STRIPS, SSM_STATE))
    s = jnp.swapaxes(s, -4, -3)
    return s.reshape(lead + (2, SSM_GROUPS, SSM_STATE))


def _ssm_params(a_re, a_im, log_dt, b_re, b_im, c_re, c_im, dvec):
    G, N, C = SSM_GROUPS, SSM_STATE, SSM_GROUP
    lam_re = jnp.minimum(a_re, -1e-4)
    lam_im = a_im
    dt = jnp.exp(log_dt)[..., None]
    mag = jnp.exp(lam_re * dt)
    abar_re, abar_im = mag * jnp.cos(lam_im * dt), mag * jnp.sin(lam_im * dt)
    den = jnp.square(lam_re) + jnp.square(lam_im)
    p, qi = abar_re - 1.0, abar_im
    f_re = (p * lam_re + qi * lam_im) / den
    f_im = (qi * lam_re - p * lam_im) / den
    bbar_re = f_re[..., None] * b_re - f_im[..., None] * b_im
    bbar_im = f_re[..., None] * b_im + f_im[..., None] * b_re
    a = _state_to_lanes(jnp.stack([abar_re, abar_im], axis=1))
    a = jnp.broadcast_to(a[:, None, :], (2, SUBLANES, a.shape[-1]))
    S = SSM_STRIPS
    gs = G // S
    eye = jnp.eye(gs, dtype=F32)

    def bd_in(bb):
        bb = bb.reshape(2, S, gs, N, C)
        return jnp.einsum('dkgnc,gh->dkgchn', bb, eye).reshape(2, S, gs * C, gs * N)

    def bd_out(cc):
        cc = cc.reshape(2, S, gs, C, N)
        return jnp.einsum('dkgcn,gh->dkgnhc', cc, eye).reshape(2, S, gs * N, gs * C)

    wb = jnp.concatenate([bd_in(bbar_re), bd_in(bbar_im)], axis=-1).astype(BF16)
    wc = jnp.concatenate([bd_out(c_re), -bd_out(c_im)], axis=-2).astype(BF16)
    d = jnp.stack([dvec, jnp.zeros_like(dvec)], axis=0).reshape(2, 1, SSM_WIDTH)
    return {"a": a, "wb": wb, "wc": wc, "d": d}


def _layer_norm(x, g, b):
    mu = jnp.mean(x, axis=-1, keepdims=True)
    xc = x - mu
    var = jnp.mean(xc * xc, axis=-1, keepdims=True)
    return xc * lax.rsqrt(var + LN_EPS) * g + b


def _pack_rows(x):
    w = x.shape[1] // 2
    return pltpu.pack_elementwise([x[:, :w], x[:, w:]], packed_dtype=BF16)


def _unpack_rows(p):
    return (pltpu.unpack_elementwise(p, index=0, packed_dtype=BF16, unpacked_dtype=F32),
            pltpu.unpack_elementwise(p, index=1, packed_dtype=BF16, unpacked_dtype=F32))


def _gelu_tanh(x):
    return 0.5 * x * (1.0 + jnp.tanh(math.sqrt(2.0 / math.pi) * (x + 0.044715 * (x * x * x))))


def _merge_kernel(x_ref, ao_ref, yf_ref, yb_ref, ga_ref, gs_ref, mod_ref, wglu_ref, wa_ref, ws_ref, wo_ref,
                  lng_ref, lnb_ref, rwt_ref, rb_ref, run0_ref, x1_ref, h2_ref, idx_ref, rank_ref, wt_ref, cnt_ref,
                  run_ref, *, alpha):
    @pl.when((pl.program_id(0) == 0) & (pl.program_id(1) == 0))
    def _():
        run_ref[...] = run0_ref[...].astype(F32)

    nrow = x_ref.shape[0]
    sub = nrow // MERGE_SPLIT
    groups = [slice(j * sub, (j + 1) * sub) for j in range(MERGE_SPLIT)]
    dot = functools.partial(jnp.dot, preferred_element_type=F32)
    z = [_gelu_tanh(yf_ref[rs, :] + yb_ref[rs, :]) for rs in groups]
    attn_br = [dot(ao_ref[rs, :], wa_ref[...]) for rs in groups]
    gate = [jax.nn.sigmoid(dot(zg.astype(BF16), wglu_ref[...])) for zg in z]
    ssm_br = [dot((zg * gg).astype(BF16), ws_ref[...]) for zg, gg in zip(z, gate)]
    merged = [jax.nn.sigmoid(ga_ref[rs, :].astype(F32)) * ab + jax.nn.sigmoid(gs_ref[rs, :].astype(F32)) * sb
              for rs, ab, sb in zip(groups, attn_br, ssm_br)]
    mix = [dot(mg.astype(BF16), wo_ref[...]) for mg in merged]
    h2s = []
    for rs, mg in zip(groups, mix):
        x1 = _layer_norm(alpha * x_ref[rs, :] + mod_ref[2:3, :] * mg, lng_ref[...], lnb_ref[...])
        h2g = x1 * (1.0 + mod_ref[4:5, :]) + mod_ref[3:4, :]
        x1_ref[rs, :] = x1
        h2_ref[rs, :] = _pack_rows(h2g)
        h2s.append(h2g)
    h2 = jnp.concatenate(h2s, axis=0)

    logits = lax.dot_general(rwt_ref[...], h2, (((1,), (1,)), ((), ())), preferred_element_type=F32,
                             precision=lax.Precision.HIGHEST)
    score = jax.nn.sigmoid(logits)
    tm = score.shape[1]
    eidx = lax.broadcasted_iota(jnp.int32, score.shape, 0).astype(F32)
    work = score + rb_ref[...]
    picks, sel = [], []
    member = jnp.zeros_like(score)
    for _ in range(TOP_K):
        best = jnp.max(work, axis=0, keepdims=True)
        pick = jnp.min(jnp.where(work == best, eidx, float(N_EXPERTS)), axis=0, keepdims=True)
        hit = eidx == pick
        picks.append(pick)
        sel.append(jnp.sum(jnp.where(hit, score, 0.0), axis=0, keepdims=True))
        member = member + hit.astype(F32)
        work = jnp.where(hit, -jnp.inf, work)
    total = sel[0]
    for s in sel[1:]:
        total = total + s
    before = (lax.broadcasted_iota(jnp.int32, (tm, tm), 0) < lax.broadcasted_iota(jnp.int32, (tm, tm), 1))
    prefix = jnp.dot(member.astype(BF16), before.astype(BF16), preferred_element_type=F32)
    base = prefix + run_ref[...]
    for k in range(TOP_K):
        idx_ref[k:k + 1, :] = picks[k].astype(jnp.int32)
        rank_ref[k:k + 1, :] = jnp.sum(jnp.where(eidx == picks[k], base, 0.0), axis=0,
                                       keepdims=True).astype(jnp.int32)
        wt_ref[k:k + 1, :] = sel[k] / total * ROUTED_SCALE
    run_ref[...] = run_ref[...] + jnp.sum(member, axis=1, keepdims=True)
    cnt_ref[...] = run_ref[...].astype(jnp.int32)


def _merge(x, attn_o, y, ga, gs, mod, wts, run0, alpha):
    B, L, D = x.shape
    tm = min(MERGE_ROWS, L)
    per_batch = mod.shape[0] > 1
    row = lambda b, i: (b, i, 0)
    full = lambda b, i: (0, 0)
    wspec = lambda a: pl.BlockSpec(a.shape, full)
    return pl.pallas_call(
        functools.partial(_merge_kernel, alpha=alpha),
        grid=(B, L // tm),
        in_specs=[pl.BlockSpec((None, tm, D), row),
                  pl.BlockSpec((None, tm, ATTN_WIDTH), row),
                  pl.BlockSpec((None, None, tm, SSM_WIDTH), lambda b, i: (0, b, i, 0)),
                  pl.BlockSpec((None, None, tm, SSM_WIDTH), lambda b, i: (1, b, i, 0)),
                  pl.BlockSpec((None, tm, D), row),
                  pl.BlockSpec((None, tm, D), row),
                  pl.BlockSpec((None, N_MOD, D), (lambda b, i: (b, 0, 0)) if per_batch else (lambda b, i: (0, 0, 0))),
                  wspec(wts["w_glu"]), wspec(wts["w_attn_br"]), wspec(wts["w_ssm_br"]), wspec(wts["w_out"]),
                  wspec(wts["ln1_g"]), wspec(wts["ln1_b"]), wspec(wts["router_wt"]), wspec(wts["router_b"]),
                  wspec(run0)],
        out_specs=[pl.BlockSpec((None, tm, D), row),
                   pl.BlockSpec((None, tm, D // 2), row),
                   pl.BlockSpec((None, TOP_K, tm), lambda b, i: (b, 0, i)),
                   pl.BlockSpec((None, TOP_K, tm), lambda b, i: (b, 0, i)),
                   pl.BlockSpec((None, TOP_K, tm), lambda b, i: (b, 0, i)),
                   pl.BlockSpec((N_EXPERTS, 1), full)],
        out_shape=[jax.ShapeDtypeStruct((B, L, D), F32),
                   jax.ShapeDtypeStruct((B, L, D // 2), jnp.int32),
                   jax.ShapeDtypeStruct((B, TOP_K, L), jnp.int32),
                   jax.ShapeDtypeStruct((B, TOP_K, L), jnp.int32),
                   jax.ShapeDtypeStruct((B, TOP_K, L), F32),
                   jax.ShapeDtypeStruct((N_EXPERTS, 1), jnp.int32)],
        scratch_shapes=[pltpu.VMEM((N_EXPERTS, 1), F32)],
        compiler_params=_cparams(("arbitrary", "arbitrary")),
        name="merge_ln1_router",
    )(x, attn_o, y, y, ga, gs, mod, wts["w_glu"], wts["w_attn_br"], wts["w_ssm_br"], wts["w_out"],
      wts["ln1_g"], wts["ln1_b"], wts["router_wt"], wts["router_b"], run0)


def _dispatch(h2s, dests):
    W = h2s[0].shape[-1]
    n_slots = sum(d.size for d in dests)
    mesh = plsc.VectorSubcoreMesh(core_axis_name="core", subcore_axis_name="subcore")
    workers = mesh.num_cores * mesh.num_subcores
    tokens = [h.shape[0] * h.shape[1] for h in h2s]
    per_worker = [t // SC_WINDOW // workers for t in tokens]
    assert all(p * SC_WINDOW * workers == t for p, t in zip(per_worker, tokens))

    @pl.kernel(out_type=jax.ShapeDtypeStruct((n_slots, W), h2s[0].dtype), mesh=mesh,
               scratch_types=[pltpu.VMEM((1, SC_WINDOW), jnp.int32), pltpu.VMEM((SC_WINDOW, W), h2s[0].dtype)])
    def scatter(*refs):
        out_hbm, index_v, rows_v = refs[-3:]
        worker = lax.axis_index("core") * mesh.num_subcores + lax.axis_index("subcore")
        for g, (ntok, per) in enumerate(zip(tokens, per_worker)):
            rows_hbm, slots_hbm = refs[2 * g], refs[2 * g + 1]

            @pl.loop(0, per)
            def _(j, rows_hbm=rows_hbm, slots_hbm=slots_hbm, ntok=ntok, per=per):
                first = (worker * per + j) * SC_WINDOW
                pltpu.sync_copy(rows_hbm.at[pl.ds(first, SC_WINDOW)], rows_v)
                for k in range(TOP_K):
                    pltpu.sync_copy(slots_hbm.at[:, pl.ds(k * ntok + first, SC_WINDOW)], index_v)
                    pltpu.sync_copy(rows_v, out_hbm.at[index_v.at[0]])

    args = []
    for h, d in zip(h2s, dests):
        args += [h.reshape(-1, W), jnp.swapaxes(d, 0, 1).reshape(1, d.size)]
    return scatter(*args)


def _gmm_kernel(tile_ref, exp_ref, valid_ref, gstart_ref, xs_ref, wg_ref, wu_ref, wd_ref, ys_ref,
                wgb_ref, wub_ref, wdb_ref):
    w = pl.program_id(0)
    e = exp_ref[w]
    t = tile_ref[w]
    prev = jnp.maximum(w - 1, 0)
    new_expert = (w == 0) | (e != exp_ref[prev])
    first_visit = (w == 0) | (t != tile_ref[prev])

    @pl.when(new_expert)
    def _():
        wgb_ref[...] = wg_ref[...].astype(BF16)
        wub_ref[...] = wu_ref[...].astype(BF16)
        wdb_ref[...] = wd_ref[...].astype(BF16)

    @pl.when(first_visit)
    def _():
        ys_ref[...] = jnp.zeros_like(ys_ref)

    tg = xs_ref.shape[0]
    sub = tg // GMM_SPLIT
    lo = gstart_ref[e]
    hi = gstart_ref[e + 1]
    for j in range(GMM_SPLIT):
        first_row = t * tg + j * sub

        @pl.when((valid_ref[w] == 1) & (lo < first_row + sub) & (hi > first_row))
        def _(j=j, first_row=first_row):
            rs = slice(j * sub, (j + 1) * sub)
            dot = functools.partial(jnp.dot, preferred_element_type=F32)
            x = jnp.concatenate(_unpack_rows(xs_ref[rs, :]), axis=1).astype(BF16)
            g = dot(x, wgb_ref[...])
            u = dot(x, wub_ref[...])
            y = _pack_rows(dot((g * jax.nn.sigmoid(g) * u).astype(BF16), wdb_ref[...]))
            rows = first_row + lax.broadcasted_iota(jnp.int32, (sub, 1), 0)
            mine = (rows >= lo) & (rows < hi)
            ys_ref[rs, :] = jnp.where(mine, y, ys_ref[rs, :])


def _grouped_experts(xs, counts, w_gate, w_up, w_down):
    A, W = xs.shape
    D = w_gate.shape[-2]
    tg = GMM_ROWS
    nt = A // tg
    n_items = nt + N_EXPERTS - 1
    ff = w_gate.shape[-1]
    gend = jnp.cumsum(counts).astype(jnp.int32)
    gstart = jnp.concatenate([jnp.zeros((1,), jnp.int32), gend])
    first_row = jnp.arange(nt, dtype=jnp.int32) * tg
    count_le = lambda ends, v: jnp.sum((ends[None, :] <= v[:, None]).astype(jnp.int32), axis=1)
    e_lo = count_le(gend, first_row)
    e_hi = count_le(gend, first_row + tg - 1)
    per_tile = e_hi - e_lo + 1
    item_end = jnp.cumsum(per_tile).astype(jnp.int32)
    total = item_end[-1]
    wi = jnp.arange(n_items, dtype=jnp.int32)
    tile = jnp.minimum(count_le(item_end, wi), nt - 1)
    in_tile = (tile[:, None] == jnp.arange(nt, dtype=jnp.int32)[None, :]).astype(jnp.int32)
    lookup = lambda table: jnp.sum(in_tile * table[None, :], axis=1)
    expert = lookup(e_lo) + wi - lookup(item_end - per_tile)
    valid = (wi < total).astype(jnp.int32)
    expert = jnp.where(valid == 1, expert, e_hi[nt - 1])
    grid_spec = pltpu.PrefetchScalarGridSpec(
        num_scalar_prefetch=4,
        grid=(n_items,),
        in_specs=[pl.BlockSpec((tg, W), lambda w, tl, ex, va, gs: (tl[w], 0)),
                  pl.BlockSpec((None, D, ff), lambda w, tl, ex, va, gs: (ex[w], 0, 0)),
                  pl.BlockSpec((None, D, ff), lambda w, tl, ex, va, gs: (ex[w], 0, 0)),
                  pl.BlockSpec((None, ff, D), lambda w, tl, ex, va, gs: (ex[w], 0, 0))],
        out_specs=pl.BlockSpec((tg, W), lambda w, tl, ex, va, gs: (tl[w], 0)),
        scratch_shapes=[pltpu.VMEM((D, ff), BF16), pltpu.VMEM((D, ff), BF16), pltpu.VMEM((ff, D), BF16)],
    )
    return pl.pallas_call(
        _gmm_kernel,
        grid_spec=grid_spec,
        out_shape=jax.ShapeDtypeStruct((A, W), xs.dtype),
        compiler_params=_cparams(("arbitrary",)),
        name="moe_grouped_experts",
    )(tile, expert, valid, gstart, xs, w_gate, w_up, w_down)


def _gather_rows_sc(rows, index):
    n = index.shape[0]
    w = rows.shape[1]
    mesh = plsc.VectorSubcoreMesh(core_axis_name="core", subcore_axis_name="subcore")
    per_subcore = n // SC_WINDOW // (mesh.num_cores * mesh.num_subcores)
    assert per_subcore * SC_WINDOW * mesh.num_cores * mesh.num_subcores == n

    @pl.kernel(out_type=jax.ShapeDtypeStruct((n, w), rows.dtype), mesh=mesh,
               scratch_types=[pltpu.VMEM((1, SC_WINDOW), jnp.int32), pltpu.VMEM((SC_WINDOW, w), rows.dtype)])
    def gather(rows_hbm, index_hbm, out_hbm, index_v, rows_v):
        worker = lax.axis_index("core") * mesh.num_subcores + lax.axis_index("subcore")

        @pl.loop(0, per_subcore)
        def _(j):
            first = (worker * per_subcore + j) * SC_WINDOW
            pltpu.sync_copy(index_hbm.at[:, pl.ds(first, SC_WINDOW)], index_v)
            pltpu.sync_copy(rows_hbm.at[index_v.at[0]], rows_v)
            pltpu.sync_copy(rows_v, out_hbm.at[pl.ds(first, SC_WINDOW)])

    return gather(rows, index.reshape(1, n))


def _combine_rows_kernel(yg_ref, wt_ref, x1_ref, h2_ref, mod_ref, sg_ref, su_ref, sd_ref, lng_ref, lnb_ref, *rest,
                         alpha):
    o_ref = rest[-1]
    h = jnp.concatenate(_unpack_rows(h2_ref[...]), axis=1).astype(BF16)
    g = jnp.dot(h, sg_ref[...], preferred_element_type=F32)
    u = jnp.dot(h, su_ref[...], preferred_element_type=F32)
    moe = jnp.dot((g * jax.nn.sigmoid(g) * u).astype(BF16), sd_ref[...], preferred_element_type=F32)
    wt = wt_ref[...]
    lo = jnp.zeros(h2_ref.shape, F32)
    hi = lo
    for k in range(TOP_K):
        rl, rh = _unpack_rows(yg_ref[k])
        lo = lo + wt[:, k:k + 1] * rl
        hi = hi + wt[:, k:k + 1] * rh
    moe = moe + jnp.concatenate([lo, hi], axis=1)
    o_ref[...] = _layer_norm(alpha * x1_ref[...] + mod_ref[5:6, :] * moe, lng_ref[...], lnb_ref[...])


def _combine_rows(ys, dest, wts, x1, h2, mod, shared, alpha):
    B, L, D = x1.shape
    W = ys.shape[-1]
    tm = min(COMBINE_ROWS, L)
    per_batch = mod.shape[0] > 1
    pieces = max(1, B * L // COMBINE_PIECE_TOKENS)
    nb = B // pieces
    full = lambda b, i: (0, 0)
    wspec = lambda a: pl.BlockSpec(a.shape, full)
    wts_t = jnp.swapaxes(wts, 1, 2)
    out = None
    for p in range(pieces):
        b0 = p * nb
        row = lambda b, i, b0=b0: (b0 + b, i, 0)
        slots = jnp.swapaxes(dest[b0:b0 + nb], 0, 1).reshape(TOP_K * nb * L)
        yg = _gather_rows_sc(ys, slots).reshape(TOP_K, nb, L, W)
        in_specs = [pl.BlockSpec((TOP_K, None, tm, W), lambda b, i: (0, b, i, 0)),
                    pl.BlockSpec((None, tm, TOP_K), row),
                    pl.BlockSpec((None, tm, D), row),
                    pl.BlockSpec((None, tm, W), row),
                    pl.BlockSpec((None, N_MOD, D), (lambda b, i, b0=b0: (b0 + b, 0, 0)) if per_batch
                                 else (lambda b, i: (0, 0, 0))),
                    wspec(shared["sh_w_gate"]), wspec(shared["sh_w_up"]), wspec(shared["sh_w_down"]),
                    wspec(shared["ln2_g"]), wspec(shared["ln2_b"])]
        args = [yg, wts_t, x1, h2, mod, shared["sh_w_gate"], shared["sh_w_up"], shared["sh_w_down"],
                shared["ln2_g"], shared["ln2_b"]]
        aliases = {}
        if out is not None:
            in_specs.append(pl.BlockSpec(memory_space=pl.ANY))
            args.append(out)
            aliases = {len(args) - 1: 0}
        out = pl.pallas_call(
            functools.partial(_combine_rows_kernel, alpha=alpha),
            grid=(nb, L // tm),
            in_specs=in_specs,
            out_specs=pl.BlockSpec((None, tm, D), row),
            out_shape=jax.ShapeDtypeStruct((B, L, D), F32),
            input_output_aliases=aliases,
            compiler_params=_cparams(("parallel", "parallel")),
            name="moe_combine_ln2",
        )(*args)
    return out


def _slots(idx, rank, counts):
    start = jnp.cumsum(counts) - counts
    pick = idx[..., None] == jnp.arange(N_EXPERTS, dtype=jnp.int32)
    return jnp.sum(jnp.where(pick, start, 0), axis=-1).astype(jnp.int32) + rank


def _mixer_and_router(x, mod, wts, ssm, cache, run0, alpha):
    B, L, D = x.shape
    latent = cache is not None
    if latent:
        k_ctx, v_ctx, h0 = cache
        cos, sin = _rope_cos_sin(L)
    else:
        cos = jnp.zeros((L, LANES), F32)
        sin = cos
        h0 = jnp.zeros((2, B, 2 * SSM_GROUPS * SSM_STATE), F32)
    q, k, v, u, ga, gs = _in_projection(x, mod, wts["w_in"], cos, sin, rope=latent)
    if latent:
        attn_o = _attention_latent(q, k, v, k_ctx, v_ctx, wts["attn_sink"])
    else:
        attn_o = _attention_context(q, k, v, wts["attn_sink"])
    y, fs = _ssm_mix(u, ssm, h0)
    x1, h2, idx, rank, rw, counts = _merge(x, attn_o, y, ga, gs, mod, wts, run0, alpha)
    return {"x1": x1, "h2": h2, "idx": idx, "rank": rank, "rw": rw, "mod": mod}, counts, k, v, fs


def _moe(groups, counts, wts, alpha):
    counts = counts.reshape(N_EXPERTS)
    for g in groups:
        g["dest"] = _slots(g["idx"], g["rank"], counts)
    xs = _dispatch([g["h2"] for g in groups], [g["dest"] for g in groups])
    ys = _grouped_experts(xs, counts, wts["exp_w_gate"], wts["exp_w_up"], wts["exp_w_down"])
    return [_combine_rows(ys, g["dest"], g["rw"], g["x1"], g["h2"], g["mod"], wts, alpha) for g in groups]


def kernel(x_prompt, x_sample, cache_k, cache_v, state_ssm, c, c_ctx, mod_w, mod_b, w_in, attn_sink, w_attn_br, ssm_a_re, ssm_a_im, ssm_log_dt, ssm_b_re, ssm_b_im, ssm_c_re, ssm_c_im, ssm_d, w_glu, w_ssm_br, w_out, ln1_g, ln1_b, ln2_g, ln2_b, router_w, router_b, exp_w_gate, exp_w_up, exp_w_down, sh_w_gate, sh_w_up, sh_w_down):
    depth = w_in.shape[0]
    assert depth == 1
    alpha = (2.0 * depth) ** 0.25
    D = x_prompt.shape[-1]
    nb_p = x_prompt.shape[0]
    nb_s = x_sample.shape[0]
    l = 0

    ncond = 1 + nb_s
    npad = -ncond % SUBLANES
    cond = jnp.concatenate([c_ctx[None, :], c, jnp.zeros((npad, D), F32)], axis=0)
    mod = _modulation(cond, mod_w[l], mod_b[l]).reshape(ncond + npad, N_MOD, D)
    mod_ctx, mod_lat = mod[0:1], mod[1:ncond]

    o1 = ATTN_WIDTH
    o3 = o1 + 2 * KV_WIDTH
    o4 = o3 + SSM_WIDTH
    wi = w_in[l]
    scale = HEAD_DIM ** -0.5 * LOG2_E
    wts = {
        "w_in": jnp.concatenate([wi[:, :o1] * scale, wi[:, o1:]], axis=1).astype(BF16),
        "attn_sink": attn_sink[l],
        "w_glu": w_glu[l].astype(BF16), "w_attn_br": w_attn_br[l].astype(BF16),
        "w_ssm_br": w_ssm_br[l].astype(BF16), "w_out": w_out[l].astype(BF16),
        "ln1_g": ln1_g[l].reshape(1, D), "ln1_b": ln1_b[l].reshape(1, D),
        "ln2_g": ln2_g[l].reshape(1, D), "ln2_b": ln2_b[l].reshape(1, D),
        "router_wt": router_w[l].T, "router_b": router_b[l].reshape(N_EXPERTS, 1),
        "exp_w_gate": exp_w_gate[l], "exp_w_up": exp_w_up[l], "exp_w_down": exp_w_down[l],
        "sh_w_gate": sh_w_gate[l].astype(BF16), "sh_w_up": sh_w_up[l].astype(BF16),
        "sh_w_down": sh_w_down[l].astype(BF16),
    }
    ssm = _ssm_params(ssm_a_re[l], ssm_a_im[l], ssm_log_dt[l], ssm_b_re[l], ssm_b_im[l],
                      ssm_c_re[l], ssm_c_im[l], ssm_d[l])

    no_tokens_yet = jnp.zeros((N_EXPERTS, 1), jnp.int32)
    grp_p, counts, k_p, v_p, fs_p = _mixer_and_router(x_prompt, mod_ctx, wts, ssm, None, no_tokens_yet, alpha)

    past = cache_k.shape[2]
    k_ctx = cache_k[:, l].reshape(nb_s, past, KV_WIDTH)
    v_ctx = cache_v[:, l].reshape(nb_s, past, KV_WIDTH)
    h0 = jnp.swapaxes(_state_to_lanes(state_ssm[:, l]), 0, 1)
    grp_s, counts, _, _, _ = _mixer_and_router(x_sample, mod_lat, wts, ssm, (k_ctx, v_ctx, h0), counts, alpha)
    yp, ys_ = _moe([grp_p, grp_s], counts, wts, alpha)

    Lp = x_prompt.shape[1]
    new_k = k_p.reshape(nb_p, 1, Lp, N_KV_HEADS, HEAD_DIM)
    new_v = v_p.reshape(nb_p, 1, Lp, N_KV_HEADS, HEAD_DIM)
    new_s = _lanes_to_state(jnp.swapaxes(fs_p, 0, 1))[:, None]
    return (yp, ys_, new_k, new_v, new_s)
```

```python
import functools
import math

import numpy as np
import jax
import jax.numpy as jnp
from jax import lax
from jax.experimental import pallas as pl
from jax.experimental.pallas import tpu as pltpu
from jax.experimental.pallas import tpu_sc as plsc

F32 = jnp.float32
BF16 = jnp.bfloat16

GRID_W = 64
HEAD_DIM = 64
N_Q_HEADS = 8
N_KV_HEADS = 2
Q_PER_KV = N_Q_HEADS // N_KV_HEADS
ATTN_WIDTH = N_Q_HEADS * HEAD_DIM
KV_WIDTH = N_KV_HEADS * HEAD_DIM
WINDOW = 128
ROPE_BASE = 10000.0
SSM_WIDTH = 512
SSM_GROUP = 16
SSM_GROUPS = SSM_WIDTH // SSM_GROUP
SSM_STATE = 64
N_EXPERTS = 64
TOP_K = 6
ROUTED_SCALE = 2.5
N_MOD = 6
LN_EPS = 1e-5
LOG2_E = math.log2(math.e)

SUBLANES = 8
LANES = 128
VMEM_LIMIT = 48 * 1024 * 1024

INPROJ_ROWS = 512
ATTN_QBLOCKS = 4
SSM_STEPS = 64
SSM_STRIPS = 4
MERGE_ROWS = 512
MERGE_SPLIT = 2
GMM_ROWS = 2048
GMM_SPLIT = 4
COMBINE_ROWS = 256
COMBINE_PIECE_TOKENS = 8192
SC_WINDOW = 128


def _cparams(sem):
    return pltpu.CompilerParams(dimension_semantics=sem, vmem_limit_bytes=VMEM_LIMIT)


def _mod_kernel(c_ref, w_ref, b_ref, o_ref):
    c = c_ref[...]
    s = c * jax.nn.sigmoid(c)
    o_ref[...] = jnp.dot(s, w_ref[...], preferred_element_type=F32,
                         precision=lax.Precision.HIGHEST) + b_ref[...]


def _modulation(cond, w, b):
    n, d = cond.shape
    nout = w.shape[1]
    tn = 512
    return pl.pallas_call(
        _mod_kernel,
        grid=(nout // tn,),
        in_specs=[pl.BlockSpec((n, d), lambda j: (0, 0)),
                  pl.BlockSpec((d, tn), lambda j: (0, j)),
                  pl.BlockSpec((1, tn), lambda j: (0, j))],
        out_specs=pl.BlockSpec((n, tn), lambda j: (0, j)),
        out_shape=jax.ShapeDtypeStruct((n, nout), F32),
        compiler_params=_cparams(("arbitrary",)),
        name="modulation",
    )(cond, w, b.reshape(1, nout))


def _rope_rotate(t, cos, sin):
    lane = lax.broadcasted_iota(jnp.int32, t.shape, 1)
    partner = jnp.where((lane % 32) < 16, pltpu.roll(t, LANES - 16, 1), pltpu.roll(t, 16, 1))
    return t * cos + partner * sin


def _inproj_kernel(x_ref, mod_ref, w_ref, cos_ref, sin_ref, q_ref, k_ref, v_ref, u_ref, ga_ref, gs_ref, *, rope):
    x = x_ref[...]
    h = (x * (1.0 + mod_ref[1:2, :]) + mod_ref[0:1, :]).astype(BF16)
    p = jnp.dot(h, w_ref[...], preferred_element_type=F32)
    o1 = ATTN_WIDTH
    o2 = o1 + KV_WIDTH
    o3 = o2 + KV_WIDTH
    d = x.shape[1]
    q = p[:, :o1]
    k = p[:, o1:o2]
    if rope:
        cos = cos_ref[...]
        sin = sin_ref[...]
        q = jnp.concatenate([_rope_rotate(q[:, j * LANES:(j + 1) * LANES], cos, sin)
                             for j in range(o1 // LANES)], axis=1)
        k = _rope_rotate(k, cos, sin)
    o4 = o3 + SSM_WIDTH
    q_ref[...] = q.astype(BF16)
    k_ref[...] = k
    v_ref[...] = p[:, o2:o3]
    u_ref[...] = p[:, o3:o4]
    ga_ref[...] = p[:, o4:o4 + d].astype(BF16)
    gs_ref[...] = p[:, o4 + d:].astype(BF16)


def _in_projection(x, mod, w, cos, sin, rope):
    B, L, D = x.shape
    tm = min(INPROJ_ROWS, L)
    per_batch = mod.shape[0] > 1
    nw = w.shape[1]
    row = lambda b, i: (b, i, 0)
    outs = pl.pallas_call(
        functools.partial(_inproj_kernel, rope=rope),
        grid=(B, L // tm),
        in_specs=[pl.BlockSpec((None, tm, D), row),
                  pl.BlockSpec((None, N_MOD, D), (lambda b, i: (b, 0, 0)) if per_batch else (lambda b, i: (0, 0, 0))),
                  pl.BlockSpec((D, nw), lambda b, i: (0, 0)),
                  pl.BlockSpec((tm, LANES), lambda b, i: (i, 0)),
                  pl.BlockSpec((tm, LANES), lambda b, i: (i, 0))],
        out_specs=[pl.BlockSpec((None, tm, ATTN_WIDTH), row),
                   pl.BlockSpec((None, tm, KV_WIDTH), row),
                   pl.BlockSpec((None, tm, KV_WIDTH), row),
                   pl.BlockSpec((None, tm, SSM_WIDTH), row),
                   pl.BlockSpec((None, tm, D), row),
                   pl.BlockSpec((None, tm, D), row)],
        out_shape=[jax.ShapeDtypeStruct((B, L, ATTN_WIDTH), BF16),
                   jax.ShapeDtypeStruct((B, L, KV_WIDTH), F32),
                   jax.ShapeDtypeStruct((B, L, KV_WIDTH), F32),
                   jax.ShapeDtypeStruct((B, L, SSM_WIDTH), F32),
                   jax.ShapeDtypeStruct((B, L, D), BF16),
                   jax.ShapeDtypeStruct((B, L, D), BF16)],
        compiler_params=_cparams(("parallel", "parallel")),
        name="in_projection",
    )(x, mod, w, cos, sin)
    return outs


def _rope_cos_sin(n_tokens):
    t = jnp.arange(n_tokens, dtype=jnp.int32)
    pos = jnp.stack([t // GRID_W, t % GRID_W], axis=-1).astype(F32)
    n_freq = HEAD_DIM // 4
    inv_freq = ROPE_BASE ** (-jnp.arange(n_freq, dtype=F32) / n_freq)
    ang = pos[:, :, None] * inv_freq
    c, s = jnp.cos(ang), jnp.sin(ang)
    cos = jnp.concatenate([c[:, 0], c[:, 0], c[:, 1], c[:, 1]], axis=-1)
    sin = jnp.concatenate([-s[:, 0], s[:, 0], -s[:, 1], s[:, 1]], axis=-1)
    return jnp.tile(cos, (1, LANES // HEAD_DIM)), jnp.tile(sin, (1, LANES // HEAD_DIM))


def _attend(sink_ref, q, kcat, vcat, mask, o_ref, row0=0):
    lq = q.shape[0]
    lane = lax.broadcasted_iota(jnp.int32, (1, LANES), 1)
    low = lane < HEAD_DIM
    k_sw = pltpu.roll(kcat, HEAD_DIM, 1)
    v_sw = pltpu.roll(vcat, HEAD_DIM, 1)
    neg = jnp.finfo(F32).min
    scores, vds = [], []
    for h in range(N_KV_HEADS):
        keep = low if h == 0 else jnp.logical_not(low)
        kd = jnp.where(keep, kcat, k_sw).astype(BF16)
        vds.append(jnp.where(keep, vcat, v_sw).astype(BF16))
        qs = []
        for j in range(Q_PER_KV):
            head = h * Q_PER_KV + j
            blk = q[:, (head // 2) * LANES:(head // 2 + 1) * LANES]
            sel = low if head % 2 == 0 else jnp.logical_not(low)
            qs.append(jnp.where(sel, blk, jnp.zeros_like(blk)))
        qstack = jnp.concatenate(qs, axis=0)
        scores.append(lax.dot_general(qstack, kd, (((1,), (1,)), ((), ())), preferred_element_type=F32))
    probs, denoms = [], []
    for h in range(N_KV_HEADS):
        s = scores[h]
        ps, ls = [], []
        for j in range(Q_PER_KV):
            sj = s[j * lq:(j + 1) * lq]
            if mask is not None:
                lm = mask.shape[1]
                sj = jnp.concatenate([jnp.where(mask, sj[:, :lm], neg), sj[:, lm:]], axis=1)
            sink = sink_ref[h * Q_PER_KV + j] * LOG2_E
            m = jnp.maximum(jnp.max(sj, axis=1, keepdims=True), sink)
            pj = jnp.exp2(sj - m)
            ls.append(jnp.sum(pj, axis=1, keepdims=True) + jnp.exp2(sink - m))
            ps.append(pj.astype(BF16))
        probs.append(jnp.concatenate(ps, axis=0))
        denoms.append(ls)
    for h in range(N_KV_HEADS):
        ls = denoms[h]
        o = jnp.dot(probs[h], vds[h], preferred_element_type=F32)
        for jj in range(Q_PER_KV // 2):
            oe = o[(2 * jj) * lq:(2 * jj + 1) * lq] / ls[2 * jj]
            oo = o[(2 * jj + 1) * lq:(2 * jj + 2) * lq] / ls[2 * jj + 1]
            cb = (h * Q_PER_KV) // 2 + jj
            o_ref[row0:row0 + lq, cb * LANES:(cb + 1) * LANES] = jnp.where(low, oe, oo).astype(o_ref.dtype)


def _attn_ctx_kernel(sink_ref, q_ref, k_ref, v_ref, o_ref):
    _attend(sink_ref, q_ref[...], k_ref[...], v_ref[...], None, o_ref)


def _attn_lat_kernel(sink_ref, q_ref, kp_ref, kc_ref, kn_ref, vp_ref, vc_ref, vn_ref, kx_ref, vx_ref, o_ref):
    n = pl.program_id(1)
    nb = pl.num_programs(1)
    blk = kp_ref.shape[0]
    nq = q_ref.shape[0] // blk
    klocal = jnp.concatenate([kp_ref[...], kc_ref[...], kn_ref[...]], axis=0)
    vlocal = jnp.concatenate([vp_ref[...], vc_ref[...], vn_ref[...]], axis=0)
    qi = lax.broadcasted_iota(jnp.int32, (blk, 3 * blk), 0)
    kj = lax.broadcasted_iota(jnp.int32, (blk, 3 * blk), 1)
    rel = kj - blk - qi
    in_band = (rel <= WINDOW) & (rel >= -WINDOW)
    for j in range(nq):
        mask = in_band
        if j == 0:
            mask = mask & ((kj >= blk) | (n > 0))
        if j == nq - 1:
            mask = mask & ((kj < 2 * blk) | (n < nb - 1))
        kcat = jnp.concatenate([klocal[j * blk:(j + 3) * blk], kx_ref[...]], axis=0)
        vcat = jnp.concatenate([vlocal[j * blk:(j + 3) * blk], vx_ref[...]], axis=0)
        _attend(sink_ref, q_ref[j * blk:(j + 1) * blk, :], kcat, vcat, mask, o_ref, j * blk)


def _attention_context(q, k, v, sink):
    B, L, _ = q.shape
    row = lambda b: (b, 0, 0)
    return pl.pallas_call(
        _attn_ctx_kernel,
        grid=(B,),
        in_specs=[pl.BlockSpec(memory_space=pltpu.SMEM),
                  pl.BlockSpec((None, L, ATTN_WIDTH), row),
                  pl.BlockSpec((None, L, KV_WIDTH), row),
                  pl.BlockSpec((None, L, KV_WIDTH), row)],
        out_specs=pl.BlockSpec((None, L, ATTN_WIDTH), row),
        out_shape=jax.ShapeDtypeStruct((B, L, ATTN_WIDTH), BF16),
        compiler_params=_cparams(("parallel",)),
        name="attention_context",
    )(sink, q, k, v)


def _attention_latent(q, k, v, k_ctx, v_ctx, sink):
    B, S, _ = q.shape
    blk = WINDOW
    nq = ATTN_QBLOCKS
    nb = S // blk
    nctx = k_ctx.shape[1]
    cur = lambda b, n: (b, n, 0)
    prv = lambda b, n: (b, jnp.maximum(n * nq - 1, 0), 0)
    nxt = lambda b, n: (b, jnp.minimum(n * nq + nq, nb - 1), 0)
    ctx = lambda b, n: (b, 0, 0)
    edge = lambda im: pl.BlockSpec((None, blk, KV_WIDTH), im)
    mid = pl.BlockSpec((None, nq * blk, KV_WIDTH), cur)
    return pl.pallas_call(
        _attn_lat_kernel,
        grid=(B, nb // nq),
        in_specs=[pl.BlockSpec(memory_space=pltpu.SMEM),
                  pl.BlockSpec((None, nq * blk, ATTN_WIDTH), cur),
                  edge(prv), mid, edge(nxt), edge(prv), mid, edge(nxt),
                  pl.BlockSpec((None, nctx, KV_WIDTH), ctx),
                  pl.BlockSpec((None, nctx, KV_WIDTH), ctx)],
        out_specs=pl.BlockSpec((None, nq * blk, ATTN_WIDTH), cur),
        out_shape=jax.ShapeDtypeStruct((B, S, ATTN_WIDTH), BF16),
        compiler_params=_cparams(("parallel", "parallel")),
        name="attention_latent",
    )(sink, q, k, k, k, v, v, v, k_ctx, v_ctx)


def _ssm_kernel(u_ref, wb_ref, wc_ref, a_ref, d_ref, h0_ref, y_ref, fs_ref, ut_ref, yt_ref, st_ref, *bu_refs):
    rev = pl.program_id(0)
    i = pl.program_id(2)
    nc = pl.num_programs(2)
    nseq, tm, _ = u_ref.shape
    sw = a_ref.shape[-1] // SSM_STRIPS
    nre = sw // 2

    @pl.when(i == 0)
    def _():
        st_ref[...] = h0_ref[...]

    for b in range(nseq):
        ub = u_ref[b]
        for s in range(SSM_STRIPS):
            ut_ref[s, pl.ds(b, tm, stride=nseq), :] = ub[:, s * LANES:(s + 1) * LANES]
    for s in range(SSM_STRIPS):
        bu_refs[s][...] = jnp.dot(ut_ref[s].astype(BF16), wb_ref[s], preferred_element_type=F32)

    for s in range(SSM_STRIPS):
        bu_ref = bu_refs[s]
        a_re = a_ref[:, s * sw:s * sw + nre]
        a_im = a_ref[:, s * sw + nre:(s + 1) * sw]

        def step(t, carry):
            xr, xi = carry
            r = pl.multiple_of((t + rev * (tm - 1 - 2 * t)) * nseq, nseq)
            nr = a_re * xr - a_im * xi + bu_ref[pl.ds(r, nseq), 0:nre]
            ni = a_re * xi + a_im * xr + bu_ref[pl.ds(r, nseq), nre:sw]
            bu_ref[pl.ds(r, nseq), 0:nre] = nr
            bu_ref[pl.ds(r, nseq), nre:sw] = ni
            return nr, ni

        xr, xi = lax.fori_loop(0, tm, step, (st_ref[:, s * sw:s * sw + nre], st_ref[:, s * sw + nre:(s + 1) * sw]),
                               unroll=True)
        st_ref[:, s * sw:s * sw + nre] = xr
        st_ref[:, s * sw + nre:(s + 1) * sw] = xi
        yt_ref[s] = jnp.dot(bu_ref[...].astype(BF16), wc_ref[s], preferred_element_type=F32)

    for b in range(nseq):
        yb = jnp.concatenate([yt_ref[s, pl.ds(b, tm, stride=nseq), :] for s in range(SSM_STRIPS)], axis=1)
        y_ref[b] = yb + u_ref[b] * d_ref[...]

    @pl.when(i == nc - 1)
    def _():
        fs_ref[...] = st_ref[...]


def _ssm_mix(u, ssm, h0):
    B, L, _ = u.shape
    nseq = SUBLANES
    tm = min(SSM_STEPS, L)
    nc = L // tm
    rows = nseq * tm
    ns = 2 * SSM_GROUPS * SSM_STATE
    chunk = lambda d, g, i: i + d * (nc - 1 - 2 * i)
    y, fs = pl.pallas_call(
        _ssm_kernel,
        grid=(2, B // nseq, nc),
        in_specs=[pl.BlockSpec((nseq, tm, SSM_WIDTH), lambda d, g, i: (g, chunk(d, g, i), 0)),
                  pl.BlockSpec((None, SSM_STRIPS, LANES, ns // SSM_STRIPS), lambda d, g, i: (d, 0, 0, 0)),
                  pl.BlockSpec((None, SSM_STRIPS, ns // SSM_STRIPS, LANES), lambda d, g, i: (d, 0, 0, 0)),
                  pl.BlockSpec((None, nseq, ns), lambda d, g, i: (d, 0, 0)),
                  pl.BlockSpec((None, 1, SSM_WIDTH), lambda d, g, i: (d, 0, 0)),
                  pl.BlockSpec((None, nseq, ns), lambda d, g, i: (d, g, 0))],
        out_specs=[pl.BlockSpec((None, nseq, tm, SSM_WIDTH), lambda d, g, i: (d, g, chunk(d, g, i), 0)),
                   pl.BlockSpec((None, nseq, ns), lambda d, g, i: (d, g, 0))],
        out_shape=[jax.ShapeDtypeStruct((2, B, L, SSM_WIDTH), F32),
                   jax.ShapeDtypeStruct((2, B, ns), F32)],
        scratch_shapes=[pltpu.VMEM((SSM_STRIPS, rows, LANES), F32), pltpu.VMEM((SSM_STRIPS, rows, LANES), F32),
                        pltpu.VMEM((nseq, ns), F32)]
        + [pltpu.VMEM((rows, ns // SSM_STRIPS), F32) for _ in range(SSM_STRIPS)],
        compiler_params=_cparams(("arbitrary", "arbitrary", "arbitrary")),
        name="ssm_scan",
    )(u, ssm["wb"], ssm["wc"], ssm["a"], ssm["d"], h0)
    return y, fs


def _state_to_lanes(s):
    lead = s.shape[:-3]
    s = s.reshape(lead + (2, SSM_STRIPS, SSM_GROUPS // SSM_STRIPS, SSM_STATE))
    s = jnp.swapaxes(s, -4, -3)
    return s.reshape(lead + (2 * SSM_GROUPS * SSM_STATE,))


def _lanes_to_state(v):
    lead = v.shape[:-1]
    s = v.reshape(lead + (SSM_STRIPS, 2, SSM_GROUPS // SSM_STRIPS, SSM_STATE))
    s = jnp.swapaxes(s, -4, -3)
    return s.reshape(lead + (2, SSM_GROUPS, SSM_STATE))


def _ssm_params(a_re, a_im, log_dt, b_re, b_im, c_re, c_im, dvec):
    G, N, C = SSM_GROUPS, SSM_STATE, SSM_GROUP
    lam_re = jnp.minimum(a_re, -1e-4)
    lam_im = a_im
    dt = jnp.exp(log_dt)[..., None]
    mag = jnp.exp(lam_re * dt)
    abar_re, abar_im = mag * jnp.cos(lam_im * dt), mag * jnp.sin(lam_im * dt)
    den = jnp.square(lam_re) + jnp.square(lam_im)
    p, qi = abar_re - 1.0, abar_im
    f_re = (p * lam_re + qi * lam_im) / den
    f_im = (qi * lam_re - p * lam_im) / den
    bbar_re = f_re[..., None] * b_re - f_im[..., None] * b_im
    bbar_im = f_re[..., None] * b_im + f_im[..., None] * b_re
    a = _state_to_lanes(jnp.stack([abar_re, abar_im], axis=1))
    a = jnp.broadcast_to(a[:, None, :], (2, SUBLANES, a.shape[-1]))
    S = SSM_STRIPS
    gs = G // S
    eye = jnp.eye(gs, dtype=F32)

    def bd_in(bb):
        bb = bb.reshape(2, S, gs, N, C)
        return jnp.einsum('dkgnc,gh->dkgchn', bb, eye).reshape(2, S, gs * C, gs * N)

    def bd_out(cc):
        cc = cc.reshape(2, S, gs, C, N)
        return jnp.einsum('dkgcn,gh->dkgnhc', cc, eye).reshape(2, S, gs * N, gs * C)

    wb = jnp.concatenate([bd_in(bbar_re), bd_in(bbar_im)], axis=-1).astype(BF16)
    wc = jnp.concatenate([bd_out(c_re), -bd_out(c_im)], axis=-2).astype(BF16)
    d = jnp.stack([dvec, jnp.zeros_like(dvec)], axis=0).reshape(2, 1, SSM_WIDTH)
    return {"a": a, "wb": wb, "wc": wc, "d": d}


def _layer_norm(x, g, b):
    mu = jnp.mean(x, axis=-1, keepdims=True)
    xc = x - mu
    var = jnp.mean(xc * xc, axis=-1, keepdims=True)
    return xc * lax.rsqrt(var + LN_EPS) * g + b


def _pack_rows(x):
    w = x.shape[1] // 2
    return pltpu.pack_elementwise([x[:, :w], x[:, w:]], packed_dtype=BF16)


def _unpack_rows(p):
    return (pltpu.unpack_elementwise(p, index=0, packed_dtype=BF16, unpacked_dtype=F32),
            pltpu.unpack_elementwise(p, index=1, packed_dtype=BF16, unpacked_dtype=F32))


def _gelu_tanh(x):
    return 0.5 * x * (1.0 + jnp.tanh(math.sqrt(2.0 / math.pi) * (x + 0.044715 * (x * x * x))))


def _merge_kernel(x_ref, ao_ref, yf_ref, yb_ref, ga_ref, gs_ref, mod_ref, wglu_ref, wa_ref, ws_ref, wo_ref,
                  lng_ref, lnb_ref, rwt_ref, rb_ref, run0_ref, x1_ref, h2_ref, idx_ref, rank_ref, wt_ref, cnt_ref,
                  run_ref, *, alpha):
    @pl.when((pl.program_id(0) == 0) & (pl.program_id(1) == 0))
    def _():
        run_ref[...] = run0_ref[...].astype(F32)

    nrow = x_ref.shape[0]
    sub = nrow // MERGE_SPLIT
    groups = [slice(j * sub, (j + 1) * sub) for j in range(MERGE_SPLIT)]
    dot = functools.partial(jnp.dot, preferred_element_type=F32)
    z = [_gelu_tanh(yf_ref[rs, :] + yb_ref[rs, :]) for rs in groups]
    attn_br = [dot(ao_ref[rs, :], wa_ref[...]) for rs in groups]
    gate = [jax.nn.sigmoid(dot(zg.astype(BF16), wglu_ref[...])) for zg in z]
    ssm_br = [dot((zg * gg).astype(BF16), ws_ref[...]) for zg, gg in zip(z, gate)]
    merged = [jax.nn.sigmoid(ga_ref[rs, :].astype(F32)) * ab + jax.nn.sigmoid(gs_ref[rs, :].astype(F32)) * sb
              for rs, ab, sb in zip(groups, attn_br, ssm_br)]
    mix = [dot(mg.astype(BF16), wo_ref[...]) for mg in merged]
    h2s = []
    for rs, mg in zip(groups, mix):
        x1 = _layer_norm(alpha * x_ref[rs, :] + mod_ref[2:3, :] * mg, lng_ref[...], lnb_ref[...])
        h2g = x1 * (1.0 + mod_ref[4:5, :]) + mod_ref[3:4, :]
        x1_ref[rs, :] = x1
        h2_ref[rs, :] = _pack_rows(h2g)
        h2s.append(h2g)
    h2 = jnp.concatenate(h2s, axis=0)

    logits = lax.dot_general(rwt_ref[...], h2, (((1,), (1,)), ((), ())), preferred_element_type=F32,
                             precision=lax.Precision.HIGHEST)
    score = jax.nn.sigmoid(logits)
    tm = score.shape[1]
    eidx = lax.broadcasted_iota(jnp.int32, score.shape, 0).astype(F32)
    work = score + rb_ref[...]
    picks, sel = [], []
    member = jnp.zeros_like(score)
    for _ in range(TOP_K):
        best = jnp.max(work, axis=0, keepdims=True)
        pick = jnp.min(jnp.where(work == best, eidx, float(N_EXPERTS)), axis=0, keepdims=True)
        hit = eidx == pick
        picks.append(pick)
        sel.append(jnp.sum(jnp.where(hit, score, 0.0), axis=0, keepdims=True))
        member = member + hit.astype(F32)
        work = jnp.where(hit, -jnp.inf, work)
    total = sel[0]
    for s in sel[1:]:
        total = total + s
    before = (lax.broadcasted_iota(jnp.int32, (tm, tm), 0) < lax.broadcasted_iota(jnp.int32, (tm, tm), 1))
    prefix = jnp.dot(member.astype(BF16), before.astype(BF16), preferred_element_type=F32)
    base = prefix + run_ref[...]
    for k in range(TOP_K):
        idx_ref[k:k + 1, :] = picks[k].astype(jnp.int32)
        rank_ref[k:k + 1, :] = jnp.sum(jnp.where(eidx == picks[k], base, 0.0), axis=0,
                                       keepdims=True).astype(jnp.int32)
        wt_ref[k:k + 1, :] = sel[k] / total * ROUTED_SCALE
    run_ref[...] = run_ref[...] + jnp.sum(member, axis=1, keepdims=True)
    cnt_ref[...] = run_ref[...].astype(jnp.int32)


def _merge(x, attn_o, y, ga, gs, mod, wts, run0, alpha):
    B, L, D = x.shape
    tm = min(MERGE_ROWS, L)
    per_batch = mod.shape[0] > 1
    row = lambda b, i: (b, i, 0)
    full = lambda b, i: (0, 0)
    wspec = lambda a: pl.BlockSpec(a.shape, full)
    return pl.pallas_call(
        functools.partial(_merge_kernel, alpha=alpha),
        grid=(B, L // tm),
        in_specs=[pl.BlockSpec((None, tm, D), row),
                  pl.BlockSpec((None, tm, ATTN_WIDTH), row),
                  pl.BlockSpec((None, None, tm, SSM_WIDTH), lambda b, i: (0, b, i, 0)),
                  pl.BlockSpec((None, None, tm, SSM_WIDTH), lambda b, i: (1, b, i, 0)),
                  pl.BlockSpec((None, tm, D), row),
                  pl.BlockSpec((None, tm, D), row),
                  pl.BlockSpec((None, N_MOD, D), (lambda b, i: (b, 0, 0)) if per_batch else (lambda b, i: (0, 0, 0))),
                  wspec(wts["w_glu"]), wspec(wts["w_attn_br"]), wspec(wts["w_ssm_br"]), wspec(wts["w_out"]),
                  wspec(wts["ln1_g"]), wspec(wts["ln1_b"]), wspec(wts["router_wt"]), wspec(wts["router_b"]),
                  wspec(run0)],
        out_specs=[pl.BlockSpec((None, tm, D), row),
                   pl.BlockSpec((None, tm, D // 2), row),
                   pl.BlockSpec((None, TOP_K, tm), lambda b, i: (b, 0, i)),
                   pl.BlockSpec((None, TOP_K, tm), lambda b, i: (b, 0, i)),
                   pl.BlockSpec((None, TOP_K, tm), lambda b, i: (b, 0, i)),
                   pl.BlockSpec((N_EXPERTS, 1), full)],
        out_shape=[jax.ShapeDtypeStruct((B, L, D), F32),
                   jax.ShapeDtypeStruct((B, L, D // 2), jnp.int32),
                   jax.ShapeDtypeStruct((B, TOP_K, L), jnp.int32),
                   jax.ShapeDtypeStruct((B, TOP_K, L), jnp.int32),
                   jax.ShapeDtypeStruct((B, TOP_K, L), F32),
                   jax.ShapeDtypeStruct((N_EXPERTS, 1), jnp.int32)],
        scratch_shapes=[pltpu.VMEM((N_EXPERTS, 1), F32)],
        compiler_params=_cparams(("arbitrary", "arbitrary")),
        name="merge_ln1_router",
    )(x, attn_o, y, y, ga, gs, mod, wts["w_glu"], wts["w_attn_br"], wts["w_ssm_br"], wts["w_out"],
      wts["ln1_g"], wts["ln1_b"], wts["router_wt"], wts["router_b"], run0)


def _dispatch(h2s, dests):
    W = h2s[0].shape[-1]
    n_slots = sum(d.size for d in dests)
    mesh = plsc.VectorSubcoreMesh(core_axis_name="core", subcore_axis_name="subcore")
    workers = mesh.num_cores * mesh.num_subcores
    tokens = [h.shape[0] * h.shape[1] for h in h2s]
    per_worker = [t // SC_WINDOW // workers for t in tokens]
    assert all(p * SC_WINDOW * workers == t for p, t in zip(per_worker, tokens))

    @pl.kernel(out_type=jax.ShapeDtypeStruct((n_slots, W), h2s[0].dtype), mesh=mesh,
               scratch_types=[pltpu.VMEM((TOP_K, SC_WINDOW), jnp.int32), pltpu.VMEM((SC_WINDOW, W), h2s[0].dtype)])
    def scatter(*refs):
        out_hbm, index_v, rows_v = refs[-3:]
        worker = lax.axis_index("core") * mesh.num_subcores + lax.axis_index("subcore")
        for g, per in enumerate(per_worker):
            rows_hbm, slots_hbm = refs[2 * g], refs[2 * g + 1]

            @pl.loop(0, per)
            def _(j, rows_hbm=rows_hbm, slots_hbm=slots_hbm, per=per):
                window = worker * per + j
                pltpu.sync_copy(rows_hbm.at[pl.ds(window * SC_WINDOW, SC_WINDOW)], rows_v)
                pltpu.sync_copy(slots_hbm.at[window], index_v)
                for k in range(TOP_K):
                    pltpu.sync_copy(rows_v, out_hbm.at[index_v.at[k]])

    args = []
    for h, d in zip(h2s, dests):
        nb, _, ln = d.shape
        slots = d.reshape(nb, TOP_K, ln // SC_WINDOW, SC_WINDOW).transpose(0, 2, 1, 3)
        args += [h.reshape(-1, W), slots.reshape(nb * ln // SC_WINDOW, TOP_K, SC_WINDOW)]
    return scatter(*args)


def _gmm_kernel(tile_ref, exp_ref, valid_ref, gstart_ref, xs_ref, wg_ref, wu_ref, wd_ref, ys_ref,
                wgb_ref, wub_ref, wdb_ref):
    w = pl.program_id(0)
    e = exp_ref[w]
    t = tile_ref[w]
    prev = jnp.maximum(w - 1, 0)
    new_expert = (w == 0) | (e != exp_ref[prev])
    first_visit = (w == 0) | (t != tile_ref[prev])

    @pl.when(new_expert)
    def _():
        wgb_ref[...] = wg_ref[...].astype(BF16)
        wub_ref[...] = wu_ref[...].astype(BF16)
        wdb_ref[...] = wd_ref[...].astype(BF16)

    @pl.when(first_visit)
    def _():
        ys_ref[...] = jnp.zeros_like(ys_ref)

    tg = xs_ref.shape[0]
    sub = tg // GMM_SPLIT
    lo = gstart_ref[e]
    hi = gstart_ref[e + 1]
    for j in range(GMM_SPLIT):
        first_row = t * tg + j * sub

        @pl.when((valid_ref[w] == 1) & (lo < first_row + sub) & (hi > first_row))
        def _(j=j, first_row=first_row):
            rs = slice(j * sub, (j + 1) * sub)
            dot = functools.partial(jnp.dot, preferred_element_type=F32)
            x = jnp.concatenate(_unpack_rows(xs_ref[rs, :]), axis=1).astype(BF16)
            g = dot(x, wgb_ref[...])
            u = dot(x, wub_ref[...])
            y = _pack_rows(dot((g * jax.nn.sigmoid(g) * u).astype(BF16), wdb_ref[...]))
            rows = first_row + lax.broadcasted_iota(jnp.int32, (sub, 1), 0)
            mine = (rows >= lo) & (rows < hi)
            ys_ref[rs, :] = jnp.where(mine, y, ys_ref[rs, :])


def _grouped_experts(xs, counts, w_gate, w_up, w_down):
    A, W = xs.shape
    D = w_gate.shape[-2]
    tg = GMM_ROWS
    nt = A // tg
    n_items = nt + N_EXPERTS - 1
    ff = w_gate.shape[-1]
    gend = jnp.cumsum(counts).astype(jnp.int32)
    gstart = jnp.concatenate([jnp.zeros((1,), jnp.int32), gend])
    first_row = jnp.arange(nt, dtype=jnp.int32) * tg
    count_le = lambda ends, v: jnp.sum((ends[None, :] <= v[:, None]).astype(jnp.int32), axis=1)
    e_lo = count_le(gend, first_row)
    e_hi = count_le(gend, first_row + tg - 1)
    per_tile = e_hi - e_lo + 1
    item_end = jnp.cumsum(per_tile).astype(jnp.int32)
    total = item_end[-1]
    wi = jnp.arange(n_items, dtype=jnp.int32)
    tile = jnp.minimum(count_le(item_end, wi), nt - 1)
    in_tile = (tile[:, None] == jnp.arange(nt, dtype=jnp.int32)[None, :]).astype(jnp.int32)
    lookup = lambda table: jnp.sum(in_tile * table[None, :], axis=1)
    expert = lookup(e_lo) + wi - lookup(item_end - per_tile)
    valid = (wi < total).astype(jnp.int32)
    expert = jnp.where(valid == 1, expert, e_hi[nt - 1])
    grid_spec = pltpu.PrefetchScalarGridSpec(
        num_scalar_prefetch=4,
        grid=(n_items,),
        in_specs=[pl.BlockSpec((tg, W), lambda w, tl, ex, va, gs: (tl[w], 0)),
                  pl.BlockSpec((None, D, ff), lambda w, tl, ex, va, gs: (ex[w], 0, 0)),
                  pl.BlockSpec((None, D, ff), lambda w, tl, ex, va, gs: (ex[w], 0, 0)),
                  pl.BlockSpec((None, ff, D), lambda w, tl, ex, va, gs: (ex[w], 0, 0))],
        out_specs=pl.BlockSpec((tg, W), lambda w, tl, ex, va, gs: (tl[w], 0)),
        scratch_shapes=[pltpu.VMEM((D, ff), BF16), pltpu.VMEM((D, ff), BF16), pltpu.VMEM((ff, D), BF16)],
    )
    return pl.pallas_call(
        _gmm_kernel,
        grid_spec=grid_spec,
        out_shape=jax.ShapeDtypeStruct((A, W), xs.dtype),
        compiler_params=_cparams(("arbitrary",)),
        name="moe_grouped_experts",
    )(tile, expert, valid, gstart, xs, w_gate, w_up, w_down)


def _gather_rows_sc(rows, index):
    n = index.shape[0]
    w = rows.shape[1]
    mesh = plsc.VectorSubcoreMesh(core_axis_name="core", subcore_axis_name="subcore")
    per_subcore = n // SC_WINDOW // (mesh.num_cores * mesh.num_subcores)
    assert per_subcore * SC_WINDOW * mesh.num_cores * mesh.num_subcores == n

    @pl.kernel(out_type=jax.ShapeDtypeStruct((n, w), rows.dtype), mesh=mesh,
               scratch_types=[pltpu.VMEM((1, SC_WINDOW), jnp.int32), pltpu.VMEM((SC_WINDOW, w), rows.dtype)])
    def gather(rows_hbm, index_hbm, out_hbm, index_v, rows_v):
        worker = lax.axis_index("core") * mesh.num_subcores + lax.axis_index("subcore")

        @pl.loop(0, per_subcore)
        def _(j):
            first = (worker * per_subcore + j) * SC_WINDOW
            pltpu.sync_copy(index_hbm.at[:, pl.ds(first, SC_WINDOW)], index_v)
            pltpu.sync_copy(rows_hbm.at[index_v.at[0]], rows_v)
            pltpu.sync_copy(rows_v, out_hbm.at[pl.ds(first, SC_WINDOW)])

    return gather(rows, index.reshape(1, n))


def _combine_rows_kernel(yg_ref, wt_ref, x1_ref, h2_ref, mod_ref, sg_ref, su_ref, sd_ref, lng_ref, lnb_ref, *rest,
                         alpha):
    o_ref = rest[-1]
    h = jnp.concatenate(_unpack_rows(h2_ref[...]), axis=1).astype(BF16)
    g = jnp.dot(h, sg_ref[...], preferred_element_type=F32)
    u = jnp.dot(h, su_ref[...], preferred_element_type=F32)
    moe = jnp.dot((g * jax.nn.sigmoid(g) * u).astype(BF16), sd_ref[...], preferred_element_type=F32)
    wt = wt_ref[...]
    lo = jnp.zeros(h2_ref.shape, F32)
    hi = lo
    for k in range(TOP_K):
        rl, rh = _unpack_rows(yg_ref[k])
        lo = lo + wt[:, k:k + 1] * rl
        hi = hi + wt[:, k:k + 1] * rh
    moe = moe + jnp.concatenate([lo, hi], axis=1)
    o_ref[...] = _layer_norm(alpha * x1_ref[...] + mod_ref[5:6, :] * moe, lng_ref[...], lnb_ref[...])


def _combine_rows(ys, dest, wts, x1, h2, mod, shared, alpha):
    B, L, D = x1.shape
    W = ys.shape[-1]
    tm = min(COMBINE_ROWS, L)
    per_batch = mod.shape[0] > 1
    pieces = max(1, B * L // COMBINE_PIECE_TOKENS)
    nb = B // pieces
    full = lambda b, i: (0, 0)
    wspec = lambda a: pl.BlockSpec(a.shape, full)
    wts_t = jnp.swapaxes(wts, 1, 2)
    out = None
    for p in range(pieces):
        b0 = p * nb
        row = lambda b, i, b0=b0: (b0 + b, i, 0)
        slots = jnp.swapaxes(dest[b0:b0 + nb], 0, 1).reshape(TOP_K * nb * L)
        yg = _gather_rows_sc(ys, slots).reshape(TOP_K, nb, L, W)
        in_specs = [pl.BlockSpec((TOP_K, None, tm, W), lambda b, i: (0, b, i, 0)),
                    pl.BlockSpec((None, tm, TOP_K), row),
                    pl.BlockSpec((None, tm, D), row),
                    pl.BlockSpec((None, tm, W), row),
                    pl.BlockSpec((None, N_MOD, D), (lambda b, i, b0=b0: (b0 + b, 0, 0)) if per_batch
                                 else (lambda b, i: (0, 0, 0))),
                    wspec(shared["sh_w_gate"]), wspec(shared["sh_w_up"]), wspec(shared["sh_w_down"]),
                    wspec(shared["ln2_g"]), wspec(shared["ln2_b"])]
        args = [yg, wts_t, x1, h2, mod, shared["sh_w_gate"], shared["sh_w_up"], shared["sh_w_down"],
                shared["ln2_g"], shared["ln2_b"]]
        aliases = {}
        if out is not None:
            in_specs.append(pl.BlockSpec(memory_space=pl.ANY))
            args.append(out)
            aliases = {len(args) - 1: 0}
        out = pl.pallas_call(
            functools.partial(_combine_rows_kernel, alpha=alpha),
            grid=(nb, L // tm),
            in_specs=in_specs,
            out_specs=pl.BlockSpec((None, tm, D), row),
            out_shape=jax.ShapeDtypeStruct((B, L, D), F32),
            input_output_aliases=aliases,
            compiler_params=_cparams(("parallel", "parallel")),
            name="moe_combine_ln2",
        )(*args)
    return out


def _slots(idx, rank, counts):
    start = jnp.cumsum(counts) - counts
    pick = idx[..., None] == jnp.arange(N_EXPERTS, dtype=jnp.int32)
    return jnp.sum(jnp.where(pick, start, 0), axis=-1).astype(jnp.int32) + rank


def _mixer_and_router(x, mod, wts, ssm, cache, run0, alpha):
    B, L, D = x.shape
    latent = cache is not None
    if latent:
        k_ctx, v_ctx, h0 = cache
        cos, sin = _rope_cos_sin(L)
    else:
        cos = jnp.zeros((L, LANES), F32)
        sin = cos
        h0 = jnp.zeros((2, B, 2 * SSM_GROUPS * SSM_STATE), F32)
    q, k, v, u, ga, gs = _in_projection(x, mod, wts["w_in"], cos, sin, rope=latent)
    if latent:
        attn_o = _attention_latent(q, k, v, k_ctx, v_ctx, wts["attn_sink"])
    else:
        attn_o = _attention_context(q, k, v, wts["attn_sink"])
    y, fs = _ssm_mix(u, ssm, h0)
    x1, h2, idx, rank, rw, counts = _merge(x, attn_o, y, ga, gs, mod, wts, run0, alpha)
    return {"x1": x1, "h2": h2, "idx": idx, "rank": rank, "rw": rw, "mod": mod}, counts, k, v, fs


def _moe(groups, counts, wts, alpha):
    counts = counts.reshape(N_EXPERTS)
    for g in groups:
        g["dest"] = _slots(g["idx"], g["rank"], counts)
    xs = _dispatch([g["h2"] for g in groups], [g["dest"] for g in groups])
    ys = _grouped_experts(xs, counts, wts["exp_w_gate"], wts["exp_w_up"], wts["exp_w_down"])
    return [_combine_rows(ys, g["dest"], g["rw"], g["x1"], g["h2"], g["mod"], wts, alpha) for g in groups]


def kernel(x_prompt, x_sample, cache_k, cache_v, state_ssm, c, c_ctx, mod_w, mod_b, w_in, attn_sink, w_attn_br, ssm_a_re, ssm_a_im, ssm_log_dt, ssm_b_re, ssm_b_im, ssm_c_re, ssm_c_im, ssm_d, w_glu, w_ssm_br, w_out, ln1_g, ln1_b, ln2_g, ln2_b, router_w, router_b, exp_w_gate, exp_w_up, exp_w_down, sh_w_gate, sh_w_up, sh_w_down):
    depth = w_in.shape[0]
    assert depth == 1
    alpha = (2.0 * depth) ** 0.25
    D = x_prompt.shape[-1]
    nb_p = x_prompt.shape[0]
    nb_s = x_sample.shape[0]
    l = 0

    ncond = 1 + nb_s
    npad = -ncond % SUBLANES
    cond = jnp.concatenate([c_ctx[None, :], c, jnp.zeros((npad, D), F32)], axis=0)
    mod = _modulation(cond, mod_w[l], mod_b[l]).reshape(ncond + npad, N_MOD, D)
    mod_ctx, mod_lat = mod[0:1], mod[1:ncond]

    o1 = ATTN_WIDTH
    o3 = o1 + 2 * KV_WIDTH
    o4 = o3 + SSM_WIDTH
    wi = w_in[l]
    scale = HEAD_DIM ** -0.5 * LOG2_E
    wts = {
        "w_in": jnp.concatenate([wi[:, :o1] * scale, wi[:, o1:]], axis=1).astype(BF16),
        "attn_sink": attn_sink[l],
        "w_glu": w_glu[l].astype(BF16), "w_attn_br": w_attn_br[l].astype(BF16),
        "w_ssm_br": w_ssm_br[l].astype(BF16), "w_out": w_out[l].astype(BF16),
        "ln1_g": ln1_g[l].reshape(1, D), "ln1_b": ln1_b[l].reshape(1, D),
        "ln2_g": ln2_g[l].reshape(1, D), "ln2_b": ln2_b[l].reshape(1, D),
        "router_wt": router_w[l].T, "router_b": router_b[l].reshape(N_EXPERTS, 1),
        "exp_w_gate": exp_w_gate[l], "exp_w_up": exp_w_up[l], "exp_w_down": exp_w_down[l],
        "sh_w_gate": sh_w_gate[l].astype(BF16), "sh_w_up": sh_w_up[l].astype(BF16),
        "sh_w_down": sh_w_down[l].astype(BF16),
    }
    ssm = _ssm_params(ssm_a_re[l], ssm_a_im[l], ssm_log_dt[l], ssm_b_re[l], ssm_b_im[l],
                      ssm_c_re[l], ssm_c_im[l], ssm_d[l])

    no_tokens_yet = jnp.zeros((N_EXPERTS, 1), jnp.int32)
    grp_p, counts, k_p, v_p, fs_p = _mixer_and_router(x_prompt, mod_ctx, wts, ssm, None, no_tokens_yet, alpha)

    past = cache_k.shape[2]
    k_ctx = cache_k[:, l].reshape(nb_s, past, KV_WIDTH)
    v_ctx = cache_v[:, l].reshape(nb_s, past, KV_WIDTH)
    h0 = jnp.swapaxes(_state_to_lanes(state_ssm[:, l]), 0, 1)
    grp_s, counts, _, _, _ = _mixer_and_router(x_sample, mod_lat, wts, ssm, (k_ctx, v_ctx, h0), counts, alpha)
    yp, ys_ = _moe([grp_p, grp_s], counts, wts, alpha)

    Lp = x_prompt.shape[1]
    new_k = k_p.reshape(nb_p, 1, Lp, N_KV_HEADS, HEAD_DIM)
    new_v = v_p.reshape(nb_p, 1, Lp, N_KV_HEADS, HEAD_DIM)
    new_s = _lanes_to_state(jnp.swapaxes(fs_p, 0, 1))[:, None]
    return (yp, ys_, new_k, new_v, new_s)
```

```python
import functools
import math

import jax
import jax.numpy as jnp
from jax import lax
from jax.experimental import pallas as pl
from jax.experimental.pallas import tpu as pltpu
from jax.experimental.pallas import tpu_sc as plsc

F32 = jnp.float32
BF16 = jnp.bfloat16

GRID_W = 64
HEAD_DIM = 64
N_Q_HEADS = 8
N_KV_HEADS = 2
Q_PER_KV = N_Q_HEADS // N_KV_HEADS
ATTN_WIDTH = N_Q_HEADS * HEAD_DIM
KV_WIDTH = N_KV_HEADS * HEAD_DIM
WINDOW = 128
ROPE_BASE = 10000.0
SSM_WIDTH = 512
SSM_GROUP = 16
SSM_GROUPS = SSM_WIDTH // SSM_GROUP
SSM_STATE = 64
N_EXPERTS = 64
TOP_K = 6
ROUTED_SCALE = 2.5
N_MOD = 6
LN_EPS = 1e-5
LOG2_E = math.log2(math.e)

SUBLANES = 8
LANES = 128
VMEM_LIMIT = 48 * 1024 * 1024

INPROJ_ROWS = 1024
ATTN_QBLOCKS = 4
SSM_STEPS = 64
SSM_STRIPS = 4
MERGE_ROWS = 512
MERGE_SPLIT = 2
GMM_ROWS = 2048
GMM_SPLIT = 4
COMBINE_ROWS = 256
COMBINE_PIECE_TOKENS = 8192
SC_WINDOW = 128


def _cparams(sem):
    return pltpu.CompilerParams(dimension_semantics=sem, vmem_limit_bytes=VMEM_LIMIT)


def _mod_kernel(c_ref, w_ref, b_ref, o_ref):
    c = c_ref[...]
    s = c * jax.nn.sigmoid(c)
    o_ref[...] = jnp.dot(s, w_ref[...], preferred_element_type=F32,
                         precision=lax.Precision.HIGHEST) + b_ref[...]


def _modulation(cond, w, b):
    n, d = cond.shape
    nout = w.shape[1]
    tn = 512
    return pl.pallas_call(
        _mod_kernel,
        grid=(nout // tn,),
        in_specs=[pl.BlockSpec((n, d), lambda j: (0, 0)),
                  pl.BlockSpec((d, tn), lambda j: (0, j)),
                  pl.BlockSpec((1, tn), lambda j: (0, j))],
        out_specs=pl.BlockSpec((n, tn), lambda j: (0, j)),
        out_shape=jax.ShapeDtypeStruct((n, nout), F32),
        compiler_params=_cparams(("arbitrary",)),
        name="modulation",
    )(cond, w, b.reshape(1, nout))


def _rope_rotate(t, cos, sin):
    lane = lax.broadcasted_iota(jnp.int32, t.shape, 1)
    partner = jnp.where((lane % 32) < 16, pltpu.roll(t, LANES - 16, 1), pltpu.roll(t, 16, 1))
    return t * cos + partner * sin


def _inproj_kernel(x_ref, mod_ref, w_ref, cos_ref, sin_ref, q_ref, k_ref, v_ref, u_ref, ga_ref, gs_ref, *, rope):
    x = x_ref[...]
    h = (x * (1.0 + mod_ref[1:2, :]) + mod_ref[0:1, :]).astype(BF16)
    p = jnp.dot(h, w_ref[...], preferred_element_type=F32)
    o1 = ATTN_WIDTH
    o2 = o1 + KV_WIDTH
    o3 = o2 + KV_WIDTH
    d = x.shape[1]
    q = p[:, :o1]
    k = p[:, o1:o2]
    if rope:
        cos = cos_ref[...]
        sin = sin_ref[...]
        q = jnp.concatenate([_rope_rotate(q[:, j * LANES:(j + 1) * LANES], cos, sin)
                             for j in range(o1 // LANES)], axis=1)
        k = _rope_rotate(k, cos, sin)
    o4 = o3 + SSM_WIDTH
    q_ref[...] = q.astype(BF16)
    k_ref[...] = k
    v_ref[...] = p[:, o2:o3]
    u_ref[...] = p[:, o3:o4]
    ga_ref[...] = p[:, o4:o4 + d].astype(BF16)
    gs_ref[...] = p[:, o4 + d:].astype(BF16)


def _in_projection(x, mod, w, cos, sin, rope):
    B, L, D = x.shape
    tm = min(INPROJ_ROWS, L)
    per_batch = mod.shape[0] > 1
    nw = w.shape[1]
    row = lambda b, i: (b, i, 0)
    outs = pl.pallas_call(
        functools.partial(_inproj_kernel, rope=rope),
        grid=(B, L // tm),
        in_specs=[pl.BlockSpec((None, tm, D), row),
                  pl.BlockSpec((None, N_MOD, D), (lambda b, i: (b, 0, 0)) if per_batch else (lambda b, i: (0, 0, 0))),
                  pl.BlockSpec((D, nw), lambda b, i: (0, 0)),
                  pl.BlockSpec((tm, LANES), lambda b, i: (i, 0)),
                  pl.BlockSpec((tm, LANES), lambda b, i: (i, 0))],
        out_specs=[pl.BlockSpec((None, tm, ATTN_WIDTH), row),
                   pl.BlockSpec((None, tm, KV_WIDTH), row),
                   pl.BlockSpec((None, tm, KV_WIDTH), row),
                   pl.BlockSpec((None, tm, SSM_WIDTH), row),
                   pl.BlockSpec((None, tm, D), row),
                   pl.BlockSpec((None, tm, D), row)],
        out_shape=[jax.ShapeDtypeStruct((B, L, ATTN_WIDTH), BF16),
                   jax.ShapeDtypeStruct((B, L, KV_WIDTH), F32),
                   jax.ShapeDtypeStruct((B, L, KV_WIDTH), F32),
                   jax.ShapeDtypeStruct((B, L, SSM_WIDTH), F32),
                   jax.ShapeDtypeStruct((B, L, D), BF16),
                   jax.ShapeDtypeStruct((B, L, D), BF16)],
        compiler_params=_cparams(("parallel", "parallel")),
        name="in_projection",
    )(x, mod, w, cos, sin)
    return outs


def _rope_cos_sin(n_tokens):
    t = jnp.arange(n_tokens, dtype=jnp.int32)
    pos = jnp.stack([t // GRID_W, t % GRID_W], axis=-1).astype(F32)
    n_freq = HEAD_DIM // 4
    inv_freq = ROPE_BASE ** (-jnp.arange(n_freq, dtype=F32) / n_freq)
    ang = pos[:, :, None] * inv_freq
    c, s = jnp.cos(ang), jnp.sin(ang)
    cos = jnp.concatenate([c[:, 0], c[:, 0], c[:, 1], c[:, 1]], axis=-1)
    sin = jnp.concatenate([-s[:, 0], s[:, 0], -s[:, 1], s[:, 1]], axis=-1)
    return jnp.tile(cos, (1, LANES // HEAD_DIM)), jnp.tile(sin, (1, LANES // HEAD_DIM))


def _attend(sink_ref, q, kcat, vcat, mask, o_ref, row0=0):
    lq = q.shape[0]
    lane = lax.broadcasted_iota(jnp.int32, (1, LANES), 1)
    low = lane < HEAD_DIM
    k_sw = pltpu.roll(kcat, HEAD_DIM, 1)
    v_sw = pltpu.roll(vcat, HEAD_DIM, 1)
    neg = jnp.finfo(F32).min
    scores, vds = [], []
    for h in range(N_KV_HEADS):
        keep = low if h == 0 else jnp.logical_not(low)
        kd = jnp.where(keep, kcat, k_sw).astype(BF16)
        vds.append(jnp.where(keep, vcat, v_sw).astype(BF16))
        qs = []
        for j in range(Q_PER_KV):
            head = h * Q_PER_KV + j
            blk = q[:, (head // 2) * LANES:(head // 2 + 1) * LANES]
            sel = low if head % 2 == 0 else jnp.logical_not(low)
            qs.append(jnp.where(sel, blk, jnp.zeros_like(blk)))
        qstack = jnp.concatenate(qs, axis=0)
        scores.append(lax.dot_general(qstack, kd, (((1,), (1,)), ((), ())), preferred_element_type=F32))
    probs, denoms = [], []
    for h in range(N_KV_HEADS):
        s = scores[h]
        ps, ls = [], []
        for j in range(Q_PER_KV):
            sj = s[j * lq:(j + 1) * lq]
            if mask is not None:
                lm = mask.shape[1]
                sj = jnp.concatenate([jnp.where(mask, sj[:, :lm], neg), sj[:, lm:]], axis=1)
            sink = sink_ref[h * Q_PER_KV + j] * LOG2_E
            m = jnp.maximum(jnp.max(sj, axis=1, keepdims=True), sink)
            pj = jnp.exp2(sj - m)
            ls.append(jnp.sum(pj, axis=1, keepdims=True) + jnp.exp2(sink - m))
            ps.append(pj.astype(BF16))
        probs.append(jnp.concatenate(ps, axis=0))
        denoms.append(ls)
    for h in range(N_KV_HEADS):
        ls = denoms[h]
        o = jnp.dot(probs[h], vds[h], preferred_element_type=F32)
        for jj in range(Q_PER_KV // 2):
            oe = o[(2 * jj) * lq:(2 * jj + 1) * lq] / ls[2 * jj]
            oo = o[(2 * jj + 1) * lq:(2 * jj + 2) * lq] / ls[2 * jj + 1]
            cb = (h * Q_PER_KV) // 2 + jj
            o_ref[row0:row0 + lq, cb * LANES:(cb + 1) * LANES] = jnp.where(low, oe, oo).astype(o_ref.dtype)


def _attn_ctx_kernel(sink_ref, q_ref, k_ref, v_ref, o_ref):
    _attend(sink_ref, q_ref[...], k_ref[...], v_ref[...], None, o_ref)


def _attn_lat_kernel(sink_ref, q_ref, kp_ref, kc_ref, kn_ref, vp_ref, vc_ref, vn_ref, kx_ref, vx_ref, o_ref):
    n = pl.program_id(1)
    nb = pl.num_programs(1)
    blk = kp_ref.shape[0]
    nq = q_ref.shape[0] // blk
    klocal = jnp.concatenate([kp_ref[...], kc_ref[...], kn_ref[...]], axis=0)
    vlocal = jnp.concatenate([vp_ref[...], vc_ref[...], vn_ref[...]], axis=0)
    qi = lax.broadcasted_iota(jnp.int32, (blk, 3 * blk), 0)
    kj = lax.broadcasted_iota(jnp.int32, (blk, 3 * blk), 1)
    rel = kj - blk - qi
    in_band = (rel <= WINDOW) & (rel >= -WINDOW)
    for j in range(nq):
        mask = in_band
        if j == 0:
            mask = mask & ((kj >= blk) | (n > 0))
        if j == nq - 1:
            mask = mask & ((kj < 2 * blk) | (n < nb - 1))
        kcat = jnp.concatenate([klocal[j * blk:(j + 3) * blk], kx_ref[...]], axis=0)
        vcat = jnp.concatenate([vlocal[j * blk:(j + 3) * blk], vx_ref[...]], axis=0)
        _attend(sink_ref, q_ref[j * blk:(j + 1) * blk, :], kcat, vcat, mask, o_ref, j * blk)


def _attention_context(q, k, v, sink):
    B, L, _ = q.shape
    row = lambda b: (b, 0, 0)
    return pl.pallas_call(
        _attn_ctx_kernel,
        grid=(B,),
        in_specs=[pl.BlockSpec(memory_space=pltpu.SMEM),
                  pl.BlockSpec((None, L, ATTN_WIDTH), row),
                  pl.BlockSpec((None, L, KV_WIDTH), row),
                  pl.BlockSpec((None, L, KV_WIDTH), row)],
        out_specs=pl.BlockSpec((None, L, ATTN_WIDTH), row),
        out_shape=jax.ShapeDtypeStruct((B, L, ATTN_WIDTH), BF16),
        compiler_params=_cparams(("parallel",)),
        name="attention_context",
    )(sink, q, k, v)


def _attention_latent(q, k, v, k_ctx, v_ctx, sink):
    B, S, _ = q.shape
    blk = WINDOW
    nq = ATTN_QBLOCKS
    nb = S // blk
    nctx = k_ctx.shape[1]
    cur = lambda b, n: (b, n, 0)
    prv = lambda b, n: (b, jnp.maximum(n * nq - 1, 0), 0)
    nxt = lambda b, n: (b, jnp.minimum(n * nq + nq, nb - 1), 0)
    ctx = lambda b, n: (b, 0, 0)
    edge = lambda im: pl.BlockSpec((None, blk, KV_WIDTH), im)
    mid = pl.BlockSpec((None, nq * blk, KV_WIDTH), cur)
    return pl.pallas_call(
        _attn_lat_kernel,
        grid=(B, nb // nq),
        in_specs=[pl.BlockSpec(memory_space=pltpu.SMEM),
                  pl.BlockSpec((None, nq * blk, ATTN_WIDTH), cur),
                  edge(prv), mid, edge(nxt), edge(prv), mid, edge(nxt),
                  pl.BlockSpec((None, nctx, KV_WIDTH), ctx),
                  pl.BlockSpec((None, nctx, KV_WIDTH), ctx)],
        out_specs=pl.BlockSpec((None, nq * blk, ATTN_WIDTH), cur),
        out_shape=jax.ShapeDtypeStruct((B, S, ATTN_WIDTH), BF16),
        compiler_params=_cparams(("parallel", "parallel")),
        name="attention_latent",
    )(sink, q, k, k, k, v, v, v, k_ctx, v_ctx)


def _ssm_kernel(u_ref, wb_ref, wc_ref, a_ref, d_ref, h0_ref, y_ref, fs_ref, ut_ref, yt_ref, st_ref, *bu_refs):
    rev = pl.program_id(0)
    i = pl.program_id(2)
    nc = pl.num_programs(2)
    nseq, tm, _ = u_ref.shape
    sw = a_ref.shape[-1] // SSM_STRIPS
    nre = sw // 2

    @pl.when(i == 0)
    def _():
        st_ref[...] = h0_ref[...]

    for b in range(nseq):
        ub = u_ref[b]
        for s in range(SSM_STRIPS):
            ut_ref[s, pl.ds(b, tm, stride=nseq), :] = ub[:, s * LANES:(s + 1) * LANES]
    for s in range(SSM_STRIPS):
        bu_refs[s][...] = jnp.dot(ut_ref[s].astype(BF16), wb_ref[s], preferred_element_type=F32)

    for s in range(SSM_STRIPS):
        bu_ref = bu_refs[s]
        a_re = a_ref[:, s * sw:s * sw + nre]
        a_im = a_ref[:, s * sw + nre:(s + 1) * sw]

        def step(t, carry):
            xr, xi = carry
            r = pl.multiple_of((t + rev * (tm - 1 - 2 * t)) * nseq, nseq)
            nr = a_re * xr - a_im * xi + bu_ref[pl.ds(r, nseq), 0:nre]
            ni = a_re * xi + a_im * xr + bu_ref[pl.ds(r, nseq), nre:sw]
            bu_ref[pl.ds(r, nseq), 0:nre] = nr
            bu_ref[pl.ds(r, nseq), nre:sw] = ni
            return nr, ni

        xr, xi = lax.fori_loop(0, tm, step, (st_ref[:, s * sw:s * sw + nre], st_ref[:, s * sw + nre:(s + 1) * sw]),
                               unroll=True)
        st_ref[:, s * sw:s * sw + nre] = xr
        st_ref[:, s * sw + nre:(s + 1) * sw] = xi
        yt_ref[s] = jnp.dot(bu_ref[...].astype(BF16), wc_ref[s], preferred_element_type=F32)

    for b in range(nseq):
        yb = jnp.concatenate([yt_ref[s, pl.ds(b, tm, stride=nseq), :] for s in range(SSM_STRIPS)], axis=1)
        y_ref[b] = yb + u_ref[b] * d_ref[...]

    @pl.when(i == nc - 1)
    def _():
        fs_ref[...] = st_ref[...]


def _ssm_mix(u, ssm, h0):
    B, L, _ = u.shape
    nseq = SUBLANES
    tm = min(SSM_STEPS, L)
    nc = L // tm
    rows = nseq * tm
    ns = 2 * SSM_GROUPS * SSM_STATE
    chunk = lambda d, g, i: i + d * (nc - 1 - 2 * i)
    y, fs = pl.pallas_call(
        _ssm_kernel,
        grid=(2, B // nseq, nc),
        in_specs=[pl.BlockSpec((nseq, tm, SSM_WIDTH), lambda d, g, i: (g, chunk(d, g, i), 0)),
                  pl.BlockSpec((None, SSM_STRIPS, LANES, ns // SSM_STRIPS), lambda d, g, i: (d, 0, 0, 0)),
                  pl.BlockSpec((None, SSM_STRIPS, ns // SSM_STRIPS, LANES), lambda d, g, i: (d, 0, 0, 0)),
                  pl.BlockSpec((None, nseq, ns), lambda d, g, i: (d, 0, 0)),
                  pl.BlockSpec((None, 1, SSM_WIDTH), lambda d, g, i: (d, 0, 0)),
                  pl.BlockSpec((None, nseq, ns), lambda d, g, i: (d, g, 0))],
        out_specs=[pl.BlockSpec((None, nseq, tm, SSM_WIDTH), lambda d, g, i: (d, g, chunk(d, g, i), 0)),
                   pl.BlockSpec((None, nseq, ns), lambda d, g, i: (d, g, 0))],
        out_shape=[jax.ShapeDtypeStruct((2, B, L, SSM_WIDTH), F32),
                   jax.ShapeDtypeStruct((2, B, ns), F32)],
        scratch_shapes=[pltpu.VMEM((SSM_STRIPS, rows, LANES), F32), pltpu.VMEM((SSM_STRIPS, rows, LANES), F32),
                        pltpu.VMEM((nseq, ns), F32)]
        + [pltpu.VMEM((rows, ns // SSM_STRIPS), F32) for _ in range(SSM_STRIPS)],
        compiler_params=_cparams(("arbitrary", "arbitrary", "arbitrary")),
        name="ssm_scan",
    )(u, ssm["wb"], ssm["wc"], ssm["a"], ssm["d"], h0)
    return y, fs


def _state_to_lanes(s):
    lead = s.shape[:-3]
    s = s.reshape(lead + (2, SSM_STRIPS, SSM_GROUPS // SSM_STRIPS, SSM_STATE))
    s = jnp.swapaxes(s, -4, -3)
    return s.reshape(lead + (2 * SSM_GROUPS * SSM_STATE,))


def _lanes_to_state(v):
    lead = v.shape[:-1]
    s = v.reshape(lead + (SSM_STRIPS, 2, SSM_GROUPS // SSM_STRIPS, SSM_STATE))
    s = jnp.swapaxes(s, -4, -3)
    return s.reshape(lead + (2, SSM_GROUPS, SSM_STATE))


def _ssm_params(a_re, a_im, log_dt, b_re, b_im, c_re, c_im, dvec):
    G, N, C = SSM_GROUPS, SSM_STATE, SSM_GROUP
    lam_re = jnp.minimum(a_re, -1e-4)
    lam_im = a_im
    dt = jnp.exp(log_dt)[..., None]
    mag = jnp.exp(lam_re * dt)
    abar_re, abar_im = mag * jnp.cos(lam_im * dt), mag * jnp.sin(lam_im * dt)
    den = jnp.square(lam_re) + jnp.square(lam_im)
    p, qi = abar_re - 1.0, abar_im
    f_re = (p * lam_re + qi * lam_im) / den
    f_im = (qi * lam_re - p * lam_im) / den
    bbar_re = f_re[..., None] * b_re - f_im[..., None] * b_im
    bbar_im = f_re[..., None] * b_im + f_im[..., None] * b_re
    a = _state_to_lanes(jnp.stack([abar_re, abar_im], axis=1))
    a = jnp.broadcast_to(a[:, None, :], (2, SUBLANES, a.shape[-1]))
    S = SSM_STRIPS
    gs = G // S
    eye = jnp.eye(gs, dtype=F32)

    def bd_in(bb):
        bb = bb.reshape(2, S, gs, N, C)
        return jnp.einsum('dkgnc,gh->dkgchn', bb, eye).reshape(2, S, gs * C, gs * N)

    def bd_out(cc):
        cc = cc.reshape(2, S, gs, C, N)
        return jnp.einsum('dkgcn,gh->dkgnhc', cc, eye).reshape(2, S, gs * N, gs * C)

    wb = jnp.concatenate([bd_in(bbar_re), bd_in(bbar_im)], axis=-1).astype(BF16)
    wc = jnp.concatenate([bd_out(c_re), -bd_out(c_im)], axis=-2).astype(BF16)
    d = jnp.stack([dvec, jnp.zeros_like(dvec)], axis=0).reshape(2, 1, SSM_WIDTH)
    return {"a": a, "wb": wb, "wc": wc, "d": d}


def _layer_norm(x, g, b):
    mu = jnp.mean(x, axis=-1, keepdims=True)
    xc = x - mu
    var = jnp.mean(xc * xc, axis=-1, keepdims=True)
    return xc * lax.rsqrt(var + LN_EPS) * g + b


def _pack_rows(x):
    w = x.shape[1] // 2
    return pltpu.pack_elementwise([x[:, :w], x[:, w:]], packed_dtype=BF16)


def _unpack_rows(p):
    return (pltpu.unpack_elementwise(p, index=0, packed_dtype=BF16, unpacked_dtype=F32),
            pltpu.unpack_elementwise(p, index=1, packed_dtype=BF16, unpacked_dtype=F32))


def _gelu_tanh(x):
    return 0.5 * x * (1.0 + jnp.tanh(math.sqrt(2.0 / math.pi) * (x + 0.044715 * (x * x * x))))


def _merge_kernel(x_ref, ao_ref, yf_ref, yb_ref, ga_ref, gs_ref, mod_ref, wglu_ref, wa_ref, ws_ref, wo_ref,
                  lng_ref, lnb_ref, rwt_ref, rb_ref, run0_ref, x1_ref, h2_ref, idx_ref, rank_ref, wt_ref, cnt_ref,
                  run_ref, *, alpha):
    @pl.when((pl.program_id(0) == 0) & (pl.program_id(1) == 0))
    def _():
        run_ref[...] = run0_ref[...].astype(F32)

    nrow = x_ref.shape[0]
    sub = nrow // MERGE_SPLIT
    groups = [slice(j * sub, (j + 1) * sub) for j in range(MERGE_SPLIT)]
    dot = functools.partial(jnp.dot, preferred_element_type=F32)
    z = [_gelu_tanh(yf_ref[rs, :] + yb_ref[rs, :]) for rs in groups]
    attn_br = [dot(ao_ref[rs, :], wa_ref[...]) for rs in groups]
    gate = [jax.nn.sigmoid(dot(zg.astype(BF16), wglu_ref[...])) for zg in z]
    ssm_br = [dot((zg * gg).astype(BF16), ws_ref[...]) for zg, gg in zip(z, gate)]
    merged = [jax.nn.sigmoid(ga_ref[rs, :].astype(F32)) * ab + jax.nn.sigmoid(gs_ref[rs, :].astype(F32)) * sb
              for rs, ab, sb in zip(groups, attn_br, ssm_br)]
    mix = [dot(mg.astype(BF16), wo_ref[...]) for mg in merged]
    h2s = []
    for rs, mg in zip(groups, mix):
        x1 = _layer_norm(alpha * x_ref[rs, :] + mod_ref[2:3, :] * mg, lng_ref[...], lnb_ref[...])
        h2g = x1 * (1.0 + mod_ref[4:5, :]) + mod_ref[3:4, :]
        x1_ref[rs, :] = x1
        h2_ref[rs, :] = _pack_rows(h2g)
        h2s.append(h2g)
    h2 = jnp.concatenate(h2s, axis=0)

    logits = lax.dot_general(rwt_ref[...], h2, (((1,), (1,)), ((), ())), preferred_element_type=F32,
                             precision=lax.Precision.HIGHEST)
    score = jax.nn.sigmoid(logits)
    tm = score.shape[1]
    eidx = lax.broadcasted_iota(jnp.int32, score.shape, 0).astype(F32)
    work = score + rb_ref[...]
    picks, sel = [], []
    member = jnp.zeros_like(score)
    for _ in range(TOP_K):
        best = jnp.max(work, axis=0, keepdims=True)
        pick = jnp.min(jnp.where(work == best, eidx, float(N_EXPERTS)), axis=0, keepdims=True)
        hit = eidx == pick
        picks.append(pick)
        sel.append(jnp.sum(jnp.where(hit, score, 0.0), axis=0, keepdims=True))
        member = member + hit.astype(F32)
        work = jnp.where(hit, -jnp.inf, work)
    total = sel[0]
    for s in sel[1:]:
        total = total + s
    before = (lax.broadcasted_iota(jnp.int32, (tm, tm), 0) < lax.broadcasted_iota(jnp.int32, (tm, tm), 1))
    prefix = jnp.dot(member.astype(BF16), before.astype(BF16), preferred_element_type=F32)
    base = prefix + run_ref[...]
    for k in range(TOP_K):
        idx_ref[k:k + 1, :] = picks[k].astype(jnp.int32)
        rank_ref[k:k + 1, :] = jnp.sum(jnp.where(eidx == picks[k], base, 0.0), axis=0,
                                       keepdims=True).astype(jnp.int32)
        wt_ref[k:k + 1, :] = sel[k] / total * ROUTED_SCALE
    run_ref[...] = run_ref[...] + jnp.sum(member, axis=1, keepdims=True)
    cnt_ref[...] = run_ref[...].astype(jnp.int32)


def _merge(x, attn_o, y, ga, gs, mod, wts, run0, alpha):
    B, L, D = x.shape
    tm = min(MERGE_ROWS, L)
    per_batch = mod.shape[0] > 1
    row = lambda b, i: (b, i, 0)
    full = lambda b, i: (0, 0)
    wspec = lambda a: pl.BlockSpec(a.shape, full)
    return pl.pallas_call(
        functools.partial(_merge_kernel, alpha=alpha),
        grid=(B, L // tm),
        in_specs=[pl.BlockSpec((None, tm, D), row),
                  pl.BlockSpec((None, tm, ATTN_WIDTH), row),
                  pl.BlockSpec((None, None, tm, SSM_WIDTH), lambda b, i: (0, b, i, 0)),
                  pl.BlockSpec((None, None, tm, SSM_WIDTH), lambda b, i: (1, b, i, 0)),
                  pl.BlockSpec((None, tm, D), row),
                  pl.BlockSpec((None, tm, D), row),
                  pl.BlockSpec((None, N_MOD, D), (lambda b, i: (b, 0, 0)) if per_batch else (lambda b, i: (0, 0, 0))),
                  wspec(wts["w_glu"]), wspec(wts["w_attn_br"]), wspec(wts["w_ssm_br"]), wspec(wts["w_out"]),
                  wspec(wts["ln1_g"]), wspec(wts["ln1_b"]), wspec(wts["router_wt"]), wspec(wts["router_b"]),
                  wspec(run0)],
        out_specs=[pl.BlockSpec((None, tm, D), row),
                   pl.BlockSpec((None, tm, D // 2), row),
                   pl.BlockSpec((None, TOP_K, tm), lambda b, i: (b, 0, i)),
                   pl.BlockSpec((None, TOP_K, tm), lambda b, i: (b, 0, i)),
                   pl.BlockSpec((None, TOP_K, tm), lambda b, i: (b, 0, i)),
                   pl.BlockSpec((N_EXPERTS, 1), full)],
        out_shape=[jax.ShapeDtypeStruct((B, L, D), F32),
                   jax.ShapeDtypeStruct((B, L, D // 2), jnp.int32),
                   jax.ShapeDtypeStruct((B, TOP_K, L), jnp.int32),
                   jax.ShapeDtypeStruct((B, TOP_K, L), jnp.int32),
                   jax.ShapeDtypeStruct((B, TOP_K, L), F32),
                   jax.ShapeDtypeStruct((N_EXPERTS, 1), jnp.int32)],
        scratch_shapes=[pltpu.VMEM((N_EXPERTS, 1), F32)],
        compiler_params=_cparams(("arbitrary", "arbitrary")),
        name="merge_ln1_router",
    )(x, attn_o, y, y, ga, gs, mod, wts["w_glu"], wts["w_attn_br"], wts["w_ssm_br"], wts["w_out"],
      wts["ln1_g"], wts["ln1_b"], wts["router_wt"], wts["router_b"], run0)


def _dispatch(h2s, dests):
    W = h2s[0].shape[-1]
    n_slots = sum(d.size for d in dests)
    mesh = plsc.VectorSubcoreMesh(core_axis_name="core", subcore_axis_name="subcore")
    workers = mesh.num_cores * mesh.num_subcores
    tokens = [h.shape[0] * h.shape[1] for h in h2s]
    per_worker = [t // SC_WINDOW // workers for t in tokens]
    assert all(p * SC_WINDOW * workers == t for p, t in zip(per_worker, tokens))

    @pl.kernel(out_type=jax.ShapeDtypeStruct((n_slots, W), h2s[0].dtype), mesh=mesh,
               scratch_types=[pltpu.VMEM((TOP_K, SC_WINDOW), jnp.int32), pltpu.VMEM((SC_WINDOW, W), h2s[0].dtype)])
    def scatter(*refs):
        out_hbm, index_v, rows_v = refs[-3:]
        worker = lax.axis_index("core") * mesh.num_subcores + lax.axis_index("subcore")
        for g, per in enumerate(per_worker):
            rows_hbm, slots_hbm = refs[2 * g], refs[2 * g + 1]

            @pl.loop(0, per)
            def _(j, rows_hbm=rows_hbm, slots_hbm=slots_hbm, per=per):
                window = worker * per + j
                pltpu.sync_copy(rows_hbm.at[pl.ds(window * SC_WINDOW, SC_WINDOW)], rows_v)
                pltpu.sync_copy(slots_hbm.at[window], index_v)
                for k in range(TOP_K):
                    pltpu.sync_copy(rows_v, out_hbm.at[index_v.at[k]])

    args = []
    for h, d in zip(h2s, dests):
        nb, _, ln = d.shape
        slots = d.reshape(nb, TOP_K, ln // SC_WINDOW, SC_WINDOW).transpose(0, 2, 1, 3)
        args += [h.reshape(-1, W), slots.reshape(nb * ln // SC_WINDOW, TOP_K, SC_WINDOW)]
    return scatter(*args)


def _gmm_kernel(tile_ref, exp_ref, valid_ref, gstart_ref, xs_ref, wg_ref, wu_ref, wd_ref, ys_ref,
                wgb_ref, wub_ref, wdb_ref):
    w = pl.program_id(0)
    e = exp_ref[w]
    t = tile_ref[w]
    prev = jnp.maximum(w - 1, 0)
    new_expert = (w == 0) | (e != exp_ref[prev])
    first_visit = (w == 0) | (t != tile_ref[prev])

    @pl.when(new_expert)
    def _():
        wgb_ref[...] = wg_ref[...].astype(BF16)
        wub_ref[...] = wu_ref[...].astype(BF16)
        wdb_ref[...] = wd_ref[...].astype(BF16)

    @pl.when(first_visit)
    def _():
        ys_ref[...] = jnp.zeros_like(ys_ref)

    tg = xs_ref.shape[0]
    sub = tg // GMM_SPLIT
    lo = gstart_ref[e]
    hi = gstart_ref[e + 1]
    for j in range(GMM_SPLIT):
        first_row = t * tg + j * sub

        @pl.when((valid_ref[w] == 1) & (lo < first_row + sub) & (hi > first_row))
        def _(j=j, first_row=first_row):
            rs = slice(j * sub, (j + 1) * sub)
            dot = functools.partial(jnp.dot, preferred_element_type=F32)
            x = jnp.concatenate(_unpack_rows(xs_ref[rs, :]), axis=1).astype(BF16)
            g = dot(x, wgb_ref[...])
            u = dot(x, wub_ref[...])
            y = _pack_rows(dot((g * jax.nn.sigmoid(g) * u).astype(BF16), wdb_ref[...]))
            rows = first_row + lax.broadcasted_iota(jnp.int32, (sub, 1), 0)
            mine = (rows >= lo) & (rows < hi)
            ys_ref[rs, :] = jnp.where(mine, y, ys_ref[rs, :])


def _grouped_experts(xs, counts, w_gate, w_up, w_down):
    A, W = xs.shape
    D = w_gate.shape[-2]
    tg = GMM_ROWS
    nt = A // tg
    n_items = nt + N_EXPERTS - 1
    ff = w_gate.shape[-1]
    gend = jnp.cumsum(counts).astype(jnp.int32)
    gstart = jnp.concatenate([jnp.zeros((1,), jnp.int32), gend])
    first_row = jnp.arange(nt, dtype=jnp.int32) * tg
    count_le = lambda ends, v: jnp.sum((ends[None, :] <= v[:, None]).astype(jnp.int32), axis=1)
    e_lo = count_le(gend, first_row)
    e_hi = count_le(gend, first_row + tg - 1)
    per_tile = e_hi - e_lo + 1
    item_end = jnp.cumsum(per_tile).astype(jnp.int32)
    total = item_end[-1]
    wi = jnp.arange(n_items, dtype=jnp.int32)
    tile = jnp.minimum(count_le(item_end, wi), nt - 1)
    in_tile = (tile[:, None] == jnp.arange(nt, dtype=jnp.int32)[None, :]).astype(jnp.int32)
    lookup = lambda table: jnp.sum(in_tile * table[None, :], axis=1)
    expert = lookup(e_lo) + wi - lookup(item_end - per_tile)
    valid = (wi < total).astype(jnp.int32)
    expert = jnp.where(valid == 1, expert, e_hi[nt - 1])
    grid_spec = pltpu.PrefetchScalarGridSpec(
        num_scalar_prefetch=4,
        grid=(n_items,),
        in_specs=[pl.BlockSpec((tg, W), lambda w, tl, ex, va, gs: (tl[w], 0)),
                  pl.BlockSpec((None, D, ff), lambda w, tl, ex, va, gs: (ex[w], 0, 0)),
                  pl.BlockSpec((None, D, ff), lambda w, tl, ex, va, gs: (ex[w], 0, 0)),
                  pl.BlockSpec((None, ff, D), lambda w, tl, ex, va, gs: (ex[w], 0, 0))],
        out_specs=pl.BlockSpec((tg, W), lambda w, tl, ex, va, gs: (tl[w], 0)),
        scratch_shapes=[pltpu.VMEM((D, ff), BF16), pltpu.VMEM((D, ff), BF16), pltpu.VMEM((ff, D), BF16)],
    )
    return pl.pallas_call(
        _gmm_kernel,
        grid_spec=grid_spec,
        out_shape=jax.ShapeDtypeStruct((A, W), xs.dtype),
        compiler_params=_cparams(("arbitrary",)),
        name="moe_grouped_experts",
    )(tile, expert, valid, gstart, xs, w_gate, w_up, w_down)


def _gather_rows_sc(rows, index):
    n = index.shape[0]
    w = rows.shape[1]
    mesh = plsc.VectorSubcoreMesh(core_axis_name="core", subcore_axis_name="subcore")
    per_subcore = n // SC_WINDOW // (mesh.num_cores * mesh.num_subcores)
    assert per_subcore * SC_WINDOW * mesh.num_cores * mesh.num_subcores == n

    @pl.kernel(out_type=jax.ShapeDtypeStruct((n, w), rows.dtype), mesh=mesh,
               scratch_types=[pltpu.VMEM((1, SC_WINDOW), jnp.int32), pltpu.VMEM((SC_WINDOW, w), rows.dtype)])
    def gather(rows_hbm, index_hbm, out_hbm, index_v, rows_v):
        worker = lax.axis_index("core") * mesh.num_subcores + lax.axis_index("subcore")

        @pl.loop(0, per_subcore)
        def _(j):
            first = (worker * per_subcore + j) * SC_WINDOW
            pltpu.sync_copy(index_hbm.at[:, pl.ds(first, SC_WINDOW)], index_v)
            pltpu.sync_copy(rows_hbm.at[index_v.at[0]], rows_v)
            pltpu.sync_copy(rows_v, out_hbm.at[pl.ds(first, SC_WINDOW)])

    return gather(rows, index.reshape(1, n))


def _combine_rows_kernel(yg_ref, wt_ref, x1_ref, h2_ref, mod_ref, sg_ref, su_ref, sd_ref, lng_ref, lnb_ref, *rest,
                         alpha):
    o_ref = rest[-1]
    h = jnp.concatenate(_unpack_rows(h2_ref[...]), axis=1).astype(BF16)
    g = jnp.dot(h, sg_ref[...], preferred_element_type=F32)
    u = jnp.dot(h, su_ref[...], preferred_element_type=F32)
    moe = jnp.dot((g * jax.nn.sigmoid(g) * u).astype(BF16), sd_ref[...], preferred_element_type=F32)
    wt = wt_ref[...]
    lo = jnp.zeros(h2_ref.shape, F32)
    hi = lo
    for k in range(TOP_K):
        rl, rh = _unpack_rows(yg_ref[k])
        lo = lo + wt[:, k:k + 1] * rl
        hi = hi + wt[:, k:k + 1] * rh
    moe = moe + jnp.concatenate([lo, hi], axis=1)
    o_ref[...] = _layer_norm(alpha * x1_ref[...] + mod_ref[5:6, :] * moe, lng_ref[...], lnb_ref[...])


def _combine_rows(ys, dest, wts, x1, h2, mod, shared, alpha):
    B, L, D = x1.shape
    W = ys.shape[-1]
    tm = min(COMBINE_ROWS, L)
    per_batch = mod.shape[0] > 1
    pieces = max(1, B * L // COMBINE_PIECE_TOKENS)
    nb = B // pieces
    full = lambda b, i: (0, 0)
    wspec = lambda a: pl.BlockSpec(a.shape, full)
    wts_t = jnp.swapaxes(wts, 1, 2)
    out = None
    for p in range(pieces):
        b0 = p * nb
        row = lambda b, i, b0=b0: (b0 + b, i, 0)
        slots = jnp.swapaxes(dest[b0:b0 + nb], 0, 1).reshape(TOP_K * nb * L)
        yg = _gather_rows_sc(ys, slots).reshape(TOP_K, nb, L, W)
        in_specs = [pl.BlockSpec((TOP_K, None, tm, W), lambda b, i: (0, b, i, 0)),
                    pl.BlockSpec((None, tm, TOP_K), row),
                    pl.BlockSpec((None, tm, D), row),
                    pl.BlockSpec((None, tm, W), row),
                    pl.BlockSpec((None, N_MOD, D), (lambda b, i, b0=b0: (b0 + b, 0, 0)) if per_batch
                                 else (lambda b, i: (0, 0, 0))),
                    wspec(shared["sh_w_gate"]), wspec(shared["sh_w_up"]), wspec(shared["sh_w_down"]),
                    wspec(shared["ln2_g"]), wspec(shared["ln2_b"])]
        args = [yg, wts_t, x1, h2, mod, shared["sh_w_gate"], shared["sh_w_up"], shared["sh_w_down"],
                shared["ln2_g"], shared["ln2_b"]]
        aliases = {}
        if out is not None:
            in_specs.append(pl.BlockSpec(memory_space=pl.ANY))
            args.append(out)
            aliases = {len(args) - 1: 0}
        out = pl.pallas_call(
            functools.partial(_combine_rows_kernel, alpha=alpha),
            grid=(nb, L // tm),
            in_specs=in_specs,
            out_specs=pl.BlockSpec((None, tm, D), row),
            out_shape=jax.ShapeDtypeStruct((B, L, D), F32),
            input_output_aliases=aliases,
            compiler_params=_cparams(("parallel", "parallel")),
            name="moe_combine_ln2",
        )(*args)
    return out


def _slots(idx, rank, counts):
    start = jnp.cumsum(counts) - counts
    pick = idx[..., None] == jnp.arange(N_EXPERTS, dtype=jnp.int32)
    return jnp.sum(jnp.where(pick, start, 0), axis=-1).astype(jnp.int32) + rank


def _mixer_and_router(x, mod, wts, ssm, cache, run0, alpha):
    B, L, D = x.shape
    latent = cache is not None
    if latent:
        k_ctx, v_ctx, h0 = cache
        cos, sin = _rope_cos_sin(L)
    else:
        cos = jnp.zeros((L, LANES), F32)
        sin = cos
        h0 = jnp.zeros((2, B, 2 * SSM_GROUPS * SSM_STATE), F32)
    q, k, v, u, ga, gs = _in_projection(x, mod, wts["w_in"], cos, sin, rope=latent)
    if latent:
        attn_o = _attention_latent(q, k, v, k_ctx, v_ctx, wts["attn_sink"])
    else:
        attn_o = _attention_context(q, k, v, wts["attn_sink"])
    y, fs = _ssm_mix(u, ssm, h0)
    x1, h2, idx, rank, rw, counts = _merge(x, attn_o, y, ga, gs, mod, wts, run0, alpha)
    return {"x1": x1, "h2": h2, "idx": idx, "rank": rank, "rw": rw, "mod": mod}, counts, k, v, fs


def _moe(groups, counts, wts, alpha):
    counts = counts.reshape(N_EXPERTS)
    for g in groups:
        g["dest"] = _slots(g["idx"], g["rank"], counts)
    xs = _dispatch([g["h2"] for g in groups], [g["dest"] for g in groups])
    ys = _grouped_experts(xs, counts, wts["exp_w_gate"], wts["exp_w_up"], wts["exp_w_down"])
    return [_combine_rows(ys, g["dest"], g["rw"], g["x1"], g["h2"], g["mod"], wts, alpha) for g in groups]


def kernel(x_prompt, x_sample, cache_k, cache_v, state_ssm, c, c_ctx, mod_w, mod_b, w_in, attn_sink, w_attn_br, ssm_a_re, ssm_a_im, ssm_log_dt, ssm_b_re, ssm_b_im, ssm_c_re, ssm_c_im, ssm_d, w_glu, w_ssm_br, w_out, ln1_g, ln1_b, ln2_g, ln2_b, router_w, router_b, exp_w_gate, exp_w_up, exp_w_down, sh_w_gate, sh_w_up, sh_w_down):
    depth = w_in.shape[0]
    assert depth == 1
    alpha = (2.0 * depth) ** 0.25
    D = x_prompt.shape[-1]
    nb_p = x_prompt.shape[0]
    nb_s = x_sample.shape[0]
    l = 0

    ncond = 1 + nb_s
    npad = -ncond % SUBLANES
    cond = jnp.concatenate([c_ctx[None, :], c, jnp.zeros((npad, D), F32)], axis=0)
    mod = _modulation(cond, mod_w[l], mod_b[l]).reshape(ncond + npad, N_MOD, D)
    mod_ctx, mod_lat = mod[0:1], mod[1:ncond]

    o1 = ATTN_WIDTH
    o3 = o1 + 2 * KV_WIDTH
    o4 = o3 + SSM_WIDTH
    wi = w_in[l]
    scale = HEAD_DIM ** -0.5 * LOG2_E
    wts = {
        "w_in": jnp.concatenate([wi[:, :o1] * scale, wi[:, o1:]], axis=1).astype(BF16),
        "attn_sink": attn_sink[l],
        "w_glu": w_glu[l].astype(BF16), "w_attn_br": w_attn_br[l].astype(BF16),
        "w_ssm_br": w_ssm_br[l].astype(BF16), "w_out": w_out[l].astype(BF16),
        "ln1_g": ln1_g[l].reshape(1, D), "ln1_b": ln1_b[l].reshape(1, D),
        "ln2_g": ln2_g[l].reshape(1, D), "ln2_b": ln2_b[l].reshape(1, D),
        "router_wt": router_w[l].T, "router_b": router_b[l].reshape(N_EXPERTS, 1),
        "exp_w_gate": exp_w_gate[l], "exp_w_up": exp_w_up[l], "exp_w_down": exp_w_down[l],
        "sh_w_gate": sh_w_gate[l].astype(BF16), "sh_w_up": sh_w_up[l].astype(BF16),
        "sh_w_down": sh_w_down[l].astype(BF16),
    }
    ssm = _ssm_params(ssm_a_re[l], ssm_a_im[l], ssm_log_dt[l], ssm_b_re[l], ssm_b_im[l],
                      ssm_c_re[l], ssm_c_im[l], ssm_d[l])

    no_tokens_yet = jnp.zeros((N_EXPERTS, 1), jnp.int32)
    grp_p, counts, k_p, v_p, fs_p = _mixer_and_router(x_prompt, mod_ctx, wts, ssm, None, no_tokens_yet, alpha)

    past = cache_k.shape[2]
    k_ctx = cache_k[:, l].reshape(nb_s, past, KV_WIDTH)
    v_ctx = cache_v[:, l].reshape(nb_s, past, KV_WIDTH)
    h0 = jnp.swapaxes(_state_to_lanes(state_ssm[:, l]), 0, 1)
    grp_s, counts, _, _, _ = _mixer_and_router(x_sample, mod_lat, wts, ssm, (k_ctx, v_ctx, h0), counts, alpha)
    yp, ys_ = _moe([grp_p, grp_s], counts, wts, alpha)

    Lp = x_prompt.shape[1]
    new_k = k_p.reshape(nb_p, 1, Lp, N_KV_HEADS, HEAD_DIM)
    new_v = v_p.reshape(nb_p, 1, Lp, N_KV_HEADS, HEAD_DIM)
    new_s = _lanes_to_state(jnp.swapaxes(fs_p, 0, 1))[:, None]
    return (yp, ys_, new_k, new_v, new_s)
```

```python
import functools
import math

import jax
import jax.numpy as jnp
from jax import lax
from jax.experimental import pallas as pl
from jax.experimental.pallas import tpu as pltpu
from jax.experimental.pallas import tpu_sc as plsc

F32 = jnp.float32
BF16 = jnp.bfloat16

GRID_W = 64
HEAD_DIM = 64
N_Q_HEADS = 8
N_KV_HEADS = 2
Q_PER_KV = N_Q_HEADS // N_KV_HEADS
ATTN_WIDTH = N_Q_HEADS * HEAD_DIM
KV_WIDTH = N_KV_HEADS * HEAD_DIM
WINDOW = 128
ROPE_BASE = 10000.0
SSM_WIDTH = 512
SSM_GROUP = 16
SSM_GROUPS = SSM_WIDTH // SSM_GROUP
SSM_STATE = 64
N_EXPERTS = 64
TOP_K = 6
ROUTED_SCALE = 2.5
N_MOD = 6
LN_EPS = 1e-5
LOG2_E = math.log2(math.e)

SUBLANES = 8
LANES = 128
VMEM_LIMIT = 48 * 1024 * 1024

INPROJ_ROWS = 1024
ATTN_QBLOCKS = 4
SSM_STEPS = 128
SSM_STRIPS = 4
MERGE_ROWS = 512
MERGE_SPLIT = 2
GMM_ROWS = 2048
GMM_SPLIT = 4
COMBINE_ROWS = 256
COMBINE_PIECE_TOKENS = 8192
SC_WINDOW = 128


def _cparams(sem):
    return pltpu.CompilerParams(dimension_semantics=sem, vmem_limit_bytes=VMEM_LIMIT)


def _mod_kernel(c_ref, w_ref, b_ref, o_ref):
    c = c_ref[...]
    s = c * jax.nn.sigmoid(c)
    o_ref[...] = jnp.dot(s, w_ref[...], preferred_element_type=F32,
                         precision=lax.Precision.HIGHEST) + b_ref[...]


def _modulation(cond, w, b):
    n, d = cond.shape
    nout = w.shape[1]
    tn = 512
    return pl.pallas_call(
        _mod_kernel,
        grid=(nout // tn,),
        in_specs=[pl.BlockSpec((n, d), lambda j: (0, 0)),
                  pl.BlockSpec((d, tn), lambda j: (0, j)),
                  pl.BlockSpec((1, tn), lambda j: (0, j))],
        out_specs=pl.BlockSpec((n, tn), lambda j: (0, j)),
        out_shape=jax.ShapeDtypeStruct((n, nout), F32),
        compiler_params=_cparams(("arbitrary",)),
        name="modulation",
    )(cond, w, b.reshape(1, nout))


def _rope_rotate(t, cos, sin):
    lane = lax.broadcasted_iota(jnp.int32, t.shape, 1)
    partner = jnp.where((lane % 32) < 16, pltpu.roll(t, LANES - 16, 1), pltpu.roll(t, 16, 1))
    return t * cos + partner * sin


def _inproj_kernel(x_ref, mod_ref, w_ref, cos_ref, sin_ref, q_ref, k_ref, v_ref, u_ref, ga_ref, gs_ref, *, rope):
    x = x_ref[...]
    h = (x * (1.0 + mod_ref[1:2, :]) + mod_ref[0:1, :]).astype(BF16)
    p = jnp.dot(h, w_ref[...], preferred_element_type=F32)
    o1 = ATTN_WIDTH
    o2 = o1 + KV_WIDTH
    o3 = o2 + KV_WIDTH
    d = x.shape[1]
    q = p[:, :o1]
    k = p[:, o1:o2]
    if rope:
        cos = cos_ref[...]
        sin = sin_ref[...]
        q = jnp.concatenate([_rope_rotate(q[:, j * LANES:(j + 1) * LANES], cos, sin)
                             for j in range(o1 // LANES)], axis=1)
        k = _rope_rotate(k, cos, sin)
    o4 = o3 + SSM_WIDTH
    q_ref[...] = q.astype(BF16)
    k_ref[...] = k
    v_ref[...] = p[:, o2:o3]
    u_ref[...] = p[:, o3:o4]
    ga_ref[...] = p[:, o4:o4 + d].astype(BF16)
    gs_ref[...] = p[:, o4 + d:].astype(BF16)


def _in_projection(x, mod, w, cos, sin, rope):
    B, L, D = x.shape
    tm = min(INPROJ_ROWS, L)
    per_batch = mod.shape[0] > 1
    nw = w.shape[1]
    row = lambda b, i: (b, i, 0)
    outs = pl.pallas_call(
        functools.partial(_inproj_kernel, rope=rope),
        grid=(B, L // tm),
        in_specs=[pl.BlockSpec((None, tm, D), row),
                  pl.BlockSpec((None, N_MOD, D), (lambda b, i: (b, 0, 0)) if per_batch else (lambda b, i: (0, 0, 0))),
                  pl.BlockSpec((D, nw), lambda b, i: (0, 0)),
                  pl.BlockSpec((tm, LANES), lambda b, i: (i, 0)),
                  pl.BlockSpec((tm, LANES), lambda b, i: (i, 0))],
        out_specs=[pl.BlockSpec((None, tm, ATTN_WIDTH), row),
                   pl.BlockSpec((None, tm, KV_WIDTH), row),
                   pl.BlockSpec((None, tm, KV_WIDTH), row),
                   pl.BlockSpec((None, tm, SSM_WIDTH), row),
                   pl.BlockSpec((None, tm, D), row),
                   pl.BlockSpec((None, tm, D), row)],
        out_shape=[jax.ShapeDtypeStruct((B, L, ATTN_WIDTH), BF16),
                   jax.ShapeDtypeStruct((B, L, KV_WIDTH), F32),
                   jax.ShapeDtypeStruct((B, L, KV_WIDTH), F32),
                   jax.ShapeDtypeStruct((B, L, SSM_WIDTH), F32),
                   jax.ShapeDtypeStruct((B, L, D), BF16),
                   jax.ShapeDtypeStruct((B, L, D), BF16)],
        compiler_params=_cparams(("parallel", "parallel")),
        name="in_projection",
    )(x, mod, w, cos, sin)
    return outs


def _rope_cos_sin(n_tokens):
    t = jnp.arange(n_tokens, dtype=jnp.int32)
    pos = jnp.stack([t // GRID_W, t % GRID_W], axis=-1).astype(F32)
    n_freq = HEAD_DIM // 4
    inv_freq = ROPE_BASE ** (-jnp.arange(n_freq, dtype=F32) / n_freq)
    ang = pos[:, :, None] * inv_freq
    c, s = jnp.cos(ang), jnp.sin(ang)
    cos = jnp.concatenate([c[:, 0], c[:, 0], c[:, 1], c[:, 1]], axis=-1)
    sin = jnp.concatenate([-s[:, 0], s[:, 0], -s[:, 1], s[:, 1]], axis=-1)
    return jnp.tile(cos, (1, LANES // HEAD_DIM)), jnp.tile(sin, (1, LANES // HEAD_DIM))


def _attend(sink_ref, q, kcat, vcat, mask, o_ref, row0=0):
    lq = q.shape[0]
    lane = lax.broadcasted_iota(jnp.int32, (1, LANES), 1)
    low = lane < HEAD_DIM
    k_sw = pltpu.roll(kcat, HEAD_DIM, 1)
    v_sw = pltpu.roll(vcat, HEAD_DIM, 1)
    neg = jnp.finfo(F32).min
    scores, vds = [], []
    for h in range(N_KV_HEADS):
        keep = low if h == 0 else jnp.logical_not(low)
        kd = jnp.where(keep, kcat, k_sw).astype(BF16)
        vds.append(jnp.where(keep, vcat, v_sw).astype(BF16))
        qs = []
        for j in range(Q_PER_KV):
            head = h * Q_PER_KV + j
            blk = q[:, (head // 2) * LANES:(head // 2 + 1) * LANES]
            sel = low if head % 2 == 0 else jnp.logical_not(low)
            qs.append(jnp.where(sel, blk, jnp.zeros_like(blk)))
        qstack = jnp.concatenate(qs, axis=0)
        scores.append(lax.dot_general(qstack, kd, (((1,), (1,)), ((), ())), preferred_element_type=F32))
    probs, denoms = [], []
    for h in range(N_KV_HEADS):
        s = scores[h]
        ps, ls = [], []
        for j in range(Q_PER_KV):
            sj = s[j * lq:(j + 1) * lq]
            if mask is not None:
                lm = mask.shape[1]
                sj = jnp.concatenate([jnp.where(mask, sj[:, :lm], neg), sj[:, lm:]], axis=1)
            sink = sink_ref[h * Q_PER_KV + j] * LOG2_E
            m = jnp.maximum(jnp.max(sj, axis=1, keepdims=True), sink)
            pj = jnp.exp2(sj - m)
            ls.append(jnp.sum(pj, axis=1, keepdims=True) + jnp.exp2(sink - m))
            ps.append(pj.astype(BF16))
        probs.append(jnp.concatenate(ps, axis=0))
        denoms.append(ls)
    for h in range(N_KV_HEADS):
        ls = denoms[h]
        o = jnp.dot(probs[h], vds[h], preferred_element_type=F32)
        for jj in range(Q_PER_KV // 2):
            oe = o[(2 * jj) * lq:(2 * jj + 1) * lq] / ls[2 * jj]
            oo = o[(2 * jj + 1) * lq:(2 * jj + 2) * lq] / ls[2 * jj + 1]
            cb = (h * Q_PER_KV) // 2 + jj
            o_ref[row0:row0 + lq, cb * LANES:(cb + 1) * LANES] = jnp.where(low, oe, oo).astype(o_ref.dtype)


def _attn_ctx_kernel(sink_ref, q_ref, k_ref, v_ref, o_ref):
    _attend(sink_ref, q_ref[...], k_ref[...], v_ref[...], None, o_ref)


def _attn_lat_kernel(sink_ref, q_ref, kp_ref, kc_ref, kn_ref, vp_ref, vc_ref, vn_ref, kx_ref, vx_ref, o_ref):
    n = pl.program_id(1)
    nb = pl.num_programs(1)
    blk = kp_ref.shape[0]
    nq = q_ref.shape[0] // blk
    klocal = jnp.concatenate([kp_ref[...], kc_ref[...], kn_ref[...]], axis=0)
    vlocal = jnp.concatenate([vp_ref[...], vc_ref[...], vn_ref[...]], axis=0)
    qi = lax.broadcasted_iota(jnp.int32, (blk, 3 * blk), 0)
    kj = lax.broadcasted_iota(jnp.int32, (blk, 3 * blk), 1)
    rel = kj - blk - qi
    in_band = (rel <= WINDOW) & (rel >= -WINDOW)
    for j in range(nq):
        mask = in_band
        if j == 0:
            mask = mask & ((kj >= blk) | (n > 0))
        if j == nq - 1:
            mask = mask & ((kj < 2 * blk) | (n < nb - 1))
        kcat = jnp.concatenate([klocal[j * blk:(j + 3) * blk], kx_ref[...]], axis=0)
        vcat = jnp.concatenate([vlocal[j * blk:(j + 3) * blk], vx_ref[...]], axis=0)
        _attend(sink_ref, q_ref[j * blk:(j + 1) * blk, :], kcat, vcat, mask, o_ref, j * blk)


def _attention_context(q, k, v, sink):
    B, L, _ = q.shape
    row = lambda b: (b, 0, 0)
    return pl.pallas_call(
        _attn_ctx_kernel,
        grid=(B,),
        in_specs=[pl.BlockSpec(memory_space=pltpu.SMEM),
                  pl.BlockSpec((None, L, ATTN_WIDTH), row),
                  pl.BlockSpec((None, L, KV_WIDTH), row),
                  pl.BlockSpec((None, L, KV_WIDTH), row)],
        out_specs=pl.BlockSpec((None, L, ATTN_WIDTH), row),
        out_shape=jax.ShapeDtypeStruct((B, L, ATTN_WIDTH), BF16),
        compiler_params=_cparams(("parallel",)),
        name="attention_context",
    )(sink, q, k, v)


def _attention_latent(q, k, v, k_ctx, v_ctx, sink):
    B, S, _ = q.shape
    blk = WINDOW
    nq = ATTN_QBLOCKS
    nb = S // blk
    nctx = k_ctx.shape[1]
    cur = lambda b, n: (b, n, 0)
    prv = lambda b, n: (b, jnp.maximum(n * nq - 1, 0), 0)
    nxt = lambda b, n: (b, jnp.minimum(n * nq + nq, nb - 1), 0)
    ctx = lambda b, n: (b, 0, 0)
    edge = lambda im: pl.BlockSpec((None, blk, KV_WIDTH), im)
    mid = pl.BlockSpec((None, nq * blk, KV_WIDTH), cur)
    return pl.pallas_call(
        _attn_lat_kernel,
        grid=(B, nb // nq),
        in_specs=[pl.BlockSpec(memory_space=pltpu.SMEM),
                  pl.BlockSpec((None, nq * blk, ATTN_WIDTH), cur),
                  edge(prv), mid, edge(nxt), edge(prv), mid, edge(nxt),
                  pl.BlockSpec((None, nctx, KV_WIDTH), ctx),
                  pl.BlockSpec((None, nctx, KV_WIDTH), ctx)],
        out_specs=pl.BlockSpec((None, nq * blk, ATTN_WIDTH), cur),
        out_shape=jax.ShapeDtypeStruct((B, S, ATTN_WIDTH), BF16),
        compiler_params=_cparams(("parallel", "parallel")),
        name="attention_latent",
    )(sink, q, k, k, k, v, v, v, k_ctx, v_ctx)


def _ssm_kernel(u_ref, wb_ref, wc_ref, a_ref, d_ref, h0_ref, y_ref, fs_ref, ut_ref, yt_ref, st_ref, *bu_refs):
    rev = pl.program_id(0)
    i = pl.program_id(2)
    nc = pl.num_programs(2)
    nseq, tm, _ = u_ref.shape
    sw = a_ref.shape[-1] // SSM_STRIPS
    nre = sw // 2

    @pl.when(i == 0)
    def _():
        st_ref[...] = h0_ref[...]

    for b in range(nseq):
        ub = u_ref[b]
        for s in range(SSM_STRIPS):
            ut_ref[s, pl.ds(b, tm, stride=nseq), :] = ub[:, s * LANES:(s + 1) * LANES]
    for s in range(SSM_STRIPS):
        bu_refs[s][...] = jnp.dot(ut_ref[s].astype(BF16), wb_ref[s], preferred_element_type=F32)

    for s in range(SSM_STRIPS):
        bu_ref = bu_refs[s]
        a_re = a_ref[:, s * sw:s * sw + nre]
        a_im = a_ref[:, s * sw + nre:(s + 1) * sw]

        def step(t, carry):
            xr, xi = carry
            r = pl.multiple_of((t + rev * (tm - 1 - 2 * t)) * nseq, nseq)
            nr = a_re * xr - a_im * xi + bu_ref[pl.ds(r, nseq), 0:nre]
            ni = a_re * xi + a_im * xr + bu_ref[pl.ds(r, nseq), nre:sw]
            bu_ref[pl.ds(r, nseq), 0:nre] = nr
            bu_ref[pl.ds(r, nseq), nre:sw] = ni
            return nr, ni

        xr, xi = lax.fori_loop(0, tm, step, (st_ref[:, s * sw:s * sw + nre], st_ref[:, s * sw + nre:(s + 1) * sw]),
                               unroll=True)
        st_ref[:, s * sw:s * sw + nre] = xr
        st_ref[:, s * sw + nre:(s + 1) * sw] = xi
        yt_ref[s] = jnp.dot(bu_ref[...].astype(BF16), wc_ref[s], preferred_element_type=F32)

    for b in range(nseq):
        yb = jnp.concatenate([yt_ref[s, pl.ds(b, tm, stride=nseq), :] for s in range(SSM_STRIPS)], axis=1)
        y_ref[b] = yb + u_ref[b] * d_ref[...]

    @pl.when(i == nc - 1)
    def _():
        fs_ref[...] = st_ref[...]


def _ssm_mix(u, ssm, h0):
    B, L, _ = u.shape
    nseq = SUBLANES
    tm = min(SSM_STEPS, L)
    nc = L // tm
    rows = nseq * tm
    ns = 2 * SSM_GROUPS * SSM_STATE
    chunk = lambda d, g, i: i + d * (nc - 1 - 2 * i)
    y, fs = pl.pallas_call(
        _ssm_kernel,
        grid=(2, B // nseq, nc),
        in_specs=[pl.BlockSpec((nseq, tm, SSM_WIDTH), lambda d, g, i: (g, chunk(d, g, i), 0)),
                  pl.BlockSpec((None, SSM_STRIPS, LANES, ns // SSM_STRIPS), lambda d, g, i: (d, 0, 0, 0)),
                  pl.BlockSpec((None, SSM_STRIPS, ns // SSM_STRIPS, LANES), lambda d, g, i: (d, 0, 0, 0)),
                  pl.BlockSpec((None, nseq, ns), lambda d, g, i: (d, 0, 0)),
                  pl.BlockSpec((None, 1, SSM_WIDTH), lambda d, g, i: (d, 0, 0)),
                  pl.BlockSpec((None, nseq, ns), lambda d, g, i: (d, g, 0))],
        out_specs=[pl.BlockSpec((None, nseq, tm, SSM_WIDTH), lambda d, g, i: (d, g, chunk(d, g, i), 0)),
                   pl.BlockSpec((None, nseq, ns), lambda d, g, i: (d, g, 0))],
        out_shape=[jax.ShapeDtypeStruct((2, B, L, SSM_WIDTH), F32),
                   jax.ShapeDtypeStruct((2, B, ns), F32)],
        scratch_shapes=[pltpu.VMEM((SSM_STRIPS, rows, LANES), F32), pltpu.VMEM((SSM_STRIPS, rows, LANES), F32),
                        pltpu.VMEM((nseq, ns), F32)]
        + [pltpu.VMEM((rows, ns // SSM_STRIPS), F32) for _ in range(SSM_STRIPS)],
        compiler_params=_cparams(("arbitrary", "arbitrary", "arbitrary")),
        name="ssm_scan",
    )(u, ssm["wb"], ssm["wc"], ssm["a"], ssm["d"], h0)
    return y, fs


def _state_to_lanes(s):
    lead = s.shape[:-3]
    s = s.reshape(lead + (2, SSM_STRIPS, SSM_GROUPS // SSM_STRIPS, SSM_STATE))
    s = jnp.swapaxes(s, -4, -3)
    return s.reshape(lead + (2 * SSM_GROUPS * SSM_STATE,))


def _lanes_to_state(v):
    lead = v.shape[:-1]
    s = v.reshape(lead + (SSM_STRIPS, 2, SSM_GROUPS // SSM_STRIPS, SSM_STATE))
    s = jnp.swapaxes(s, -4, -3)
    return s.reshape(lead + (2, SSM_GROUPS, SSM_STATE))


def _ssm_params(a_re, a_im, log_dt, b_re, b_im, c_re, c_im, dvec):
    G, N, C = SSM_GROUPS, SSM_STATE, SSM_GROUP
    lam_re = jnp.minimum(a_re, -1e-4)
    lam_im = a_im
    dt = jnp.exp(log_dt)[..., None]
    mag = jnp.exp(lam_re * dt)
    abar_re, abar_im = mag * jnp.cos(lam_im * dt), mag * jnp.sin(lam_im * dt)
    den = jnp.square(lam_re) + jnp.square(lam_im)
    p, qi = abar_re - 1.0, abar_im
    f_re = (p * lam_re + qi * lam_im) / den
    f_im = (qi * lam_re - p * lam_im) / den
    bbar_re = f_re[..., None] * b_re - f_im[..., None] * b_im
    bbar_im = f_re[..., None] * b_im + f_im[..., None] * b_re
    a = _state_to_lanes(jnp.stack([abar_re, abar_im], axis=1))
    a = jnp.broadcast_to(a[:, None, :], (2, SUBLANES, a.shape[-1]))
    S = SSM_STRIPS
    gs = G // S
    eye = jnp.eye(gs, dtype=F32)

    def bd_in(bb):
        bb = bb.reshape(2, S, gs, N, C)
        return jnp.einsum('dkgnc,gh->dkgchn', bb, eye).reshape(2, S, gs * C, gs * N)

    def bd_out(cc):
        cc = cc.reshape(2, S, gs, C, N)
        return jnp.einsum('dkgcn,gh->dkgnhc', cc, eye).reshape(2, S, gs * N, gs * C)

    wb = jnp.concatenate([bd_in(bbar_re), bd_in(bbar_im)], axis=-1).astype(BF16)
    wc = jnp.concatenate([bd_out(c_re), -bd_out(c_im)], axis=-2).astype(BF16)
    d = jnp.stack([dvec, jnp.zeros_like(dvec)], axis=0).reshape(2, 1, SSM_WIDTH)
    return {"a": a, "wb": wb, "wc": wc, "d": d}


def _layer_norm(x, g, b):
    mu = jnp.mean(x, axis=-1, keepdims=True)
    xc = x - mu
    var = jnp.mean(xc * xc, axis=-1, keepdims=True)
    return xc * lax.rsqrt(var + LN_EPS) * g + b


def _pack_rows(x):
    w = x.shape[1] // 2
    return pltpu.pack_elementwise([x[:, :w], x[:, w:]], packed_dtype=BF16)


def _unpack_rows(p):
    return (pltpu.unpack_elementwise(p, index=0, packed_dtype=BF16, unpacked_dtype=F32),
            pltpu.unpack_elementwise(p, index=1, packed_dtype=BF16, unpacked_dtype=F32))


def _gelu_tanh(x):
    return 0.5 * x * (1.0 + jnp.tanh(math.sqrt(2.0 / math.pi) * (x + 0.044715 * (x * x * x))))


def _merge_kernel(x_ref, ao_ref, yf_ref, yb_ref, ga_ref, gs_ref, mod_ref, wglu_ref, wa_ref, ws_ref, wo_ref,
                  lng_ref, lnb_ref, rwt_ref, rb_ref, run0_ref, x1_ref, h2_ref, idx_ref, rank_ref, wt_ref, cnt_ref,
                  run_ref, *, alpha):
    @pl.when((pl.program_id(0) == 0) & (pl.program_id(1) == 0))
    def _():
        run_ref[...] = run0_ref[...].astype(F32)

    nrow = x_ref.shape[0]
    sub = nrow // MERGE_SPLIT
    groups = [slice(j * sub, (j + 1) * sub) for j in range(MERGE_SPLIT)]
    dot = functools.partial(jnp.dot, preferred_element_type=F32)
    z = [_gelu_tanh(yf_ref[rs, :] + yb_ref[rs, :]) for rs in groups]
    attn_br = [dot(ao_ref[rs, :], wa_ref[...]) for rs in groups]
    gate = [jax.nn.sigmoid(dot(zg.astype(BF16), wglu_ref[...])) for zg in z]
    ssm_br = [dot((zg * gg).astype(BF16), ws_ref[...]) for zg, gg in zip(z, gate)]
    merged = [jax.nn.sigmoid(ga_ref[rs, :].astype(F32)) * ab + jax.nn.sigmoid(gs_ref[rs, :].astype(F32)) * sb
              for rs, ab, sb in zip(groups, attn_br, ssm_br)]
    mix = [dot(mg.astype(BF16), wo_ref[...]) for mg in merged]
    h2s = []
    for rs, mg in zip(groups, mix):
        x1 = _layer_norm(alpha * x_ref[rs, :] + mod_ref[2:3, :] * mg, lng_ref[...], lnb_ref[...])
        h2g = x1 * (1.0 + mod_ref[4:5, :]) + mod_ref[3:4, :]
        x1_ref[rs, :] = x1
        h2_ref[rs, :] = _pack_rows(h2g)
        h2s.append(h2g)
    h2 = jnp.concatenate(h2s, axis=0)

    logits = lax.dot_general(rwt_ref[...], h2, (((1,), (1,)), ((), ())), preferred_element_type=F32,
                             precision=lax.Precision.HIGHEST)
    score = jax.nn.sigmoid(logits)
    tm = score.shape[1]
    eidx = lax.broadcasted_iota(jnp.int32, score.shape, 0).astype(F32)
    work = score + rb_ref[...]
    picks, sel = [], []
    member = jnp.zeros_like(score)
    for _ in range(TOP_K):
        best = jnp.max(work, axis=0, keepdims=True)
        pick = jnp.min(jnp.where(work == best, eidx, float(N_EXPERTS)), axis=0, keepdims=True)
        hit = eidx == pick
        picks.append(pick)
        sel.append(jnp.sum(jnp.where(hit, score, 0.0), axis=0, keepdims=True))
        member = member + hit.astype(F32)
        work = jnp.where(hit, -jnp.inf, work)
    total = sel[0]
    for s in sel[1:]:
        total = total + s
    before = (lax.broadcasted_iota(jnp.int32, (tm, tm), 0) < lax.broadcasted_iota(jnp.int32, (tm, tm), 1))
    prefix = jnp.dot(member.astype(BF16), before.astype(BF16), preferred_element_type=F32)
    base = prefix + run_ref[...]
    for k in range(TOP_K):
        idx_ref[k:k + 1, :] = picks[k].astype(jnp.int32)
        rank_ref[k:k + 1, :] = jnp.sum(jnp.where(eidx == picks[k], base, 0.0), axis=0,
                                       keepdims=True).astype(jnp.int32)
        wt_ref[k:k + 1, :] = sel[k] / total * ROUTED_SCALE
    run_ref[...] = run_ref[...] + jnp.sum(member, axis=1, keepdims=True)
    cnt_ref[...] = run_ref[...].astype(jnp.int32)


def _merge(x, attn_o, y, ga, gs, mod, wts, run0, alpha):
    B, L, D = x.shape
    tm = min(MERGE_ROWS, L)
    per_batch = mod.shape[0] > 1
    row = lambda b, i: (b, i, 0)
    full = lambda b, i: (0, 0)
    wspec = lambda a: pl.BlockSpec(a.shape, full)
    return pl.pallas_call(
        functools.partial(_merge_kernel, alpha=alpha),
        grid=(B, L // tm),
        in_specs=[pl.BlockSpec((None, tm, D), row),
                  pl.BlockSpec((None, tm, ATTN_WIDTH), row),
                  pl.BlockSpec((None, None, tm, SSM_WIDTH), lambda b, i: (0, b, i, 0)),
                  pl.BlockSpec((None, None, tm, SSM_WIDTH), lambda b, i: (1, b, i, 0)),
                  pl.BlockSpec((None, tm, D), row),
                  pl.BlockSpec((None, tm, D), row),
                  pl.BlockSpec((None, N_MOD, D), (lambda b, i: (b, 0, 0)) if per_batch else (lambda b, i: (0, 0, 0))),
                  wspec(wts["w_glu"]), wspec(wts["w_attn_br"]), wspec(wts["w_ssm_br"]), wspec(wts["w_out"]),
                  wspec(wts["ln1_g"]), wspec(wts["ln1_b"]), wspec(wts["router_wt"]), wspec(wts["router_b"]),
                  wspec(run0)],
        out_specs=[pl.BlockSpec((None, tm, D), row),
                   pl.BlockSpec((None, tm, D // 2), row),
                   pl.BlockSpec((None, TOP_K, tm), lambda b, i: (b, 0, i)),
                   pl.BlockSpec((None, TOP_K, tm), lambda b, i: (b, 0, i)),
                   pl.BlockSpec((None, TOP_K, tm), lambda b, i: (b, 0, i)),
                   pl.BlockSpec((N_EXPERTS, 1), full)],
        out_shape=[jax.ShapeDtypeStruct((B, L, D), F32),
                   jax.ShapeDtypeStruct((B, L, D // 2), jnp.int32),
                   jax.ShapeDtypeStruct((B, TOP_K, L), jnp.int32),
                   jax.ShapeDtypeStruct((B, TOP_K, L), jnp.int32),
                   jax.ShapeDtypeStruct((B, TOP_K, L), F32),
                   jax.ShapeDtypeStruct((N_EXPERTS, 1), jnp.int32)],
        scratch_shapes=[pltpu.VMEM((N_EXPERTS, 1), F32)],
        compiler_params=_cparams(("arbitrary", "arbitrary")),
        name="merge_ln1_router",
    )(x, attn_o, y, y, ga, gs, mod, wts["w_glu"], wts["w_attn_br"], wts["w_ssm_br"], wts["w_out"],
      wts["ln1_g"], wts["ln1_b"], wts["router_wt"], wts["router_b"], run0)


def _dispatch(h2s, dests):
    W = h2s[0].shape[-1]
    n_slots = sum(d.size for d in dests)
    mesh = plsc.VectorSubcoreMesh(core_axis_name="core", subcore_axis_name="subcore")
    workers = mesh.num_cores * mesh.num_subcores
    tokens = [h.shape[0] * h.shape[1] for h in h2s]
    per_worker = [t // SC_WINDOW // workers for t in tokens]
    assert all(p * SC_WINDOW * workers == t for p, t in zip(per_worker, tokens))

    @pl.kernel(out_type=jax.ShapeDtypeStruct((n_slots, W), h2s[0].dtype), mesh=mesh,
               scratch_types=[pltpu.VMEM((TOP_K, SC_WINDOW), jnp.int32), pltpu.VMEM((SC_WINDOW, W), h2s[0].dtype)])
    def scatter(*refs):
        out_hbm, index_v, rows_v = refs[-3:]
        worker = lax.axis_index("core") * mesh.num_subcores + lax.axis_index("subcore")
        for g, per in enumerate(per_worker):
            rows_hbm, slots_hbm = refs[2 * g], refs[2 * g + 1]

            @pl.loop(0, per)
            def _(j, rows_hbm=rows_hbm, slots_hbm=slots_hbm, per=per):
                window = worker * per + j
                pltpu.sync_copy(rows_hbm.at[pl.ds(window * SC_WINDOW, SC_WINDOW)], rows_v)
                pltpu.sync_copy(slots_hbm.at[window], index_v)
                for k in range(TOP_K):
                    pltpu.sync_copy(rows_v, out_hbm.at[index_v.at[k]])

    args = []
    for h, d in zip(h2s, dests):
        nb, _, ln = d.shape
        slots = d.reshape(nb, TOP_K, ln // SC_WINDOW, SC_WINDOW).transpose(0, 2, 1, 3)
        args += [h.reshape(-1, W), slots.reshape(nb * ln // SC_WINDOW, TOP_K, SC_WINDOW)]
    return scatter(*args)


def _gmm_kernel(tile_ref, exp_ref, valid_ref, gstart_ref, xs_ref, wg_ref, wu_ref, wd_ref, ys_ref,
                wgb_ref, wub_ref, wdb_ref):
    w = pl.program_id(0)
    e = exp_ref[w]
    t = tile_ref[w]
    prev = jnp.maximum(w - 1, 0)
    new_expert = (w == 0) | (e != exp_ref[prev])
    first_visit = (w == 0) | (t != tile_ref[prev])

    @pl.when(new_expert)
    def _():
        wgb_ref[...] = wg_ref[...].astype(BF16)
        wub_ref[...] = wu_ref[...].astype(BF16)
        wdb_ref[...] = wd_ref[...].astype(BF16)

    @pl.when(first_visit)
    def _():
        ys_ref[...] = jnp.zeros_like(ys_ref)

    tg = xs_ref.shape[0]
    sub = tg // GMM_SPLIT
    lo = gstart_ref[e]
    hi = gstart_ref[e + 1]
    for j in range(GMM_SPLIT):
        first_row = t * tg + j * sub

        @pl.when((valid_ref[w] == 1) & (lo < first_row + sub) & (hi > first_row))
        def _(j=j, first_row=first_row):
            rs = slice(j * sub, (j + 1) * sub)
            dot = functools.partial(jnp.dot, preferred_element_type=F32)
            x = jnp.concatenate(_unpack_rows(xs_ref[rs, :]), axis=1).astype(BF16)
            g = dot(x, wgb_ref[...])
            u = dot(x, wub_ref[...])
            y = _pack_rows(dot((g * jax.nn.sigmoid(g) * u).astype(BF16), wdb_ref[...]))
            rows = first_row + lax.broadcasted_iota(jnp.int32, (sub, 1), 0)
            mine = (rows >= lo) & (rows < hi)
            ys_ref[rs, :] = jnp.where(mine, y, ys_ref[rs, :])


def _grouped_experts(xs, counts, w_gate, w_up, w_down):
    A, W = xs.shape
    D = w_gate.shape[-2]
    tg = GMM_ROWS
    nt = A // tg
    n_items = nt + N_EXPERTS - 1
    ff = w_gate.shape[-1]
    gend = jnp.cumsum(counts).astype(jnp.int32)
    gstart = jnp.concatenate([jnp.zeros((1,), jnp.int32), gend])
    first_row = jnp.arange(nt, dtype=jnp.int32) * tg
    count_le = lambda ends, v: jnp.sum((ends[None, :] <= v[:, None]).astype(jnp.int32), axis=1)
    e_lo = count_le(gend, first_row)
    e_hi = count_le(gend, first_row + tg - 1)
    per_tile = e_hi - e_lo + 1
    item_end = jnp.cumsum(per_tile).astype(jnp.int32)
    total = item_end[-1]
    wi = jnp.arange(n_items, dtype=jnp.int32)
    tile = jnp.minimum(count_le(item_end, wi), nt - 1)
    in_tile = (tile[:, None] == jnp.arange(nt, dtype=jnp.int32)[None, :]).astype(jnp.int32)
    lookup = lambda table: jnp.sum(in_tile * table[None, :], axis=1)
    expert = lookup(e_lo) + wi - lookup(item_end - per_tile)
    valid = (wi < total).astype(jnp.int32)
    expert = jnp.where(valid == 1, expert, e_hi[nt - 1])
    grid_spec = pltpu.PrefetchScalarGridSpec(
        num_scalar_prefetch=4,
        grid=(n_items,),
        in_specs=[pl.BlockSpec((tg, W), lambda w, tl, ex, va, gs: (tl[w], 0)),
                  pl.BlockSpec((None, D, ff), lambda w, tl, ex, va, gs: (ex[w], 0, 0)),
                  pl.BlockSpec((None, D, ff), lambda w, tl, ex, va, gs: (ex[w], 0, 0)),
                  pl.BlockSpec((None, ff, D), lambda w, tl, ex, va, gs: (ex[w], 0, 0))],
        out_specs=pl.BlockSpec((tg, W), lambda w, tl, ex, va, gs: (tl[w], 0)),
        scratch_shapes=[pltpu.VMEM((D, ff), BF16), pltpu.VMEM((D, ff), BF16), pltpu.VMEM((ff, D), BF16)],
    )
    return pl.pallas_call(
        _gmm_kernel,
        grid_spec=grid_spec,
        out_shape=jax.ShapeDtypeStruct((A, W), xs.dtype),
        compiler_params=_cparams(("arbitrary",)),
        name="moe_grouped_experts",
    )(tile, expert, valid, gstart, xs, w_gate, w_up, w_down)


def _gather_rows_sc(rows, index):
    n = index.shape[0]
    w = rows.shape[1]
    mesh = plsc.VectorSubcoreMesh(core_axis_name="core", subcore_axis_name="subcore")
    per_subcore = n // SC_WINDOW // (mesh.num_cores * mesh.num_subcores)
    assert per_subcore * SC_WINDOW * mesh.num_cores * mesh.num_subcores == n

    @pl.kernel(out_type=jax.ShapeDtypeStruct((n, w), rows.dtype), mesh=mesh,
               scratch_types=[pltpu.VMEM((1, SC_WINDOW), jnp.int32), pltpu.VMEM((SC_WINDOW, w), rows.dtype)])
    def gather(rows_hbm, index_hbm, out_hbm, index_v, rows_v):
        worker = lax.axis_index("core") * mesh.num_subcores + lax.axis_index("subcore")

        @pl.loop(0, per_subcore)
        def _(j):
            first = (worker * per_subcore + j) * SC_WINDOW
            pltpu.sync_copy(index_hbm.at[:, pl.ds(first, SC_WINDOW)], index_v)
            pltpu.sync_copy(rows_hbm.at[index_v.at[0]], rows_v)
            pltpu.sync_copy(rows_v, out_hbm.at[pl.ds(first, SC_WINDOW)])

    return gather(rows, index.reshape(1, n))


def _combine_rows_kernel(yg_ref, wt_ref, x1_ref, h2_ref, mod_ref, sg_ref, su_ref, sd_ref, lng_ref, lnb_ref, *rest,
                         alpha):
    o_ref = rest[-1]
    h = jnp.concatenate(_unpack_rows(h2_ref[...]), axis=1).astype(BF16)
    g = jnp.dot(h, sg_ref[...], preferred_element_type=F32)
    u = jnp.dot(h, su_ref[...], preferred_element_type=F32)
    moe = jnp.dot((g * jax.nn.sigmoid(g) * u).astype(BF16), sd_ref[...], preferred_element_type=F32)
    wt = wt_ref[...]
    lo = jnp.zeros(h2_ref.shape, F32)
    hi = lo
    for k in range(TOP_K):
        rl, rh = _unpack_rows(yg_ref[k])
        lo = lo + wt[:, k:k + 1] * rl
        hi = hi + wt[:, k:k + 1] * rh
    moe = moe + jnp.concatenate([lo, hi], axis=1)
    o_ref[...] = _layer_norm(alpha * x1_ref[...] + mod_ref[5:6, :] * moe, lng_ref[...], lnb_ref[...])


def _combine_rows(ys, dest, wts, x1, h2, mod, shared, alpha):
    B, L, D = x1.shape
    W = ys.shape[-1]
    tm = min(COMBINE_ROWS, L)
    per_batch = mod.shape[0] > 1
    pieces = max(1, B * L // COMBINE_PIECE_TOKENS)
    nb = B // pieces
    full = lambda b, i: (0, 0)
    wspec = lambda a: pl.BlockSpec(a.shape, full)
    wts_t = jnp.swapaxes(wts, 1, 2)
    out = None
    for p in range(pieces):
        b0 = p * nb
        row = lambda b, i, b0=b0: (b0 + b, i, 0)
        slots = jnp.swapaxes(dest[b0:b0 + nb], 0, 1).reshape(TOP_K * nb * L)
        yg = _gather_rows_sc(ys, slots).reshape(TOP_K, nb, L, W)
        in_specs = [pl.BlockSpec((TOP_K, None, tm, W), lambda b, i: (0, b, i, 0)),
                    pl.BlockSpec((None, tm, TOP_K), row),
                    pl.BlockSpec((None, tm, D), row),
                    pl.BlockSpec((None, tm, W), row),
                    pl.BlockSpec((None, N_MOD, D), (lambda b, i, b0=b0: (b0 + b, 0, 0)) if per_batch
                                 else (lambda b, i: (0, 0, 0))),
                    wspec(shared["sh_w_gate"]), wspec(shared["sh_w_up"]), wspec(shared["sh_w_down"]),
                    wspec(shared["ln2_g"]), wspec(shared["ln2_b"])]
        args = [yg, wts_t, x1, h2, mod, shared["sh_w_gate"], shared["sh_w_up"], shared["sh_w_down"],
                shared["ln2_g"], shared["ln2_b"]]
        aliases = {}
        if out is not None:
            in_specs.append(pl.BlockSpec(memory_space=pl.ANY))
            args.append(out)
            aliases = {len(args) - 1: 0}
        out = pl.pallas_call(
            functools.partial(_combine_rows_kernel, alpha=alpha),
            grid=(nb, L // tm),
            in_specs=in_specs,
            out_specs=pl.BlockSpec((None, tm, D), row),
            out_shape=jax.ShapeDtypeStruct((B, L, D), F32),
            input_output_aliases=aliases,
            compiler_params=_cparams(("parallel", "parallel")),
            name="moe_combine_ln2",
        )(*args)
    return out


def _slots(idx, rank, counts):
    start = jnp.cumsum(counts) - counts
    pick = idx[..., None] == jnp.arange(N_EXPERTS, dtype=jnp.int32)
    return jnp.sum(jnp.where(pick, start, 0), axis=-1).astype(jnp.int32) + rank


def _mixer_and_router(x, mod, wts, ssm, cache, run0, alpha):
    B, L, D = x.shape
    latent = cache is not None
    if latent:
        k_ctx, v_ctx, h0 = cache
        cos, sin = _rope_cos_sin(L)
    else:
        cos = jnp.zeros((L, LANES), F32)
        sin = cos
        h0 = jnp.zeros((2, B, 2 * SSM_GROUPS * SSM_STATE), F32)
    q, k, v, u, ga, gs = _in_projection(x, mod, wts["w_in"], cos, sin, rope=latent)
    if latent:
        attn_o = _attention_latent(q, k, v, k_ctx, v_ctx, wts["attn_sink"])
    else:
        attn_o = _attention_context(q, k, v, wts["attn_sink"])
    y, fs = _ssm_mix(u, ssm, h0)
    x1, h2, idx, rank, rw, counts = _merge(x, attn_o, y, ga, gs, mod, wts, run0, alpha)
    return {"x1": x1, "h2": h2, "idx": idx, "rank": rank, "rw": rw, "mod": mod}, counts, k, v, fs


def _moe(groups, counts, wts, alpha):
    counts = counts.reshape(N_EXPERTS)
    for g in groups:
        g["dest"] = _slots(g["idx"], g["rank"], counts)
    xs = _dispatch([g["h2"] for g in groups], [g["dest"] for g in groups])
    ys = _grouped_experts(xs, counts, wts["exp_w_gate"], wts["exp_w_up"], wts["exp_w_down"])
    return [_combine_rows(ys, g["dest"], g["rw"], g["x1"], g["h2"], g["mod"], wts, alpha) for g in groups]


def kernel(x_prompt, x_sample, cache_k, cache_v, state_ssm, c, c_ctx, mod_w, mod_b, w_in, attn_sink, w_attn_br, ssm_a_re, ssm_a_im, ssm_log_dt, ssm_b_re, ssm_b_im, ssm_c_re, ssm_c_im, ssm_d, w_glu, w_ssm_br, w_out, ln1_g, ln1_b, ln2_g, ln2_b, router_w, router_b, exp_w_gate, exp_w_up, exp_w_down, sh_w_gate, sh_w_up, sh_w_down):
    depth = w_in.shape[0]
    assert depth == 1
    alpha = (2.0 * depth) ** 0.25
    D = x_prompt.shape[-1]
    nb_p = x_prompt.shape[0]
    nb_s = x_sample.shape[0]
    l = 0

    ncond = 1 + nb_s
    npad = -ncond % SUBLANES
    cond = jnp.concatenate([c_ctx[None, :], c, jnp.zeros((npad, D), F32)], axis=0)
    mod = _modulation(cond, mod_w[l], mod_b[l]).reshape(ncond + npad, N_MOD, D)
    mod_ctx, mod_lat = mod[0:1], mod[1:ncond]

    o1 = ATTN_WIDTH
    o3 = o1 + 2 * KV_WIDTH
    o4 = o3 + SSM_WIDTH
    wi = w_in[l]
    scale = HEAD_DIM ** -0.5 * LOG2_E
    wts = {
        "w_in": jnp.concatenate([wi[:, :o1] * scale, wi[:, o1:]], axis=1).astype(BF16),
        "attn_sink": attn_sink[l],
        "w_glu": w_glu[l].astype(BF16), "w_attn_br": w_attn_br[l].astype(BF16),
        "w_ssm_br": w_ssm_br[l].astype(BF16), "w_out": w_out[l].astype(BF16),
        "ln1_g": ln1_g[l].reshape(1, D), "ln1_b": ln1_b[l].reshape(1, D),
        "ln2_g": ln2_g[l].reshape(1, D), "ln2_b": ln2_b[l].reshape(1, D),
        "router_wt": router_w[l].T, "router_b": router_b[l].reshape(N_EXPERTS, 1),
        "exp_w_gate": exp_w_gate[l], "exp_w_up": exp_w_up[l], "exp_w_down": exp_w_down[l],
        "sh_w_gate": sh_w_gate[l].astype(BF16), "sh_w_up": sh_w_up[l].astype(BF16),
        "sh_w_down": sh_w_down[l].astype(BF16),
    }
    ssm = _ssm_params(ssm_a_re[l], ssm_a_im[l], ssm_log_dt[l], ssm_b_re[l], ssm_b_im[l],
                      ssm_c_re[l], ssm_c_im[l], ssm_d[l])

    no_tokens_yet = jnp.zeros((N_EXPERTS, 1), jnp.int32)
    grp_p, counts, k_p, v_p, fs_p = _mixer_and_router(x_prompt, mod_ctx, wts, ssm, None, no_tokens_yet, alpha)

    past = cache_k.shape[2]
    k_ctx = cache_k[:, l].reshape(nb_s, past, KV_WIDTH)
    v_ctx = cache_v[:, l].reshape(nb_s, past, KV_WIDTH)
    h0 = jnp.swapaxes(_state_to_lanes(state_ssm[:, l]), 0, 1)
    grp_s, counts, _, _, _ = _mixer_and_router(x_sample, mod_lat, wts, ssm, (k_ctx, v_ctx, h0), counts, alpha)
    yp, ys_ = _moe([grp_p, grp_s], counts, wts, alpha)

    Lp = x_prompt.shape[1]
    new_k = k_p.reshape(nb_p, 1, Lp, N_KV_HEADS, HEAD_DIM)
    new_v = v_p.reshape(nb_p, 1, Lp, N_KV_HEADS, HEAD_DIM)
    new_s = _lanes_to_state(jnp.swapaxes(fs_p, 0, 1))[:, None]
    return (yp, ys_, new_k, new_v, new_s)
```

```python
import functools
import math

import jax
import jax.numpy as jnp
from jax import lax
from jax.experimental import pallas as pl
from jax.experimental.pallas import tpu as pltpu
from jax.experimental.pallas import tpu_sc as plsc

F32 = jnp.float32
BF16 = jnp.bfloat16

GRID_W = 64
HEAD_DIM = 64
N_Q_HEADS = 8
N_KV_HEADS = 2
Q_PER_KV = N_Q_HEADS // N_KV_HEADS
ATTN_WIDTH = N_Q_HEADS * HEAD_DIM
KV_WIDTH = N_KV_HEADS * HEAD_DIM
WINDOW = 128
ROPE_BASE = 10000.0
SSM_WIDTH = 512
SSM_GROUP = 16
SSM_GROUPS = SSM_WIDTH // SSM_GROUP
SSM_STATE = 64
N_EXPERTS = 64
TOP_K = 6
ROUTED_SCALE = 2.5
N_MOD = 6
LN_EPS = 1e-5
LOG2_E = math.log2(math.e)

SUBLANES = 8
LANES = 128
VMEM_LIMIT = 48 * 1024 * 1024

INPROJ_ROWS = 1024
ATTN_QBLOCKS = 8
SSM_STEPS = 128
SSM_STRIPS = 4
MERGE_ROWS = 512
MERGE_SPLIT = 2
GMM_ROWS = 4096
GMM_SPLIT = 8
COMBINE_ROWS = 256
COMBINE_PIECE_TOKENS = 8192
SC_WINDOW = 128


def _cparams(sem):
    return pltpu.CompilerParams(dimension_semantics=sem, vmem_limit_bytes=VMEM_LIMIT)


def _mod_kernel(c_ref, w_ref, b_ref, o_ref):
    c = c_ref[...]
    s = c * jax.nn.sigmoid(c)
    o_ref[...] = jnp.dot(s, w_ref[...], preferred_element_type=F32,
                         precision=lax.Precision.HIGHEST) + b_ref[...]


def _modulation(cond, w, b):
    n, d = cond.shape
    nout = w.shape[1]
    tn = 512
    return pl.pallas_call(
        _mod_kernel,
        grid=(nout // tn,),
        in_specs=[pl.BlockSpec((n, d), lambda j: (0, 0)),
                  pl.BlockSpec((d, tn), lambda j: (0, j)),
                  pl.BlockSpec((1, tn), lambda j: (0, j))],
        out_specs=pl.BlockSpec((n, tn), lambda j: (0, j)),
        out_shape=jax.ShapeDtypeStruct((n, nout), F32),
        compiler_params=_cparams(("arbitrary",)),
        name="modulation",
    )(cond, w, b.reshape(1, nout))


def _rope_rotate(t, cos, sin):
    lane = lax.broadcasted_iota(jnp.int32, t.shape, 1)
    partner = jnp.where((lane % 32) < 16, pltpu.roll(t, LANES - 16, 1), pltpu.roll(t, 16, 1))
    return t * cos + partner * sin


def _inproj_kernel(x_ref, mod_ref, w_ref, cos_ref, sin_ref, q_ref, k_ref, v_ref, u_ref, ga_ref, gs_ref, *, rope):
    x = x_ref[...]
    h = (x * (1.0 + mod_ref[1:2, :]) + mod_ref[0:1, :]).astype(BF16)
    p = jnp.dot(h, w_ref[...], preferred_element_type=F32)
    o1 = ATTN_WIDTH
    o2 = o1 + KV_WIDTH
    o3 = o2 + KV_WIDTH
    d = x.shape[1]
    q = p[:, :o1]
    k = p[:, o1:o2]
    if rope:
        cos = cos_ref[...]
        sin = sin_ref[...]
        q = jnp.concatenate([_rope_rotate(q[:, j * LANES:(j + 1) * LANES], cos, sin)
                             for j in range(o1 // LANES)], axis=1)
        k = _rope_rotate(k, cos, sin)
    o4 = o3 + SSM_WIDTH
    q_ref[...] = q.astype(BF16)
    k_ref[...] = k
    v_ref[...] = p[:, o2:o3]
    u_ref[...] = p[:, o3:o4]
    ga_ref[...] = p[:, o4:o4 + d].astype(BF16)
    gs_ref[...] = p[:, o4 + d:].astype(BF16)


def _in_projection(x, mod, w, cos, sin, rope):
    B, L, D = x.shape
    tm = min(INPROJ_ROWS, L)
    per_batch = mod.shape[0] > 1
    nw = w.shape[1]
    row = lambda b, i: (b, i, 0)
    outs = pl.pallas_call(
        functools.partial(_inproj_kernel, rope=rope),
        grid=(B, L // tm),
        in_specs=[pl.BlockSpec((None, tm, D), row),
                  pl.BlockSpec((None, N_MOD, D), (lambda b, i: (b, 0, 0)) if per_batch else (lambda b, i: (0, 0, 0))),
                  pl.BlockSpec((D, nw), lambda b, i: (0, 0)),
                  pl.BlockSpec((tm, LANES), lambda b, i: (i, 0)),
                  pl.BlockSpec((tm, LANES), lambda b, i: (i, 0))],
        out_specs=[pl.BlockSpec((None, tm, ATTN_WIDTH), row),
                   pl.BlockSpec((None, tm, KV_WIDTH), row),
                   pl.BlockSpec((None, tm, KV_WIDTH), row),
                   pl.BlockSpec((None, tm, SSM_WIDTH), row),
                   pl.BlockSpec((None, tm, D), row),
                   pl.BlockSpec((None, tm, D), row)],
        out_shape=[jax.ShapeDtypeStruct((B, L, ATTN_WIDTH), BF16),
                   jax.ShapeDtypeStruct((B, L, KV_WIDTH), F32),
                   jax.ShapeDtypeStruct((B, L, KV_WIDTH), F32),
                   jax.ShapeDtypeStruct((B, L, SSM_WIDTH), F32),
                   jax.ShapeDtypeStruct((B, L, D), BF16),
                   jax.ShapeDtypeStruct((B, L, D), BF16)],
        compiler_params=_cparams(("parallel", "parallel")),
        name="in_projection",
    )(x, mod, w, cos, sin)
    return outs


def _rope_cos_sin(n_tokens):
    t = jnp.arange(n_tokens, dtype=jnp.int32)
    pos = jnp.stack([t // GRID_W, t % GRID_W], axis=-1).astype(F32)
    n_freq = HEAD_DIM // 4
    inv_freq = ROPE_BASE ** (-jnp.arange(n_freq, dtype=F32) / n_freq)
    ang = pos[:, :, None] * inv_freq
    c, s = jnp.cos(ang), jnp.sin(ang)
    cos = jnp.concatenate([c[:, 0], c[:, 0], c[:, 1], c[:, 1]], axis=-1)
    sin = jnp.concatenate([-s[:, 0], s[:, 0], -s[:, 1], s[:, 1]], axis=-1)
    return jnp.tile(cos, (1, LANES // HEAD_DIM)), jnp.tile(sin, (1, LANES // HEAD_DIM))


def _attend(sink_ref, q, kcat, vcat, mask, o_ref, row0=0):
    lq = q.shape[0]
    lane = lax.broadcasted_iota(jnp.int32, (1, LANES), 1)
    low = lane < HEAD_DIM
    k_sw = pltpu.roll(kcat, HEAD_DIM, 1)
    v_sw = pltpu.roll(vcat, HEAD_DIM, 1)
    neg = jnp.finfo(F32).min
    scores, vds = [], []
    for h in range(N_KV_HEADS):
        keep = low if h == 0 else jnp.logical_not(low)
        kd = jnp.where(keep, kcat, k_sw).astype(BF16)
        vds.append(jnp.where(keep, vcat, v_sw).astype(BF16))
        qs = []
        for j in range(Q_PER_KV):
            head = h * Q_PER_KV + j
            blk = q[:, (head // 2) * LANES:(head // 2 + 1) * LANES]
            sel = low if head % 2 == 0 else jnp.logical_not(low)
            qs.append(jnp.where(sel, blk, jnp.zeros_like(blk)))
        qstack = jnp.concatenate(qs, axis=0)
        scores.append(lax.dot_general(qstack, kd, (((1,), (1,)), ((), ())), preferred_element_type=F32))
    probs, denoms = [], []
    for h in range(N_KV_HEADS):
        s = scores[h]
        ps, ls = [], []
        for j in range(Q_PER_KV):
            sj = s[j * lq:(j + 1) * lq]
            if mask is not None:
                lm = mask.shape[1]
                sj = jnp.concatenate([jnp.where(mask, sj[:, :lm], neg), sj[:, lm:]], axis=1)
            sink = sink_ref[h * Q_PER_KV + j] * LOG2_E
            m = jnp.maximum(jnp.max(sj, axis=1, keepdims=True), sink)
            pj = jnp.exp2(sj - m)
            ls.append(jnp.sum(pj, axis=1, keepdims=True) + jnp.exp2(sink - m))
            ps.append(pj.astype(BF16))
        probs.append(jnp.concatenate(ps, axis=0))
        denoms.append(ls)
    for h in range(N_KV_HEADS):
        ls = denoms[h]
        o = jnp.dot(probs[h], vds[h], preferred_element_type=F32)
        for jj in range(Q_PER_KV // 2):
            oe = o[(2 * jj) * lq:(2 * jj + 1) * lq] / ls[2 * jj]
            oo = o[(2 * jj + 1) * lq:(2 * jj + 2) * lq] / ls[2 * jj + 1]
            cb = (h * Q_PER_KV) // 2 + jj
            o_ref[row0:row0 + lq, cb * LANES:(cb + 1) * LANES] = jnp.where(low, oe, oo).astype(o_ref.dtype)


def _attn_ctx_kernel(sink_ref, q_ref, k_ref, v_ref, o_ref):
    _attend(sink_ref, q_ref[...], k_ref[...], v_ref[...], None, o_ref)


def _attn_lat_kernel(sink_ref, q_ref, kp_ref, kc_ref, kn_ref, vp_ref, vc_ref, vn_ref, kx_ref, vx_ref, o_ref):
    n = pl.program_id(1)
    nb = pl.num_programs(1)
    blk = kp_ref.shape[0]
    nq = q_ref.shape[0] // blk
    klocal = jnp.concatenate([kp_ref[...], kc_ref[...], kn_ref[...]], axis=0)
    vlocal = jnp.concatenate([vp_ref[...], vc_ref[...], vn_ref[...]], axis=0)
    qi = lax.broadcasted_iota(jnp.int32, (blk, 3 * blk), 0)
    kj = lax.broadcasted_iota(jnp.int32, (blk, 3 * blk), 1)
    rel = kj - blk - qi
    in_band = (rel <= WINDOW) & (rel >= -WINDOW)
    for j in range(nq):
        mask = in_band
        if j == 0:
            mask = mask & ((kj >= blk) | (n > 0))
        if j == nq - 1:
            mask = mask & ((kj < 2 * blk) | (n < nb - 1))
        kcat = jnp.concatenate([klocal[j * blk:(j + 3) * blk], kx_ref[...]], axis=0)
        vcat = jnp.concatenate([vlocal[j * blk:(j + 3) * blk], vx_ref[...]], axis=0)
        _attend(sink_ref, q_ref[j * blk:(j + 1) * blk, :], kcat, vcat, mask, o_ref, j * blk)


def _attention_context(q, k, v, sink):
    B, L, _ = q.shape
    row = lambda b: (b, 0, 0)
    return pl.pallas_call(
        _attn_ctx_kernel,
        grid=(B,),
        in_specs=[pl.BlockSpec(memory_space=pltpu.SMEM),
                  pl.BlockSpec((None, L, ATTN_WIDTH), row),
                  pl.BlockSpec((None, L, KV_WIDTH), row),
                  pl.BlockSpec((None, L, KV_WIDTH), row)],
        out_specs=pl.BlockSpec((None, L, ATTN_WIDTH), row),
        out_shape=jax.ShapeDtypeStruct((B, L, ATTN_WIDTH), BF16),
        compiler_params=_cparams(("parallel",)),
        name="attention_context",
    )(sink, q, k, v)


def _attention_latent(q, k, v, k_ctx, v_ctx, sink):
    B, S, _ = q.shape
    blk = WINDOW
    nq = ATTN_QBLOCKS
    nb = S // blk
    nctx = k_ctx.shape[1]
    cur = lambda b, n: (b, n, 0)
    prv = lambda b, n: (b, jnp.maximum(n * nq - 1, 0), 0)
    nxt = lambda b, n: (b, jnp.minimum(n * nq + nq, nb - 1), 0)
    ctx = lambda b, n: (b, 0, 0)
    edge = lambda im: pl.BlockSpec((None, blk, KV_WIDTH), im)
    mid = pl.BlockSpec((None, nq * blk, KV_WIDTH), cur)
    return pl.pallas_call(
        _attn_lat_kernel,
        grid=(B, nb // nq),
        in_specs=[pl.BlockSpec(memory_space=pltpu.SMEM),
                  pl.BlockSpec((None, nq * blk, ATTN_WIDTH), cur),
                  edge(prv), mid, edge(nxt), edge(prv), mid, edge(nxt),
                  pl.BlockSpec((None, nctx, KV_WIDTH), ctx),
                  pl.BlockSpec((None, nctx, KV_WIDTH), ctx)],
        out_specs=pl.BlockSpec((None, nq * blk, ATTN_WIDTH), cur),
        out_shape=jax.ShapeDtypeStruct((B, S, ATTN_WIDTH), BF16),
        compiler_params=_cparams(("parallel", "parallel")),
        name="attention_latent",
    )(sink, q, k, k, k, v, v, v, k_ctx, v_ctx)


def _ssm_kernel(u_ref, wb_ref, wc_ref, a_ref, d_ref, h0_ref, y_ref, fs_ref, ut_ref, yt_ref, st_ref, *bu_refs):
    rev = pl.program_id(0)
    i = pl.program_id(2)
    nc = pl.num_programs(2)
    nseq, tm, _ = u_ref.shape
    sw = a_ref.shape[-1] // SSM_STRIPS
    nre = sw // 2

    @pl.when(i == 0)
    def _():
        st_ref[...] = h0_ref[...]

    for b in range(nseq):
        ub = u_ref[b]
        for s in range(SSM_STRIPS):
            ut_ref[s, pl.ds(b, tm, stride=nseq), :] = ub[:, s * LANES:(s + 1) * LANES]
    for s in range(SSM_STRIPS):
        bu_refs[s][...] = jnp.dot(ut_ref[s].astype(BF16), wb_ref[s], preferred_element_type=F32)

    for s in range(SSM_STRIPS):
        bu_ref = bu_refs[s]
        a_re = a_ref[:, s * sw:s * sw + nre]
        a_im = a_ref[:, s * sw + nre:(s + 1) * sw]

        def step(t, carry):
            xr, xi = carry
            r = pl.multiple_of((t + rev * (tm - 1 - 2 * t)) * nseq, nseq)
            nr = a_re * xr - a_im * xi + bu_ref[pl.ds(r, nseq), 0:nre]
            ni = a_re * xi + a_im * xr + bu_ref[pl.ds(r, nseq), nre:sw]
            bu_ref[pl.ds(r, nseq), 0:nre] = nr
            bu_ref[pl.ds(r, nseq), nre:sw] = ni
            return nr, ni

        xr, xi = lax.fori_loop(0, tm, step, (st_ref[:, s * sw:s * sw + nre], st_ref[:, s * sw + nre:(s + 1) * sw]),
                               unroll=True)
        st_ref[:, s * sw:s * sw + nre] = xr
        st_ref[:, s * sw + nre:(s + 1) * sw] = xi
        yt_ref[s] = jnp.dot(bu_ref[...].astype(BF16), wc_ref[s], preferred_element_type=F32)

    for b in range(nseq):
        yb = jnp.concatenate([yt_ref[s, pl.ds(b, tm, stride=nseq), :] for s in range(SSM_STRIPS)], axis=1)
        y_ref[b] = yb + u_ref[b] * d_ref[...]

    @pl.when(i == nc - 1)
    def _():
        fs_ref[...] = st_ref[...]


def _ssm_mix(u, ssm, h0):
    B, L, _ = u.shape
    nseq = SUBLANES
    tm = min(SSM_STEPS, L)
    nc = L // tm
    rows = nseq * tm
    ns = 2 * SSM_GROUPS * SSM_STATE
    chunk = lambda d, g, i: i + d * (nc - 1 - 2 * i)
    y, fs = pl.pallas_call(
        _ssm_kernel,
        grid=(2, B // nseq, nc),
        in_specs=[pl.BlockSpec((nseq, tm, SSM_WIDTH), lambda d, g, i: (g, chunk(d, g, i), 0)),
                  pl.BlockSpec((None, SSM_STRIPS, LANES, ns // SSM_STRIPS), lambda d, g, i: (d, 0, 0, 0)),
                  pl.BlockSpec((None, SSM_STRIPS, ns // SSM_STRIPS, LANES), lambda d, g, i: (d, 0, 0, 0)),
                  pl.BlockSpec((None, nseq, ns), lambda d, g, i: (d, 0, 0)),
                  pl.BlockSpec((None, 1, SSM_WIDTH), lambda d, g, i: (d, 0, 0)),
                  pl.BlockSpec((None, nseq, ns), lambda d, g, i: (d, g, 0))],
        out_specs=[pl.BlockSpec((None, nseq, tm, SSM_WIDTH), lambda d, g, i: (d, g, chunk(d, g, i), 0)),
                   pl.BlockSpec((None, nseq, ns), lambda d, g, i: (d, g, 0))],
        out_shape=[jax.ShapeDtypeStruct((2, B, L, SSM_WIDTH), F32),
                   jax.ShapeDtypeStruct((2, B, ns), F32)],
        scratch_shapes=[pltpu.VMEM((SSM_STRIPS, rows, LANES), F32), pltpu.VMEM((SSM_STRIPS, rows, LANES), F32),
                        pltpu.VMEM((nseq, ns), F32)]
        + [pltpu.VMEM((rows, ns // SSM_STRIPS), F32) for _ in range(SSM_STRIPS)],
        compiler_params=_cparams(("arbitrary", "arbitrary", "arbitrary")),
        name="ssm_scan",
    )(u, ssm["wb"], ssm["wc"], ssm["a"], ssm["d"], h0)
    return y, fs


def _state_to_lanes(s):
    lead = s.shape[:-3]
    s = s.reshape(lead + (2, SSM_STRIPS, SSM_GROUPS // SSM_STRIPS, SSM_STATE))
    s = jnp.swapaxes(s, -4, -3)
    return s.reshape(lead + (2 * SSM_GROUPS * SSM_STATE,))


def _lanes_to_state(v):
    lead = v.shape[:-1]
    s = v.reshape(lead + (SSM_STRIPS, 2, SSM_GROUPS // SSM_STRIPS, SSM_STATE))
    s = jnp.swapaxes(s, -4, -3)
    return s.reshape(lead + (2, SSM_GROUPS, SSM_STATE))


def _ssm_params(a_re, a_im, log_dt, b_re, b_im, c_re, c_im, dvec):
    G, N, C = SSM_GROUPS, SSM_STATE, SSM_GROUP
    lam_re = jnp.minimum(a_re, -1e-4)
    lam_im = a_im
    dt = jnp.exp(log_dt)[..., None]
    mag = jnp.exp(lam_re * dt)
    abar_re, abar_im = mag * jnp.cos(lam_im * dt), mag * jnp.sin(lam_im * dt)
    den = jnp.square(lam_re) + jnp.square(lam_im)
    p, qi = abar_re - 1.0, abar_im
    f_re = (p * lam_re + qi * lam_im) / den
    f_im = (qi * lam_re - p * lam_im) / den
    bbar_re = f_re[..., None] * b_re - f_im[..., None] * b_im
    bbar_im = f_re[..., None] * b_im + f_im[..., None] * b_re
    a = _state_to_lanes(jnp.stack([abar_re, abar_im], axis=1))
    a = jnp.broadcast_to(a[:, None, :], (2, SUBLANES, a.shape[-1]))
    S = SSM_STRIPS
    gs = G // S
    eye = jnp.eye(gs, dtype=F32)

    def bd_in(bb):
        bb = bb.reshape(2, S, gs, N, C)
        return jnp.einsum('dkgnc,gh->dkgchn', bb, eye).reshape(2, S, gs * C, gs * N)

    def bd_out(cc):
        cc = cc.reshape(2, S, gs, C, N)
        return jnp.einsum('dkgcn,gh->dkgnhc', cc, eye).reshape(2, S, gs * N, gs * C)

    wb = jnp.concatenate([bd_in(bbar_re), bd_in(bbar_im)], axis=-1).astype(BF16)
    wc = jnp.concatenate([bd_out(c_re), -bd_out(c_im)], axis=-2).astype(BF16)
    d = jnp.stack([dvec, jnp.zeros_like(dvec)], axis=0).reshape(2, 1, SSM_WIDTH)
    return {"a": a, "wb": wb, "wc": wc, "d": d}


def _layer_norm(x, g, b):
    mu = jnp.mean(x, axis=-1, keepdims=True)
    xc = x - mu
    var = jnp.mean(xc * xc, axis=-1, keepdims=True)
    return xc * lax.rsqrt(var + LN_EPS) * g + b


def _pack_rows(x):
    w = x.shape[1] // 2
    return pltpu.pack_elementwise([x[:, :w], x[:, w:]], packed_dtype=BF16)


def _unpack_rows(p):
    return (pltpu.unpack_elementwise(p, index=0, packed_dtype=BF16, unpacked_dtype=F32),
            pltpu.unpack_elementwise(p, index=1, packed_dtype=BF16, unpacked_dtype=F32))


def _gelu_tanh(x):
    return 0.5 * x * (1.0 + jnp.tanh(math.sqrt(2.0 / math.pi) * (x + 0.044715 * (x * x * x))))


def _merge_kernel(x_ref, ao_ref, yf_ref, yb_ref, ga_ref, gs_ref, mod_ref, wglu_ref, wa_ref, ws_ref, wo_ref,
                  lng_ref, lnb_ref, rwt_ref, rb_ref, run0_ref, x1_ref, h2_ref, idx_ref, rank_ref, wt_ref, cnt_ref,
                  run_ref, *, alpha):
    @pl.when((pl.program_id(0) == 0) & (pl.program_id(1) == 0))
    def _():
        run_ref[...] = run0_ref[...].astype(F32)

    nrow = x_ref.shape[0]
    sub = nrow // MERGE_SPLIT
    groups = [slice(j * sub, (j + 1) * sub) for j in range(MERGE_SPLIT)]
    dot = functools.partial(jnp.dot, preferred_element_type=F32)
    z = [_gelu_tanh(yf_ref[rs, :] + yb_ref[rs, :]) for rs in groups]
    attn_br = [dot(ao_ref[rs, :], wa_ref[...]) for rs in groups]
    gate = [jax.nn.sigmoid(dot(zg.astype(BF16), wglu_ref[...])) for zg in z]
    ssm_br = [dot((zg * gg).astype(BF16), ws_ref[...]) for zg, gg in zip(z, gate)]
    merged = [jax.nn.sigmoid(ga_ref[rs, :].astype(F32)) * ab + jax.nn.sigmoid(gs_ref[rs, :].astype(F32)) * sb
              for rs, ab, sb in zip(groups, attn_br, ssm_br)]
    mix = [dot(mg.astype(BF16), wo_ref[...]) for mg in merged]
    h2s = []
    for rs, mg in zip(groups, mix):
        x1 = _layer_norm(alpha * x_ref[rs, :] + mod_ref[2:3, :] * mg, lng_ref[...], lnb_ref[...])
        h2g = x1 * (1.0 + mod_ref[4:5, :]) + mod_ref[3:4, :]
        x1_ref[rs, :] = x1
        h2_ref[rs, :] = _pack_rows(h2g)
        h2s.append(h2g)
    h2 = jnp.concatenate(h2s, axis=0)

    logits = lax.dot_general(rwt_ref[...], h2, (((1,), (1,)), ((), ())), preferred_element_type=F32,
                             precision=lax.Precision.HIGHEST)
    score = jax.nn.sigmoid(logits)
    tm = score.shape[1]
    eidx = lax.broadcasted_iota(jnp.int32, score.shape, 0).astype(F32)
    work = score + rb_ref[...]
    picks, sel = [], []
    member = jnp.zeros_like(score)
    for _ in range(TOP_K):
        best = jnp.max(work, axis=0, keepdims=True)
        pick = jnp.min(jnp.where(work == best, eidx, float(N_EXPERTS)), axis=0, keepdims=True)
        hit = eidx == pick
        picks.append(pick)
        sel.append(jnp.sum(jnp.where(hit, score, 0.0), axis=0, keepdims=True))
        member = member + hit.astype(F32)
        work = jnp.where(hit, -jnp.inf, work)
    total = sel[0]
    for s in sel[1:]:
        total = total + s
    before = (lax.broadcasted_iota(jnp.int32, (tm, tm), 0) < lax.broadcasted_iota(jnp.int32, (tm, tm), 1))
    prefix = jnp.dot(member.astype(BF16), before.astype(BF16), preferred_element_type=F32)
    base = prefix + run_ref[...]
    for k in range(TOP_K):
        idx_ref[k:k + 1, :] = picks[k].astype(jnp.int32)
        rank_ref[k:k + 1, :] = jnp.sum(jnp.where(eidx == picks[k], base, 0.0), axis=0,
                                       keepdims=True).astype(jnp.int32)
        wt_ref[k:k + 1, :] = sel[k] / total * ROUTED_SCALE
    run_ref[...] = run_ref[...] + jnp.sum(member, axis=1, keepdims=True)
    cnt_ref[...] = run_ref[...].astype(jnp.int32)


def _merge(x, attn_o, y, ga, gs, mod, wts, run0, alpha):
    B, L, D = x.shape
    tm = min(MERGE_ROWS, L)
    per_batch = mod.shape[0] > 1
    row = lambda b, i: (b, i, 0)
    full = lambda b, i: (0, 0)
    wspec = lambda a: pl.BlockSpec(a.shape, full)
    return pl.pallas_call(
        functools.partial(_merge_kernel, alpha=alpha),
        grid=(B, L // tm),
        in_specs=[pl.BlockSpec((None, tm, D), row),
                  pl.BlockSpec((None, tm, ATTN_WIDTH), row),
                  pl.BlockSpec((None, None, tm, SSM_WIDTH), lambda b, i: (0, b, i, 0)),
                  pl.BlockSpec((None, None, tm, SSM_WIDTH), lambda b, i: (1, b, i, 0)),
                  pl.BlockSpec((None, tm, D), row),
                  pl.BlockSpec((None, tm, D), row),
                  pl.BlockSpec((None, N_MOD, D), (lambda b, i: (b, 0, 0)) if per_batch else (lambda b, i: (0, 0, 0))),
                  wspec(wts["w_glu"]), wspec(wts["w_attn_br"]), wspec(wts["w_ssm_br"]), wspec(wts["w_out"]),
                  wspec(wts["ln1_g"]), wspec(wts["ln1_b"]), wspec(wts["router_wt"]), wspec(wts["router_b"]),
                  wspec(run0)],
        out_specs=[pl.BlockSpec((None, tm, D), row),
                   pl.BlockSpec((None, tm, D // 2), row),
                   pl.BlockSpec((None, TOP_K, tm), lambda b, i: (b, 0, i)),
                   pl.BlockSpec((None, TOP_K, tm), lambda b, i: (b, 0, i)),
                   pl.BlockSpec((None, TOP_K, tm), lambda b, i: (b, 0, i)),
                   pl.BlockSpec((N_EXPERTS, 1), full)],
        out_shape=[jax.ShapeDtypeStruct((B, L, D), F32),
                   jax.ShapeDtypeStruct((B, L, D // 2), jnp.int32),
                   jax.ShapeDtypeStruct((B, TOP_K, L), jnp.int32),
                   jax.ShapeDtypeStruct((B, TOP_K, L), jnp.int32),
                   jax.ShapeDtypeStruct((B, TOP_K, L), F32),
                   jax.ShapeDtypeStruct((N_EXPERTS, 1), jnp.int32)],
        scratch_shapes=[pltpu.VMEM((N_EXPERTS, 1), F32)],
        compiler_params=_cparams(("arbitrary", "arbitrary")),
        name="merge_ln1_router",
    )(x, attn_o, y, y, ga, gs, mod, wts["w_glu"], wts["w_attn_br"], wts["w_ssm_br"], wts["w_out"],
      wts["ln1_g"], wts["ln1_b"], wts["router_wt"], wts["router_b"], run0)


def _dispatch(h2s, dests):
    W = h2s[0].shape[-1]
    n_slots = sum(d.size for d in dests)
    mesh = plsc.VectorSubcoreMesh(core_axis_name="core", subcore_axis_name="subcore")
    workers = mesh.num_cores * mesh.num_subcores
    tokens = [h.shape[0] * h.shape[1] for h in h2s]
    per_worker = [t // SC_WINDOW // workers for t in tokens]
    assert all(p * SC_WINDOW * workers == t for p, t in zip(per_worker, tokens))

    @pl.kernel(out_type=jax.ShapeDtypeStruct((n_slots, W), h2s[0].dtype), mesh=mesh,
               scratch_types=[pltpu.VMEM((TOP_K, SC_WINDOW), jnp.int32), pltpu.VMEM((SC_WINDOW, W), h2s[0].dtype)])
    def scatter(*refs):
        out_hbm, index_v, rows_v = refs[-3:]
        worker = lax.axis_index("core") * mesh.num_subcores + lax.axis_index("subcore")
        for g, per in enumerate(per_worker):
            rows_hbm, slots_hbm = refs[2 * g], refs[2 * g + 1]

            @pl.loop(0, per)
            def _(j, rows_hbm=rows_hbm, slots_hbm=slots_hbm, per=per):
                window = worker * per + j
                pltpu.sync_copy(rows_hbm.at[pl.ds(window * SC_WINDOW, SC_WINDOW)], rows_v)
                pltpu.sync_copy(slots_hbm.at[window], index_v)
                for k in range(TOP_K):
                    pltpu.sync_copy(rows_v, out_hbm.at[index_v.at[k]])

    args = []
    for h, d in zip(h2s, dests):
        nb, _, ln = d.shape
        slots = d.reshape(nb, TOP_K, ln // SC_WINDOW, SC_WINDOW).transpose(0, 2, 1, 3)
        args += [h.reshape(-1, W), slots.reshape(nb * ln // SC_WINDOW, TOP_K, SC_WINDOW)]
    return scatter(*args)


def _gmm_kernel(tile_ref, exp_ref, valid_ref, gstart_ref, xs_ref, wg_ref, wu_ref, wd_ref, ys_ref,
                wgb_ref, wub_ref, wdb_ref):
    w = pl.program_id(0)
    e = exp_ref[w]
    t = tile_ref[w]
    prev = jnp.maximum(w - 1, 0)
    new_expert = (w == 0) | (e != exp_ref[prev])
    first_visit = (w == 0) | (t != tile_ref[prev])

    @pl.when(new_expert)
    def _():
        wgb_ref[...] = wg_ref[...].astype(BF16)
        wub_ref[...] = wu_ref[...].astype(BF16)
        wdb_ref[...] = wd_ref[...].astype(BF16)

    @pl.when(first_visit)
    def _():
        ys_ref[...] = jnp.zeros_like(ys_ref)

    tg = xs_ref.shape[0]
    sub = tg // GMM_SPLIT
    lo = gstart_ref[e]
    hi = gstart_ref[e + 1]
    for j in range(GMM_SPLIT):
        first_row = t * tg + j * sub

        @pl.when((valid_ref[w] == 1) & (lo < first_row + sub) & (hi > first_row))
        def _(j=j, first_row=first_row):
            rs = slice(j * sub, (j + 1) * sub)
            dot = functools.partial(jnp.dot, preferred_element_type=F32)
            x = jnp.concatenate(_unpack_rows(xs_ref[rs, :]), axis=1).astype(BF16)
            g = dot(x, wgb_ref[...])
            u = dot(x, wub_ref[...])
            y = _pack_rows(dot((g * jax.nn.sigmoid(g) * u).astype(BF16), wdb_ref[...]))
            rows = first_row + lax.broadcasted_iota(jnp.int32, (sub, 1), 0)
            mine = (rows >= lo) & (rows < hi)
            ys_ref[rs, :] = jnp.where(mine, y, ys_ref[rs, :])


def _grouped_experts(xs, counts, w_gate, w_up, w_down):
    A, W = xs.shape
    D = w_gate.shape[-2]
    tg = GMM_ROWS
    nt = A // tg
    n_items = nt + N_EXPERTS - 1
    ff = w_gate.shape[-1]
    gend = jnp.cumsum(counts).astype(jnp.int32)
    gstart = jnp.concatenate([jnp.zeros((1,), jnp.int32), gend])
    first_row = jnp.arange(nt, dtype=jnp.int32) * tg
    count_le = lambda ends, v: jnp.sum((ends[None, :] <= v[:, None]).astype(jnp.int32), axis=1)
    e_lo = count_le(gend, first_row)
    e_hi = count_le(gend, first_row + tg - 1)
    per_tile = e_hi - e_lo + 1
    item_end = jnp.cumsum(per_tile).astype(jnp.int32)
    total = item_end[-1]
    wi = jnp.arange(n_items, dtype=jnp.int32)
    tile = jnp.minimum(count_le(item_end, wi), nt - 1)
    in_tile = (tile[:, None] == jnp.arange(nt, dtype=jnp.int32)[None, :]).astype(jnp.int32)
    lookup = lambda table: jnp.sum(in_tile * table[None, :], axis=1)
    expert = lookup(e_lo) + wi - lookup(item_end - per_tile)
    valid = (wi < total).astype(jnp.int32)
    expert = jnp.where(valid == 1, expert, e_hi[nt - 1])
    grid_spec = pltpu.PrefetchScalarGridSpec(
        num_scalar_prefetch=4,
        grid=(n_items,),
        in_specs=[pl.BlockSpec((tg, W), lambda w, tl, ex, va, gs: (tl[w], 0)),
                  pl.BlockSpec((None, D, ff), lambda w, tl, ex, va, gs: (ex[w], 0, 0)),
                  pl.BlockSpec((None, D, ff), lambda w, tl, ex, va, gs: (ex[w], 0, 0)),
                  pl.BlockSpec((None, ff, D), lambda w, tl, ex, va, gs: (ex[w], 0, 0))],
        out_specs=pl.BlockSpec((tg, W), lambda w, tl, ex, va, gs: (tl[w], 0)),
        scratch_shapes=[pltpu.VMEM((D, ff), BF16), pltpu.VMEM((D, ff), BF16), pltpu.VMEM((ff, D), BF16)],
    )
    return pl.pallas_call(
        _gmm_kernel,
        grid_spec=grid_spec,
        out_shape=jax.ShapeDtypeStruct((A, W), xs.dtype),
        compiler_params=_cparams(("arbitrary",)),
        name="moe_grouped_experts",
    )(tile, expert, valid, gstart, xs, w_gate, w_up, w_down)


def _gather_rows_sc(rows, index):
    n = index.shape[0]
    w = rows.shape[1]
    mesh = plsc.VectorSubcoreMesh(core_axis_name="core", subcore_axis_name="subcore")
    per_subcore = n // SC_WINDOW // (mesh.num_cores * mesh.num_subcores)
    assert per_subcore * SC_WINDOW * mesh.num_cores * mesh.num_subcores == n

    @pl.kernel(out_type=jax.ShapeDtypeStruct((n, w), rows.dtype), mesh=mesh,
               scratch_types=[pltpu.VMEM((1, SC_WINDOW), jnp.int32), pltpu.VMEM((SC_WINDOW, w), rows.dtype)])
    def gather(rows_hbm, index_hbm, out_hbm, index_v, rows_v):
        worker = lax.axis_index("core") * mesh.num_subcores + lax.axis_index("subcore")

        @pl.loop(0, per_subcore)
        def _(j):
            first = (worker * per_subcore + j) * SC_WINDOW
            pltpu.sync_copy(index_hbm.at[:, pl.ds(first, SC_WINDOW)], index_v)
            pltpu.sync_copy(rows_hbm.at[index_v.at[0]], rows_v)
            pltpu.sync_copy(rows_v, out_hbm.at[pl.ds(first, SC_WINDOW)])

    return gather(rows, index.reshape(1, n))


def _combine_rows_kernel(yg_ref, wt_ref, x1_ref, h2_ref, mod_ref, sg_ref, su_ref, sd_ref, lng_ref, lnb_ref, *rest,
                         alpha):
    o_ref = rest[-1]
    h = jnp.concatenate(_unpack_rows(h2_ref[...]), axis=1).astype(BF16)
    g = jnp.dot(h, sg_ref[...], preferred_element_type=F32)
    u = jnp.dot(h, su_ref[...], preferred_element_type=F32)
    moe = jnp.dot((g * jax.nn.sigmoid(g) * u).astype(BF16), sd_ref[...], preferred_element_type=F32)
    wt = wt_ref[...]
    lo = jnp.zeros(h2_ref.shape, F32)
    hi = lo
    for k in range(TOP_K):
        rl, rh = _unpack_rows(yg_ref[k])
        lo = lo + wt[:, k:k + 1] * rl
        hi = hi + wt[:, k:k + 1] * rh
    moe = moe + jnp.concatenate([lo, hi], axis=1)
    o_ref[...] = _layer_norm(alpha * x1_ref[...] + mod_ref[5:6, :] * moe, lng_ref[...], lnb_ref[...])


def _combine_rows(ys, dest, wts, x1, h2, mod, shared, alpha):
    B, L, D = x1.shape
    W = ys.shape[-1]
    tm = min(COMBINE_ROWS, L)
    per_batch = mod.shape[0] > 1
    pieces = max(1, B * L // COMBINE_PIECE_TOKENS)
    nb = B // pieces
    full = lambda b, i: (0, 0)
    wspec = lambda a: pl.BlockSpec(a.shape, full)
    wts_t = jnp.swapaxes(wts, 1, 2)
    out = None
    for p in range(pieces):
        b0 = p * nb
        row = lambda b, i, b0=b0: (b0 + b, i, 0)
        slots = jnp.swapaxes(dest[b0:b0 + nb], 0, 1).reshape(TOP_K * nb * L)
        yg = _gather_rows_sc(ys, slots).reshape(TOP_K, nb, L, W)
        in_specs = [pl.BlockSpec((TOP_K, None, tm, W), lambda b, i: (0, b, i, 0)),
                    pl.BlockSpec((None, tm, TOP_K), row),
                    pl.BlockSpec((None, tm, D), row),
                    pl.BlockSpec((None, tm, W), row),
                    pl.BlockSpec((None, N_MOD, D), (lambda b, i, b0=b0: (b0 + b, 0, 0)) if per_batch
                                 else (lambda b, i: (0, 0, 0))),
                    wspec(shared["sh_w_gate"]), wspec(shared["sh_w_up"]), wspec(shared["sh_w_down"]),
                    wspec(shared["ln2_g"]), wspec(shared["ln2_b"])]
        args = [yg, wts_t, x1, h2, mod, shared["sh_w_gate"], shared["sh_w_up"], shared["sh_w_down"],
                shared["ln2_g"], shared["ln2_b"]]
        aliases = {}
        if out is not None:
            in_specs.append(pl.BlockSpec(memory_space=pl.ANY))
            args.append(out)
            aliases = {len(args) - 1: 0}
        out = pl.pallas_call(
            functools.partial(_combine_rows_kernel, alpha=alpha),
            grid=(nb, L // tm),
            in_specs=in_specs,
            out_specs=pl.BlockSpec((None, tm, D), row),
            out_shape=jax.ShapeDtypeStruct((B, L, D), F32),
            input_output_aliases=aliases,
            compiler_params=_cparams(("parallel", "parallel")),
            name="moe_combine_ln2",
        )(*args)
    return out


def _slots(idx, rank, counts):
    start = jnp.cumsum(counts) - counts
    pick = idx[..., None] == jnp.arange(N_EXPERTS, dtype=jnp.int32)
    return jnp.sum(jnp.where(pick, start, 0), axis=-1).astype(jnp.int32) + rank


def _mixer_and_router(x, mod, wts, ssm, cache, run0, alpha):
    B, L, D = x.shape
    latent = cache is not None
    if latent:
        k_ctx, v_ctx, h0 = cache
        cos, sin = _rope_cos_sin(L)
    else:
        cos = jnp.zeros((L, LANES), F32)
        sin = cos
        h0 = jnp.zeros((2, B, 2 * SSM_GROUPS * SSM_STATE), F32)
    q, k, v, u, ga, gs = _in_projection(x, mod, wts["w_in"], cos, sin, rope=latent)
    if latent:
        attn_o = _attention_latent(q, k, v, k_ctx, v_ctx, wts["attn_sink"])
    else:
        attn_o = _attention_context(q, k, v, wts["attn_sink"])
    y, fs = _ssm_mix(u, ssm, h0)
    x1, h2, idx, rank, rw, counts = _merge(x, attn_o, y, ga, gs, mod, wts, run0, alpha)
    return {"x1": x1, "h2": h2, "idx": idx, "rank": rank, "rw": rw, "mod": mod}, counts, k, v, fs


def _moe(groups, counts, wts, alpha):
    counts = counts.reshape(N_EXPERTS)
    for g in groups:
        g["dest"] = _slots(g["idx"], g["rank"], counts)
    xs = _dispatch([g["h2"] for g in groups], [g["dest"] for g in groups])
    ys = _grouped_experts(xs, counts, wts["exp_w_gate"], wts["exp_w_up"], wts["exp_w_down"])
    return [_combine_rows(ys, g["dest"], g["rw"], g["x1"], g["h2"], g["mod"], wts, alpha) for g in groups]


def kernel(x_prompt, x_sample, cache_k, cache_v, state_ssm, c, c_ctx, mod_w, mod_b, w_in, attn_sink, w_attn_br, ssm_a_re, ssm_a_im, ssm_log_dt, ssm_b_re, ssm_b_im, ssm_c_re, ssm_c_im, ssm_d, w_glu, w_ssm_br, w_out, ln1_g, ln1_b, ln2_g, ln2_b, router_w, router_b, exp_w_gate, exp_w_up, exp_w_down, sh_w_gate, sh_w_up, sh_w_down):
    depth = w_in.shape[0]
    assert depth == 1
    alpha = (2.0 * depth) ** 0.25
    D = x_prompt.shape[-1]
    nb_p = x_prompt.shape[0]
    nb_s = x_sample.shape[0]
    l = 0

    ncond = 1 + nb_s
    npad = -ncond % SUBLANES
    cond = jnp.concatenate([c_ctx[None, :], c, jnp.zeros((npad, D), F32)], axis=0)
    mod = _modulation(cond, mod_w[l], mod_b[l]).reshape(ncond + npad, N_MOD, D)
    mod_ctx, mod_lat = mod[0:1], mod[1:ncond]

    o1 = ATTN_WIDTH
    o3 = o1 + 2 * KV_WIDTH
    o4 = o3 + SSM_WIDTH
    wi = w_in[l]
    scale = HEAD_DIM ** -0.5 * LOG2_E
    wts = {
        "w_in": jnp.concatenate([wi[:, :o1] * scale, wi[:, o1:]], axis=1).astype(BF16),
        "attn_sink": attn_sink[l],
        "w_glu": w_glu[l].astype(BF16), "w_attn_br": w_attn_br[l].astype(BF16),
        "w_ssm_br": w_ssm_br[l].astype(BF16), "w_out": w_out[l].astype(BF16),
        "ln1_g": ln1_g[l].reshape(1, D), "ln1_b": ln1_b[l].reshape(1, D),
        "ln2_g": ln2_g[l].reshape(1, D), "ln2_b": ln2_b[l].reshape(1, D),
        "router_wt": router_w[l].T, "router_b": router_b[l].reshape(N_EXPERTS, 1),
        "exp_w_gate": exp_w_gate[l], "exp_w_up": exp_w_up[l], "exp_w_down": exp_w_down[l],
        "sh_w_gate": sh_w_gate[l].astype(BF16), "sh_w_up": sh_w_up[l].astype(BF16),
        "sh_w_down": sh_w_down[l].astype(BF16),
    }
    ssm = _ssm_params(ssm_a_re[l], ssm_a_im[l], ssm_log_dt[l], ssm_b_re[l], ssm_b_im[l],
                      ssm_c_re[l], ssm_c_im[l], ssm_d[l])

    no_tokens_yet = jnp.zeros((N_EXPERTS, 1), jnp.int32)
    grp_p, counts, k_p, v_p, fs_p = _mixer_and_router(x_prompt, mod_ctx, wts, ssm, None, no_tokens_yet, alpha)

    past = cache_k.shape[2]
    k_ctx = cache_k[:, l].reshape(nb_s, past, KV_WIDTH)
    v_ctx = cache_v[:, l].reshape(nb_s, past, KV_WIDTH)
    h0 = jnp.swapaxes(_state_to_lanes(state_ssm[:, l]), 0, 1)
    grp_s, counts, _, _, _ = _mixer_and_router(x_sample, mod_lat, wts, ssm, (k_ctx, v_ctx, h0), counts, alpha)
    yp, ys_ = _moe([grp_p, grp_s], counts, wts, alpha)

    Lp = x_prompt.shape[1]
    new_k = k_p.reshape(nb_p, 1, Lp, N_KV_HEADS, HEAD_DIM)
    new_v = v_p.reshape(nb_p, 1, Lp, N_KV_HEADS, HEAD_DIM)
    new_s = _lanes_to_state(jnp.swapaxes(fs_p, 0, 1))[:, None]
    return (yp, ys_, new_k, new_v, new_s)
```

```python
import functools
import math

import jax
import jax.numpy as jnp
from jax import lax
from jax.experimental import pallas as pl
from jax.experimental.pallas import tpu as pltpu
from jax.experimental.pallas import tpu_sc as plsc

F32 = jnp.float32
BF16 = jnp.bfloat16

GRID_W = 64
HEAD_DIM = 64
N_Q_HEADS = 8
N_KV_HEADS = 2
Q_PER_KV = N_Q_HEADS // N_KV_HEADS
ATTN_WIDTH = N_Q_HEADS * HEAD_DIM
KV_WIDTH = N_KV_HEADS * HEAD_DIM
WINDOW = 128
ROPE_BASE = 10000.0
ROPE_FREQS = HEAD_DIM // 4
SSM_WIDTH = 512
SSM_GROUP = 16
SSM_GROUPS = SSM_WIDTH // SSM_GROUP
SSM_STATE = 64
N_EXPERTS = 64
TOP_K = 6
ROUTED_SCALE = 2.5
N_MOD = 6
LN_EPS = 1e-5
LOG2_E = math.log2(math.e)

SUBLANES = 8
LANES = 128
VMEM_LIMIT = 48 * 1024 * 1024

MOD_COLS = 512
INPROJ_ROWS = 1024
ATTN_QBLOCKS = 8
SSM_STEPS = 128
SSM_STRIPS = 4
MERGE_ROWS = 512
MERGE_SPLIT = 2
GMM_ROWS = 2048
GMM_SPLIT = 4
COMBINE_ROWS = 256
COMBINE_PIECE_TOKENS = 8192
SC_WINDOW = 128


def _cparams(sem):
    return pltpu.CompilerParams(dimension_semantics=sem, vmem_limit_bytes=VMEM_LIMIT)


def _mod_kernel(c_ref, w_ref, b_ref, o_ref):
    c = c_ref[...]
    s = c * jax.nn.sigmoid(c)
    o_ref[...] = jnp.dot(s, w_ref[...], preferred_element_type=F32,
                         precision=lax.Precision.HIGHEST) + b_ref[...]


def _modulation(cond, w, b):
    n, d = cond.shape
    nout = w.shape[1]
    tn = MOD_COLS
    return pl.pallas_call(
        _mod_kernel,
        grid=(nout // tn,),
        in_specs=[pl.BlockSpec((n, d), lambda j: (0, 0)),
                  pl.BlockSpec((d, tn), lambda j: (0, j)),
                  pl.BlockSpec((1, tn), lambda j: (0, j))],
        out_specs=pl.BlockSpec((n, tn), lambda j: (0, j)),
        out_shape=jax.ShapeDtypeStruct((n, nout), F32),
        compiler_params=_cparams(("arbitrary",)),
        name="modulation",
    )(cond, w, b.reshape(1, nout))


def _rope_rotate(t, cos, sin):
    lane = lax.broadcasted_iota(jnp.int32, t.shape, 1)
    first_of_pair = (lane % (2 * ROPE_FREQS)) < ROPE_FREQS
    partner = jnp.where(first_of_pair, pltpu.roll(t, LANES - ROPE_FREQS, 1), pltpu.roll(t, ROPE_FREQS, 1))
    return t * cos + partner * sin


def _inproj_kernel(x_ref, mod_ref, w_ref, cos_ref, sin_ref, q_ref, k_ref, v_ref, u_ref, ga_ref, gs_ref, *, rope):
    x = x_ref[...]
    h = (x * (1.0 + mod_ref[1:2, :]) + mod_ref[0:1, :]).astype(BF16)
    p = jnp.dot(h, w_ref[...], preferred_element_type=F32)
    o1 = ATTN_WIDTH
    o2 = o1 + KV_WIDTH
    o3 = o2 + KV_WIDTH
    d = x.shape[1]
    q = p[:, :o1]
    k = p[:, o1:o2]
    if rope:
        cos = cos_ref[...]
        sin = sin_ref[...]
        q = jnp.concatenate([_rope_rotate(q[:, j * LANES:(j + 1) * LANES], cos, sin)
                             for j in range(o1 // LANES)], axis=1)
        k = _rope_rotate(k, cos, sin)
    o4 = o3 + SSM_WIDTH
    q_ref[...] = q.astype(BF16)
    k_ref[...] = k
    v_ref[...] = p[:, o2:o3]
    u_ref[...] = p[:, o3:o4]
    ga_ref[...] = p[:, o4:o4 + d].astype(BF16)
    gs_ref[...] = p[:, o4 + d:].astype(BF16)


def _in_projection(x, mod, w, cos, sin, rope):
    B, L, D = x.shape
    tm = min(INPROJ_ROWS, L)
    per_batch = mod.shape[0] > 1
    nw = w.shape[1]
    row = lambda b, i: (b, i, 0)
    outs = pl.pallas_call(
        functools.partial(_inproj_kernel, rope=rope),
        grid=(B, L // tm),
        in_specs=[pl.BlockSpec((None, tm, D), row),
                  pl.BlockSpec((None, N_MOD, D), (lambda b, i: (b, 0, 0)) if per_batch else (lambda b, i: (0, 0, 0))),
                  pl.BlockSpec((D, nw), lambda b, i: (0, 0)),
                  pl.BlockSpec((tm, LANES), lambda b, i: (i, 0)),
                  pl.BlockSpec((tm, LANES), lambda b, i: (i, 0))],
        out_specs=[pl.BlockSpec((None, tm, ATTN_WIDTH), row),
                   pl.BlockSpec((None, tm, KV_WIDTH), row),
                   pl.BlockSpec((None, tm, KV_WIDTH), row),
                   pl.BlockSpec((None, tm, SSM_WIDTH), row),
                   pl.BlockSpec((None, tm, D), row),
                   pl.BlockSpec((None, tm, D), row)],
        out_shape=[jax.ShapeDtypeStruct((B, L, ATTN_WIDTH), BF16),
                   jax.ShapeDtypeStruct((B, L, KV_WIDTH), F32),
                   jax.ShapeDtypeStruct((B, L, KV_WIDTH), F32),
                   jax.ShapeDtypeStruct((B, L, SSM_WIDTH), F32),
                   jax.ShapeDtypeStruct((B, L, D), BF16),
                   jax.ShapeDtypeStruct((B, L, D), BF16)],
        compiler_params=_cparams(("parallel", "parallel")),
        name="in_projection",
    )(x, mod, w, cos, sin)
    return outs


def _rope_cos_sin(n_tokens):
    t = jnp.arange(n_tokens, dtype=jnp.int32)
    pos = jnp.stack([t // GRID_W, t % GRID_W], axis=-1).astype(F32)
    n_freq = ROPE_FREQS
    inv_freq = ROPE_BASE ** (-jnp.arange(n_freq, dtype=F32) / n_freq)
    ang = pos[:, :, None] * inv_freq
    c, s = jnp.cos(ang), jnp.sin(ang)
    cos = jnp.concatenate([c[:, 0], c[:, 0], c[:, 1], c[:, 1]], axis=-1)
    sin = jnp.concatenate([-s[:, 0], s[:, 0], -s[:, 1], s[:, 1]], axis=-1)
    return jnp.tile(cos, (1, LANES // HEAD_DIM)), jnp.tile(sin, (1, LANES // HEAD_DIM))


def _attend(sink_ref, q, kcat, vcat, mask, o_ref, row0=0):
    lq = q.shape[0]
    lane = lax.broadcasted_iota(jnp.int32, (1, LANES), 1)
    low = lane < HEAD_DIM
    k_sw = pltpu.roll(kcat, HEAD_DIM, 1)
    v_sw = pltpu.roll(vcat, HEAD_DIM, 1)
    neg = jnp.finfo(F32).min
    scores, vds = [], []
    for h in range(N_KV_HEADS):
        keep = low if h == 0 else jnp.logical_not(low)
        kd = jnp.where(keep, kcat, k_sw).astype(BF16)
        vds.append(jnp.where(keep, vcat, v_sw).astype(BF16))
        qs = []
        for j in range(Q_PER_KV):
            head = h * Q_PER_KV + j
            blk = q[:, (head // 2) * LANES:(head // 2 + 1) * LANES]
            sel = low if head % 2 == 0 else jnp.logical_not(low)
            qs.append(jnp.where(sel, blk, jnp.zeros_like(blk)))
        qstack = jnp.concatenate(qs, axis=0)
        scores.append(lax.dot_general(qstack, kd, (((1,), (1,)), ((), ())), preferred_element_type=F32))
    probs, denoms = [], []
    for h in range(N_KV_HEADS):
        s = scores[h]
        ps, ls = [], []
        for j in range(Q_PER_KV):
            sj = s[j * lq:(j + 1) * lq]
            if mask is not None:
                lm = mask.shape[1]
                sj = jnp.concatenate([jnp.where(mask, sj[:, :lm], neg), sj[:, lm:]], axis=1)
            sink = sink_ref[h * Q_PER_KV + j] * LOG2_E
            m = jnp.maximum(jnp.max(sj, axis=1, keepdims=True), sink)
            pj = jnp.exp2(sj - m)
            ls.append(jnp.sum(pj, axis=1, keepdims=True) + jnp.exp2(sink - m))
            ps.append(pj.astype(BF16))
        probs.append(jnp.concatenate(ps, axis=0))
        denoms.append(ls)
    for h in range(N_KV_HEADS):
        ls = denoms[h]
        o = jnp.dot(probs[h], vds[h], preferred_element_type=F32)
        for jj in range(Q_PER_KV // 2):
            oe = o[(2 * jj) * lq:(2 * jj + 1) * lq] / ls[2 * jj]
            oo = o[(2 * jj + 1) * lq:(2 * jj + 2) * lq] / ls[2 * jj + 1]
            cb = (h * Q_PER_KV) // 2 + jj
            o_ref[row0:row0 + lq, cb * LANES:(cb + 1) * LANES] = jnp.where(low, oe, oo).astype(o_ref.dtype)


def _attn_ctx_kernel(sink_ref, q_ref, k_ref, v_ref, o_ref):
    _attend(sink_ref, q_ref[...], k_ref[...], v_ref[...], None, o_ref)


def _attn_lat_kernel(sink_ref, q_ref, kp_ref, kc_ref, kn_ref, vp_ref, vc_ref, vn_ref, kx_ref, vx_ref, o_ref):
    n = pl.program_id(1)
    nb = pl.num_programs(1)
    blk = kp_ref.shape[0]
    nq = q_ref.shape[0] // blk
    klocal = jnp.concatenate([kp_ref[...], kc_ref[...], kn_ref[...]], axis=0)
    vlocal = jnp.concatenate([vp_ref[...], vc_ref[...], vn_ref[...]], axis=0)
    qi = lax.broadcasted_iota(jnp.int32, (blk, 3 * blk), 0)
    kj = lax.broadcasted_iota(jnp.int32, (blk, 3 * blk), 1)
    rel = kj - blk - qi
    in_band = (rel <= WINDOW) & (rel >= -WINDOW)
    for j in range(nq):
        mask = in_band
        if j == 0:
            mask = mask & ((kj >= blk) | (n > 0))
        if j == nq - 1:
            mask = mask & ((kj < 2 * blk) | (n < nb - 1))
        kcat = jnp.concatenate([klocal[j * blk:(j + 3) * blk], kx_ref[...]], axis=0)
        vcat = jnp.concatenate([vlocal[j * blk:(j + 3) * blk], vx_ref[...]], axis=0)
        _attend(sink_ref, q_ref[j * blk:(j + 1) * blk, :], kcat, vcat, mask, o_ref, j * blk)


def _attention_context(q, k, v, sink):
    B, L, _ = q.shape
    row = lambda b: (b, 0, 0)
    return pl.pallas_call(
        _attn_ctx_kernel,
        grid=(B,),
        in_specs=[pl.BlockSpec(memory_space=pltpu.SMEM),
                  pl.BlockSpec((None, L, ATTN_WIDTH), row),
                  pl.BlockSpec((None, L, KV_WIDTH), row),
                  pl.BlockSpec((None, L, KV_WIDTH), row)],
        out_specs=pl.BlockSpec((None, L, ATTN_WIDTH), row),
        out_shape=jax.ShapeDtypeStruct((B, L, ATTN_WIDTH), BF16),
        compiler_params=_cparams(("parallel",)),
        name="attention_context",
    )(sink, q, k, v)


def _attention_latent(q, k, v, k_ctx, v_ctx, sink):
    B, S, _ = q.shape
    blk = WINDOW
    nq = ATTN_QBLOCKS
    nb = S // blk
    nctx = k_ctx.shape[1]
    cur = lambda b, n: (b, n, 0)
    prv = lambda b, n: (b, jnp.maximum(n * nq - 1, 0), 0)
    nxt = lambda b, n: (b, jnp.minimum(n * nq + nq, nb - 1), 0)
    ctx = lambda b, n: (b, 0, 0)
    edge = lambda im: pl.BlockSpec((None, blk, KV_WIDTH), im)
    mid = pl.BlockSpec((None, nq * blk, KV_WIDTH), cur)
    return pl.pallas_call(
        _attn_lat_kernel,
        grid=(B, nb // nq),
        in_specs=[pl.BlockSpec(memory_space=pltpu.SMEM),
                  pl.BlockSpec((None, nq * blk, ATTN_WIDTH), cur),
                  edge(prv), mid, edge(nxt), edge(prv), mid, edge(nxt),
                  pl.BlockSpec((None, nctx, KV_WIDTH), ctx),
                  pl.BlockSpec((None, nctx, KV_WIDTH), ctx)],
        out_specs=pl.BlockSpec((None, nq * blk, ATTN_WIDTH), cur),
        out_shape=jax.ShapeDtypeStruct((B, S, ATTN_WIDTH), BF16),
        compiler_params=_cparams(("parallel", "parallel")),
        name="attention_latent",
    )(sink, q, k, k, k, v, v, v, k_ctx, v_ctx)


def _ssm_kernel(u_ref, wb_ref, wc_ref, a_ref, d_ref, h0_ref, y_ref, fs_ref, ut_ref, yt_ref, st_ref, *bu_refs):
    rev = pl.program_id(0)
    i = pl.program_id(2)
    nc = pl.num_programs(2)
    nseq, tm, _ = u_ref.shape
    sw = a_ref.shape[-1] // SSM_STRIPS
    nre = sw // 2

    @pl.when(i == 0)
    def _():
        st_ref[...] = h0_ref[...]

    for b in range(nseq):
        ub = u_ref[b]
        for s in range(SSM_STRIPS):
            ut_ref[s, pl.ds(b, tm, stride=nseq), :] = ub[:, s * LANES:(s + 1) * LANES]
    for s in range(SSM_STRIPS):
        bu_refs[s][...] = jnp.dot(ut_ref[s].astype(BF16), wb_ref[s], preferred_element_type=F32)

    for s in range(SSM_STRIPS):
        bu_ref = bu_refs[s]
        a_re = a_ref[:, s * sw:s * sw + nre]
        a_im = a_ref[:, s * sw + nre:(s + 1) * sw]

        def step(t, carry):
            xr, xi = carry
            r = pl.multiple_of((t + rev * (tm - 1 - 2 * t)) * nseq, nseq)
            nr = a_re * xr - a_im * xi + bu_ref[pl.ds(r, nseq), 0:nre]
            ni = a_re * xi + a_im * xr + bu_ref[pl.ds(r, nseq), nre:sw]
            bu_ref[pl.ds(r, nseq), 0:nre] = nr
            bu_ref[pl.ds(r, nseq), nre:sw] = ni
            return nr, ni

        xr, xi = lax.fori_loop(0, tm, step, (st_ref[:, s * sw:s * sw + nre], st_ref[:, s * sw + nre:(s + 1) * sw]),
                               unroll=True)
        st_ref[:, s * sw:s * sw + nre] = xr
        st_ref[:, s * sw + nre:(s + 1) * sw] = xi
        yt_ref[s] = jnp.dot(bu_ref[...].astype(BF16), wc_ref[s], preferred_element_type=F32)

    for b in range(nseq):
        yb = jnp.concatenate([yt_ref[s, pl.ds(b, tm, stride=nseq), :] for s in range(SSM_STRIPS)], axis=1)
        y_ref[b] = yb + u_ref[b] * d_ref[...]

    @pl.when(i == nc - 1)
    def _():
        fs_ref[...] = st_ref[...]


def _ssm_mix(u, ssm, h0):
    B, L, _ = u.shape
    nseq = SUBLANES
    tm = min(SSM_STEPS, L)
    nc = L // tm
    rows = nseq * tm
    ns = 2 * SSM_GROUPS * SSM_STATE
    chunk = lambda d, g, i: i + d * (nc - 1 - 2 * i)
    y, fs = pl.pallas_call(
        _ssm_kernel,
        grid=(2, B // nseq, nc),
        in_specs=[pl.BlockSpec((nseq, tm, SSM_WIDTH), lambda d, g, i: (g, chunk(d, g, i), 0)),
                  pl.BlockSpec((None, SSM_STRIPS, LANES, ns // SSM_STRIPS), lambda d, g, i: (d, 0, 0, 0)),
                  pl.BlockSpec((None, SSM_STRIPS, ns // SSM_STRIPS, LANES), lambda d, g, i: (d, 0, 0, 0)),
                  pl.BlockSpec((None, nseq, ns), lambda d, g, i: (d, 0, 0)),
                  pl.BlockSpec((None, 1, SSM_WIDTH), lambda d, g, i: (d, 0, 0)),
                  pl.BlockSpec((None, nseq, ns), lambda d, g, i: (d, g, 0))],
        out_specs=[pl.BlockSpec((None, nseq, tm, SSM_WIDTH), lambda d, g, i: (d, g, chunk(d, g, i), 0)),
                   pl.BlockSpec((None, nseq, ns), lambda d, g, i: (d, g, 0))],
        out_shape=[jax.ShapeDtypeStruct((2, B, L, SSM_WIDTH), F32),
                   jax.ShapeDtypeStruct((2, B, ns), F32)],
        scratch_shapes=[pltpu.VMEM((SSM_STRIPS, rows, LANES), F32), pltpu.VMEM((SSM_STRIPS, rows, LANES), F32),
                        pltpu.VMEM((nseq, ns), F32)]
        + [pltpu.VMEM((rows, ns // SSM_STRIPS), F32) for _ in range(SSM_STRIPS)],
        compiler_params=_cparams(("arbitrary", "arbitrary", "arbitrary")),
        name="ssm_scan",
    )(u, ssm["wb"], ssm["wc"], ssm["a"], ssm["d"], h0)
    return y, fs


def _state_to_lanes(s):
    lead = s.shape[:-3]
    s = s.reshape(lead + (2, SSM_STRIPS, SSM_GROUPS // SSM_STRIPS, SSM_STATE))
    s = jnp.swapaxes(s, -4, -3)
    return s.reshape(lead + (2 * SSM_GROUPS * SSM_STATE,))


def _lanes_to_state(v):
    lead = v.shape[:-1]
    s = v.reshape(lead + (SSM_STRIPS, 2, SSM_GROUPS // SSM_STRIPS, SSM_STATE))
    s = jnp.swapaxes(s, -4, -3)
    return s.reshape(lead + (2, SSM_GROUPS, SSM_STATE))


def _ssm_params(a_re, a_im, log_dt, b_re, b_im, c_re, c_im, dvec):
    G, N, C = SSM_GROUPS, SSM_STATE, SSM_GROUP
    lam_re = jnp.minimum(a_re, -1e-4)
    lam_im = a_im
    dt = jnp.exp(log_dt)[..., None]
    mag = jnp.exp(lam_re * dt)
    abar_re, abar_im = mag * jnp.cos(lam_im * dt), mag * jnp.sin(lam_im * dt)
    den = jnp.square(lam_re) + jnp.square(lam_im)
    p, qi = abar_re - 1.0, abar_im
    f_re = (p * lam_re + qi * lam_im) / den
    f_im = (qi * lam_re - p * lam_im) / den
    bbar_re = f_re[..., None] * b_re - f_im[..., None] * b_im
    bbar_im = f_re[..., None] * b_im + f_im[..., None] * b_re
    a = _state_to_lanes(jnp.stack([abar_re, abar_im], axis=1))
    a = jnp.broadcast_to(a[:, None, :], (2, SUBLANES, a.shape[-1]))
    S = SSM_STRIPS
    gs = G // S
    eye = jnp.eye(gs, dtype=F32)

    def bd_in(bb):
        bb = bb.reshape(2, S, gs, N, C)
        return jnp.einsum('dkgnc,gh->dkgchn', bb, eye).reshape(2, S, gs * C, gs * N)

    def bd_out(cc):
        cc = cc.reshape(2, S, gs, C, N)
        return jnp.einsum('dkgcn,gh->dkgnhc', cc, eye).reshape(2, S, gs * N, gs * C)

    wb = jnp.concatenate([bd_in(bbar_re), bd_in(bbar_im)], axis=-1).astype(BF16)
    wc = jnp.concatenate([bd_out(c_re), -bd_out(c_im)], axis=-2).astype(BF16)
    d = jnp.stack([dvec, jnp.zeros_like(dvec)], axis=0).reshape(2, 1, SSM_WIDTH)
    return {"a": a, "wb": wb, "wc": wc, "d": d}


def _layer_norm(x, g, b):
    mu = jnp.mean(x, axis=-1, keepdims=True)
    xc = x - mu
    var = jnp.mean(xc * xc, axis=-1, keepdims=True)
    return xc * lax.rsqrt(var + LN_EPS) * g + b


def _pack_rows(x):
    w = x.shape[1] // 2
    return pltpu.pack_elementwise([x[:, :w], x[:, w:]], packed_dtype=BF16)


def _unpack_rows(p):
    return (pltpu.unpack_elementwise(p, index=0, packed_dtype=BF16, unpacked_dtype=F32),
            pltpu.unpack_elementwise(p, index=1, packed_dtype=BF16, unpacked_dtype=F32))


def _gelu_tanh(x):
    return 0.5 * x * (1.0 + jnp.tanh(math.sqrt(2.0 / math.pi) * (x + 0.044715 * (x * x * x))))


def _merge_kernel(x_ref, ao_ref, yf_ref, yb_ref, ga_ref, gs_ref, mod_ref, wglu_ref, wa_ref, ws_ref, wo_ref,
                  lng_ref, lnb_ref, rwt_ref, rb_ref, run0_ref, x1_ref, h2_ref, idx_ref, rank_ref, wt_ref, cnt_ref,
                  run_ref, *, alpha):
    @pl.when((pl.program_id(0) == 0) & (pl.program_id(1) == 0))
    def _():
        run_ref[...] = run0_ref[...].astype(F32)

    nrow = x_ref.shape[0]
    sub = nrow // MERGE_SPLIT
    groups = [slice(j * sub, (j + 1) * sub) for j in range(MERGE_SPLIT)]
    dot = functools.partial(jnp.dot, preferred_element_type=F32)
    z = [_gelu_tanh(yf_ref[rs, :] + yb_ref[rs, :]) for rs in groups]
    attn_br = [dot(ao_ref[rs, :], wa_ref[...]) for rs in groups]
    gate = [jax.nn.sigmoid(dot(zg.astype(BF16), wglu_ref[...])) for zg in z]
    ssm_br = [dot((zg * gg).astype(BF16), ws_ref[...]) for zg, gg in zip(z, gate)]
    merged = [jax.nn.sigmoid(ga_ref[rs, :].astype(F32)) * ab + jax.nn.sigmoid(gs_ref[rs, :].astype(F32)) * sb
              for rs, ab, sb in zip(groups, attn_br, ssm_br)]
    mix = [dot(mg.astype(BF16), wo_ref[...]) for mg in merged]
    h2s = []
    for rs, mg in zip(groups, mix):
        x1 = _layer_norm(alpha * x_ref[rs, :] + mod_ref[2:3, :] * mg, lng_ref[...], lnb_ref[...])
        h2g = x1 * (1.0 + mod_ref[4:5, :]) + mod_ref[3:4, :]
        x1_ref[rs, :] = x1
        h2_ref[rs, :] = _pack_rows(h2g)
        h2s.append(h2g)
    h2 = jnp.concatenate(h2s, axis=0)

    logits = lax.dot_general(rwt_ref[...], h2, (((1,), (1,)), ((), ())), preferred_element_type=F32,
                             precision=lax.Precision.HIGHEST)
    score = jax.nn.sigmoid(logits)
    tm = score.shape[1]
    eidx = lax.broadcasted_iota(jnp.int32, score.shape, 0).astype(F32)
    work = score + rb_ref[...]
    picks, sel = [], []
    member = jnp.zeros_like(score)
    for _ in range(TOP_K):
        best = jnp.max(work, axis=0, keepdims=True)
        pick = jnp.min(jnp.where(work == best, eidx, float(N_EXPERTS)), axis=0, keepdims=True)
        hit = eidx == pick
        picks.append(pick)
        sel.append(jnp.sum(jnp.where(hit, score, 0.0), axis=0, keepdims=True))
        member = member + hit.astype(F32)
        work = jnp.where(hit, -jnp.inf, work)
    total = sel[0]
    for s in sel[1:]:
        total = total + s
    before = (lax.broadcasted_iota(jnp.int32, (tm, tm), 0) < lax.broadcasted_iota(jnp.int32, (tm, tm), 1))
    prefix = jnp.dot(member.astype(BF16), before.astype(BF16), preferred_element_type=F32)
    base = prefix + run_ref[...]
    for k in range(TOP_K):
        idx_ref[k:k + 1, :] = picks[k].astype(jnp.int32)
        rank_ref[k:k + 1, :] = jnp.sum(jnp.where(eidx == picks[k], base, 0.0), axis=0,
                                       keepdims=True).astype(jnp.int32)
        wt_ref[k:k + 1, :] = sel[k] / total * ROUTED_SCALE
    run_ref[...] = run_ref[...] + jnp.sum(member, axis=1, keepdims=True)
    cnt_ref[...] = run_ref[...].astype(jnp.int32)


def _merge(x, attn_o, y, ga, gs, mod, wts, run0, alpha):
    B, L, D = x.shape
    tm = min(MERGE_ROWS, L)
    per_batch = mod.shape[0] > 1
    row = lambda b, i: (b, i, 0)
    full = lambda b, i: (0, 0)
    wspec = lambda a: pl.BlockSpec(a.shape, full)
    return pl.pallas_call(
        functools.partial(_merge_kernel, alpha=alpha),
        grid=(B, L // tm),
        in_specs=[pl.BlockSpec((None, tm, D), row),
                  pl.BlockSpec((None, tm, ATTN_WIDTH), row),
                  pl.BlockSpec((None, None, tm, SSM_WIDTH), lambda b, i: (0, b, i, 0)),
                  pl.BlockSpec((None, None, tm, SSM_WIDTH), lambda b, i: (1, b, i, 0)),
                  pl.BlockSpec((None, tm, D), row),
                  pl.BlockSpec((None, tm, D), row),
                  pl.BlockSpec((None, N_MOD, D), (lambda b, i: (b, 0, 0)) if per_batch else (lambda b, i: (0, 0, 0))),
                  wspec(wts["w_glu"]), wspec(wts["w_attn_br"]), wspec(wts["w_ssm_br"]), wspec(wts["w_out"]),
                  wspec(wts["ln1_g"]), wspec(wts["ln1_b"]), wspec(wts["router_wt"]), wspec(wts["router_b"]),
                  wspec(run0)],
        out_specs=[pl.BlockSpec((None, tm, D), row),
                   pl.BlockSpec((None, tm, D // 2), row),
                   pl.BlockSpec((None, TOP_K, tm), lambda b, i: (b, 0, i)),
                   pl.BlockSpec((None, TOP_K, tm), lambda b, i: (b, 0, i)),
                   pl.BlockSpec((None, TOP_K, tm), lambda b, i: (b, 0, i)),
                   pl.BlockSpec((N_EXPERTS, 1), full)],
        out_shape=[jax.ShapeDtypeStruct((B, L, D), F32),
                   jax.ShapeDtypeStruct((B, L, D // 2), jnp.int32),
                   jax.ShapeDtypeStruct((B, TOP_K, L), jnp.int32),
                   jax.ShapeDtypeStruct((B, TOP_K, L), jnp.int32),
                   jax.ShapeDtypeStruct((B, TOP_K, L), F32),
                   jax.ShapeDtypeStruct((N_EXPERTS, 1), jnp.int32)],
        scratch_shapes=[pltpu.VMEM((N_EXPERTS, 1), F32)],
        compiler_params=_cparams(("arbitrary", "arbitrary")),
        name="merge_ln1_router",
    )(x, attn_o, y, y, ga, gs, mod, wts["w_glu"], wts["w_attn_br"], wts["w_ssm_br"], wts["w_out"],
      wts["ln1_g"], wts["ln1_b"], wts["router_wt"], wts["router_b"], run0)


def _dispatch(h2s, dests):
    W = h2s[0].shape[-1]
    n_slots = sum(d.size for d in dests)
    mesh = plsc.VectorSubcoreMesh(core_axis_name="core", subcore_axis_name="subcore")
    workers = mesh.num_cores * mesh.num_subcores
    tokens = [h.shape[0] * h.shape[1] for h in h2s]
    per_worker = [t // SC_WINDOW // workers for t in tokens]
    assert all(p * SC_WINDOW * workers == t for p, t in zip(per_worker, tokens))

    @pl.kernel(out_type=jax.ShapeDtypeStruct((n_slots, W), h2s[0].dtype), mesh=mesh,
               scratch_types=[pltpu.VMEM((TOP_K, SC_WINDOW), jnp.int32), pltpu.VMEM((SC_WINDOW, W), h2s[0].dtype)])
    def scatter(*refs):
        out_hbm, index_v, rows_v = refs[-3:]
        worker = lax.axis_index("core") * mesh.num_subcores + lax.axis_index("subcore")
        for g, per in enumerate(per_worker):
            rows_hbm, slots_hbm = refs[2 * g], refs[2 * g + 1]

            @pl.loop(0, per)
            def _(j, rows_hbm=rows_hbm, slots_hbm=slots_hbm, per=per):
                window = worker * per + j
                pltpu.sync_copy(rows_hbm.at[pl.ds(window * SC_WINDOW, SC_WINDOW)], rows_v)
                pltpu.sync_copy(slots_hbm.at[window], index_v)
                for k in range(TOP_K):
                    pltpu.sync_copy(rows_v, out_hbm.at[index_v.at[k]])

    args = []
    for h, d in zip(h2s, dests):
        nb, _, ln = d.shape
        slots = d.reshape(nb, TOP_K, ln // SC_WINDOW, SC_WINDOW).transpose(0, 2, 1, 3)
        args += [h.reshape(-1, W), slots.reshape(nb * ln // SC_WINDOW, TOP_K, SC_WINDOW)]
    return scatter(*args)


def _gmm_kernel(tile_ref, exp_ref, valid_ref, gstart_ref, xs_ref, wg_ref, wu_ref, wd_ref, ys_ref,
                wgb_ref, wub_ref, wdb_ref):
    w = pl.program_id(0)
    e = exp_ref[w]
    t = tile_ref[w]
    prev = jnp.maximum(w - 1, 0)
    new_expert = (w == 0) | (e != exp_ref[prev])
    first_visit = (w == 0) | (t != tile_ref[prev])

    @pl.when(new_expert)
    def _():
        wgb_ref[...] = wg_ref[...].astype(BF16)
        wub_ref[...] = wu_ref[...].astype(BF16)
        wdb_ref[...] = wd_ref[...].astype(BF16)

    @pl.when(first_visit)
    def _():
        ys_ref[...] = jnp.zeros_like(ys_ref)

    tg = xs_ref.shape[0]
    sub = tg // GMM_SPLIT
    lo = gstart_ref[e]
    hi = gstart_ref[e + 1]
    for j in range(GMM_SPLIT):
        first_row = t * tg + j * sub

        @pl.when((valid_ref[w] == 1) & (lo < first_row + sub) & (hi > first_row))
        def _(j=j, first_row=first_row):
            rs = slice(j * sub, (j + 1) * sub)
            dot = functools.partial(jnp.dot, preferred_element_type=F32)
            x = jnp.concatenate(_unpack_rows(xs_ref[rs, :]), axis=1).astype(BF16)
            g = dot(x, wgb_ref[...])
            u = dot(x, wub_ref[...])
            y = _pack_rows(dot((g * jax.nn.sigmoid(g) * u).astype(BF16), wdb_ref[...]))
            rows = first_row + lax.broadcasted_iota(jnp.int32, (sub, 1), 0)
            mine = (rows >= lo) & (rows < hi)
            ys_ref[rs, :] = jnp.where(mine, y, ys_ref[rs, :])


def _grouped_experts(xs, counts, w_gate, w_up, w_down):
    A, W = xs.shape
    D = w_gate.shape[-2]
    tg = GMM_ROWS
    nt = A // tg
    n_items = nt + N_EXPERTS - 1
    ff = w_gate.shape[-1]
    gend = jnp.cumsum(counts).astype(jnp.int32)
    gstart = jnp.concatenate([jnp.zeros((1,), jnp.int32), gend])
    first_row = jnp.arange(nt, dtype=jnp.int32) * tg
    count_le = lambda ends, v: jnp.sum((ends[None, :] <= v[:, None]).astype(jnp.int32), axis=1)
    e_lo = count_le(gend, first_row)
    e_hi = count_le(gend, first_row + tg - 1)
    per_tile = e_hi - e_lo + 1
    item_end = jnp.cumsum(per_tile).astype(jnp.int32)
    total = item_end[-1]
    wi = jnp.arange(n_items, dtype=jnp.int32)
    tile = jnp.minimum(count_le(item_end, wi), nt - 1)
    in_tile = (tile[:, None] == jnp.arange(nt, dtype=jnp.int32)[None, :]).astype(jnp.int32)
    lookup = lambda table: jnp.sum(in_tile * table[None, :], axis=1)
    expert = lookup(e_lo) + wi - lookup(item_end - per_tile)
    valid = (wi < total).astype(jnp.int32)
    expert = jnp.where(valid == 1, expert, e_hi[nt - 1])
    grid_spec = pltpu.PrefetchScalarGridSpec(
        num_scalar_prefetch=4,
        grid=(n_items,),
        in_specs=[pl.BlockSpec((tg, W), lambda w, tl, ex, va, gs: (tl[w], 0)),
                  pl.BlockSpec((None, D, ff), lambda w, tl, ex, va, gs: (ex[w], 0, 0)),
                  pl.BlockSpec((None, D, ff), lambda w, tl, ex, va, gs: (ex[w], 0, 0)),
                  pl.BlockSpec((None, ff, D), lambda w, tl, ex, va, gs: (ex[w], 0, 0))],
        out_specs=pl.BlockSpec((tg, W), lambda w, tl, ex, va, gs: (tl[w], 0)),
        scratch_shapes=[pltpu.VMEM((D, ff), BF16), pltpu.VMEM((D, ff), BF16), pltpu.VMEM((ff, D), BF16)],
    )
    return pl.pallas_call(
        _gmm_kernel,
        grid_spec=grid_spec,
        out_shape=jax.ShapeDtypeStruct((A, W), xs.dtype),
        compiler_params=_cparams(("arbitrary",)),
        name="moe_grouped_experts",
    )(tile, expert, valid, gstart, xs, w_gate, w_up, w_down)


def _gather_rows_sc(rows, index):
    n = index.shape[0]
    w = rows.shape[1]
    mesh = plsc.VectorSubcoreMesh(core_axis_name="core", subcore_axis_name="subcore")
    per_subcore = n // SC_WINDOW // (mesh.num_cores * mesh.num_subcores)
    assert per_subcore * SC_WINDOW * mesh.num_cores * mesh.num_subcores == n

    @pl.kernel(out_type=jax.ShapeDtypeStruct((n, w), rows.dtype), mesh=mesh,
               scratch_types=[pltpu.VMEM((1, SC_WINDOW), jnp.int32), pltpu.VMEM((SC_WINDOW, w), rows.dtype)])
    def gather(rows_hbm, index_hbm, out_hbm, index_v, rows_v):
        worker = lax.axis_index("core") * mesh.num_subcores + lax.axis_index("subcore")

        @pl.loop(0, per_subcore)
        def _(j):
            first = (worker * per_subcore + j) * SC_WINDOW
            pltpu.sync_copy(index_hbm.at[:, pl.ds(first, SC_WINDOW)], index_v)
            pltpu.sync_copy(rows_hbm.at[index_v.at[0]], rows_v)
            pltpu.sync_copy(rows_v, out_hbm.at[pl.ds(first, SC_WINDOW)])

    return gather(rows, index.reshape(1, n))


def _combine_rows_kernel(yg_ref, wt_ref, x1_ref, h2_ref, mod_ref, sg_ref, su_ref, sd_ref, lng_ref, lnb_ref, *rest,
                         alpha):
    o_ref = rest[-1]
    h = jnp.concatenate(_unpack_rows(h2_ref[...]), axis=1).astype(BF16)
    g = jnp.dot(h, sg_ref[...], preferred_element_type=F32)
    u = jnp.dot(h, su_ref[...], preferred_element_type=F32)
    moe = jnp.dot((g * jax.nn.sigmoid(g) * u).astype(BF16), sd_ref[...], preferred_element_type=F32)
    wt = wt_ref[...]
    lo = jnp.zeros(h2_ref.shape, F32)
    hi = lo
    for k in range(TOP_K):
        rl, rh = _unpack_rows(yg_ref[k])
        lo = lo + wt[:, k:k + 1] * rl
        hi = hi + wt[:, k:k + 1] * rh
    moe = moe + jnp.concatenate([lo, hi], axis=1)
    o_ref[...] = _layer_norm(alpha * x1_ref[...] + mod_ref[5:6, :] * moe, lng_ref[...], lnb_ref[...])


def _combine_rows(ys, dest, wts, x1, h2, mod, shared, alpha):
    B, L, D = x1.shape
    W = ys.shape[-1]
    tm = min(COMBINE_ROWS, L)
    per_batch = mod.shape[0] > 1
    pieces = max(1, B * L // COMBINE_PIECE_TOKENS)
    nb = B // pieces
    full = lambda b, i: (0, 0)
    wspec = lambda a: pl.BlockSpec(a.shape, full)
    wts_t = jnp.swapaxes(wts, 1, 2)
    out = None
    for p in range(pieces):
        b0 = p * nb
        row = lambda b, i, b0=b0: (b0 + b, i, 0)
        slots = jnp.swapaxes(dest[b0:b0 + nb], 0, 1).reshape(TOP_K * nb * L)
        yg = _gather_rows_sc(ys, slots).reshape(TOP_K, nb, L, W)
        in_specs = [pl.BlockSpec((TOP_K, None, tm, W), lambda b, i: (0, b, i, 0)),
                    pl.BlockSpec((None, tm, TOP_K), row),
                    pl.BlockSpec((None, tm, D), row),
                    pl.BlockSpec((None, tm, W), row),
                    pl.BlockSpec((None, N_MOD, D), (lambda b, i, b0=b0: (b0 + b, 0, 0)) if per_batch
                                 else (lambda b, i: (0, 0, 0))),
                    wspec(shared["sh_w_gate"]), wspec(shared["sh_w_up"]), wspec(shared["sh_w_down"]),
                    wspec(shared["ln2_g"]), wspec(shared["ln2_b"])]
        args = [yg, wts_t, x1, h2, mod, shared["sh_w_gate"], shared["sh_w_up"], shared["sh_w_down"],
                shared["ln2_g"], shared["ln2_b"]]
        aliases = {}
        if out is not None:
            in_specs.append(pl.BlockSpec(memory_space=pl.ANY))
            args.append(out)
            aliases = {len(args) - 1: 0}
        out = pl.pallas_call(
            functools.partial(_combine_rows_kernel, alpha=alpha),
            grid=(nb, L // tm),
            in_specs=in_specs,
            out_specs=pl.BlockSpec((None, tm, D), row),
            out_shape=jax.ShapeDtypeStruct((B, L, D), F32),
            input_output_aliases=aliases,
            compiler_params=_cparams(("parallel", "parallel")),
            name="moe_combine_ln2",
        )(*args)
    return out


def _slots(idx, rank, counts):
    start = jnp.cumsum(counts) - counts
    pick = idx[..., None] == jnp.arange(N_EXPERTS, dtype=jnp.int32)
    return jnp.sum(jnp.where(pick, start, 0), axis=-1).astype(jnp.int32) + rank


def _mixer_and_router(x, mod, wts, ssm, cache, run0, alpha):
    B, L, D = x.shape
    latent = cache is not None
    if latent:
        k_ctx, v_ctx, h0 = cache
        cos, sin = _rope_cos_sin(L)
    else:
        cos = jnp.zeros((L, LANES), F32)
        sin = cos
        h0 = jnp.zeros((2, B, 2 * SSM_GROUPS * SSM_STATE), F32)
    q, k, v, u, ga, gs = _in_projection(x, mod, wts["w_in"], cos, sin, rope=latent)
    if latent:
        attn_o = _attention_latent(q, k, v, k_ctx, v_ctx, wts["attn_sink"])
    else:
        attn_o = _attention_context(q, k, v, wts["attn_sink"])
    y, fs = _ssm_mix(u, ssm, h0)
    x1, h2, idx, rank, rw, counts = _merge(x, attn_o, y, ga, gs, mod, wts, run0, alpha)
    return {"x1": x1, "h2": h2, "idx": idx, "rank": rank, "rw": rw, "mod": mod}, counts, k, v, fs


def _moe(groups, counts, wts, alpha):
    counts = counts.reshape(N_EXPERTS)
    for g in groups:
        g["dest"] = _slots(g["idx"], g["rank"], counts)
    xs = _dispatch([g["h2"] for g in groups], [g["dest"] for g in groups])
    ys = _grouped_experts(xs, counts, wts["exp_w_gate"], wts["exp_w_up"], wts["exp_w_down"])
    return [_combine_rows(ys, g["dest"], g["rw"], g["x1"], g["h2"], g["mod"], wts, alpha) for g in groups]


def kernel(x_prompt, x_sample, cache_k, cache_v, state_ssm, c, c_ctx, mod_w, mod_b, w_in, attn_sink, w_attn_br, ssm_a_re, ssm_a_im, ssm_log_dt, ssm_b_re, ssm_b_im, ssm_c_re, ssm_c_im, ssm_d, w_glu, w_ssm_br, w_out, ln1_g, ln1_b, ln2_g, ln2_b, router_w, router_b, exp_w_gate, exp_w_up, exp_w_down, sh_w_gate, sh_w_up, sh_w_down):
    depth = w_in.shape[0]
    assert depth == 1
    alpha = (2.0 * depth) ** 0.25
    D = x_prompt.shape[-1]
    nb_p = x_prompt.shape[0]
    nb_s = x_sample.shape[0]
    l = 0

    ncond = 1 + nb_s
    npad = -ncond % SUBLANES
    cond = jnp.concatenate([c_ctx[None, :], c, jnp.zeros((npad, D), F32)], axis=0)
    mod = _modulation(cond, mod_w[l], mod_b[l]).reshape(ncond + npad, N_MOD, D)
    mod_ctx, mod_lat = mod[0:1], mod[1:ncond]

    o1 = ATTN_WIDTH
    o3 = o1 + 2 * KV_WIDTH
    o4 = o3 + SSM_WIDTH
    wi = w_in[l]
    scale = HEAD_DIM ** -0.5 * LOG2_E
    wts = {
        "w_in": jnp.concatenate([wi[:, :o1] * scale, wi[:, o1:]], axis=1).astype(BF16),
        "attn_sink": attn_sink[l],
        "w_glu": w_glu[l].astype(BF16), "w_attn_br": w_attn_br[l].astype(BF16),
        "w_ssm_br": w_ssm_br[l].astype(BF16), "w_out": w_out[l].astype(BF16),
        "ln1_g": ln1_g[l].reshape(1, D), "ln1_b": ln1_b[l].reshape(1, D),
        "ln2_g": ln2_g[l].reshape(1, D), "ln2_b": ln2_b[l].reshape(1, D),
        "router_wt": router_w[l].T, "router_b": router_b[l].reshape(N_EXPERTS, 1),
        "exp_w_gate": exp_w_gate[l], "exp_w_up": exp_w_up[l], "exp_w_down": exp_w_down[l],
        "sh_w_gate": sh_w_gate[l].astype(BF16), "sh_w_up": sh_w_up[l].astype(BF16),
        "sh_w_down": sh_w_down[l].astype(BF16),
    }
    ssm = _ssm_params(ssm_a_re[l], ssm_a_im[l], ssm_log_dt[l], ssm_b_re[l], ssm_b_im[l],
                      ssm_c_re[l], ssm_c_im[l], ssm_d[l])

    no_tokens_yet = jnp.zeros((N_EXPERTS, 1), jnp.int32)
    grp_p, counts, k_p, v_p, fs_p = _mixer_and_router(x_prompt, mod_ctx, wts, ssm, None, no_tokens_yet, alpha)

    past = cache_k.shape[2]
    k_ctx = cache_k[:, l].reshape(nb_s, past, KV_WIDTH)
    v_ctx = cache_v[:, l].reshape(nb_s, past, KV_WIDTH)
    h0 = jnp.swapaxes(_state_to_lanes(state_ssm[:, l]), 0, 1)
    grp_s, counts, _, _, _ = _mixer_and_router(x_sample, mod_lat, wts, ssm, (k_ctx, v_ctx, h0), counts, alpha)
    yp, ys_ = _moe([grp_p, grp_s], counts, wts, alpha)

    Lp = x_prompt.shape[1]
    new_k = k_p.reshape(nb_p, 1, Lp, N_KV_HEADS, HEAD_DIM)
    new_v = v_p.reshape(nb_p, 1, Lp, N_KV_HEADS, HEAD_DIM)
    new_s = _lanes_to_state(jnp.swapaxes(fs_p, 0, 1))[:, None]
    return (yp, ys_, new_k, new_v, new_s)
```

```python
import functools
import math

import jax
import jax.numpy as jnp
from jax import lax
from jax.experimental import pallas as pl
from jax.experimental.pallas import tpu as pltpu
from jax.experimental.pallas import tpu_sc as plsc

F32 = jnp.float32
BF16 = jnp.bfloat16

GRID_W = 64
HEAD_DIM = 64
N_Q_HEADS = 8
N_KV_HEADS = 2
Q_PER_KV = N_Q_HEADS // N_KV_HEADS
ATTN_WIDTH = N_Q_HEADS * HEAD_DIM
KV_WIDTH = N_KV_HEADS * HEAD_DIM
WINDOW = 128
ROPE_BASE = 10000.0
ROPE_FREQS = HEAD_DIM // 4
SSM_WIDTH = 512
SSM_GROUP = 16
SSM_GROUPS = SSM_WIDTH // SSM_GROUP
SSM_STATE = 64
N_EXPERTS = 64
TOP_K = 6
ROUTED_SCALE = 2.5
N_MOD = 6
LN_EPS = 1e-5
LOG2_E = math.log2(math.e)

SUBLANES = 8
LANES = 128
VMEM_LIMIT = 48 * 1024 * 1024

MOD_COLS = 512
INPROJ_ROWS = 1024
ATTN_QBLOCKS = 8
SSM_STEPS = 128
SSM_STRIPS = 4
MERGE_ROWS = 512
MERGE_SPLIT = 2
GMM_ROWS = 2048
GMM_SPLIT = 4
COMBINE_ROWS = 512
COMBINE_PIECE_TOKENS = 8192
SC_WINDOW = 128


def _cparams(sem):
    return pltpu.CompilerParams(dimension_semantics=sem, vmem_limit_bytes=VMEM_LIMIT)


def _mod_kernel(c_ref, w_ref, b_ref, o_ref):
    c = c_ref[...]
    s = c * jax.nn.sigmoid(c)
    o_ref[...] = jnp.dot(s, w_ref[...], preferred_element_type=F32,
                         precision=lax.Precision.HIGHEST) + b_ref[...]


def _modulation(cond, w, b):
    n, d = cond.shape
    nout = w.shape[1]
    tn = MOD_COLS
    return pl.pallas_call(
        _mod_kernel,
        grid=(nout // tn,),
        in_specs=[pl.BlockSpec((n, d), lambda j: (0, 0)),
                  pl.BlockSpec((d, tn), lambda j: (0, j)),
                  pl.BlockSpec((1, tn), lambda j: (0, j))],
        out_specs=pl.BlockSpec((n, tn), lambda j: (0, j)),
        out_shape=jax.ShapeDtypeStruct((n, nout), F32),
        compiler_params=_cparams(("arbitrary",)),
        name="modulation",
    )(cond, w, b.reshape(1, nout))


def _rope_rotate(t, cos, sin):
    lane = lax.broadcasted_iota(jnp.int32, t.shape, 1)
    first_of_pair = (lane % (2 * ROPE_FREQS)) < ROPE_FREQS
    partner = jnp.where(first_of_pair, pltpu.roll(t, LANES - ROPE_FREQS, 1), pltpu.roll(t, ROPE_FREQS, 1))
    return t * cos + partner * sin


def _inproj_kernel(x_ref, mod_ref, w_ref, cos_ref, sin_ref, q_ref, k_ref, v_ref, u_ref, ga_ref, gs_ref, *, rope):
    x = x_ref[...]
    h = (x * (1.0 + mod_ref[1:2, :]) + mod_ref[0:1, :]).astype(BF16)
    p = jnp.dot(h, w_ref[...], preferred_element_type=F32)
    o1 = ATTN_WIDTH
    o2 = o1 + KV_WIDTH
    o3 = o2 + KV_WIDTH
    d = x.shape[1]
    q = p[:, :o1]
    k = p[:, o1:o2]
    if rope:
        cos = cos_ref[...]
        sin = sin_ref[...]
        q = jnp.concatenate([_rope_rotate(q[:, j * LANES:(j + 1) * LANES], cos, sin)
                             for j in range(o1 // LANES)], axis=1)
        k = _rope_rotate(k, cos, sin)
    o4 = o3 + SSM_WIDTH
    q_ref[...] = q.astype(BF16)
    k_ref[...] = k
    v_ref[...] = p[:, o2:o3]
    u_ref[...] = p[:, o3:o4]
    ga_ref[...] = p[:, o4:o4 + d].astype(BF16)
    gs_ref[...] = p[:, o4 + d:].astype(BF16)


def _in_projection(x, mod, w, cos, sin, rope):
    B, L, D = x.shape
    tm = min(INPROJ_ROWS, L)
    per_batch = mod.shape[0] > 1
    nw = w.shape[1]
    row = lambda b, i: (b, i, 0)
    outs = pl.pallas_call(
        functools.partial(_inproj_kernel, rope=rope),
        grid=(B, L // tm),
        in_specs=[pl.BlockSpec((None, tm, D), row),
                  pl.BlockSpec((None, N_MOD, D), (lambda b, i: (b, 0, 0)) if per_batch else (lambda b, i: (0, 0, 0))),
                  pl.BlockSpec((D, nw), lambda b, i: (0, 0)),
                  pl.BlockSpec((tm, LANES), lambda b, i: (i, 0)),
                  pl.BlockSpec((tm, LANES), lambda b, i: (i, 0))],
        out_specs=[pl.BlockSpec((None, tm, ATTN_WIDTH), row),
                   pl.BlockSpec((None, tm, KV_WIDTH), row),
                   pl.BlockSpec((None, tm, KV_WIDTH), row),
                   pl.BlockSpec((None, tm, SSM_WIDTH), row),
                   pl.BlockSpec((None, tm, D), row),
                   pl.BlockSpec((None, tm, D), row)],
        out_shape=[jax.ShapeDtypeStruct((B, L, ATTN_WIDTH), BF16),
                   jax.ShapeDtypeStruct((B, L, KV_WIDTH), F32),
                   jax.ShapeDtypeStruct((B, L, KV_WIDTH), F32),
                   jax.ShapeDtypeStruct((B, L, SSM_WIDTH), F32),
                   jax.ShapeDtypeStruct((B, L, D), BF16),
                   jax.ShapeDtypeStruct((B, L, D), BF16)],
        compiler_params=_cparams(("parallel", "parallel")),
        name="in_projection",
    )(x, mod, w, cos, sin)
    return outs


def _rope_cos_sin(n_tokens):
    t = jnp.arange(n_tokens, dtype=jnp.int32)
    pos = jnp.stack([t // GRID_W, t % GRID_W], axis=-1).astype(F32)
    n_freq = ROPE_FREQS
    inv_freq = ROPE_BASE ** (-jnp.arange(n_freq, dtype=F32) / n_freq)
    ang = pos[:, :, None] * inv_freq
    c, s = jnp.cos(ang), jnp.sin(ang)
    cos = jnp.concatenate([c[:, 0], c[:, 0], c[:, 1], c[:, 1]], axis=-1)
    sin = jnp.concatenate([-s[:, 0], s[:, 0], -s[:, 1], s[:, 1]], axis=-1)
    return jnp.tile(cos, (1, LANES // HEAD_DIM)), jnp.tile(sin, (1, LANES // HEAD_DIM))


def _attend(sink_ref, q, kcat, vcat, mask, o_ref, row0=0):
    lq = q.shape[0]
    lane = lax.broadcasted_iota(jnp.int32, (1, LANES), 1)
    low = lane < HEAD_DIM
    k_sw = pltpu.roll(kcat, HEAD_DIM, 1)
    v_sw = pltpu.roll(vcat, HEAD_DIM, 1)
    neg = jnp.finfo(F32).min
    scores, vds = [], []
    for h in range(N_KV_HEADS):
        keep = low if h == 0 else jnp.logical_not(low)
        kd = jnp.where(keep, kcat, k_sw).astype(BF16)
        vds.append(jnp.where(keep, vcat, v_sw).astype(BF16))
        qs = []
        for j in range(Q_PER_KV):
            head = h * Q_PER_KV + j
            blk = q[:, (head // 2) * LANES:(head // 2 + 1) * LANES]
            sel = low if head % 2 == 0 else jnp.logical_not(low)
            qs.append(jnp.where(sel, blk, jnp.zeros_like(blk)))
        qstack = jnp.concatenate(qs, axis=0)
        scores.append(lax.dot_general(qstack, kd, (((1,), (1,)), ((), ())), preferred_element_type=F32))
    probs, denoms = [], []
    for h in range(N_KV_HEADS):
        s = scores[h]
        ps, ls = [], []
        for j in range(Q_PER_KV):
            sj = s[j * lq:(j + 1) * lq]
            if mask is not None:
                lm = mask.shape[1]
                sj = jnp.concatenate([jnp.where(mask, sj[:, :lm], neg), sj[:, lm:]], axis=1)
            sink = sink_ref[h * Q_PER_KV + j] * LOG2_E
            m = jnp.maximum(jnp.max(sj, axis=1, keepdims=True), sink)
            pj = jnp.exp2(sj - m)
            ls.append(jnp.sum(pj, axis=1, keepdims=True) + jnp.exp2(sink - m))
            ps.append(pj.astype(BF16))
        probs.append(jnp.concatenate(ps, axis=0))
        denoms.append(ls)
    for h in range(N_KV_HEADS):
        ls = denoms[h]
        o = jnp.dot(probs[h], vds[h], preferred_element_type=F32)
        for jj in range(Q_PER_KV // 2):
            oe = o[(2 * jj) * lq:(2 * jj + 1) * lq] / ls[2 * jj]
            oo = o[(2 * jj + 1) * lq:(2 * jj + 2) * lq] / ls[2 * jj + 1]
            cb = (h * Q_PER_KV) // 2 + jj
            o_ref[row0:row0 + lq, cb * LANES:(cb + 1) * LANES] = jnp.where(low, oe, oo).astype(o_ref.dtype)


def _attn_ctx_kernel(sink_ref, q_ref, k_ref, v_ref, o_ref):
    _attend(sink_ref, q_ref[...], k_ref[...], v_ref[...], None, o_ref)


def _attn_lat_kernel(sink_ref, q_ref, kp_ref, kc_ref, kn_ref, vp_ref, vc_ref, vn_ref, kx_ref, vx_ref, o_ref):
    n = pl.program_id(1)
    nb = pl.num_programs(1)
    blk = kp_ref.shape[0]
    nq = q_ref.shape[0] // blk
    klocal = jnp.concatenate([kp_ref[...], kc_ref[...], kn_ref[...]], axis=0)
    vlocal = jnp.concatenate([vp_ref[...], vc_ref[...], vn_ref[...]], axis=0)
    qi = lax.broadcasted_iota(jnp.int32, (blk, 3 * blk), 0)
    kj = lax.broadcasted_iota(jnp.int32, (blk, 3 * blk), 1)
    rel = kj - blk - qi
    in_band = (rel <= WINDOW) & (rel >= -WINDOW)
    for j in range(nq):
        mask = in_band
        if j == 0:
            mask = mask & ((kj >= blk) | (n > 0))
        if j == nq - 1:
            mask = mask & ((kj < 2 * blk) | (n < nb - 1))
        kcat = jnp.concatenate([klocal[j * blk:(j + 3) * blk], kx_ref[...]], axis=0)
        vcat = jnp.concatenate([vlocal[j * blk:(j + 3) * blk], vx_ref[...]], axis=0)
        _attend(sink_ref, q_ref[j * blk:(j + 1) * blk, :], kcat, vcat, mask, o_ref, j * blk)


def _attention_context(q, k, v, sink):
    B, L, _ = q.shape
    row = lambda b: (b, 0, 0)
    return pl.pallas_call(
        _attn_ctx_kernel,
        grid=(B,),
        in_specs=[pl.BlockSpec(memory_space=pltpu.SMEM),
                  pl.BlockSpec((None, L, ATTN_WIDTH), row),
                  pl.BlockSpec((None, L, KV_WIDTH), row),
                  pl.BlockSpec((None, L, KV_WIDTH), row)],
        out_specs=pl.BlockSpec((None, L, ATTN_WIDTH), row),
        out_shape=jax.ShapeDtypeStruct((B, L, ATTN_WIDTH), BF16),
        compiler_params=_cparams(("parallel",)),
        name="attention_context",
    )(sink, q, k, v)


def _attention_latent(q, k, v, k_ctx, v_ctx, sink):
    B, S, _ = q.shape
    blk = WINDOW
    nq = ATTN_QBLOCKS
    nb = S // blk
    nctx = k_ctx.shape[1]
    cur = lambda b, n: (b, n, 0)
    prv = lambda b, n: (b, jnp.maximum(n * nq - 1, 0), 0)
    nxt = lambda b, n: (b, jnp.minimum(n * nq + nq, nb - 1), 0)
    ctx = lambda b, n: (b, 0, 0)
    edge = lambda im: pl.BlockSpec((None, blk, KV_WIDTH), im)
    mid = pl.BlockSpec((None, nq * blk, KV_WIDTH), cur)
    return pl.pallas_call(
        _attn_lat_kernel,
        grid=(B, nb // nq),
        in_specs=[pl.BlockSpec(memory_space=pltpu.SMEM),
                  pl.BlockSpec((None, nq * blk, ATTN_WIDTH), cur),
                  edge(prv), mid, edge(nxt), edge(prv), mid, edge(nxt),
                  pl.BlockSpec((None, nctx, KV_WIDTH), ctx),
                  pl.BlockSpec((None, nctx, KV_WIDTH), ctx)],
        out_specs=pl.BlockSpec((None, nq * blk, ATTN_WIDTH), cur),
        out_shape=jax.ShapeDtypeStruct((B, S, ATTN_WIDTH), BF16),
        compiler_params=_cparams(("parallel", "parallel")),
        name="attention_latent",
    )(sink, q, k, k, k, v, v, v, k_ctx, v_ctx)


def _ssm_kernel(u_ref, wb_ref, wc_ref, a_ref, d_ref, h0_ref, y_ref, fs_ref, ut_ref, yt_ref, st_ref, *bu_refs):
    rev = pl.program_id(0)
    i = pl.program_id(2)
    nc = pl.num_programs(2)
    nseq, tm, _ = u_ref.shape
    sw = a_ref.shape[-1] // SSM_STRIPS
    nre = sw // 2

    @pl.when(i == 0)
    def _():
        st_ref[...] = h0_ref[...]

    for b in range(nseq):
        ub = u_ref[b]
        for s in range(SSM_STRIPS):
            ut_ref[s, pl.ds(b, tm, stride=nseq), :] = ub[:, s * LANES:(s + 1) * LANES]
    for s in range(SSM_STRIPS):
        bu_refs[s][...] = jnp.dot(ut_ref[s].astype(BF16), wb_ref[s], preferred_element_type=F32)

    for s in range(SSM_STRIPS):
        bu_ref = bu_refs[s]
        a_re = a_ref[:, s * sw:s * sw + nre]
        a_im = a_ref[:, s * sw + nre:(s + 1) * sw]

        def step(t, carry):
            xr, xi = carry
            r = pl.multiple_of((t + rev * (tm - 1 - 2 * t)) * nseq, nseq)
            nr = a_re * xr - a_im * xi + bu_ref[pl.ds(r, nseq), 0:nre]
            ni = a_re * xi + a_im * xr + bu_ref[pl.ds(r, nseq), nre:sw]
            bu_ref[pl.ds(r, nseq), 0:nre] = nr
            bu_ref[pl.ds(r, nseq), nre:sw] = ni
            return nr, ni

        xr, xi = lax.fori_loop(0, tm, step, (st_ref[:, s * sw:s * sw + nre], st_ref[:, s * sw + nre:(s + 1) * sw]),
                               unroll=True)
        st_ref[:, s * sw:s * sw + nre] = xr
        st_ref[:, s * sw + nre:(s + 1) * sw] = xi
        yt_ref[s] = jnp.dot(bu_ref[...].astype(BF16), wc_ref[s], preferred_element_type=F32)

    for b in range(nseq):
        yb = jnp.concatenate([yt_ref[s, pl.ds(b, tm, stride=nseq), :] for s in range(SSM_STRIPS)], axis=1)
        y_ref[b] = yb + u_ref[b] * d_ref[...]

    @pl.when(i == nc - 1)
    def _():
        fs_ref[...] = st_ref[...]


def _ssm_mix(u, ssm, h0):
    B, L, _ = u.shape
    nseq = SUBLANES
    tm = min(SSM_STEPS, L)
    nc = L // tm
    rows = nseq * tm
    ns = 2 * SSM_GROUPS * SSM_STATE
    chunk = lambda d, g, i: i + d * (nc - 1 - 2 * i)
    y, fs = pl.pallas_call(
        _ssm_kernel,
        grid=(2, B // nseq, nc),
        in_specs=[pl.BlockSpec((nseq, tm, SSM_WIDTH), lambda d, g, i: (g, chunk(d, g, i), 0)),
                  pl.BlockSpec((None, SSM_STRIPS, LANES, ns // SSM_STRIPS), lambda d, g, i: (d, 0, 0, 0)),
                  pl.BlockSpec((None, SSM_STRIPS, ns // SSM_STRIPS, LANES), lambda d, g, i: (d, 0, 0, 0)),
                  pl.BlockSpec((None, nseq, ns), lambda d, g, i: (d, 0, 0)),
                  pl.BlockSpec((None, 1, SSM_WIDTH), lambda d, g, i: (d, 0, 0)),
                  pl.BlockSpec((None, nseq, ns), lambda d, g, i: (d, g, 0))],
        out_specs=[pl.BlockSpec((None, nseq, tm, SSM_WIDTH), lambda d, g, i: (d, g, chunk(d, g, i), 0)),
                   pl.BlockSpec((None, nseq, ns), lambda d, g, i: (d, g, 0))],
        out_shape=[jax.ShapeDtypeStruct((2, B, L, SSM_WIDTH), F32),
                   jax.ShapeDtypeStruct((2, B, ns), F32)],
        scratch_shapes=[pltpu.VMEM((SSM_STRIPS, rows, LANES), F32), pltpu.VMEM((SSM_STRIPS, rows, LANES), F32),
                        pltpu.VMEM((nseq, ns), F32)]
        + [pltpu.VMEM((rows, ns // SSM_STRIPS), F32) for _ in range(SSM_STRIPS)],
        compiler_params=_cparams(("arbitrary", "arbitrary", "arbitrary")),
        name="ssm_scan",
    )(u, ssm["wb"], ssm["wc"], ssm["a"], ssm["d"], h0)
    return y, fs


def _state_to_lanes(s):
    lead = s.shape[:-3]
    s = s.reshape(lead + (2, SSM_STRIPS, SSM_GROUPS // SSM_STRIPS, SSM_STATE))
    s = jnp.swapaxes(s, -4, -3)
    return s.reshape(lead + (2 * SSM_GROUPS * SSM_STATE,))


def _lanes_to_state(v):
    lead = v.shape[:-1]
    s = v.reshape(lead + (SSM_STRIPS, 2, SSM_GROUPS // SSM_STRIPS, SSM_STATE))
    s = jnp.swapaxes(s, -4, -3)
    return s.reshape(lead + (2, SSM_GROUPS, SSM_STATE))


def _ssm_params(a_re, a_im, log_dt, b_re, b_im, c_re, c_im, dvec):
    G, N, C = SSM_GROUPS, SSM_STATE, SSM_GROUP
    lam_re = jnp.minimum(a_re, -1e-4)
    lam_im = a_im
    dt = jnp.exp(log_dt)[..., None]
    mag = jnp.exp(lam_re * dt)
    abar_re, abar_im = mag * jnp.cos(lam_im * dt), mag * jnp.sin(lam_im * dt)
    den = jnp.square(lam_re) + jnp.square(lam_im)
    p, qi = abar_re - 1.0, abar_im
    f_re = (p * lam_re + qi * lam_im) / den
    f_im = (qi * lam_re - p * lam_im) / den
    bbar_re = f_re[..., None] * b_re - f_im[..., None] * b_im
    bbar_im = f_re[..., None] * b_im + f_im[..., None] * b_re
    a = _state_to_lanes(jnp.stack([abar_re, abar_im], axis=1))
    a = jnp.broadcast_to(a[:, None, :], (2, SUBLANES, a.shape[-1]))
    S = SSM_STRIPS
    gs = G // S
    eye = jnp.eye(gs, dtype=F32)

    def bd_in(bb):
        bb = bb.reshape(2, S, gs, N, C)
        return jnp.einsum('dkgnc,gh->dkgchn', bb, eye).reshape(2, S, gs * C, gs * N)

    def bd_out(cc):
        cc = cc.reshape(2, S, gs, C, N)
        return jnp.einsum('dkgcn,gh->dkgnhc', cc, eye).reshape(2, S, gs * N, gs * C)

    wb = jnp.concatenate([bd_in(bbar_re), bd_in(bbar_im)], axis=-1).astype(BF16)
    wc = jnp.concatenate([bd_out(c_re), -bd_out(c_im)], axis=-2).astype(BF16)
    d = jnp.stack([dvec, jnp.zeros_like(dvec)], axis=0).reshape(2, 1, SSM_WIDTH)
    return {"a": a, "wb": wb, "wc": wc, "d": d}


def _layer_norm(x, g, b):
    mu = jnp.mean(x, axis=-1, keepdims=True)
    xc = x - mu
    var = jnp.mean(xc * xc, axis=-1, keepdims=True)
    return xc * lax.rsqrt(var + LN_EPS) * g + b


def _pack_rows(x):
    w = x.shape[1] // 2
    return pltpu.pack_elementwise([x[:, :w], x[:, w:]], packed_dtype=BF16)


def _unpack_rows(p):
    return (pltpu.unpack_elementwise(p, index=0, packed_dtype=BF16, unpacked_dtype=F32),
            pltpu.unpack_elementwise(p, index=1, packed_dtype=BF16, unpacked_dtype=F32))


def _gelu_tanh(x):
    return 0.5 * x * (1.0 + jnp.tanh(math.sqrt(2.0 / math.pi) * (x + 0.044715 * (x * x * x))))


def _merge_kernel(x_ref, ao_ref, yf_ref, yb_ref, ga_ref, gs_ref, mod_ref, wglu_ref, wa_ref, ws_ref, wo_ref,
                  lng_ref, lnb_ref, rwt_ref, rb_ref, run0_ref, x1_ref, h2_ref, idx_ref, rank_ref, wt_ref, cnt_ref,
                  run_ref, *, alpha):
    @pl.when((pl.program_id(0) == 0) & (pl.program_id(1) == 0))
    def _():
        run_ref[...] = run0_ref[...].astype(F32)

    nrow = x_ref.shape[0]
    sub = nrow // MERGE_SPLIT
    groups = [slice(j * sub, (j + 1) * sub) for j in range(MERGE_SPLIT)]
    dot = functools.partial(jnp.dot, preferred_element_type=F32)
    z = [_gelu_tanh(yf_ref[rs, :] + yb_ref[rs, :]) for rs in groups]
    attn_br = [dot(ao_ref[rs, :], wa_ref[...]) for rs in groups]
    gate = [jax.nn.sigmoid(dot(zg.astype(BF16), wglu_ref[...])) for zg in z]
    ssm_br = [dot((zg * gg).astype(BF16), ws_ref[...]) for zg, gg in zip(z, gate)]
    merged = [jax.nn.sigmoid(ga_ref[rs, :].astype(F32)) * ab + jax.nn.sigmoid(gs_ref[rs, :].astype(F32)) * sb
              for rs, ab, sb in zip(groups, attn_br, ssm_br)]
    mix = [dot(mg.astype(BF16), wo_ref[...]) for mg in merged]
    h2s = []
    for rs, mg in zip(groups, mix):
        x1 = _layer_norm(alpha * x_ref[rs, :] + mod_ref[2:3, :] * mg, lng_ref[...], lnb_ref[...])
        h2g = x1 * (1.0 + mod_ref[4:5, :]) + mod_ref[3:4, :]
        x1_ref[rs, :] = x1
        h2_ref[rs, :] = _pack_rows(h2g)
        h2s.append(h2g)
    h2 = jnp.concatenate(h2s, axis=0)

    logits = lax.dot_general(rwt_ref[...], h2, (((1,), (1,)), ((), ())), preferred_element_type=F32,
                             precision=lax.Precision.HIGHEST)
    score = jax.nn.sigmoid(logits)
    tm = score.shape[1]
    eidx = lax.broadcasted_iota(jnp.int32, score.shape, 0).astype(F32)
    work = score + rb_ref[...]
    picks, sel = [], []
    member = jnp.zeros_like(score)
    for _ in range(TOP_K):
        best = jnp.max(work, axis=0, keepdims=True)
        pick = jnp.min(jnp.where(work == best, eidx, float(N_EXPERTS)), axis=0, keepdims=True)
        hit = eidx == pick
        picks.append(pick)
        sel.append(jnp.sum(jnp.where(hit, score, 0.0), axis=0, keepdims=True))
        member = member + hit.astype(F32)
        work = jnp.where(hit, -jnp.inf, work)
    total = sel[0]
    for s in sel[1:]:
        total = total + s
    before = (lax.broadcasted_iota(jnp.int32, (tm, tm), 0) < lax.broadcasted_iota(jnp.int32, (tm, tm), 1))
    prefix = jnp.dot(member.astype(BF16), before.astype(BF16), preferred_element_type=F32)
    base = prefix + run_ref[...]
    for k in range(TOP_K):
        idx_ref[k:k + 1, :] = picks[k].astype(jnp.int32)
        rank_ref[k:k + 1, :] = jnp.sum(jnp.where(eidx == picks[k], base, 0.0), axis=0,
                                       keepdims=True).astype(jnp.int32)
        wt_ref[k:k + 1, :] = sel[k] / total * ROUTED_SCALE
    run_ref[...] = run_ref[...] + jnp.sum(member, axis=1, keepdims=True)
    cnt_ref[...] = run_ref[...].astype(jnp.int32)


def _merge(x, attn_o, y, ga, gs, mod, wts, run0, alpha):
    B, L, D = x.shape
    tm = min(MERGE_ROWS, L)
    per_batch = mod.shape[0] > 1
    row = lambda b, i: (b, i, 0)
    full = lambda b, i: (0, 0)
    wspec = lambda a: pl.BlockSpec(a.shape, full)
    return pl.pallas_call(
        functools.partial(_merge_kernel, alpha=alpha),
        grid=(B, L // tm),
        in_specs=[pl.BlockSpec((None, tm, D), row),
                  pl.BlockSpec((None, tm, ATTN_WIDTH), row),
                  pl.BlockSpec((None, None, tm, SSM_WIDTH), lambda b, i: (0, b, i, 0)),
                  pl.BlockSpec((None, None, tm, SSM_WIDTH), lambda b, i: (1, b, i, 0)),
                  pl.BlockSpec((None, tm, D), row),
                  pl.BlockSpec((None, tm, D), row),
                  pl.BlockSpec((None, N_MOD, D), (lambda b, i: (b, 0, 0)) if per_batch else (lambda b, i: (0, 0, 0))),
                  wspec(wts["w_glu"]), wspec(wts["w_attn_br"]), wspec(wts["w_ssm_br"]), wspec(wts["w_out"]),
                  wspec(wts["ln1_g"]), wspec(wts["ln1_b"]), wspec(wts["router_wt"]), wspec(wts["router_b"]),
                  wspec(run0)],
        out_specs=[pl.BlockSpec((None, tm, D), row),
                   pl.BlockSpec((None, tm, D // 2), row),
                   pl.BlockSpec((None, TOP_K, tm), lambda b, i: (b, 0, i)),
                   pl.BlockSpec((None, TOP_K, tm), lambda b, i: (b, 0, i)),
                   pl.BlockSpec((None, TOP_K, tm), lambda b, i: (b, 0, i)),
                   pl.BlockSpec((N_EXPERTS, 1), full)],
        out_shape=[jax.ShapeDtypeStruct((B, L, D), F32),
                   jax.ShapeDtypeStruct((B, L, D // 2), jnp.int32),
                   jax.ShapeDtypeStruct((B, TOP_K, L), jnp.int32),
                   jax.ShapeDtypeStruct((B, TOP_K, L), jnp.int32),
                   jax.ShapeDtypeStruct((B, TOP_K, L), F32),
                   jax.ShapeDtypeStruct((N_EXPERTS, 1), jnp.int32)],
        scratch_shapes=[pltpu.VMEM((N_EXPERTS, 1), F32)],
        compiler_params=_cparams(("arbitrary", "arbitrary")),
        name="merge_ln1_router",
    )(x, attn_o, y, y, ga, gs, mod, wts["w_glu"], wts["w_attn_br"], wts["w_ssm_br"], wts["w_out"],
      wts["ln1_g"], wts["ln1_b"], wts["router_wt"], wts["router_b"], run0)


def _dispatch(h2s, dests):
    W = h2s[0].shape[-1]
    n_slots = sum(d.size for d in dests)
    mesh = plsc.VectorSubcoreMesh(core_axis_name="core", subcore_axis_name="subcore")
    workers = mesh.num_cores * mesh.num_subcores
    tokens = [h.shape[0] * h.shape[1] for h in h2s]
    per_worker = [t // SC_WINDOW // workers for t in tokens]
    assert all(p * SC_WINDOW * workers == t for p, t in zip(per_worker, tokens))

    @pl.kernel(out_type=jax.ShapeDtypeStruct((n_slots, W), h2s[0].dtype), mesh=mesh,
               scratch_types=[pltpu.VMEM((TOP_K, SC_WINDOW), jnp.int32), pltpu.VMEM((SC_WINDOW, W), h2s[0].dtype)])
    def scatter(*refs):
        out_hbm, index_v, rows_v = refs[-3:]
        worker = lax.axis_index("core") * mesh.num_subcores + lax.axis_index("subcore")
        for g, per in enumerate(per_worker):
            rows_hbm, slots_hbm = refs[2 * g], refs[2 * g + 1]

            @pl.loop(0, per)
            def _(j, rows_hbm=rows_hbm, slots_hbm=slots_hbm, per=per):
                window = worker * per + j
                pltpu.sync_copy(rows_hbm.at[pl.ds(window * SC_WINDOW, SC_WINDOW)], rows_v)
                pltpu.sync_copy(slots_hbm.at[window], index_v)
                for k in range(TOP_K):
                    pltpu.sync_copy(rows_v, out_hbm.at[index_v.at[k]])

    args = []
    for h, d in zip(h2s, dests):
        nb, _, ln = d.shape
        slots = d.reshape(nb, TOP_K, ln // SC_WINDOW, SC_WINDOW).transpose(0, 2, 1, 3)
        args += [h.reshape(-1, W), slots.reshape(nb * ln // SC_WINDOW, TOP_K, SC_WINDOW)]
    return scatter(*args)


def _gmm_kernel(tile_ref, exp_ref, valid_ref, gstart_ref, xs_ref, wg_ref, wu_ref, wd_ref, ys_ref,
                wgb_ref, wub_ref, wdb_ref):
    w = pl.program_id(0)
    e = exp_ref[w]
    t = tile_ref[w]
    prev = jnp.maximum(w - 1, 0)
    new_expert = (w == 0) | (e != exp_ref[prev])
    first_visit = (w == 0) | (t != tile_ref[prev])

    @pl.when(new_expert)
    def _():
        wgb_ref[...] = wg_ref[...].astype(BF16)
        wub_ref[...] = wu_ref[...].astype(BF16)
        wdb_ref[...] = wd_ref[...].astype(BF16)

    @pl.when(first_visit)
    def _():
        ys_ref[...] = jnp.zeros_like(ys_ref)

    tg = xs_ref.shape[0]
    sub = tg // GMM_SPLIT
    lo = gstart_ref[e]
    hi = gstart_ref[e + 1]
    for j in range(GMM_SPLIT):
        first_row = t * tg + j * sub

        @pl.when((valid_ref[w] == 1) & (lo < first_row + sub) & (hi > first_row))
        def _(j=j, first_row=first_row):
            rs = slice(j * sub, (j + 1) * sub)
            dot = functools.partial(jnp.dot, preferred_element_type=F32)
            x = jnp.concatenate(_unpack_rows(xs_ref[rs, :]), axis=1).astype(BF16)
            g = dot(x, wgb_ref[...])
            u = dot(x, wub_ref[...])
            y = _pack_rows(dot((g * jax.nn.sigmoid(g) * u).astype(BF16), wdb_ref[...]))
            rows = first_row + lax.broadcasted_iota(jnp.int32, (sub, 1), 0)
            mine = (rows >= lo) & (rows < hi)
            ys_ref[rs, :] = jnp.where(mine, y, ys_ref[rs, :])


def _grouped_experts(xs, counts, w_gate, w_up, w_down):
    A, W = xs.shape
    D = w_gate.shape[-2]
    tg = GMM_ROWS
    nt = A // tg
    n_items = nt + N_EXPERTS - 1
    ff = w_gate.shape[-1]
    gend = jnp.cumsum(counts).astype(jnp.int32)
    gstart = jnp.concatenate([jnp.zeros((1,), jnp.int32), gend])
    first_row = jnp.arange(nt, dtype=jnp.int32) * tg
    count_le = lambda ends, v: jnp.sum((ends[None, :] <= v[:, None]).astype(jnp.int32), axis=1)
    e_lo = count_le(gend, first_row)
    e_hi = count_le(gend, first_row + tg - 1)
    per_tile = e_hi - e_lo + 1
    item_end = jnp.cumsum(per_tile).astype(jnp.int32)
    total = item_end[-1]
    wi = jnp.arange(n_items, dtype=jnp.int32)
    tile = jnp.minimum(count_le(item_end, wi), nt - 1)
    in_tile = (tile[:, None] == jnp.arange(nt, dtype=jnp.int32)[None, :]).astype(jnp.int32)
    lookup = lambda table: jnp.sum(in_tile * table[None, :], axis=1)
    expert = lookup(e_lo) + wi - lookup(item_end - per_tile)
    valid = (wi < total).astype(jnp.int32)
    expert = jnp.where(valid == 1, expert, e_hi[nt - 1])
    grid_spec = pltpu.PrefetchScalarGridSpec(
        num_scalar_prefetch=4,
        grid=(n_items,),
        in_specs=[pl.BlockSpec((tg, W), lambda w, tl, ex, va, gs: (tl[w], 0)),
                  pl.BlockSpec((None, D, ff), lambda w, tl, ex, va, gs: (ex[w], 0, 0)),
                  pl.BlockSpec((None, D, ff), lambda w, tl, ex, va, gs: (ex[w], 0, 0)),
                  pl.BlockSpec((None, ff, D), lambda w, tl, ex, va, gs: (ex[w], 0, 0))],
        out_specs=pl.BlockSpec((tg, W), lambda w, tl, ex, va, gs: (tl[w], 0)),
        scratch_shapes=[pltpu.VMEM((D, ff), BF16), pltpu.VMEM((D, ff), BF16), pltpu.VMEM((ff, D), BF16)],
    )
    return pl.pallas_call(
        _gmm_kernel,
        grid_spec=grid_spec,
        out_shape=jax.ShapeDtypeStruct((A, W), xs.dtype),
        compiler_params=_cparams(("arbitrary",)),
        name="moe_grouped_experts",
    )(tile, expert, valid, gstart, xs, w_gate, w_up, w_down)


def _gather_rows_sc(rows, index):
    n = index.shape[0]
    w = rows.shape[1]
    mesh = plsc.VectorSubcoreMesh(core_axis_name="core", subcore_axis_name="subcore")
    per_subcore = n // SC_WINDOW // (mesh.num_cores * mesh.num_subcores)
    assert per_subcore * SC_WINDOW * mesh.num_cores * mesh.num_subcores == n

    @pl.kernel(out_type=jax.ShapeDtypeStruct((n, w), rows.dtype), mesh=mesh,
               scratch_types=[pltpu.VMEM((1, SC_WINDOW), jnp.int32), pltpu.VMEM((SC_WINDOW, w), rows.dtype)])
    def gather(rows_hbm, index_hbm, out_hbm, index_v, rows_v):
        worker = lax.axis_index("core") * mesh.num_subcores + lax.axis_index("subcore")

        @pl.loop(0, per_subcore)
        def _(j):
            first = (worker * per_subcore + j) * SC_WINDOW
            pltpu.sync_copy(index_hbm.at[:, pl.ds(first, SC_WINDOW)], index_v)
            pltpu.sync_copy(rows_hbm.at[index_v.at[0]], rows_v)
            pltpu.sync_copy(rows_v, out_hbm.at[pl.ds(first, SC_WINDOW)])

    return gather(rows, index.reshape(1, n))


def _combine_rows_kernel(yg_ref, wt_ref, x1_ref, h2_ref, mod_ref, sg_ref, su_ref, sd_ref, lng_ref, lnb_ref, *rest,
                         alpha):
    o_ref = rest[-1]
    h = jnp.concatenate(_unpack_rows(h2_ref[...]), axis=1).astype(BF16)
    g = jnp.dot(h, sg_ref[...], preferred_element_type=F32)
    u = jnp.dot(h, su_ref[...], preferred_element_type=F32)
    moe = jnp.dot((g * jax.nn.sigmoid(g) * u).astype(BF16), sd_ref[...], preferred_element_type=F32)
    wt = wt_ref[...]
    lo = jnp.zeros(h2_ref.shape, F32)
    hi = lo
    for k in range(TOP_K):
        rl, rh = _unpack_rows(yg_ref[k])
        lo = lo + wt[:, k:k + 1] * rl
        hi = hi + wt[:, k:k + 1] * rh
    moe = moe + jnp.concatenate([lo, hi], axis=1)
    o_ref[...] = _layer_norm(alpha * x1_ref[...] + mod_ref[5:6, :] * moe, lng_ref[...], lnb_ref[...])


def _combine_rows(ys, dest, wts, x1, h2, mod, shared, alpha):
    B, L, D = x1.shape
    W = ys.shape[-1]
    tm = min(COMBINE_ROWS, L)
    per_batch = mod.shape[0] > 1
    pieces = max(1, B * L // COMBINE_PIECE_TOKENS)
    nb = B // pieces
    full = lambda b, i: (0, 0)
    wspec = lambda a: pl.BlockSpec(a.shape, full)
    wts_t = jnp.swapaxes(wts, 1, 2)
    out = None
    for p in range(pieces):
        b0 = p * nb
        row = lambda b, i, b0=b0: (b0 + b, i, 0)
        slots = jnp.swapaxes(dest[b0:b0 + nb], 0, 1).reshape(TOP_K * nb * L)
        yg = _gather_rows_sc(ys, slots).reshape(TOP_K, nb, L, W)
        in_specs = [pl.BlockSpec((TOP_K, None, tm, W), lambda b, i: (0, b, i, 0)),
                    pl.BlockSpec((None, tm, TOP_K), row),
                    pl.BlockSpec((None, tm, D), row),
                    pl.BlockSpec((None, tm, W), row),
                    pl.BlockSpec((None, N_MOD, D), (lambda b, i, b0=b0: (b0 + b, 0, 0)) if per_batch
                                 else (lambda b, i: (0, 0, 0))),
                    wspec(shared["sh_w_gate"]), wspec(shared["sh_w_up"]), wspec(shared["sh_w_down"]),
                    wspec(shared["ln2_g"]), wspec(shared["ln2_b"])]
        args = [yg, wts_t, x1, h2, mod, shared["sh_w_gate"], shared["sh_w_up"], shared["sh_w_down"],
                shared["ln2_g"], shared["ln2_b"]]
        aliases = {}
        if out is not None:
            in_specs.append(pl.BlockSpec(memory_space=pl.ANY))
            args.append(out)
            aliases = {len(args) - 1: 0}
        out = pl.pallas_call(
            functools.partial(_combine_rows_kernel, alpha=alpha),
            grid=(nb, L // tm),
            in_specs=in_specs,
            out_specs=pl.BlockSpec((None, tm, D), row),
            out_shape=jax.ShapeDtypeStruct((B, L, D), F32),
            input_output_aliases=aliases,
            compiler_params=_cparams(("parallel", "parallel")),
            name="moe_combine_ln2",
        )(*args)
    return out


def _slots(idx, rank, counts):
    start = jnp.cumsum(counts) - counts
    pick = idx[..., None] == jnp.arange(N_EXPERTS, dtype=jnp.int32)
    return jnp.sum(jnp.where(pick, start, 0), axis=-1).astype(jnp.int32) + rank


def _mixer_and_router(x, mod, wts, ssm, cache, run0, alpha):
    B, L, D = x.shape
    latent = cache is not None
    if latent:
        k_ctx, v_ctx, h0 = cache
        cos, sin = _rope_cos_sin(L)
    else:
        cos = jnp.zeros((L, LANES), F32)
        sin = cos
        h0 = jnp.zeros((2, B, 2 * SSM_GROUPS * SSM_STATE), F32)
    q, k, v, u, ga, gs = _in_projection(x, mod, wts["w_in"], cos, sin, rope=latent)
    if latent:
        attn_o = _attention_latent(q, k, v, k_ctx, v_ctx, wts["attn_sink"])
    else:
        attn_o = _attention_context(q, k, v, wts["attn_sink"])
    y, fs = _ssm_mix(u, ssm, h0)
    x1, h2, idx, rank, rw, counts = _merge(x, attn_o, y, ga, gs, mod, wts, run0, alpha)
    return {"x1": x1, "h2": h2, "idx": idx, "rank": rank, "rw": rw, "mod": mod}, counts, k, v, fs


def _moe(groups, counts, wts, alpha):
    counts = counts.reshape(N_EXPERTS)
    for g in groups:
        g["dest"] = _slots(g["idx"], g["rank"], counts)
    xs = _dispatch([g["h2"] for g in groups], [g["dest"] for g in groups])
    ys = _grouped_experts(xs, counts, wts["exp_w_gate"], wts["exp_w_up"], wts["exp_w_down"])
    return [_combine_rows(ys, g["dest"], g["rw"], g["x1"], g["h2"], g["mod"], wts, alpha) for g in groups]


def kernel(x_prompt, x_sample, cache_k, cache_v, state_ssm, c, c_ctx, mod_w, mod_b, w_in, attn_sink, w_attn_br, ssm_a_re, ssm_a_im, ssm_log_dt, ssm_b_re, ssm_b_im, ssm_c_re, ssm_c_im, ssm_d, w_glu, w_ssm_br, w_out, ln1_g, ln1_b, ln2_g, ln2_b, router_w, router_b, exp_w_gate, exp_w_up, exp_w_down, sh_w_gate, sh_w_up, sh_w_down):
    depth = w_in.shape[0]
    assert depth == 1
    alpha = (2.0 * depth) ** 0.25
    D = x_prompt.shape[-1]
    nb_p = x_prompt.shape[0]
    nb_s = x_sample.shape[0]
    l = 0

    ncond = 1 + nb_s
    npad = -ncond % SUBLANES
    cond = jnp.concatenate([c_ctx[None, :], c, jnp.zeros((npad, D), F32)], axis=0)
    mod = _modulation(cond, mod_w[l], mod_b[l]).reshape(ncond + npad, N_MOD, D)
    mod_ctx, mod_lat = mod[0:1], mod[1:ncond]

    o1 = ATTN_WIDTH
    o3 = o1 + 2 * KV_WIDTH
    o4 = o3 + SSM_WIDTH
    wi = w_in[l]
    scale = HEAD_DIM ** -0.5 * LOG2_E
    wts = {
        "w_in": jnp.concatenate([wi[:, :o1] * scale, wi[:, o1:]], axis=1).astype(BF16),
        "attn_sink": attn_sink[l],
        "w_glu": w_glu[l].astype(BF16), "w_attn_br": w_attn_br[l].astype(BF16),
        "w_ssm_br": w_ssm_br[l].astype(BF16), "w_out": w_out[l].astype(BF16),
        "ln1_g": ln1_g[l].reshape(1, D), "ln1_b": ln1_b[l].reshape(1, D),
        "ln2_g": ln2_g[l].reshape(1, D), "ln2_b": ln2_b[l].reshape(1, D),
        "router_wt": router_w[l].T, "router_b": router_b[l].reshape(N_EXPERTS, 1),
        "exp_w_gate": exp_w_gate[l], "exp_w_up": exp_w_up[l], "exp_w_down": exp_w_down[l],
        "sh_w_gate": sh_w_gate[l].astype(BF16), "sh_w_up": sh_w_up[l].astype(BF16),
        "sh_w_down": sh_w_down[l].astype(BF16),
    }
    ssm = _ssm_params(ssm_a_re[l], ssm_a_im[l], ssm_log_dt[l], ssm_b_re[l], ssm_b_im[l],
                      ssm_c_re[l], ssm_c_im[l], ssm_d[l])

    no_tokens_yet = jnp.zeros((N_EXPERTS, 1), jnp.int32)
    grp_p, counts, k_p, v_p, fs_p = _mixer_and_router(x_prompt, mod_ctx, wts, ssm, None, no_tokens_yet, alpha)

    past = cache_k.shape[2]
    k_ctx = cache_k[:, l].reshape(nb_s, past, KV_WIDTH)
    v_ctx = cache_v[:, l].reshape(nb_s, past, KV_WIDTH)
    h0 = jnp.swapaxes(_state_to_lanes(state_ssm[:, l]), 0, 1)
    grp_s, counts, _, _, _ = _mixer_and_router(x_sample, mod_lat, wts, ssm, (k_ctx, v_ctx, h0), counts, alpha)
    yp, ys_ = _moe([grp_p, grp_s], counts, wts, alpha)

    Lp = x_prompt.shape[1]
    new_k = k_p.reshape(nb_p, 1, Lp, N_KV_HEADS, HEAD_DIM)
    new_v = v_p.reshape(nb_p, 1, Lp, N_KV_HEADS, HEAD_DIM)
    new_s = _lanes_to_state(jnp.swapaxes(fs_p, 0, 1))[:, None]
    return (yp, ys_, new_k, new_v, new_s)
```

```python
import functools
import math

import jax
import jax.numpy as jnp
from jax import lax
from jax.experimental import pallas as pl
from jax.experimental.pallas import tpu as pltpu
from jax.experimental.pallas import tpu_sc as plsc

F32 = jnp.float32
BF16 = jnp.bfloat16

GRID_W = 64
HEAD_DIM = 64
N_Q_HEADS = 8
N_KV_HEADS = 2
Q_PER_KV = N_Q_HEADS // N_KV_HEADS
ATTN_WIDTH = N_Q_HEADS * HEAD_DIM
KV_WIDTH = N_KV_HEADS * HEAD_DIM
WINDOW = 128
ROPE_BASE = 10000.0
ROPE_FREQS = HEAD_DIM // 4
SSM_WIDTH = 512
SSM_GROUP = 16
SSM_GROUPS = SSM_WIDTH // SSM_GROUP
SSM_STATE = 64
N_EXPERTS = 64
TOP_K = 6
ROUTED_SCALE = 2.5
N_MOD = 6
LN_EPS = 1e-5
LOG2_E = math.log2(math.e)

SUBLANES = 8
LANES = 128
VMEM_LIMIT = 48 * 1024 * 1024

MOD_COLS = 512
INPROJ_ROWS = 1024
ATTN_QBLOCKS = 8
SSM_STEPS = 128
SSM_STRIPS = 4
MERGE_ROWS = 512
MERGE_SPLIT = 2
GMM_ROWS = 2048
GMM_SPLIT = 4
COMBINE_ROWS = 512
COMBINE_PIECE_TOKENS = 8192
SC_WINDOW = 128


def _cparams(sem):
    return pltpu.CompilerParams(dimension_semantics=sem, vmem_limit_bytes=VMEM_LIMIT)


def _mod_kernel(c_ref, w_ref, b_ref, o_ref):
    c = c_ref[...]
    s = c * jax.nn.sigmoid(c)
    o_ref[...] = jnp.dot(s, w_ref[...], preferred_element_type=F32,
                         precision=lax.Precision.HIGHEST) + b_ref[...]


def _modulation(cond, w, b):
    n, d = cond.shape
    nout = w.shape[1]
    tn = MOD_COLS
    return pl.pallas_call(
        _mod_kernel,
        grid=(nout // tn,),
        in_specs=[pl.BlockSpec((n, d), lambda j: (0, 0)),
                  pl.BlockSpec((d, tn), lambda j: (0, j)),
                  pl.BlockSpec((1, tn), lambda j: (0, j))],
        out_specs=pl.BlockSpec((n, tn), lambda j: (0, j)),
        out_shape=jax.ShapeDtypeStruct((n, nout), F32),
        compiler_params=_cparams(("arbitrary",)),
        name="modulation",
    )(cond, w, b.reshape(1, nout))


def _rope_rotate(t, cos, sin):
    lane = lax.broadcasted_iota(jnp.int32, t.shape, 1)
    first_of_pair = (lane % (2 * ROPE_FREQS)) < ROPE_FREQS
    partner = jnp.where(first_of_pair, pltpu.roll(t, LANES - ROPE_FREQS, 1), pltpu.roll(t, ROPE_FREQS, 1))
    return t * cos + partner * sin


def _inproj_kernel(x_ref, mod_ref, w_ref, cos_ref, sin_ref, q_ref, k_ref, v_ref, u_ref, ga_ref, gs_ref, *, rope):
    x = x_ref[...]
    h = (x * (1.0 + mod_ref[1:2, :]) + mod_ref[0:1, :]).astype(BF16)
    p = jnp.dot(h, w_ref[...], preferred_element_type=F32)
    o1 = ATTN_WIDTH
    o2 = o1 + KV_WIDTH
    o3 = o2 + KV_WIDTH
    d = x.shape[1]
    q = p[:, :o1]
    k = p[:, o1:o2]
    if rope:
        cos = cos_ref[...]
        sin = sin_ref[...]
        q = jnp.concatenate([_rope_rotate(q[:, j * LANES:(j + 1) * LANES], cos, sin)
                             for j in range(o1 // LANES)], axis=1)
        k = _rope_rotate(k, cos, sin)
    o4 = o3 + SSM_WIDTH
    q_ref[...] = q.astype(BF16)
    k_ref[...] = k
    v_ref[...] = p[:, o2:o3]
    u_ref[...] = p[:, o3:o4]
    ga_ref[...] = p[:, o4:o4 + d].astype(BF16)
    gs_ref[...] = p[:, o4 + d:].astype(BF16)


def _in_projection(x, mod, w, cos, sin, rope):
    B, L, D = x.shape
    tm = min(INPROJ_ROWS, L)
    per_batch = mod.shape[0] > 1
    nw = w.shape[1]
    row = lambda b, i: (b, i, 0)
    outs = pl.pallas_call(
        functools.partial(_inproj_kernel, rope=rope),
        grid=(B, L // tm),
        in_specs=[pl.BlockSpec((None, tm, D), row),
                  pl.BlockSpec((None, N_MOD, D), (lambda b, i: (b, 0, 0)) if per_batch else (lambda b, i: (0, 0, 0))),
                  pl.BlockSpec((D, nw), lambda b, i: (0, 0)),
                  pl.BlockSpec((tm, LANES), lambda b, i: (i, 0)),
                  pl.BlockSpec((tm, LANES), lambda b, i: (i, 0))],
        out_specs=[pl.BlockSpec((None, tm, ATTN_WIDTH), row),
                   pl.BlockSpec((None, tm, KV_WIDTH), row),
                   pl.BlockSpec((None, tm, KV_WIDTH), row),
                   pl.BlockSpec((None, tm, SSM_WIDTH), row),
                   pl.BlockSpec((None, tm, D), row),
                   pl.BlockSpec((None, tm, D), row)],
        out_shape=[jax.ShapeDtypeStruct((B, L, ATTN_WIDTH), BF16),
                   jax.ShapeDtypeStruct((B, L, KV_WIDTH), F32),
                   jax.ShapeDtypeStruct((B, L, KV_WIDTH), F32),
                   jax.ShapeDtypeStruct((B, L, SSM_WIDTH), F32),
                   jax.ShapeDtypeStruct((B, L, D), BF16),
                   jax.ShapeDtypeStruct((B, L, D), BF16)],
        compiler_params=_cparams(("parallel", "parallel")),
        name="in_projection",
    )(x, mod, w, cos, sin)
    return outs


def _rope_cos_sin(n_tokens):
    t = jnp.arange(n_tokens, dtype=jnp.int32)
    pos = jnp.stack([t // GRID_W, t % GRID_W], axis=-1).astype(F32)
    n_freq = ROPE_FREQS
    inv_freq = ROPE_BASE ** (-jnp.arange(n_freq, dtype=F32) / n_freq)
    ang = pos[:, :, None] * inv_freq
    c, s = jnp.cos(ang), jnp.sin(ang)
    cos = jnp.concatenate([c[:, 0], c[:, 0], c[:, 1], c[:, 1]], axis=-1)
    sin = jnp.concatenate([-s[:, 0], s[:, 0], -s[:, 1], s[:, 1]], axis=-1)
    return jnp.tile(cos, (1, LANES // HEAD_DIM)), jnp.tile(sin, (1, LANES // HEAD_DIM))


def _attend(sink_ref, q, kcat, vcat, mask, o_ref, row0=0):
    lq = q.shape[0]
    lane = lax.broadcasted_iota(jnp.int32, (1, LANES), 1)
    low = lane < HEAD_DIM
    k_sw = pltpu.roll(kcat, HEAD_DIM, 1)
    v_sw = pltpu.roll(vcat, HEAD_DIM, 1)
    neg = jnp.finfo(F32).min
    scores, vds = [], []
    for h in range(N_KV_HEADS):
        keep = low if h == 0 else jnp.logical_not(low)
        kd = jnp.where(keep, kcat, k_sw).astype(BF16)
        vds.append(jnp.where(keep, vcat, v_sw).astype(BF16))
        qs = []
        for j in range(Q_PER_KV):
            head = h * Q_PER_KV + j
            blk = q[:, (head // 2) * LANES:(head // 2 + 1) * LANES]
            sel = low if head % 2 == 0 else jnp.logical_not(low)
            qs.append(jnp.where(sel, blk, jnp.zeros_like(blk)))
        qstack = jnp.concatenate(qs, axis=0)
        scores.append(lax.dot_general(qstack, kd, (((1,), (1,)), ((), ())), preferred_element_type=F32))
    probs, denoms = [], []
    for h in range(N_KV_HEADS):
        s = scores[h]
        ps, ls = [], []
        for j in range(Q_PER_KV):
            sj = s[j * lq:(j + 1) * lq]
            if mask is not None:
                lm = mask.shape[1]
                sj = jnp.concatenate([jnp.where(mask, sj[:, :lm], neg), sj[:, lm:]], axis=1)
            sink = sink_ref[h * Q_PER_KV + j] * LOG2_E
            m = jnp.maximum(jnp.max(sj, axis=1, keepdims=True), sink)
            pj = jnp.exp2(sj - m)
            ls.append(jnp.sum(pj, axis=1, keepdims=True) + jnp.exp2(sink - m))
            ps.append(pj.astype(BF16))
        probs.append(jnp.concatenate(ps, axis=0))
        denoms.append(ls)
    for h in range(N_KV_HEADS):
        ls = denoms[h]
        o = jnp.dot(probs[h], vds[h], preferred_element_type=F32)
        for jj in range(Q_PER_KV // 2):
            oe = o[(2 * jj) * lq:(2 * jj + 1) * lq] / ls[2 * jj]
            oo = o[(2 * jj + 1) * lq:(2 * jj + 2) * lq] / ls[2 * jj + 1]
            cb = (h * Q_PER_KV) // 2 + jj
            o_ref[row0:row0 + lq, cb * LANES:(cb + 1) * LANES] = jnp.where(low, oe, oo).astype(o_ref.dtype)


def _attn_ctx_kernel(sink_ref, q_ref, k_ref, v_ref, o_ref):
    _attend(sink_ref, q_ref[...], k_ref[...], v_ref[...], None, o_ref)


def _attn_lat_kernel(sink_ref, q_ref, kp_ref, kc_ref, kn_ref, vp_ref, vc_ref, vn_ref, kx_ref, vx_ref, o_ref):
    n = pl.program_id(1)
    nb = pl.num_programs(1)
    blk = kp_ref.shape[0]
    nq = q_ref.shape[0] // blk
    klocal = jnp.concatenate([kp_ref[...], kc_ref[...], kn_ref[...]], axis=0)
    vlocal = jnp.concatenate([vp_ref[...], vc_ref[...], vn_ref[...]], axis=0)
    qi = lax.broadcasted_iota(jnp.int32, (blk, 3 * blk), 0)
    kj = lax.broadcasted_iota(jnp.int32, (blk, 3 * blk), 1)
    rel = kj - blk - qi
    in_band = (rel <= WINDOW) & (rel >= -WINDOW)
    for j in range(nq):
        mask = in_band
        if j == 0:
            mask = mask & ((kj >= blk) | (n > 0))
        if j == nq - 1:
            mask = mask & ((kj < 2 * blk) | (n < nb - 1))
        kcat = jnp.concatenate([klocal[j * blk:(j + 3) * blk], kx_ref[...]], axis=0)
        vcat = jnp.concatenate([vlocal[j * blk:(j + 3) * blk], vx_ref[...]], axis=0)
        _attend(sink_ref, q_ref[j * blk:(j + 1) * blk, :], kcat, vcat, mask, o_ref, j * blk)


def _attention_context(q, k, v, sink):
    B, L, _ = q.shape
    row = lambda b: (b, 0, 0)
    return pl.pallas_call(
        _attn_ctx_kernel,
        grid=(B,),
        in_specs=[pl.BlockSpec(memory_space=pltpu.SMEM),
                  pl.BlockSpec((None, L, ATTN_WIDTH), row),
                  pl.BlockSpec((None, L, KV_WIDTH), row),
                  pl.BlockSpec((None, L, KV_WIDTH), row)],
        out_specs=pl.BlockSpec((None, L, ATTN_WIDTH), row),
        out_shape=jax.ShapeDtypeStruct((B, L, ATTN_WIDTH), BF16),
        compiler_params=_cparams(("parallel",)),
        name="attention_context",
    )(sink, q, k, v)


def _attention_latent(q, k, v, k_ctx, v_ctx, sink):
    B, S, _ = q.shape
    blk = WINDOW
    nq = ATTN_QBLOCKS
    nb = S // blk
    nctx = k_ctx.shape[1]
    cur = lambda b, n: (b, n, 0)
    prv = lambda b, n: (b, jnp.maximum(n * nq - 1, 0), 0)
    nxt = lambda b, n: (b, jnp.minimum(n * nq + nq, nb - 1), 0)
    ctx = lambda b, n: (b, 0, 0)
    edge = lambda im: pl.BlockSpec((None, blk, KV_WIDTH), im)
    mid = pl.BlockSpec((None, nq * blk, KV_WIDTH), cur)
    return pl.pallas_call(
        _attn_lat_kernel,
        grid=(B, nb // nq),
        in_specs=[pl.BlockSpec(memory_space=pltpu.SMEM),
                  pl.BlockSpec((None, nq * blk, ATTN_WIDTH), cur),
                  edge(prv), mid, edge(nxt), edge(prv), mid, edge(nxt),
                  pl.BlockSpec((None, nctx, KV_WIDTH), ctx),
                  pl.BlockSpec((None, nctx, KV_WIDTH), ctx)],
        out_specs=pl.BlockSpec((None, nq * blk, ATTN_WIDTH), cur),
        out_shape=jax.ShapeDtypeStruct((B, S, ATTN_WIDTH), BF16),
        compiler_params=_cparams(("parallel", "parallel")),
        name="attention_latent",
    )(sink, q, k, k, k, v, v, v, k_ctx, v_ctx)


def _ssm_kernel(u_ref, wb_ref, wc_ref, a_ref, d_ref, h0_ref, y_ref, fs_ref, ut_ref, yt_ref, st_ref, *bu_refs):
    rev = pl.program_id(0)
    i = pl.program_id(2)
    nc = pl.num_programs(2)
    nseq, tm, _ = u_ref.shape
    sw = a_ref.shape[-1] // SSM_STRIPS
    nre = sw // 2

    @pl.when(i == 0)
    def _():
        st_ref[...] = h0_ref[...]

    for b in range(nseq):
        ub = u_ref[b]
        for s in range(SSM_STRIPS):
            ut_ref[s, pl.ds(b, tm, stride=nseq), :] = ub[:, s * LANES:(s + 1) * LANES]
    for s in range(SSM_STRIPS):
        bu_refs[s][...] = jnp.dot(ut_ref[s].astype(BF16), wb_ref[s], preferred_element_type=F32)

    for s in range(SSM_STRIPS):
        bu_ref = bu_refs[s]
        a_re = a_ref[:, s * sw:s * sw + nre]
        a_im = a_ref[:, s * sw + nre:(s + 1) * sw]

        def step(t, carry):
            xr, xi = carry
            r = pl.multiple_of((t + rev * (tm - 1 - 2 * t)) * nseq, nseq)
            nr = a_re * xr - a_im * xi + bu_ref[pl.ds(r, nseq), 0:nre]
            ni = a_re * xi + a_im * xr + bu_ref[pl.ds(r, nseq), nre:sw]
            bu_ref[pl.ds(r, nseq), 0:nre] = nr
            bu_ref[pl.ds(r, nseq), nre:sw] = ni
            return nr, ni

        xr, xi = lax.fori_loop(0, tm, step, (st_ref[:, s * sw:s * sw + nre], st_ref[:, s * sw + nre:(s + 1) * sw]),
                               unroll=True)
        st_ref[:, s * sw:s * sw + nre] = xr
        st_ref[:, s * sw + nre:(s + 1) * sw] = xi
        yt_ref[s] = jnp.dot(bu_ref[...].astype(BF16), wc_ref[s], preferred_element_type=F32)

    for b in range(nseq):
        yb = jnp.concatenate([yt_ref[s, pl.ds(b, tm, stride=nseq), :] for s in range(SSM_STRIPS)], axis=1)
        y_ref[b] = yb + u_ref[b] * d_ref[...]

    @pl.when(i == nc - 1)
    def _():
        fs_ref[...] = st_ref[...]


def _ssm_mix(u, ssm, h0):
    B, L, _ = u.shape
    nseq = SUBLANES
    tm = min(SSM_STEPS, L)
    nc = L // tm
    rows = nseq * tm
    ns = 2 * SSM_GROUPS * SSM_STATE
    chunk = lambda d, g, i: i + d * (nc - 1 - 2 * i)
    y, fs = pl.pallas_call(
        _ssm_kernel,
        grid=(2, B // nseq, nc),
        in_specs=[pl.BlockSpec((nseq, tm, SSM_WIDTH), lambda d, g, i: (g, chunk(d, g, i), 0)),
                  pl.BlockSpec((None, SSM_STRIPS, LANES, ns // SSM_STRIPS), lambda d, g, i: (d, 0, 0, 0)),
                  pl.BlockSpec((None, SSM_STRIPS, ns // SSM_STRIPS, LANES), lambda d, g, i: (d, 0, 0, 0)),
                  pl.BlockSpec((None, nseq, ns), lambda d, g, i: (d, 0, 0)),
                  pl.BlockSpec((None, 1, SSM_WIDTH), lambda d, g, i: (d, 0, 0)),
                  pl.BlockSpec((None, nseq, ns), lambda d, g, i: (d, g, 0))],
        out_specs=[pl.BlockSpec((None, nseq, tm, SSM_WIDTH), lambda d, g, i: (d, g, chunk(d, g, i), 0)),
                   pl.BlockSpec((None, nseq, ns), lambda d, g, i: (d, g, 0))],
        out_shape=[jax.ShapeDtypeStruct((2, B, L, SSM_WIDTH), F32),
                   jax.ShapeDtypeStruct((2, B, ns), F32)],
        scratch_shapes=[pltpu.VMEM((SSM_STRIPS, rows, LANES), F32), pltpu.VMEM((SSM_STRIPS, rows, LANES), F32),
                        pltpu.VMEM((nseq, ns), F32)]
        + [pltpu.VMEM((rows, ns // SSM_STRIPS), F32) for _ in range(SSM_STRIPS)],
        compiler_params=_cparams(("arbitrary", "arbitrary", "arbitrary")),
        name="ssm_scan",
    )(u, ssm["wb"], ssm["wc"], ssm["a"], ssm["d"], h0)
    return y, fs


def _state_to_lanes(s):
    lead = s.shape[:-3]
    s = s.reshape(lead + (2, SSM_STRIPS, SSM_GROUPS // SSM_STRIPS, SSM_STATE))
    s = jnp.swapaxes(s, -4, -3)
    return s.reshape(lead + (2 * SSM_GROUPS * SSM_STATE,))


def _lanes_to_state(v):
    lead = v.shape[:-1]
    s = v.reshape(lead + (SSM_STRIPS, 2, SSM_GROUPS // SSM_STRIPS, SSM_STATE))
    s = jnp.swapaxes(s, -4, -3)
    return s.reshape(lead + (2, SSM_GROUPS, SSM_STATE))


def _ssm_params(a_re, a_im, log_dt, b_re, b_im, c_re, c_im, dvec):
    G, N, C = SSM_GROUPS, SSM_STATE, SSM_GROUP
    lam_re = jnp.minimum(a_re, -1e-4)
    lam_im = a_im
    dt = jnp.exp(log_dt)[..., None]
    mag = jnp.exp(lam_re * dt)
    abar_re, abar_im = mag * jnp.cos(lam_im * dt), mag * jnp.sin(lam_im * dt)
    den = jnp.square(lam_re) + jnp.square(lam_im)
    p, qi = abar_re - 1.0, abar_im
    f_re = (p * lam_re + qi * lam_im) / den
    f_im = (qi * lam_re - p * lam_im) / den
    bbar_re = f_re[..., None] * b_re - f_im[..., None] * b_im
    bbar_im = f_re[..., None] * b_im + f_im[..., None] * b_re
    a = _state_to_lanes(jnp.stack([abar_re, abar_im], axis=1))
    a = jnp.broadcast_to(a[:, None, :], (2, SUBLANES, a.shape[-1]))
    S = SSM_STRIPS
    gs = G // S
    eye = jnp.eye(gs, dtype=F32)

    def bd_in(bb):
        bb = bb.reshape(2, S, gs, N, C)
        return jnp.einsum('dkgnc,gh->dkgchn', bb, eye).reshape(2, S, gs * C, gs * N)

    def bd_out(cc):
        cc = cc.reshape(2, S, gs, C, N)
        return jnp.einsum('dkgcn,gh->dkgnhc', cc, eye).reshape(2, S, gs * N, gs * C)

    wb = jnp.concatenate([bd_in(bbar_re), bd_in(bbar_im)], axis=-1).astype(BF16)
    wc = jnp.concatenate([bd_out(c_re), -bd_out(c_im)], axis=-2).astype(BF16)
    d = jnp.stack([dvec, jnp.zeros_like(dvec)], axis=0).reshape(2, 1, SSM_WIDTH)
    return {"a": a, "wb": wb, "wc": wc, "d": d}


def _layer_norm(x, g, b):
    mu = jnp.mean(x, axis=-1, keepdims=True)
    xc = x - mu
    var = jnp.mean(xc * xc, axis=-1, keepdims=True)
    return xc * lax.rsqrt(var + LN_EPS) * g + b


def _pack_rows(x):
    w = x.shape[1] // 2
    return pltpu.pack_elementwise([x[:, :w], x[:, w:]], packed_dtype=BF16)


def _unpack_rows(p):
    return (pltpu.unpack_elementwise(p, index=0, packed_dtype=BF16, unpacked_dtype=F32),
            pltpu.unpack_elementwise(p, index=1, packed_dtype=BF16, unpacked_dtype=F32))


def _gelu_tanh(x):
    return 0.5 * x * (1.0 + jnp.tanh(math.sqrt(2.0 / math.pi) * (x + 0.044715 * (x * x * x))))


def _merge_kernel(x_ref, ao_ref, yf_ref, yb_ref, ga_ref, gs_ref, mod_ref, wglu_ref, wa_ref, ws_ref, wo_ref,
                  lng_ref, lnb_ref, rwt_ref, rb_ref, run0_ref, x1_ref, h2_ref, idx_ref, rank_ref, wt_ref, cnt_ref,
                  run_ref, *, alpha):
    @pl.when((pl.program_id(0) == 0) & (pl.program_id(1) == 0))
    def _():
        run_ref[...] = run0_ref[...].astype(F32)

    nrow = x_ref.shape[0]
    sub = nrow // MERGE_SPLIT
    groups = [slice(j * sub, (j + 1) * sub) for j in range(MERGE_SPLIT)]
    dot = functools.partial(jnp.dot, preferred_element_type=F32)
    z = [_gelu_tanh(yf_ref[rs, :] + yb_ref[rs, :]) for rs in groups]
    attn_br = [dot(ao_ref[rs, :], wa_ref[...]) for rs in groups]
    gate = [jax.nn.sigmoid(dot(zg.astype(BF16), wglu_ref[...])) for zg in z]
    ssm_br = [dot((zg * gg).astype(BF16), ws_ref[...]) for zg, gg in zip(z, gate)]
    merged = [jax.nn.sigmoid(ga_ref[rs, :].astype(F32)) * ab + jax.nn.sigmoid(gs_ref[rs, :].astype(F32)) * sb
              for rs, ab, sb in zip(groups, attn_br, ssm_br)]
    mix = [dot(mg.astype(BF16), wo_ref[...]) for mg in merged]
    h2s = []
    for rs, mg in zip(groups, mix):
        x1 = _layer_norm(alpha * x_ref[rs, :] + mod_ref[2:3, :] * mg, lng_ref[...], lnb_ref[...])
        h2g = x1 * (1.0 + mod_ref[4:5, :]) + mod_ref[3:4, :]
        x1_ref[rs, :] = x1
        h2_ref[rs, :] = _pack_rows(h2g)
        h2s.append(h2g)
    h2 = jnp.concatenate(h2s, axis=0)

    logits = lax.dot_general(rwt_ref[...], h2, (((1,), (1,)), ((), ())), preferred_element_type=F32,
                             precision=lax.Precision.HIGHEST)
    score = jax.nn.sigmoid(logits)
    tm = score.shape[1]
    eidx = lax.broadcasted_iota(jnp.int32, score.shape, 0).astype(F32)
    work = score + rb_ref[...]
    picks, sel = [], []
    member = jnp.zeros_like(score)
    for _ in range(TOP_K):
        best = jnp.max(work, axis=0, keepdims=True)
        pick = jnp.min(jnp.where(work == best, eidx, float(N_EXPERTS)), axis=0, keepdims=True)
        hit = eidx == pick
        picks.append(pick)
        sel.append(jnp.sum(jnp.where(hit, score, 0.0), axis=0, keepdims=True))
        member = member + hit.astype(F32)
        work = jnp.where(hit, -jnp.inf, work)
    total = sel[0]
    for s in sel[1:]:
        total = total + s
    before = (lax.broadcasted_iota(jnp.int32, (tm, tm), 0) < lax.broadcasted_iota(jnp.int32, (tm, tm), 1))
    prefix = jnp.dot(member.astype(BF16), before.astype(BF16), preferred_element_type=F32)
    base = prefix + run_ref[...]
    for k in range(TOP_K):
        idx_ref[k:k + 1, :] = picks[k].astype(jnp.int32)
        rank_ref[k:k + 1, :] = jnp.sum(jnp.where(eidx == picks[k], base, 0.0), axis=0,
                                       keepdims=True).astype(jnp.int32)
        wt_ref[k:k + 1, :] = sel[k] / total * ROUTED_SCALE
    run_ref[...] = run_ref[...] + jnp.sum(member, axis=1, keepdims=True)
    cnt_ref[...] = run_ref[...].astype(jnp.int32)


def _merge(x, attn_o, y, ga, gs, mod, wts, run0, alpha):
    B, L, D = x.shape
    tm = min(MERGE_ROWS, L)
    per_batch = mod.shape[0] > 1
    row = lambda b, i: (b, i, 0)
    full = lambda b, i: (0, 0)
    wspec = lambda a: pl.BlockSpec(a.shape, full)
    return pl.pallas_call(
        functools.partial(_merge_kernel, alpha=alpha),
        grid=(B, L // tm),
        in_specs=[pl.BlockSpec((None, tm, D), row),
                  pl.BlockSpec((None, tm, ATTN_WIDTH), row),
                  pl.BlockSpec((None, None, tm, SSM_WIDTH), lambda b, i: (0, b, i, 0)),
                  pl.BlockSpec((None, None, tm, SSM_WIDTH), lambda b, i: (1, b, i, 0)),
                  pl.BlockSpec((None, tm, D), row),
                  pl.BlockSpec((None, tm, D), row),
                  pl.BlockSpec((None, N_MOD, D), (lambda b, i: (b, 0, 0)) if per_batch else (lambda b, i: (0, 0, 0))),
                  wspec(wts["w_glu"]), wspec(wts["w_attn_br"]), wspec(wts["w_ssm_br"]), wspec(wts["w_out"]),
                  wspec(wts["ln1_g"]), wspec(wts["ln1_b"]), wspec(wts["router_wt"]), wspec(wts["router_b"]),
                  wspec(run0)],
        out_specs=[pl.BlockSpec((None, tm, D), row),
                   pl.BlockSpec((None, tm, D // 2), row),
                   pl.BlockSpec((None, TOP_K, tm), lambda b, i: (b, 0, i)),
                   pl.BlockSpec((None, TOP_K, tm), lambda b, i: (b, 0, i)),
                   pl.BlockSpec((None, TOP_K, tm), lambda b, i: (b, 0, i)),
                   pl.BlockSpec((N_EXPERTS, 1), full)],
        out_shape=[jax.ShapeDtypeStruct((B, L, D), F32),
                   jax.ShapeDtypeStruct((B, L, D // 2), jnp.int32),
                   jax.ShapeDtypeStruct((B, TOP_K, L), jnp.int32),
                   jax.ShapeDtypeStruct((B, TOP_K, L), jnp.int32),
                   jax.ShapeDtypeStruct((B, TOP_K, L), F32),
                   jax.ShapeDtypeStruct((N_EXPERTS, 1), jnp.int32)],
        scratch_shapes=[pltpu.VMEM((N_EXPERTS, 1), F32)],
        compiler_params=_cparams(("arbitrary", "arbitrary")),
        name="merge_ln1_router",
    )(x, attn_o, y, y, ga, gs, mod, wts["w_glu"], wts["w_attn_br"], wts["w_ssm_br"], wts["w_out"],
      wts["ln1_g"], wts["ln1_b"], wts["router_wt"], wts["router_b"], run0)


def _dispatch(h2s, dests):
    W = h2s[0].shape[-1]
    n_slots = sum(d.size for d in dests)
    mesh = plsc.VectorSubcoreMesh(core_axis_name="core", subcore_axis_name="subcore")
    workers = mesh.num_cores * mesh.num_subcores
    tokens = [h.shape[0] * h.shape[1] for h in h2s]
    per_worker = [t // SC_WINDOW // workers for t in tokens]
    assert all(p * SC_WINDOW * workers == t for p, t in zip(per_worker, tokens))

    @pl.kernel(out_type=jax.ShapeDtypeStruct((n_slots, W), h2s[0].dtype), mesh=mesh,
               scratch_types=[pltpu.VMEM((TOP_K, SC_WINDOW), jnp.int32), pltpu.VMEM((SC_WINDOW, W), h2s[0].dtype)])
    def scatter(*refs):
        out_hbm, index_v, rows_v = refs[-3:]
        worker = lax.axis_index("core") * mesh.num_subcores + lax.axis_index("subcore")
        for g, per in enumerate(per_worker):
            rows_hbm, slots_hbm = refs[2 * g], refs[2 * g + 1]

            @pl.loop(0, per)
            def _(j, rows_hbm=rows_hbm, slots_hbm=slots_hbm, per=per):
                window = worker * per + j
                pltpu.sync_copy(rows_hbm.at[pl.ds(window * SC_WINDOW, SC_WINDOW)], rows_v)
                pltpu.sync_copy(slots_hbm.at[window], index_v)
                for k in range(TOP_K):
                    pltpu.sync_copy(rows_v, out_hbm.at[index_v.at[k]])

    args = []
    for h, d in zip(h2s, dests):
        nb, _, ln = d.shape
        slots = d.reshape(nb, TOP_K, ln // SC_WINDOW, SC_WINDOW).transpose(0, 2, 1, 3)
        args += [h.reshape(-1, W), slots.reshape(nb * ln // SC_WINDOW, TOP_K, SC_WINDOW)]
    return scatter(*args)


def _gmm_kernel(tile_ref, exp_ref, valid_ref, gstart_ref, xs_ref, wg_ref, wu_ref, wd_ref, ys_ref,
                wgb_ref, wub_ref, wdb_ref):
    w = pl.program_id(0)
    e = exp_ref[w]
    t = tile_ref[w]
    prev = jnp.maximum(w - 1, 0)
    new_expert = (w == 0) | (e != exp_ref[prev])
    first_visit = (w == 0) | (t != tile_ref[prev])

    @pl.when(new_expert)
    def _():
        wgb_ref[...] = wg_ref[...].astype(BF16)
        wub_ref[...] = wu_ref[...].astype(BF16)
        wdb_ref[...] = wd_ref[...].astype(BF16)

    @pl.when(first_visit)
    def _():
        ys_ref[...] = jnp.zeros_like(ys_ref)

    tg = xs_ref.shape[0]
    sub = tg // GMM_SPLIT
    lo = gstart_ref[e]
    hi = gstart_ref[e + 1]
    for j in range(GMM_SPLIT):
        first_row = t * tg + j * sub

        @pl.when((valid_ref[w] == 1) & (lo < first_row + sub) & (hi > first_row))
        def _(j=j, first_row=first_row):
            rs = slice(j * sub, (j + 1) * sub)
            dot = functools.partial(jnp.dot, preferred_element_type=F32)
            x = jnp.concatenate(_unpack_rows(xs_ref[rs, :]), axis=1).astype(BF16)
            g = dot(x, wgb_ref[...])
            u = dot(x, wub_ref[...])
            y = _pack_rows(dot((g * jax.nn.sigmoid(g) * u).astype(BF16), wdb_ref[...]))
            rows = first_row + lax.broadcasted_iota(jnp.int32, (sub, 1), 0)
            mine = (rows >= lo) & (rows < hi)
            ys_ref[rs, :] = jnp.where(mine, y, ys_ref[rs, :])


def _grouped_experts(xs, counts, w_gate, w_up, w_down):
    A, W = xs.shape
    D = w_gate.shape[-2]
    tg = GMM_ROWS
    nt = A // tg
    n_items = nt + N_EXPERTS - 1
    ff = w_gate.shape[-1]
    gend = jnp.cumsum(counts).astype(jnp.int32)
    gstart = jnp.concatenate([jnp.zeros((1,), jnp.int32), gend])
    first_row = jnp.arange(nt, dtype=jnp.int32) * tg
    count_le = lambda ends, v: jnp.sum((ends[None, :] <= v[:, None]).astype(jnp.int32), axis=1)
    e_lo = count_le(gend, first_row)
    e_hi = count_le(gend, first_row + tg - 1)
    per_tile = e_hi - e_lo + 1
    item_end = jnp.cumsum(per_tile).astype(jnp.int32)
    total = item_end[-1]
    wi = jnp.arange(n_items, dtype=jnp.int32)
    tile = jnp.minimum(count_le(item_end, wi), nt - 1)
    in_tile = (tile[:, None] == jnp.arange(nt, dtype=jnp.int32)[None, :]).astype(jnp.int32)
    lookup = lambda table: jnp.sum(in_tile * table[None, :], axis=1)
    expert = lookup(e_lo) + wi - lookup(item_end - per_tile)
    valid = (wi < total).astype(jnp.int32)
    expert = jnp.where(valid == 1, expert, e_hi[nt - 1])
    grid_spec = pltpu.PrefetchScalarGridSpec(
        num_scalar_prefetch=4,
        grid=(n_items,),
        in_specs=[pl.BlockSpec((tg, W), lambda w, tl, ex, va, gs: (tl[w], 0)),
                  pl.BlockSpec((None, D, ff), lambda w, tl, ex, va, gs: (ex[w], 0, 0)),
                  pl.BlockSpec((None, D, ff), lambda w, tl, ex, va, gs: (ex[w], 0, 0)),
                  pl.BlockSpec((None, ff, D), lambda w, tl, ex, va, gs: (ex[w], 0, 0))],
        out_specs=pl.BlockSpec((tg, W), lambda w, tl, ex, va, gs: (tl[w], 0)),
        scratch_shapes=[pltpu.VMEM((D, ff), BF16), pltpu.VMEM((D, ff), BF16), pltpu.VMEM((ff, D), BF16)],
    )
    return pl.pallas_call(
        _gmm_kernel,
        grid_spec=grid_spec,
        out_shape=jax.ShapeDtypeStruct((A, W), xs.dtype),
        compiler_params=_cparams(("arbitrary",)),
        name="moe_grouped_experts",
    )(tile, expert, valid, gstart, xs, w_gate, w_up, w_down)


def _gather_rows_sc(rows, index):
    n = index.shape[0]
    w = rows.shape[1]
    mesh = plsc.VectorSubcoreMesh(core_axis_name="core", subcore_axis_name="subcore")
    per_subcore = n // SC_WINDOW // (mesh.num_cores * mesh.num_subcores)
    assert per_subcore * SC_WINDOW * mesh.num_cores * mesh.num_subcores == n

    @pl.kernel(out_type=jax.ShapeDtypeStruct((n, w), rows.dtype), mesh=mesh,
               scratch_types=[pltpu.VMEM((1, SC_WINDOW), jnp.int32), pltpu.VMEM((SC_WINDOW, w), rows.dtype)])
    def gather(rows_hbm, index_hbm, out_hbm, index_v, rows_v):
        worker = lax.axis_index("core") * mesh.num_subcores + lax.axis_index("subcore")

        @pl.loop(0, per_subcore)
        def _(j):
            first = (worker * per_subcore + j) * SC_WINDOW
            pltpu.sync_copy(index_hbm.at[:, pl.ds(first, SC_WINDOW)], index_v)
            pltpu.sync_copy(rows_hbm.at[index_v.at[0]], rows_v)
            pltpu.sync_copy(rows_v, out_hbm.at[pl.ds(first, SC_WINDOW)])

    return gather(rows, index.reshape(1, n))


def _shared_expert_kernel(h2_ref, sg_ref, su_ref, sd_ref, o_ref):
    h = jnp.concatenate(_unpack_rows(h2_ref[...]), axis=1).astype(BF16)
    g = jnp.dot(h, sg_ref[...], preferred_element_type=F32)
    u = jnp.dot(h, su_ref[...], preferred_element_type=F32)
    o_ref[...] = _pack_rows(jnp.dot((g * jax.nn.sigmoid(g) * u).astype(BF16), sd_ref[...],
                                    preferred_element_type=F32))


def _shared_expert(h2, shared):
    B, L, W = h2.shape
    tm = min(COMBINE_ROWS, L)
    row = lambda b, i: (b, i, 0)
    wspec = lambda a: pl.BlockSpec(a.shape, lambda b, i: (0, 0))
    return pl.pallas_call(
        _shared_expert_kernel,
        grid=(B, L // tm),
        in_specs=[pl.BlockSpec((None, tm, W), row),
                  wspec(shared["sh_w_gate"]), wspec(shared["sh_w_up"]), wspec(shared["sh_w_down"])],
        out_specs=pl.BlockSpec((None, tm, W), row),
        out_shape=jax.ShapeDtypeStruct((B, L, W), h2.dtype),
        compiler_params=_cparams(("parallel", "parallel")),
        name="shared_expert",
    )(h2, shared["sh_w_gate"], shared["sh_w_up"], shared["sh_w_down"])


def _combine_rows_kernel(yg_ref, wt_ref, x1_ref, sh_ref, mod_ref, lng_ref, lnb_ref, *rest, alpha):
    o_ref = rest[-1]
    wt = wt_ref[...]
    lo, hi = _unpack_rows(sh_ref[...])
    for k in range(TOP_K):
        rl, rh = _unpack_rows(yg_ref[k])
        lo = lo + wt[:, k:k + 1] * rl
        hi = hi + wt[:, k:k + 1] * rh
    moe = jnp.concatenate([lo, hi], axis=1)
    o_ref[...] = _layer_norm(alpha * x1_ref[...] + mod_ref[5:6, :] * moe, lng_ref[...], lnb_ref[...])


def _combine_rows(ys, dest, wts, x1, sh, mod, shared, alpha):
    B, L, D = x1.shape
    W = ys.shape[-1]
    tm = min(COMBINE_ROWS, L)
    per_batch = mod.shape[0] > 1
    pieces = max(1, B * L // COMBINE_PIECE_TOKENS)
    nb = B // pieces
    full = lambda b, i: (0, 0)
    wspec = lambda a: pl.BlockSpec(a.shape, full)
    wts_t = jnp.swapaxes(wts, 1, 2)
    out = None
    for p in range(pieces):
        b0 = p * nb
        row = lambda b, i, b0=b0: (b0 + b, i, 0)
        slots = jnp.swapaxes(dest[b0:b0 + nb], 0, 1).reshape(TOP_K * nb * L)
        yg = _gather_rows_sc(ys, slots).reshape(TOP_K, nb, L, W)
        in_specs = [pl.BlockSpec((TOP_K, None, tm, W), lambda b, i: (0, b, i, 0)),
                    pl.BlockSpec((None, tm, TOP_K), row),
                    pl.BlockSpec((None, tm, D), row),
                    pl.BlockSpec((None, tm, W), row),
                    pl.BlockSpec((None, N_MOD, D), (lambda b, i, b0=b0: (b0 + b, 0, 0)) if per_batch
                                 else (lambda b, i: (0, 0, 0))),
                    wspec(shared["ln2_g"]), wspec(shared["ln2_b"])]
        args = [yg, wts_t, x1, sh, mod, shared["ln2_g"], shared["ln2_b"]]
        aliases = {}
        if out is not None:
            in_specs.append(pl.BlockSpec(memory_space=pl.ANY))
            args.append(out)
            aliases = {len(args) - 1: 0}
        out = pl.pallas_call(
            functools.partial(_combine_rows_kernel, alpha=alpha),
            grid=(nb, L // tm),
            in_specs=in_specs,
            out_specs=pl.BlockSpec((None, tm, D), row),
            out_shape=jax.ShapeDtypeStruct((B, L, D), F32),
            input_output_aliases=aliases,
            compiler_params=_cparams(("parallel", "parallel")),
            name="moe_combine_ln2",
        )(*args)
    return out


def _slots(idx, rank, counts):
    start = jnp.cumsum(counts) - counts
    pick = idx[..., None] == jnp.arange(N_EXPERTS, dtype=jnp.int32)
    return jnp.sum(jnp.where(pick, start, 0), axis=-1).astype(jnp.int32) + rank


def _mixer_and_router(x, mod, wts, ssm, cache, run0, alpha):
    B, L, D = x.shape
    latent = cache is not None
    if latent:
        k_ctx, v_ctx, h0 = cache
        cos, sin = _rope_cos_sin(L)
    else:
        cos = jnp.zeros((L, LANES), F32)
        sin = cos
        h0 = jnp.zeros((2, B, 2 * SSM_GROUPS * SSM_STATE), F32)
    q, k, v, u, ga, gs = _in_projection(x, mod, wts["w_in"], cos, sin, rope=latent)
    if latent:
        attn_o = _attention_latent(q, k, v, k_ctx, v_ctx, wts["attn_sink"])
    else:
        attn_o = _attention_context(q, k, v, wts["attn_sink"])
    y, fs = _ssm_mix(u, ssm, h0)
    x1, h2, idx, rank, rw, counts = _merge(x, attn_o, y, ga, gs, mod, wts, run0, alpha)
    return {"x1": x1, "h2": h2, "idx": idx, "rank": rank, "rw": rw, "mod": mod}, counts, k, v, fs


def _moe(groups, counts, wts, alpha):
    counts = counts.reshape(N_EXPERTS)
    for g in groups:
        g["dest"] = _slots(g["idx"], g["rank"], counts)
    xs = _dispatch([g["h2"] for g in groups], [g["dest"] for g in groups])
    shared_out = [_shared_expert(g["h2"], wts) for g in groups]
    ys = _grouped_experts(xs, counts, wts["exp_w_gate"], wts["exp_w_up"], wts["exp_w_down"])
    return [_combine_rows(ys, g["dest"], g["rw"], g["x1"], sh, g["mod"], wts, alpha)
            for g, sh in zip(groups, shared_out)]


def kernel(x_prompt, x_sample, cache_k, cache_v, state_ssm, c, c_ctx, mod_w, mod_b, w_in, attn_sink, w_attn_br, ssm_a_re, ssm_a_im, ssm_log_dt, ssm_b_re, ssm_b_im, ssm_c_re, ssm_c_im, ssm_d, w_glu, w_ssm_br, w_out, ln1_g, ln1_b, ln2_g, ln2_b, router_w, router_b, exp_w_gate, exp_w_up, exp_w_down, sh_w_gate, sh_w_up, sh_w_down):
    depth = w_in.shape[0]
    assert depth == 1
    alpha = (2.0 * depth) ** 0.25
    D = x_prompt.shape[-1]
    nb_p = x_prompt.shape[0]
    nb_s = x_sample.shape[0]
    l = 0

    ncond = 1 + nb_s
    npad = -ncond % SUBLANES
    cond = jnp.concatenate([c_ctx[None, :], c, jnp.zeros((npad, D), F32)], axis=0)
    mod = _modulation(cond, mod_w[l], mod_b[l]).reshape(ncond + npad, N_MOD, D)
    mod_ctx, mod_lat = mod[0:1], mod[1:ncond]

    o1 = ATTN_WIDTH
    o3 = o1 + 2 * KV_WIDTH
    o4 = o3 + SSM_WIDTH
    wi = w_in[l]
    scale = HEAD_DIM ** -0.5 * LOG2_E
    wts = {
        "w_in": jnp.concatenate([wi[:, :o1] * scale, wi[:, o1:]], axis=1).astype(BF16),
        "attn_sink": attn_sink[l],
        "w_glu": w_glu[l].astype(BF16), "w_attn_br": w_attn_br[l].astype(BF16),
        "w_ssm_br": w_ssm_br[l].astype(BF16), "w_out": w_out[l].astype(BF16),
        "ln1_g": ln1_g[l].reshape(1, D), "ln1_b": ln1_b[l].reshape(1, D),
        "ln2_g": ln2_g[l].reshape(1, D), "ln2_b": ln2_b[l].reshape(1, D),
        "router_wt": router_w[l].T, "router_b": router_b[l].reshape(N_EXPERTS, 1),
        "exp_w_gate": exp_w_gate[l], "exp_w_up": exp_w_up[l], "exp_w_down": exp_w_down[l],
        "sh_w_gate": sh_w_gate[l].astype(BF16), "sh_w_up": sh_w_up[l].astype(BF16),
        "sh_w_down": sh_w_down[l].astype(BF16),
    }
    ssm = _ssm_params(ssm_a_re[l], ssm_a_im[l], ssm_log_dt[l], ssm_b_re[l], ssm_b_im[l],
                      ssm_c_re[l], ssm_c_im[l], ssm_d[l])

    no_tokens_yet = jnp.zeros((N_EXPERTS, 1), jnp.int32)
    grp_p, counts, k_p, v_p, fs_p = _mixer_and_router(x_prompt, mod_ctx, wts, ssm, None, no_tokens_yet, alpha)

    past = cache_k.shape[2]
    k_ctx = cache_k[:, l].reshape(nb_s, past, KV_WIDTH)
    v_ctx = cache_v[:, l].reshape(nb_s, past, KV_WIDTH)
    h0 = jnp.swapaxes(_state_to_lanes(state_ssm[:, l]), 0, 1)
    grp_s, counts, _, _, _ = _mixer_and_router(x_sample, mod_lat, wts, ssm, (k_ctx, v_ctx, h0), counts, alpha)
    yp, ys_ = _moe([grp_p, grp_s], counts, wts, alpha)

    Lp = x_prompt.shape[1]
    new_k = k_p.reshape(nb_p, 1, Lp, N_KV_HEADS, HEAD_DIM)
    new_v = v_p.reshape(nb_p, 1, Lp, N_KV_HEADS, HEAD_DIM)
    new_s = _lanes_to_state(jnp.swapaxes(fs_p, 0, 1))[:, None]
    return (yp, ys_, new_k, new_v, new_s)
```
